```python
import math
import jax
import jax.numpy as jnp
from jax import lax
import numpy as np

D_MODEL = 1024
BATCH = 2
SEQ = 8192
DEPTH = 1
DEC_BATCH = 32
DEC_SEQ = 1
PAST_LEN = 16384
PAGE_SIZE = 128

GLA_HEADS = 4
GLA_DK = D_MODEL // 2 // GLA_HEADS
GLA_DV = D_MODEL // GLA_HEADS
GLA_RANK = 16
GLA_TAU = 16.0
GLA_CHUNK = 64
ATT_GROUPS = ((128, 1), (512, 4), (2048, 16))
N_GROUPS = 3
ATT_HEADS = 4
ATT_HD = 128
ATT_BLOCK = 128
ROPE_DIM = ATT_HD // 4
ROPE_THETA = 500000.0
PEER_HEADS = 8
PEER_NKEYS = 128
PEER_NEXPERTS = PEER_NKEYS * PEER_NKEYS
PEER_DQ = 128
PEER_TOPK = 16
PEER_BLOCK = 128
NORM_EPS = 1e-6

W_QA = GLA_HEADS * GLA_DK
W_VA = GLA_HEADS * GLA_DV
W_QB = N_GROUPS * ATT_HEADS * ATT_HD
W_OB = ATT_HEADS * ATT_HD
IN_WIDTHS = (W_QA, W_QA, W_VA, W_VA, GLA_RANK, W_QB, W_QB, W_QB, D_MODEL, D_MODEL)
IN_TOTAL = 2 * W_QA + 2 * W_VA + GLA_RANK + 3 * W_QB + 2 * D_MODEL

kernel_name = 'hybrid_gla_dilated_peer_step'


def rms_norm(x, g):
    xf = x.astype(jnp.float32)
    y = xf * lax.rsqrt(jnp.mean(xf * xf, axis=-1, keepdims=True) + NORM_EPS)
    return (y * g.astype(jnp.float32)).astype(x.dtype)


def partial_rope(x, pos):
    half = ROPE_DIM // 2
    inv_freq = ROPE_THETA ** (-jnp.arange(half, dtype=jnp.float32) / half)
    ang = pos.astype(jnp.float32)[:, None] * inv_freq[None, :]
    cos = jnp.cos(ang)[:, None, :]
    sin = jnp.sin(ang)[:, None, :]
    xr = x[..., :ROPE_DIM].astype(jnp.float32)
    x1, x2 = xr[..., :half], xr[..., half:]
    rot = jnp.concatenate([x1 * cos - x2 * sin, x2 * cos + x1 * sin], axis=-1)
    return jnp.concatenate([rot.astype(x.dtype), x[..., ROPE_DIM:]], axis=-1)


def gla_chunked(q, k, v, la, s0):
    n, l, h, dk = q.shape
    dv = v.shape[-1]
    c = min(GLA_CHUNK, l)
    lp = -(-l // c) * c

    def to_chunks(t):
        t = jnp.pad(t.astype(jnp.float32), ((0, 0), (0, lp - l), (0, 0), (0, 0)))
        return t.reshape(n, lp // c, c, h, t.shape[-1]).transpose(1, 0, 3, 2, 4)

    causal = jnp.tril(jnp.ones((c, c), dtype=bool))[:, :, None]

    def step(s, xs):
        qc, kc, vc, lc = xs
        b = jnp.cumsum(lc, axis=2)
        inter = jnp.einsum('nhtk,nhkv->nhtv', qc * jnp.exp(b), s)
        diff = b[:, :, :, None, :] - b[:, :, None, :, :]
        decay = jnp.exp(jnp.where(causal, diff, -jnp.inf))
        att = jnp.einsum('nhtk,nhsk,nhtsk->nhts', qc, kc, decay)
        o = inter + jnp.einsum('nhts,nhsv->nhtv', att, vc)
        b_last = b[:, :, -1:, :]
        s = jnp.exp(b_last[:, :, 0, :])[..., None] * s + jnp.einsum('nhsk,nhsv->nhkv', kc * jnp.exp(b_last - b), vc)
        return s, o

    s_fin, o = lax.scan(step, s0.astype(jnp.float32), (to_chunks(q), to_chunks(k), to_chunks(v), to_chunks(la)))
    o = o.transpose(1, 0, 3, 2, 4).reshape(n, lp, h, dv)[:, :l]
    return o, s_fin


def band_attention(q, k, v, steps):
    n, l, h, d = q.shape
    blk = ATT_BLOCK
    lp = -(-l // blk) * blk
    nb = lp // blk

    def blocks(t):
        return jnp.pad(t, ((0, 0), (0, lp - l), (0, 0), (0, 0))).reshape(n, nb, blk, h, d)

    def two_blocks(t):
        tb = blocks(t)
        prev = jnp.pad(tb, ((0, 0), (1, 0), (0, 0), (0, 0), (0, 0)))[:, :-1]
        return jnp.concatenate([prev, tb], axis=2)

    qb, kb, vb = blocks(q), two_blocks(k), two_blocks(v)
    s = jnp.einsum('nbqhd,nbkhd->nbhqk', qb, kb, preferred_element_type=jnp.float32) * (d ** -0.5)
    qi = jnp.arange(blk)[:, None]
    ki = jnp.arange(2 * blk)[None, :]
    dist = blk + qi - ki
    band = (dist >= 0) & (dist <= steps)
    valid = (jnp.arange(nb)[:, None, None] > 0) | (ki >= blk)[None]
    mask = band[None] & valid
    s = jnp.where(mask[None, :, None], s, -jnp.inf)
    lse = jax.nn.logsumexp(s, axis=-1)
    p = jnp.exp(s - lse[..., None])
    o = jnp.einsum('nbhqk,nbkhd->nbqhd', p, vb.astype(jnp.float32)).reshape(n, lp, h, d)[:, :l]
    lse = lse.transpose(0, 1, 3, 2).reshape(n, lp, h)[:, :l]
    return o, lse


def dilated_prompt(q, k, v, window, dil):
    b, l, h, d = q.shape

    def split(t):
        return t.reshape(b, l // dil, dil, h, d).transpose(0, 2, 1, 3, 4).reshape(b * dil, l // dil, h, d)

    o, lse = band_attention(split(q), split(k), split(v), window // dil)
    o = o.reshape(b, dil, l // dil, h, d).transpose(0, 2, 1, 3, 4).reshape(b, l, h, d)
    lse = lse.reshape(b, dil, l // dil, h).transpose(0, 2, 1, 3).reshape(b, l, h)
    return o, lse


def dilated_step(q, kv_new, buf, window, dil):
    kv = jnp.concatenate([buf, kv_new.astype(buf.dtype)], axis=1)
    wb = buf.shape[1]
    t = q.shape[1]
    idx = wb + jnp.arange(t)[:, None] - dil * jnp.arange(window // dil + 1)[None, :]
    valid = idx >= 0
    g = kv[:, jnp.maximum(idx, 0)]
    s = jnp.einsum('nthd,ntkhd->nthk', q, g[:, :, :, 0], preferred_element_type=jnp.float32) * (q.shape[-1] ** -0.5)
    s = jnp.where(valid[None, :, None, :], s, -jnp.inf)
    lse = jax.nn.logsumexp(s, axis=-1)
    p = jnp.exp(s - lse[..., None])
    o = jnp.einsum('nthk,ntkhd->nthd', p, g[:, :, :, 1].astype(jnp.float32))
    keep = min(window, wb + t)
    return o, lse, kv[:, wb + t - keep:]


def peer_ffn(h, w_q, k1, k2, u, v):
    t, d = h.shape
    q = (h @ w_q).reshape(t, PEER_HEADS, 2, PEER_DQ // 2).astype(jnp.float32)
    s1 = jnp.einsum('thd,nd->thn', q[:, :, 0], k1.astype(jnp.float32))
    s2 = jnp.einsum('thd,nd->thn', q[:, :, 1], k2.astype(jnp.float32))
    v1, i1 = lax.top_k(s1, PEER_TOPK)
    v2, i2 = lax.top_k(s2, PEER_TOPK)
    cand = (v1[..., :, None] + v2[..., None, :]).reshape(t, PEER_HEADS, PEER_TOPK * PEER_TOPK)
    sc, ci = lax.top_k(cand, PEER_TOPK)
    e = (jnp.take_along_axis(i1, ci // PEER_TOPK, axis=-1) * PEER_NKEYS
         + jnp.take_along_axis(i2, ci % PEER_TOPK, axis=-1))
    gate = jax.nn.softmax(sc, axis=-1)
    blk = min(PEER_BLOCK, t)
    tp = -(-t // blk) * blk
    nb = tp // blk
    hp = jnp.pad(h, ((0, tp - t), (0, 0))).reshape(nb, blk, d)
    ep = jnp.pad(e, ((0, tp - t), (0, 0), (0, 0))).reshape(nb, blk, PEER_HEADS, PEER_TOPK)
    gp = jnp.pad(gate, ((0, tp - t), (0, 0), (0, 0))).reshape(nb, blk, PEER_HEADS, PEER_TOPK)

    def one_block(args):
        hb, eb, gb = args
        a = jax.nn.gelu(jnp.einsum('td,thkd->thk', hb, u[eb], preferred_element_type=jnp.float32))
        return jnp.einsum('thk,thkd->td', (gb * a).astype(v.dtype), v[eb])

    out = lax.map(one_block, (hp, ep, gp))
    return out.reshape(tp, d)[:t].astype(h.dtype)


def decoder_layer(x, c, pos, gla_s0, att_bufs, ada_w, ada_b, norm1_g, w_in, gla_a_w2, gla_a_b, gla_gn_g,
                  w_branch_a, w_branch_b, w_out, norm2_g, peer_wq, peer_k1, peer_k2, peer_u, peer_v):
    n, l, d = x.shape
    mod = jax.nn.silu(c) @ ada_w + ada_b
    sh1, sc1, g1, sh2, sc2, g2 = jnp.split(mod[:, None, :], 6, axis=-1)
    h = rms_norm(x, norm1_g) * (1 + sc1) + sh1
    z = h @ w_in
    points = [int(p) for p in np.cumsum(IN_WIDTHS)[:-1]]
    qa, ka, va, ra, alr, qb, kb, vb, gza, gzb = jnp.split(z, points, axis=-1)

    la = jax.nn.log_sigmoid((alr @ gla_a_w2 + gla_a_b).astype(jnp.float32)) / GLA_TAU
    o_a, s_a = gla_chunked(qa.reshape(n, l, GLA_HEADS, GLA_DK) * (GLA_DK ** -0.5),
                           ka.reshape(n, l, GLA_HEADS, GLA_DK),
                           va.reshape(n, l, GLA_HEADS, GLA_DV),
                           la.reshape(n, l, GLA_HEADS, GLA_DK), gla_s0)
    o_a = rms_norm(o_a, gla_gn_g.reshape(GLA_HEADS, GLA_DV)).reshape(n, l, W_VA)
    o_a = o_a.astype(x.dtype) * jax.nn.silu(ra)

    q_all = partial_rope(qb.reshape(n, l, N_GROUPS * ATT_HEADS, ATT_HD), pos)
    k_all = partial_rope(kb.reshape(n, l, N_GROUPS * ATT_HEADS, ATT_HD), pos)
    v_all = vb.reshape(n, l, N_GROUPS * ATT_HEADS, ATT_HD)
    outs, lses, new_bufs = [], [], []
    for gi in range(N_GROUPS):
        window, dil = ATT_GROUPS[gi]
        sl = slice(gi * ATT_HEADS, (gi + 1) * ATT_HEADS)
        q_g, k_g, v_g = q_all[:, :, sl], k_all[:, :, sl], v_all[:, :, sl]
        kv_g = jnp.stack([k_g, v_g], axis=2)
        if att_bufs is None:
            o_g, lse_g = dilated_prompt(q_g, k_g, v_g, window, dil)
            new_bufs.append(kv_g[:, l - min(window, l):])
        else:
            o_g, lse_g, buf_g = dilated_step(q_g, kv_g, att_bufs[gi], window, dil)
            new_bufs.append(buf_g)
        outs.append(o_g)
        lses.append(lse_g)
    wts = jax.nn.softmax(jnp.stack(lses, axis=0), axis=0)
    o_b = jnp.einsum('gnlh,gnlhd->nlhd', wts, jnp.stack(outs, axis=0)).reshape(n, l, W_OB).astype(x.dtype)

    merged = jax.nn.sigmoid(gza) * (o_a @ w_branch_a) + jax.nn.sigmoid(gzb) * (o_b @ w_branch_b)
    x = x + g1 * (merged @ w_out)

    h2 = rms_norm(x, norm2_g) * (1 + sc2) + sh2
    x = x + g2 * peer_ffn(h2.reshape(n * l, d), peer_wq, peer_k1, peer_k2, peer_u, peer_v).reshape(n, l, d)
    return x, s_a.astype(gla_s0.dtype), new_bufs


def setup_inputs(seed: int = 0) -> dict:
    key = jax.random.key(seed)
    ks = jax.random.split(key, 32)

    def nrm(k, shape, scale):
        return jax.random.normal(k, shape, jnp.float32) * scale

    def kv_shape(w):
        return (DEPTH, DEC_BATCH, min(w, PAST_LEN), 2, ATT_HEADS, ATT_HD)

    return {
        'x_prompt': nrm(ks[0], (BATCH, SEQ, D_MODEL), 1.0),
        'x_sample': nrm(ks[1], (DEC_BATCH, DEC_SEQ, D_MODEL), 1.0),
        'c_prompt': nrm(ks[2], (BATCH, D_MODEL), 1.0),
        'c_sample': nrm(ks[3], (DEC_BATCH, D_MODEL), 1.0),
        'state_gla': nrm(ks[4], (DEPTH, DEC_BATCH, GLA_HEADS, GLA_DK, GLA_DV), 0.3),
        'cache_kv_w128': nrm(ks[5], kv_shape(128), 1.0),
        'cache_kv_w512': nrm(ks[6], kv_shape(512), 1.0),
        'cache_kv_w2048': nrm(ks[7], kv_shape(2048), 1.0),
        'ada_w': nrm(ks[8], (DEPTH, D_MODEL, 6 * D_MODEL), 0.5 * D_MODEL ** -0.5),
        'ada_b': nrm(ks[9], (DEPTH, 6 * D_MODEL), 0.01),
        'norm1_g': 1.0 + nrm(ks[10], (DEPTH, D_MODEL), 0.05),
        'w_in': nrm(ks[11], (DEPTH, D_MODEL, IN_TOTAL), D_MODEL ** -0.5),
        'gla_a_w2': nrm(ks[12], (DEPTH, GLA_RANK, W_QA), GLA_RANK ** -0.5),
        'gla_a_b': nrm(ks[13], (DEPTH, W_QA), 0.1),
        'gla_gn_g': 1.0 + nrm(ks[14], (DEPTH, W_VA), 0.05),
        'w_branch_a': nrm(ks[15], (DEPTH, W_VA, D_MODEL), W_VA ** -0.5),
        'w_branch_b': nrm(ks[16], (DEPTH, W_OB, D_MODEL), W_OB ** -0.5),
        'w_out': nrm(ks[17], (DEPTH, D_MODEL, D_MODEL), D_MODEL ** -0.5),
        'norm2_g': 1.0 + nrm(ks[18], (DEPTH, D_MODEL), 0.05),
        'peer_wq': nrm(ks[19], (DEPTH, D_MODEL, PEER_HEADS * PEER_DQ), D_MODEL ** -0.5),
        'peer_k1': nrm(ks[20], (DEPTH, PEER_NKEYS, PEER_DQ // 2), (PEER_DQ // 2) ** -0.5),
        'peer_k2': nrm(ks[21], (DEPTH, PEER_NKEYS, PEER_DQ // 2), (PEER_DQ // 2) ** -0.5),
        'peer_u': nrm(ks[22], (DEPTH, PEER_NEXPERTS, D_MODEL), D_MODEL ** -0.5),
        'peer_v': nrm(ks[23], (DEPTH, PEER_NEXPERTS, D_MODEL), PEER_HEADS ** -0.5),
        'final_g': 1.0 + nrm(ks[24], (D_MODEL,), 0.05),
    }


def reference(x_prompt, x_sample, c_prompt, c_sample, state_gla, cache_kv_w128, cache_kv_w512, cache_kv_w2048,
              ada_w, ada_b, norm1_g, w_in, gla_a_w2, gla_a_b, gla_gn_g, w_branch_a, w_branch_b, w_out,
              norm2_g, peer_wq, peer_k1, peer_k2, peer_u, peer_v, final_g):
    n_p, l_p = x_prompt.shape[0], x_prompt.shape[1]
    l_s = x_sample.shape[1]
    pos_p = jnp.arange(l_p, dtype=jnp.int32)
    pos_s = PAST_LEN + jnp.arange(l_s, dtype=jnp.int32)
    s0_p = jnp.zeros((n_p, GLA_HEADS, GLA_DK, GLA_DV), state_gla.dtype)
    xp, xs = x_prompt, x_sample
    gla_p, gla_s = [], []
    kv_p = ([], [], [])
    kv_s = ([], [], [])
    for i in range(DEPTH):
        lw = (ada_w[i], ada_b[i], norm1_g[i], w_in[i], gla_a_w2[i], gla_a_b[i], gla_gn_g[i],
              w_branch_a[i], w_branch_b[i], w_out[i], norm2_g[i], peer_wq[i], peer_k1[i], peer_k2[i],
              peer_u[i], peer_v[i])
        xp, sp, bp = decoder_layer(xp, c_prompt, pos_p, s0_p, None, *lw)
        xs, ss, bs = decoder_layer(xs, c_sample, pos_s, state_gla[i],
                                   (cache_kv_w128[i], cache_kv_w512[i], cache_kv_w2048[i]), *lw)
        gla_p.append(sp)
        gla_s.append(ss)
        for g in range(N_GROUPS):
            kv_p[g].append(bp[g])
            kv_s[g].append(bs[g])
    y_prompt = rms_norm(xp, final_g)
    y_sample = rms_norm(xs, final_g)
    return (y_prompt, y_sample, jnp.stack(gla_p, 0), jnp.stack(gla_s, 0),
            jnp.stack(kv_p[0], 0), jnp.stack(kv_s[0], 0), jnp.stack(kv_p[1], 0), jnp.stack(kv_s[1], 0),
            jnp.stack(kv_p[2], 0), jnp.stack(kv_s[2], 0))
```

```python
import functools
import math

import jax
import jax.numpy as jnp
from jax import lax
from jax.experimental import pallas as pl
from jax.experimental.pallas import tpu as pltpu

F32 = jnp.float32
BF16 = jnp.bfloat16

PAST_LEN = 16384
GLA_HEADS = 4
GLA_RANK = 16
GLA_TAU = 16.0
GLA_CHUNK = 64
ATT_GROUPS = ((128, 1), (512, 4), (2048, 16))
ATT_HEADS = 4
ATT_HD = 128
ATT_BLOCK = 128
ROPE_DIM = ATT_HD // 4
ROPE_THETA = 500000.0
PEER_HEADS = 8
PEER_NKEYS = 128
PEER_DQ = 128
PEER_TOPK = 16
NORM_EPS = 1e-6
GELU_C = math.sqrt(2.0 / math.pi)

LANES = 128
SUBLANES = 8
VMEM_LIMIT = 56 * 1024 * 1024
NEG_BIG = -1e30


def _cparams(*sem):
    return pltpu.CompilerParams(dimension_semantics=sem, vmem_limit_bytes=VMEM_LIMIT)


def _sigmoid(x):
    return 1.0 / (1.0 + jnp.exp(-x))


def _silu(x):
    return x * _sigmoid(x)


def _log_sigmoid(x):
    return jnp.minimum(x, 0.0) - jnp.log(1.0 + jnp.exp(-jnp.abs(x)))


def _rms(x, g):
    return x * lax.rsqrt(jnp.mean(x * x, axis=-1, keepdims=True) + NORM_EPS) * g


def _dot(a, b):
    return jnp.dot(a, b, preferred_element_type=F32)


def _dot_nt(a, b):
    return lax.dot_general(a, b, (((1,), (1,)), ((), ())), preferred_element_type=F32)


def _dot_tn(a, b):
    return lax.dot_general(a, b, (((0,), (0,)), ((), ())), preferred_element_type=F32)


def _mod_kernel(c_ref, w_ref, b_ref, o_ref):
    s = _silu(c_ref[...]).astype(BF16)
    o_ref[...] = _dot(s, w_ref[...].astype(BF16)) + b_ref[...]


def _modulation(c, ada_w, ada_b):
    n, d = c.shape
    n6 = ada_w.shape[1]
    tn = n6 // 4
    return pl.pallas_call(
        _mod_kernel,
        grid=(n6 // tn,),
        in_specs=[pl.BlockSpec((n, d), lambda j: (0, 0)),
                  pl.BlockSpec((d, tn), lambda j: (0, j)),
                  pl.BlockSpec((1, tn), lambda j: (0, j))],
        out_specs=pl.BlockSpec((n, tn), lambda j: (0, j)),
        out_shape=jax.ShapeDtypeStruct((n, n6), F32),
        compiler_params=_cparams("arbitrary"),
        name="modulation",
    )(c, ada_w, ada_b.reshape(1, n6))


TN_IN = 512


def _rope_tile(z, c, s1, s2):
    reps = z.shape[1] // LANES
    cc = jnp.concatenate([c] * reps, axis=1)
    a = jnp.concatenate([s1] * reps, axis=1)
    b = jnp.concatenate([s2] * reps, axis=1)
    n = z.shape[1]
    return z * cc + pltpu.roll(z, n - ROPE_DIM // 2, 1) * a + pltpu.roll(z, ROPE_DIM // 2, 1) * b


def _inproj_kernel(x_ref, g_ref, sc_ref, sh_ref, w_ref, walr_ref, w2_ref, ab_ref, c_ref, s1_ref, s2_ref,
                   z_ref, la_ref, h_scr, *, rope_lo, rope_hi):
    j = pl.program_id(1)

    @pl.when(j == 0)
    def _():
        h = _rms(x_ref[...], g_ref[...]) * (1.0 + sc_ref[...]) + sh_ref[...]
        hb = h.astype(BF16)
        h_scr[...] = hb
        alr = _dot(hb, walr_ref[...])
        p = _dot(alr.astype(BF16), w2_ref[...]) + ab_ref[...]
        la_ref[...] = _log_sigmoid(p) * (1.0 / GLA_TAU)

    z = _dot(h_scr[...], w_ref[...])
    is_rope = jnp.logical_and(j >= rope_lo, j < rope_hi)

    @pl.when(is_rope)
    def _():
        z_ref[...] = _rope_tile(z, c_ref[...], s1_ref[...], s2_ref[...])

    @pl.when(jnp.logical_not(is_rope))
    def _():
        z_ref[...] = z


def _in_proj(x2, g, sc, sh, w_main, w_alr, w2p, a_b, tabs, tm, rows_per_mod, rope_lo, rope_hi):
    t, d = x2.shape
    nm = w_main.shape[1]
    r = sc.shape[1]
    wq = a_b.shape[1]
    tiles_per_mod = rows_per_mod // tm
    tab_tiles = tabs[0].shape[0] // tm
    mod_spec = pl.BlockSpec((None, r, d), lambda i, j: (i // tiles_per_mod, 0, 0))
    tab_spec = pl.BlockSpec((tm, LANES), lambda i, j: (i % tab_tiles, 0))
    const = lambda shape: pl.BlockSpec(shape, lambda i, j: (0, 0))
    return pl.pallas_call(
        functools.partial(_inproj_kernel, rope_lo=rope_lo, rope_hi=rope_hi),
        grid=(t // tm, nm // TN_IN),
        in_specs=[pl.BlockSpec((tm, d), lambda i, j: (i, 0)), const((1, d)), mod_spec, mod_spec,
                  pl.BlockSpec((d, TN_IN), lambda i, j: (0, j)), const((d, LANES)), const((LANES, wq)),
                  const((1, wq)), tab_spec, tab_spec, tab_spec],
        out_specs=[pl.BlockSpec((tm, TN_IN), lambda i, j: (i, j)), pl.BlockSpec((tm, wq), lambda i, j: (i, 0))],
        out_shape=[jax.ShapeDtypeStruct((t, nm), F32), jax.ShapeDtypeStruct((t, wq), F32)],
        scratch_shapes=[pltpu.VMEM((tm, d), BF16)],
        compiler_params=_cparams("arbitrary", "arbitrary"),
        name="in_proj",
    )(x2, g, sc, sh, w_main, w_alr, w2p, a_b, *tabs)


GLA_EXP_CLAMP = 80.0


def _gla_kernel(q_ref, k_ref, v_ref, ra_ref, la_ref, s0_ref, gn_ref, o_ref, sfin_ref, st_scr, *, rows, dk, dv):
    c = pl.program_id(1)
    nc = pl.num_programs(1)
    ch = GLA_CHUNK

    @pl.when(c == 0)
    def _():
        for h in range(GLA_HEADS):
            st_scr[h] = s0_ref[h].T

    ti = lax.broadcasted_iota(jnp.int32, (ch, ch), 0)
    si = lax.broadcasted_iota(jnp.int32, (ch, ch), 1)
    causal = si <= ti
    tri = causal.astype(F32)
    scale = dk ** -0.5

    def chunk(ci, carry):
        r0 = pl.multiple_of(ci * ch, ch)
        for h in range(GLA_HEADS):
            la = la_ref[pl.ds(r0, ch), h * dk:(h + 1) * dk]
            b = jnp.dot(tri, la, preferred_element_type=F32, precision=lax.Precision.HIGHEST)
            bl = b[ch - 1:ch, :]
            q = q_ref[pl.ds(r0, ch), h * dk:(h + 1) * dk] * scale
            k = k_ref[pl.ds(r0, ch), h * dk:(h + 1) * dk]
            v = v_ref[pl.ds(r0, ch), h * dv:(h + 1) * dv].astype(BF16)
            qe = (q * jnp.exp(b)).astype(BF16)
            ke = (k * jnp.exp(jnp.minimum(-b, GLA_EXP_CLAMP))).astype(BF16)
            kh = (k * jnp.exp(bl - b)).astype(BF16)
            att = jnp.where(causal, _dot_nt(qe, ke), 0.0)
            st = st_scr[h]
            o = _dot_nt(qe, st.astype(BF16)) + _dot(att.astype(BF16), v)
            st_scr[h] = st * jnp.exp(bl) + _dot_tn(v, kh)
            on = _rms(o, gn_ref[:, h * dv:(h + 1) * dv])
            gate = _silu(ra_ref[pl.ds(r0, ch), h * dv:(h + 1) * dv])
            o_ref[pl.ds(r0, ch), h * dv:(h + 1) * dv] = (on * gate).astype(o_ref.dtype)
        return carry

    lax.fori_loop(0, rows // ch, chunk, 0)

    @pl.when(c == nc - 1)
    def _():
        for h in range(GLA_HEADS):
            sfin_ref[h] = st_scr[h].T


def _gla_prompt(z, la, s0, gn_g, n, l, rows):
    t = n * l
    dk = la.shape[1] // GLA_HEADS
    dv = gn_g.shape[1] // GLA_HEADS
    wk, wv = GLA_HEADS * dk, GLA_HEADS * dv
    cpb = l // rows
    row = lambda b, c: b * cpb + c
    return pl.pallas_call(
        functools.partial(_gla_kernel, rows=rows, dk=dk, dv=dv),
        grid=(n, cpb),
        in_specs=[pl.BlockSpec((rows, wk), lambda b, c: (row(b, c), 0)),
                  pl.BlockSpec((rows, wk), lambda b, c: (row(b, c), 1)),
                  pl.BlockSpec((rows, wv), lambda b, c: (row(b, c), 1)),
                  pl.BlockSpec((rows, wv), lambda b, c: (row(b, c), 2)),
                  pl.BlockSpec((rows, wk), lambda b, c: (row(b, c), 0)),
                  pl.BlockSpec((None, GLA_HEADS, dk, dv), lambda b, c: (b, 0, 0, 0)),
                  pl.BlockSpec((1, wv), lambda b, c: (0, 0))],
        out_specs=[pl.BlockSpec((rows, wv), lambda b, c: (row(b, c), 0)),
                   pl.BlockSpec((None, GLA_HEADS, dk, dv), lambda b, c: (b, 0, 0, 0))],
        out_shape=[jax.ShapeDtypeStruct((t, wv), BF16), jax.ShapeDtypeStruct((n, GLA_HEADS, dk, dv), F32)],
        scratch_shapes=[pltpu.VMEM((GLA_HEADS, dv, dk), F32)],
        compiler_params=_cparams("arbitrary", "arbitrary"),
        name="gla_prompt",
    )(z, z, z, z, la, s0, gn_g)


def _gla_step_kernel(q_ref, k_ref, la_ref, v_ref, ra_ref, s_ref, gn_ref, o_ref, so_ref, *, dk):
    sn = jnp.exp(la_ref[...]) * s_ref[...] + k_ref[...] * v_ref[...]
    so_ref[...] = sn
    o = jnp.sum((q_ref[...] * dk ** -0.5) * sn, axis=1, keepdims=True)
    on = _rms(o, gn_ref[...])
    o_ref[...] = (on * _silu(ra_ref[...])).astype(o_ref.dtype)


def _gla_step(q, k, la, v, ra, s, gn_g):
    nb, nh, dk, dv = s.shape
    col = pl.BlockSpec((None, nh, dk, 1), lambda b: (b, 0, 0, 0))
    rowv = pl.BlockSpec((None, nh, 1, dv), lambda b: (b, 0, 0, 0))
    st = pl.BlockSpec((None, nh, dk, dv), lambda b: (b, 0, 0, 0))
    o, so = pl.pallas_call(
        functools.partial(_gla_step_kernel, dk=dk),
        grid=(nb,),
        in_specs=[col, col, col, rowv, rowv, st, pl.BlockSpec((nh, 1, dv), lambda b: (0, 0, 0))],
        out_specs=[rowv, st],
        out_shape=[jax.ShapeDtypeStruct((nb, nh, 1, dv), BF16), jax.ShapeDtypeStruct(s.shape, F32)],
        compiler_params=_cparams("arbitrary"),
        name="gla_step",
    )(q.reshape(nb, nh, dk, 1), k.reshape(nb, nh, dk, 1), la.reshape(nb, nh, dk, 1),
      v.reshape(nb, nh, 1, dv), ra.reshape(nb, nh, 1, dv), s, gn_g.reshape(nh, 1, dv))
    return o.reshape(nb, nh * dv), so


def _lane_pack(cols):
    m = cols[0].shape[0]
    lane = lax.broadcasted_iota(jnp.int32, (m, LANES), 1)
    out = jnp.zeros((m, LANES), F32)
    for i, cidx in enumerate(cols):
        out = jnp.where(lane == i, cidx, out)
    return out


def _band_kernel(q_ref, k_ref, kp_ref, v_ref, vp_ref, o_ref, lse_ref, *, qb):
    i = pl.program_id(2)
    blk = ATT_BLOCK
    hd = ATT_HD
    qi = lax.broadcasted_iota(jnp.int32, (blk, 2 * blk), 0)
    ki = lax.broadcasted_iota(jnp.int32, (blk, 2 * blk), 1)
    dist = blk + qi - ki
    band = jnp.logical_and(dist >= 0, dist <= ATT_BLOCK)
    band_first = jnp.logical_and(band, ki >= jnp.where(i > 0, 0, blk))
    scale = hd ** -0.5
    for sb in range(qb // blk):
        lses = []
        for h in range(ATT_HEADS):
            cs = slice(h * hd, (h + 1) * hd)
            rs = slice(sb * blk, (sb + 1) * blk)
            q = q_ref[rs, cs].astype(BF16)
            if sb == 0:
                kprev, vprev = kp_ref[:, cs], vp_ref[:, cs]
                mask = band_first
            else:
                ps = slice((sb - 1) * blk, sb * blk)
                kprev, vprev = k_ref[ps, cs], v_ref[ps, cs]
                mask = band
            kk = jnp.concatenate([kprev, k_ref[rs, cs]], axis=0).astype(BF16)
            vv = jnp.concatenate([vprev, v_ref[rs, cs]], axis=0).astype(BF16)
            s = jnp.where(mask, _dot_nt(q, kk) * scale, NEG_BIG)
            m = jnp.max(s, axis=-1, keepdims=True)
            p = jnp.exp(s - m)
            lsum = jnp.sum(p, axis=-1, keepdims=True)
            o_ref[rs, cs] = _dot(p.astype(BF16), vv) / lsum
            lses.append(m + jnp.log(lsum))
        lse_ref[sb * blk:(sb + 1) * blk, :] = _lane_pack(lses)


def _band_attention(z, n, l, dil, q_off, k_off, v_off, nm):
    lq = l // dil
    wh = ATT_HEADS * ATT_HD
    qb = min(4 * ATT_BLOCK, lq)
    nqb = lq // qb
    cpr = nm // wh
    zv = z.reshape(n * lq, dil * nm)
    sub = qb // ATT_BLOCK

    def main(off):
        return pl.BlockSpec((qb, wh), lambda b, r, i: (b * nqb + i, r * cpr + off // wh))

    def prev(off):
        return pl.BlockSpec((ATT_BLOCK, wh),
                            lambda b, r, i: (jnp.maximum(b * nqb * sub + i * sub - 1, 0), r * cpr + off // wh))

    o, lse = pl.pallas_call(
        functools.partial(_band_kernel, qb=qb),
        grid=(n, dil, nqb),
        in_specs=[main(q_off), main(k_off), prev(k_off), main(v_off), prev(v_off)],
        out_specs=[pl.BlockSpec((qb, wh), lambda b, r, i: (b * nqb + i, r)),
                   pl.BlockSpec((qb, LANES), lambda b, r, i: (b * nqb + i, r))],
        out_shape=[jax.ShapeDtypeStruct((n * lq, dil * wh), F32), jax.ShapeDtypeStruct((n * lq, dil * LANES), F32)],
        compiler_params=_cparams("arbitrary", "arbitrary", "arbitrary"),
        name=f"band_attention_d{dil}",
    )(zv, zv, zv, zv, zv)
    return o.reshape(n * l, wh), lse.reshape(n * l, LANES)


def _att_step_kernel(q_ref, kn_ref, vn_ref, c0_ref, c1_ref, c2_ref, o0_ref, o1_ref, o2_ref,
                     l0_ref, l1_ref, l2_ref):
    hd = ATT_HD
    wh = ATT_HEADS * hd
    scale = hd ** -0.5
    for g, (c_ref, o_ref, l_ref) in enumerate(((c0_ref, o0_ref, l0_ref), (c1_ref, o1_ref, l1_ref),
                                               (c2_ref, o2_ref, l2_ref))):
        lses = []
        for h in range(ATT_HEADS):
            cs = slice(g * wh + h * hd, g * wh + (h + 1) * hd)
            q = q_ref[:, cs]
            kn = kn_ref[:, cs]
            vn = vn_ref[:, cs]
            kc = c_ref[:, h * hd:(h + 1) * hd].astype(BF16)
            vc = c_ref[:, wh + h * hd:wh + (h + 1) * hd].astype(BF16)
            q8 = jnp.broadcast_to(q, (SUBLANES, hd)).astype(BF16)
            s = _dot_nt(q8, kc)[0:1, :] * scale
            sn = jnp.sum(q * kn, axis=-1, keepdims=True) * scale
            m = jnp.maximum(jnp.max(s, axis=-1, keepdims=True), sn)
            p = jnp.exp(s - m)
            pn = jnp.exp(sn - m)
            lsum = jnp.sum(p, axis=-1, keepdims=True) + pn
            p8 = jnp.broadcast_to(p, (SUBLANES, p.shape[1])).astype(BF16)
            o = (_dot(p8, vc)[0:1, :] + pn * vn) / lsum
            o_ref[:, h * hd:(h + 1) * hd] = o
            lses.append(m + jnp.log(lsum))
        l_ref[...] = _lane_pack(lses)


def _att_step(qb, kb, vb, caches):
    nb = qb.shape[0]
    wh = ATT_HEADS * ATT_HD
    views = []
    for (window, dil), cch in zip(ATT_GROUPS, caches):
        assert cch.shape[1] == window, "sample step expects full caches"
        views.append(cch.reshape(nb, window // dil, dil * 2 * wh))
    row = lambda w: pl.BlockSpec((None, 1, w), lambda b: (b, 0, 0))
    cspec = pl.BlockSpec((None, ATT_BLOCK, 2 * wh), lambda b: (b, 0, 0))
    outs = pl.pallas_call(
        _att_step_kernel,
        grid=(nb,),
        in_specs=[row(qb.shape[1])] * 3 + [cspec] * 3,
        out_specs=[row(wh)] * 3 + [row(LANES)] * 3,
        out_shape=[jax.ShapeDtypeStruct((nb, 1, wh), F32)] * 3 + [jax.ShapeDtypeStruct((nb, 1, LANES), F32)] * 3,
        compiler_params=_cparams("arbitrary"),
        name="att_step",
    )(qb.reshape(nb, 1, -1), kb.reshape(nb, 1, -1), vb.reshape(nb, 1, -1), *views)
    return [o.reshape(nb, -1) for o in outs[:3]], [x.reshape(nb, LANES) for x in outs[3:]]


def _kv_shift_kernel(a_ref, nxt_ref, new_ref, o_ref, *, rows):
    i = pl.program_id(1)
    last = pl.num_programs(1) - 1
    o_ref[0:rows - 1] = a_ref[1:rows]

    @pl.when(i == last)
    def _():
        o_ref[rows - 1:rows] = new_ref[...]

    @pl.when(i != last)
    def _():
        o_ref[rows - 1:rows] = nxt_ref[...]


def _kv_shift(cache, new_row):
    nb, wb = cache.shape[0], cache.shape[1]
    rw = cache.shape[2] * cache.shape[3] * cache.shape[4]
    sub = rw // LANES
    rows = min(wb, 512)
    cv = cache.reshape(nb, wb, sub, LANES)
    out = pl.pallas_call(
        functools.partial(_kv_shift_kernel, rows=rows),
        grid=(nb, wb // rows),
        in_specs=[pl.BlockSpec((None, rows, sub, LANES), lambda b, i: (b, i, 0, 0)),
                  pl.BlockSpec((None, 1, sub, LANES), lambda b, i: (b, jnp.minimum((i + 1) * rows, wb - 1), 0, 0)),
                  pl.BlockSpec((None, 1, sub, LANES), lambda b, i: (b, 0, 0, 0))],
        out_specs=pl.BlockSpec((None, rows, sub, LANES), lambda b, i: (b, i, 0, 0)),
        out_shape=jax.ShapeDtypeStruct(cv.shape, cache.dtype),
        compiler_params=_cparams("arbitrary", "arbitrary"),
        name=f"kv_shift_w{wb}",
    )(cv, cv, new_row.reshape(nb, 1, sub, LANES))
    return out.reshape(cache.shape)


def _merge_kernel(oa_ref, o0_ref, o1_ref, o2_ref, l0_ref, l1_ref, l2_ref, gza_ref, gzb_ref, x_ref,
                  g1_ref, sc2_ref, sh2_ref, n2_ref, wa_ref, wb_ref, wo_ref, wq_ref, x1_ref, h2_ref, pq_ref):
    hd = ATT_HD
    l0, l1, l2 = l0_ref[...], l1_ref[...], l2_ref[...]
    mx = jnp.maximum(jnp.maximum(l0, l1), l2)
    e0, e1, e2 = jnp.exp(l0 - mx), jnp.exp(l1 - mx), jnp.exp(l2 - mx)
    inv = 1.0 / (e0 + e1 + e2)
    w0, w1, w2 = e0 * inv, e1 * inv, e2 * inv
    parts = []
    for h in range(ATT_HEADS):
        cs = slice(h * hd, (h + 1) * hd)
        parts.append(w0[:, h:h + 1] * o0_ref[:, cs] + w1[:, h:h + 1] * o1_ref[:, cs] + w2[:, h:h + 1] * o2_ref[:, cs])
    ob = jnp.concatenate(parts, axis=1).astype(BF16)
    ma = _dot(oa_ref[...], wa_ref[...])
    mb = _dot(ob, wb_ref[...])
    merged = _sigmoid(gza_ref[...]) * ma + _sigmoid(gzb_ref[...]) * mb
    x1 = x_ref[...] + g1_ref[...] * _dot(merged.astype(BF16), wo_ref[...])
    x1_ref[...] = x1
    h2 = (_rms(x1, n2_ref[...]) * (1.0 + sc2_ref[...]) + sh2_ref[...]).astype(BF16)
    h2_ref[...] = h2
    pq_ref[...] = _dot(h2, wq_ref[...])


def _merge(oa, og, lse, z, x2, g1, sc2, sh2, n2g, wa, wb, wo, wq, tm, rows_per_mod, gza_blk, gzb_blk):
    t, d = x2.shape
    wh = og[0].shape[1]
    r = g1.shape[1]
    tiles_per_mod = rows_per_mod // tm
    tok = lambda w: pl.BlockSpec((tm, w), lambda i: (i, 0))
    mod = pl.BlockSpec((None, r, d), lambda i: (i // tiles_per_mod, 0, 0))
    const = lambda a: pl.BlockSpec(a.shape, lambda i: (0, 0))
    return pl.pallas_call(
        _merge_kernel,
        grid=(t // tm,),
        in_specs=[tok(oa.shape[1]), tok(wh), tok(wh), tok(wh), tok(LANES), tok(LANES), tok(LANES),
                  pl.BlockSpec((tm, d), lambda i: (i, gza_blk)), pl.BlockSpec((tm, d), lambda i: (i, gzb_blk)),
                  tok(d), mod, mod, mod, const(n2g), const(wa), const(wb), const(wo), const(wq)],
        out_specs=[tok(d), tok(d), tok(wq.shape[1])],
        out_shape=[jax.ShapeDtypeStruct((t, d), F32), jax.ShapeDtypeStruct((t, d), BF16),
                   jax.ShapeDtypeStruct((t, wq.shape[1]), F32)],
        compiler_params=_cparams("arbitrary"),
        name="merge",
    )(oa, *og, *lse, z, z, x2, g1, sc2, sh2, n2g, wa, wb, wo, wq)


def _strict_max_below(rows, prev):
    m = None
    for x in rows:
        y = jnp.where(x < prev, x, NEG_BIG) if prev is not None else x
        m = y if m is None else jnp.maximum(m, y)
    return m


def _topk_desc(s, k):
    vals = []
    prev = None
    for _ in range(k):
        y = s if prev is None else jnp.where(s < prev, s, NEG_BIG)
        prev = jnp.max(y, axis=0, keepdims=True)
        vals.append(prev)
    return vals


def _route_kernel(pq_ref, k1_ref, k2_ref, s1_ref, s2_ref, aux_ref):
    nk = PEER_NKEYS
    pq = pq_ref[...].astype(BF16)
    s1 = _dot_nt(k1_ref[...], pq)
    s2 = _dot_nt(k2_ref[...], pq)
    v1h, v2h = [], []
    for h in range(PEER_HEADS):
        a = s1[h * nk:(h + 1) * nk]
        b = s2[h * nk:(h + 1) * nk]
        s1_ref[h] = a
        s2_ref[h] = b
        v1h.append(_topk_desc(a, PEER_TOPK))
        v2h.append(_topk_desc(b, PEER_TOPK))
    v1 = [jnp.concatenate([v1h[h][k] for h in range(PEER_HEADS)], axis=0) for k in range(PEER_TOPK)]
    v2 = [jnp.concatenate([v2h[h][k] for h in range(PEER_HEADS)], axis=0) for k in range(PEER_TOPK)]
    cands = [v1[i] + v2[j] for i in range(PEER_TOPK) for j in range(PEER_TOPK)
             if (i + 1) * (j + 1) <= PEER_TOPK]
    tau = None
    for _ in range(PEER_TOPK):
        tau = _strict_max_below(cands, tau)
    cmax = v1[0] + v2[0]
    zsum = None
    for cnd in cands:
        e = jnp.where(cnd >= tau, jnp.exp(cnd - cmax), 0.0)
        zsum = e if zsum is None else zsum + e
    aux_ref[0] = tau
    aux_ref[1] = v1[0]
    aux_ref[2] = v2[0]
    aux_ref[3] = 1.0 / zsum


def _peer_route(pq, k1big, k2big, tt):
    t, w = pq.shape
    nh, nk = PEER_HEADS, PEER_NKEYS
    sspec = pl.BlockSpec((nh, nk, tt), lambda i: (0, 0, i))
    return pl.pallas_call(
        _route_kernel,
        grid=(t // tt,),
        in_specs=[pl.BlockSpec((tt, w), lambda i: (i, 0)),
                  pl.BlockSpec(k1big.shape, lambda i: (0, 0)), pl.BlockSpec(k2big.shape, lambda i: (0, 0))],
        out_specs=[sspec, sspec, pl.BlockSpec((4, nh, tt), lambda i: (0, 0, i))],
        out_shape=[jax.ShapeDtypeStruct((nh, nk, t), F32), jax.ShapeDtypeStruct((nh, nk, t), F32),
                   jax.ShapeDtypeStruct((4, nh, t), F32)],
        compiler_params=_cparams("arbitrary"),
        name="peer_route",
    )(pq, k1big, k2big)


def _gelu_tanh(x):
    return 0.5 * x * (1.0 + jnp.tanh(GELU_C * (x + 0.044715 * (x * x * x))))


def _peer_dense_kernel(h2_ref, u_ref, vt_ref, s1_ref, s2_ref, aux_ref, o_ref, e2_scr, acc_scr, wg_scr, *, et):
    e = pl.program_id(1)
    nk = PEER_NKEYS

    @pl.when(e == 0)
    def _():
        for h in range(PEER_HEADS):
            e2_scr[h] = jnp.exp(s2_ref[h] - aux_ref[2, h:h + 1, :])
        acc_scr[...] = jnp.zeros_like(acc_scr)

    st = _dot_nt(u_ref[...], h2_ref[...])
    tau = aux_ref[0]
    m1 = aux_ref[1]
    zinv = aux_ref[3]
    for ii in range(et // nk):
        w = None
        for h in range(PEER_HEADS):
            s1row = s1_ref[h, ii:ii + 1, :]
            e1 = jnp.exp(s1row - m1[h:h + 1, :]) * zinv[h:h + 1, :]
            sel = (s1row + s2_ref[h]) >= tau[h:h + 1, :]
            c = jnp.where(sel, e2_scr[h], 0.0) * e1
            w = c if w is None else w + c
        wg_scr[ii * nk:(ii + 1) * nk, :] = (w * _gelu_tanh(st[ii * nk:(ii + 1) * nk, :])).astype(BF16)
    acc_scr[...] += _dot(vt_ref[...], wg_scr[...])

    @pl.when(e == pl.num_programs(1) - 1)
    def _():
        o_ref[...] = acc_scr[...].T


def _peer_dense(h2, u_b, vt_b, s1, s2, aux, tt, et):
    t, d = h2.shape
    ne = u_b.shape[0]
    nh, nk = PEER_HEADS, PEER_NKEYS
    return pl.pallas_call(
        functools.partial(_peer_dense_kernel, et=et),
        grid=(t // tt, ne // et),
        in_specs=[pl.BlockSpec((tt, d), lambda i, e: (i, 0)),
                  pl.BlockSpec((et, d), lambda i, e: (e, 0)),
                  pl.BlockSpec((d, et), lambda i, e: (0, e)),
                  pl.BlockSpec((nh, et // nk, tt), lambda i, e: (0, e, i)),
                  pl.BlockSpec((nh, nk, tt), lambda i, e: (0, 0, i)),
                  pl.BlockSpec((4, nh, tt), lambda i, e: (0, 0, i))],
        out_specs=pl.BlockSpec((tt, d), lambda i, e: (i, 0)),
        out_shape=jax.ShapeDtypeStruct((t, d), F32),
        scratch_shapes=[pltpu.VMEM((nh, nk, tt), F32), pltpu.VMEM((d, tt), F32), pltpu.VMEM((et, tt), BF16)],
        compiler_params=_cparams("arbitrary", "arbitrary"),
        name="peer_dense",
    )(h2, u_b, vt_b, s1, s2, aux)


def _final_kernel(x1_ref, p_ref, g2_ref, fg_ref, y_ref):
    y_ref[...] = _rms(x1_ref[...] + g2_ref[...] * p_ref[...], fg_ref[...])


def _final(x1, p, g2, fg, tm, rows_per_mod):
    t, d = x1.shape
    r = g2.shape[1]
    tiles_per_mod = rows_per_mod // tm
    tok = pl.BlockSpec((tm, d), lambda i: (i, 0))
    return pl.pallas_call(
        _final_kernel,
        grid=(t // tm,),
        in_specs=[tok, tok, pl.BlockSpec((None, r, d), lambda i: (i // tiles_per_mod, 0, 0)),
                  pl.BlockSpec((1, d), lambda i: (0, 0))],
        out_specs=tok,
        out_shape=jax.ShapeDtypeStruct((t, d), F32),
        compiler_params=_cparams("arbitrary"),
        name="final_norm",
    )(x1, p, g2, fg)


def _rope_tables(pos):
    half = ROPE_DIM // 2
    inv_freq = ROPE_THETA ** (-jnp.arange(half, dtype=F32) / half)
    ang = pos.astype(F32)[:, None] * inv_freq[None, :]
    cos, sin = jnp.cos(ang), jnp.sin(ang)
    n = pos.shape[0]
    ones = jnp.ones((n, LANES - ROPE_DIM), F32)
    zeros = jnp.zeros((n, LANES - ROPE_DIM), F32)
    zh = jnp.zeros((n, half), F32)
    return (jnp.concatenate([cos, cos, ones], axis=1),
            jnp.concatenate([-sin, zh, zeros], axis=1),
            jnp.concatenate([zh, sin, zeros], axis=1))


def _block_diag_keys(keys, half):
    nk, dh = keys.shape
    eye = jnp.eye(PEER_HEADS, dtype=keys.dtype)
    blk = jnp.zeros((PEER_HEADS, nk, PEER_HEADS, 2, dh), keys.dtype)
    blk = blk.at[:, :, :, half, :].set(eye[:, None, :, None] * keys[None, :, None, :])
    return blk.reshape(PEER_HEADS * nk, PEER_HEADS * 2 * dh).astype(BF16)


def _pick_tile(n, pref):
    t = min(pref, n)
    while n % t:
        t //= 2
    return t


def kernel(x_prompt, x_sample, c_prompt, c_sample, state_gla, cache_kv_w128, cache_kv_w512, cache_kv_w2048,
           ada_w, ada_b, norm1_g, w_in, gla_a_w2, gla_a_b, gla_gn_g, w_branch_a, w_branch_b, w_out,
           norm2_g, peer_wq, peer_k1, peer_k2, peer_u, peer_v, final_g):
    depth = ada_w.shape[0]
    assert depth == 1, "single-layer trunk"
    n_p, l_p, d = x_prompt.shape
    n_s, l_s, _ = x_sample.shape
    assert l_s == 1, "sample group decodes one token per sequence"
    caches = (cache_kv_w128[0], cache_kv_w512[0], cache_kv_w2048[0])

    w_qa = gla_a_w2.shape[2]
    w_va = gla_gn_g.shape[1]
    w_qb = len(ATT_GROUPS) * ATT_HEADS * ATT_HD
    w_ob = ATT_HEADS * ATT_HD
    widths = (w_qa, w_qa, w_va, w_va, GLA_RANK, w_qb, w_qb, w_qb, d, d)
    offs = [0]
    for w in widths:
        offs.append(offs[-1] + w)
    seg = lambda i: w_in[0][:, offs[i]:offs[i + 1]]
    w_main = jnp.concatenate([seg(0), seg(1), seg(2), seg(3), seg(8), seg(9), seg(5), seg(6), seg(7)],
                             axis=1).astype(BF16)
    nm = w_main.shape[1]
    q_off = 2 * w_qa + 2 * w_va + 2 * d
    k_off, v_off = q_off + w_qb, q_off + 2 * w_qb
    gza_blk = (2 * w_qa + 2 * w_va) // d
    gzb_blk = gza_blk + 1
    w_alr = jnp.pad(seg(4), ((0, 0), (0, LANES - GLA_RANK))).astype(BF16)
    w2p = jnp.pad(gla_a_w2[0], ((0, LANES - GLA_RANK), (0, 0))).astype(BF16)
    a_b = gla_a_b[0].reshape(1, w_qa)
    g1n = norm1_g[0].reshape(1, d)
    gn_g = gla_gn_g[0].reshape(1, w_va)
    n2g = norm2_g[0].reshape(1, d)
    wa, wb = w_branch_a[0].astype(BF16), w_branch_b[0].astype(BF16)
    wo, wq = w_out[0].astype(BF16), peer_wq[0].astype(BF16)

    n_mod = n_p + n_s
    n_pad = -(-n_mod // SUBLANES) * SUBLANES
    c_all = jnp.concatenate([c_prompt, c_sample, jnp.zeros((n_pad - n_mod, d), F32)], axis=0)
    mod = _modulation(c_all, ada_w[0], ada_b[0])
    mods_p = [mod[:n_p, i * d:(i + 1) * d].reshape(n_p, 1, d) for i in range(6)]
    mods_s = [mod[n_p:n_mod, i * d:(i + 1) * d].reshape(1, n_s, d) for i in range(6)]

    rope_lo, rope_hi = q_off // TN_IN, v_off // TN_IN

    xp2 = x_prompt.reshape(n_p * l_p, d)
    tm_p = _pick_tile(l_p, 1024)
    tabs_p = _rope_tables(jnp.arange(l_p, dtype=jnp.int32))
    z_p, la_p = _in_proj(xp2, g1n, mods_p[1], mods_p[0], w_main, w_alr, w2p, a_b, tabs_p, tm_p, l_p,
                         rope_lo, rope_hi)
    s0_p = jnp.zeros((n_p,) + state_gla.shape[2:], F32)
    oa_p, sfin_p = _gla_prompt(z_p, la_p, s0_p, gn_g, n_p, l_p, _pick_tile(l_p, 256))
    og_p, lse_p = [], []
    for gi, (window, dil) in enumerate(ATT_GROUPS):
        assert window // dil == ATT_BLOCK
        o_g, lse_g = _band_attention(z_p, n_p, l_p, dil, q_off + gi * w_ob, k_off + gi * w_ob,
                                     v_off + gi * w_ob, nm)
        og_p.append(o_g)
        lse_p.append(lse_g)
    tm_m = _pick_tile(l_p, 256)
    x1_p, h2_p, pq_p = _merge(oa_p, og_p, lse_p, z_p, xp2, mods_p[2], mods_p[4], mods_p[3], n2g,
                              wa, wb, wo, wq, tm_m, l_p, gza_blk, gzb_blk)
    kv_p = []
    for gi, (window, dil) in enumerate(ATT_GROUPS):
        keep = min(window, l_p)
        z3 = z_p.reshape(n_p, l_p, nm)[:, l_p - keep:]
        kk = z3[:, :, k_off + gi * w_ob:k_off + (gi + 1) * w_ob].reshape(n_p, keep, ATT_HEADS, ATT_HD)
        vv = z3[:, :, v_off + gi * w_ob:v_off + (gi + 1) * w_ob].reshape(n_p, keep, ATT_HEADS, ATT_HD)
        kv_p.append(jnp.stack([kk, vv], axis=2)[None])

    xs2 = x_sample.reshape(n_s, d)
    tabs_s = _rope_tables(jnp.full((n_s,), PAST_LEN, dtype=jnp.int32))
    z_s, la_s = _in_proj(xs2, g1n, mods_s[1], mods_s[0], w_main, w_alr, w2p, a_b, tabs_s, n_s, n_s,
                         rope_lo, rope_hi)
    oa_s, state_s = _gla_step(z_s[:, 0:w_qa], z_s[:, w_qa:2 * w_qa], la_s,
                              z_s[:, 2 * w_qa:2 * w_qa + w_va], z_s[:, 2 * w_qa + w_va:2 * w_qa + 2 * w_va],
                              state_gla[0], gn_g)
    og_s, lse_s = _att_step(z_s[:, q_off:q_off + w_qb], z_s[:, k_off:k_off + w_qb], z_s[:, v_off:v_off + w_qb],
                            caches)
    x1_s, h2_s, pq_s = _merge(oa_s, og_s, lse_s, z_s, xs2, mods_s[2], mods_s[4], mods_s[3], n2g,
                              wa, wb, wo, wq, n_s, n_s, gza_blk, gzb_blk)
    kv_s = []
    for gi, cch in enumerate(caches):
        new_row = jnp.concatenate([z_s[:, k_off + gi * w_ob:k_off + (gi + 1) * w_ob],
                                   z_s[:, v_off + gi * w_ob:v_off + (gi + 1) * w_ob]], axis=1)
        kv_s.append(_kv_shift(cch, new_row)[None])

    t_p = n_p * l_p
    tt = _pick_tile(t_p, 512)
    t_all = t_p + (-(-n_s // tt)) * tt
    pad = t_all - t_p - n_s
    h2_all = jnp.concatenate([h2_p, h2_s, jnp.zeros((pad, d), BF16)], axis=0)
    pq_all = jnp.concatenate([pq_p, pq_s, jnp.zeros((pad, pq_p.shape[1]), F32)], axis=0)
    k1big = _block_diag_keys(peer_k1[0], 0)
    k2big = _block_diag_keys(peer_k2[0], 1)
    s1, s2, aux = _peer_route(pq_all, k1big, k2big, _pick_tile(tt, 256))
    u_b = peer_u[0].astype(BF16)
    vt_b = peer_v[0].astype(BF16).T
    p_all = _peer_dense(h2_all, u_b, vt_b, s1, s2, aux, tt, PEER_NKEYS * SUBLANES)

    y_p = _final(x1_p, p_all[:t_p], mods_p[5], final_g.reshape(1, d), tm_m, l_p)
    y_s = _final(x1_s, p_all[t_p:t_p + n_s], mods_s[5], final_g.reshape(1, d), n_s, n_s)

    return (y_p.reshape(n_p, l_p, d), y_s.reshape(n_s, l_s, d), sfin_p[None], state_s[None],
            kv_p[0], kv_s[0], kv_p[1], kv_s[1], kv_p[2], kv_s[2])
```

```python
import functools
import math

import jax
import jax.numpy as jnp
from jax import lax
from jax.experimental import pallas as pl
from jax.experimental.pallas import tpu as pltpu

F32 = jnp.float32
BF16 = jnp.bfloat16

PAST_LEN = 16384
GLA_HEADS = 4
GLA_RANK = 16
GLA_TAU = 16.0
GLA_CHUNK = 64
ATT_GROUPS = ((128, 1), (512, 4), (2048, 16))
ATT_HEADS = 4
ATT_HD = 128
ATT_BLOCK = 128
ATT_TOKENS = 2048
ROPE_DIM = ATT_HD // 4
ROPE_THETA = 500000.0
PEER_HEADS = 8
PEER_NKEYS = 128
PEER_DQ = 128
PEER_TOPK = 16
NORM_EPS = 1e-6
GELU_C = math.sqrt(2.0 / math.pi)

LANES = 128
SUBLANES = 8
VMEM_LIMIT = 56 * 1024 * 1024
NEG_BIG = -1e30


def _cparams(*sem):
    return pltpu.CompilerParams(dimension_semantics=sem, vmem_limit_bytes=VMEM_LIMIT)


def _sigmoid(x):
    return 1.0 / (1.0 + jnp.exp(-x))


def _silu(x):
    return x * _sigmoid(x)


def _log_sigmoid(x):
    return jnp.minimum(x, 0.0) - jnp.log(1.0 + jnp.exp(-jnp.abs(x)))


def _rms(x, g):
    return x * lax.rsqrt(jnp.mean(x * x, axis=-1, keepdims=True) + NORM_EPS) * g


def _dot(a, b):
    return jnp.dot(a, b, preferred_element_type=F32)


def _dot_nt(a, b):
    return lax.dot_general(a, b, (((1,), (1,)), ((), ())), preferred_element_type=F32)


def _dot_tn(a, b):
    return lax.dot_general(a, b, (((0,), (0,)), ((), ())), preferred_element_type=F32)


def _mod_kernel(c_ref, w_ref, b_ref, o_ref):
    s = _silu(c_ref[...]).astype(BF16)
    o_ref[...] = _dot(s, w_ref[...].astype(BF16)) + b_ref[...]


def _modulation(c, ada_w, ada_b):
    n, d = c.shape
    n6 = ada_w.shape[1]
    tn = n6 // 4
    return pl.pallas_call(
        _mod_kernel,
        grid=(n6 // tn,),
        in_specs=[pl.BlockSpec((n, d), lambda j: (0, 0)),
                  pl.BlockSpec((d, tn), lambda j: (0, j)),
                  pl.BlockSpec((1, tn), lambda j: (0, j))],
        out_specs=pl.BlockSpec((n, tn), lambda j: (0, j)),
        out_shape=jax.ShapeDtypeStruct((n, n6), F32),
        compiler_params=_cparams("arbitrary"),
        name="modulation",
    )(c, ada_w, ada_b.reshape(1, n6))


TN_IN = 512
ATT_TILE0 = 10


def _rope_tile(z, c, s1, s2):
    reps = z.shape[1] // LANES
    cc = jnp.concatenate([c] * reps, axis=1)
    a = jnp.concatenate([s1] * reps, axis=1)
    b = jnp.concatenate([s2] * reps, axis=1)
    n = z.shape[1]
    return z * cc + pltpu.roll(z, n - ROPE_DIM // 2, 1) * a + pltpu.roll(z, ROPE_DIM // 2, 1) * b


def _inproj_kernel(x_ref, g_ref, sc_ref, sh_ref, w_ref, walr_ref, w2_ref, ab_ref, c_ref, s1_ref, s2_ref,
                   z_ref, la_ref, *rest, n_main, rope_tiles, split_tiles):
    split_refs, (h_scr, zt_scr) = rest[:-2], rest[-2:]
    j = pl.program_id(1)
    tm = x_ref.shape[0]

    @pl.when(j == 0)
    def _():
        h = _rms(x_ref[...], g_ref[...]) * (1.0 + sc_ref[...]) + sh_ref[...]
        hb = h.astype(BF16)
        h_scr[...] = hb
        alr = _dot(hb, walr_ref[...])
        p = _dot(alr.astype(BF16), w2_ref[...]) + ab_ref[...]
        la_ref[...] = _log_sigmoid(p) * (1.0 / GLA_TAU)

    z = _dot(h_scr[...], w_ref[...])
    is_rope = functools.reduce(jnp.logical_or, [j == t for t in rope_tiles])
    is_main = j < n_main

    @pl.when(jnp.logical_and(is_main, is_rope))
    def _():
        z_ref[...] = _rope_tile(z, c_ref[...], s1_ref[...], s2_ref[...])

    @pl.when(jnp.logical_and(is_main, jnp.logical_not(is_rope)))
    def _():
        z_ref[...] = z

    for t, slot, dil in split_tiles:
        @pl.when(j == t)
        def _(t=t, slot=slot, dil=dil):
            zr = _rope_tile(z, c_ref[...], s1_ref[...], s2_ref[...]) if t in rope_tiles else z
            for c in range(TN_IN // LANES):
                cs = slice(c * LANES, (c + 1) * LANES)
                zt_scr[c] = zr[:, cs]
                for r in range(dil):
                    split_refs[slot][r, :, cs] = zt_scr[c, pl.ds(r, tm // dil, stride=dil), :]


def _in_proj(x2, g, sc, sh, w_main, w_alr, w2p, a_b, tabs, tm, rows_per_mod, n_main, rope_tiles, split_tiles):
    t, d = x2.shape
    r = sc.shape[1]
    wq = a_b.shape[1]
    n_tiles = w_main.shape[1] // TN_IN
    tiles_per_mod = rows_per_mod // tm
    tab_tiles = tabs[0].shape[0] // tm
    nmod = t // rows_per_mod
    mod_spec = pl.BlockSpec((None, r, d), lambda i, j: (i // tiles_per_mod, 0, 0))
    tab_spec = pl.BlockSpec((tm, LANES), lambda i, j: (i % tab_tiles, 0))
    const = lambda shape: pl.BlockSpec(shape, lambda i, j: (0, 0))
    out_specs = [pl.BlockSpec((tm, TN_IN), lambda i, j: (i, jnp.minimum(j, n_main - 1))),
                 pl.BlockSpec((tm, wq), lambda i, j: (i, 0))]
    out_shape = [jax.ShapeDtypeStruct((t, n_main * TN_IN), F32), jax.ShapeDtypeStruct((t, wq), F32)]
    for _, _, dil in split_tiles:
        out_specs.append(pl.BlockSpec((None, dil, tm // dil, TN_IN),
                                      lambda i, j: (i // tiles_per_mod, 0, i % tiles_per_mod, 0)))
        out_shape.append(jax.ShapeDtypeStruct((nmod, dil, rows_per_mod // dil, TN_IN), F32))
    return pl.pallas_call(
        functools.partial(_inproj_kernel, n_main=n_main, rope_tiles=tuple(rope_tiles),
                          split_tiles=tuple(split_tiles)),
        grid=(t // tm, n_tiles),
        in_specs=[pl.BlockSpec((tm, d), lambda i, j: (i, 0)), const((1, d)), mod_spec, mod_spec,
                  pl.BlockSpec((d, TN_IN), lambda i, j: (0, j)), const((d, LANES)), const((LANES, wq)),
                  const((1, wq)), tab_spec, tab_spec, tab_spec],
        out_specs=out_specs,
        out_shape=out_shape,
        scratch_shapes=[pltpu.VMEM((tm, d), BF16), pltpu.VMEM((TN_IN // LANES, tm, LANES), F32)],
        compiler_params=_cparams("arbitrary", "arbitrary"),
        name="in_proj",
    )(x2, g, sc, sh, w_main, w_alr, w2p, a_b, *tabs)


GLA_EXP_CLAMP = 80.0


def _gla_kernel(q_ref, k_ref, v_ref, ra_ref, la_ref, s0_ref, gn_ref, o_ref, sfin_ref, st_scr, *, rows, dk, dv):
    c = pl.program_id(1)
    nc = pl.num_programs(1)
    ch = GLA_CHUNK

    @pl.when(c == 0)
    def _():
        for h in range(GLA_HEADS):
            st_scr[h] = s0_ref[h].T

    ti = lax.broadcasted_iota(jnp.int32, (ch, ch), 0)
    si = lax.broadcasted_iota(jnp.int32, (ch, ch), 1)
    causal = si <= ti
    tri = causal.astype(F32)
    scale = dk ** -0.5

    def chunk(ci, carry):
        r0 = pl.multiple_of(ci * ch, ch)
        for h in range(GLA_HEADS):
            la = la_ref[pl.ds(r0, ch), h * dk:(h + 1) * dk]
            b = jnp.dot(tri, la, preferred_element_type=F32, precision=lax.Precision.HIGHEST)
            bl = b[ch - 1:ch, :]
            q = q_ref[pl.ds(r0, ch), h * dk:(h + 1) * dk] * scale
            k = k_ref[pl.ds(r0, ch), h * dk:(h + 1) * dk]
            v = v_ref[pl.ds(r0, ch), h * dv:(h + 1) * dv].astype(BF16)
            qe = (q * jnp.exp(b)).astype(BF16)
            ke = (k * jnp.exp(jnp.minimum(-b, GLA_EXP_CLAMP))).astype(BF16)
            kh = (k * jnp.exp(bl - b)).astype(BF16)
            att = jnp.where(causal, _dot_nt(qe, ke), 0.0)
            st = st_scr[h]
            o = _dot_nt(qe, st.astype(BF16)) + _dot(att.astype(BF16), v)
            st_scr[h] = st * jnp.exp(bl) + _dot_tn(v, kh)
            on = _rms(o, gn_ref[:, h * dv:(h + 1) * dv])
            gate = _silu(ra_ref[pl.ds(r0, ch), h * dv:(h + 1) * dv])
            o_ref[pl.ds(r0, ch), h * dv:(h + 1) * dv] = (on * gate).astype(o_ref.dtype)
        return carry

    lax.fori_loop(0, rows // ch, chunk, 0)

    @pl.when(c == nc - 1)
    def _():
        for h in range(GLA_HEADS):
            sfin_ref[h] = st_scr[h].T


def _gla_prompt(z, la, s0, gn_g, n, l, rows):
    t = n * l
    dk = la.shape[1] // GLA_HEADS
    dv = gn_g.shape[1] // GLA_HEADS
    wk, wv = GLA_HEADS * dk, GLA_HEADS * dv
    cpb = l // rows
    row = lambda b, c: b * cpb + c
    return pl.pallas_call(
        functools.partial(_gla_kernel, rows=rows, dk=dk, dv=dv),
        grid=(n, cpb),
        in_specs=[pl.BlockSpec((rows, wk), lambda b, c: (row(b, c), 0)),
                  pl.BlockSpec((rows, wk), lambda b, c: (row(b, c), 1)),
                  pl.BlockSpec((rows, wv), lambda b, c: (row(b, c), 1)),
                  pl.BlockSpec((rows, wv), lambda b, c: (row(b, c), 2)),
                  pl.BlockSpec((rows, wk), lambda b, c: (row(b, c), 0)),
                  pl.BlockSpec((None, GLA_HEADS, dk, dv), lambda b, c: (b, 0, 0, 0)),
                  pl.BlockSpec((1, wv), lambda b, c: (0, 0))],
        out_specs=[pl.BlockSpec((rows, wv), lambda b, c: (row(b, c), 0)),
                   pl.BlockSpec((None, GLA_HEADS, dk, dv), lambda b, c: (b, 0, 0, 0))],
        out_shape=[jax.ShapeDtypeStruct((t, wv), BF16), jax.ShapeDtypeStruct((n, GLA_HEADS, dk, dv), F32)],
        scratch_shapes=[pltpu.VMEM((GLA_HEADS, dv, dk), F32)],
        compiler_params=_cparams("arbitrary", "arbitrary"),
        name="gla_prompt",
    )(z, z, z, z, la, s0, gn_g)


def _gla_step_kernel(q_ref, k_ref, la_ref, v_ref, ra_ref, s_ref, gn_ref, o_ref, so_ref, *, dk):
    sn = jnp.exp(la_ref[...]) * s_ref[...] + k_ref[...] * v_ref[...]
    so_ref[...] = sn
    o = jnp.sum((q_ref[...] * dk ** -0.5) * sn, axis=1, keepdims=True)
    on = _rms(o, gn_ref[...])
    o_ref[...] = (on * _silu(ra_ref[...])).astype(o_ref.dtype)


def _gla_step(q, k, la, v, ra, s, gn_g):
    nb, nh, dk, dv = s.shape
    col = pl.BlockSpec((None, nh, dk, 1), lambda b: (b, 0, 0, 0))
    rowv = pl.BlockSpec((None, nh, 1, dv), lambda b: (b, 0, 0, 0))
    st = pl.BlockSpec((None, nh, dk, dv), lambda b: (b, 0, 0, 0))
    o, so = pl.pallas_call(
        functools.partial(_gla_step_kernel, dk=dk),
        grid=(nb,),
        in_specs=[col, col, col, rowv, rowv, st, pl.BlockSpec((nh, 1, dv), lambda b: (0, 0, 0))],
        out_specs=[rowv, st],
        out_shape=[jax.ShapeDtypeStruct((nb, nh, 1, dv), BF16), jax.ShapeDtypeStruct(s.shape, F32)],
        compiler_params=_cparams("arbitrary"),
        name="gla_step",
    )(q.reshape(nb, nh, dk, 1), k.reshape(nb, nh, dk, 1), la.reshape(nb, nh, dk, 1),
      v.reshape(nb, nh, 1, dv), ra.reshape(nb, nh, 1, dv), s, gn_g.reshape(nh, 1, dv))
    return o.reshape(nb, nh * dv), so


def _band_kernel(q_ref, k_ref, kp_ref, v_ref, vp_ref, o_ref, lse_ref, *, dil, qb):
    i = pl.program_id(1)
    blk = ATT_BLOCK
    qi = lax.broadcasted_iota(jnp.int32, (blk, 2 * blk), 0)
    ki = lax.broadcasted_iota(jnp.int32, (blk, 2 * blk), 1)
    dist = blk + qi - ki
    band = jnp.logical_and(dist >= 0, dist <= ATT_BLOCK)
    band_first = jnp.logical_and(band, ki >= jnp.where(i > 0, 0, blk))
    scale = ATT_HD ** -0.5
    for r in range(dil):
        for sb in range(qb // blk):
            rs = slice(sb * blk, (sb + 1) * blk)
            q = q_ref[r, rs, :].astype(BF16)
            if sb == 0:
                kprev, vprev, mask = kp_ref[r], vp_ref[r], band_first
            else:
                ps = slice((sb - 1) * blk, sb * blk)
                kprev, vprev, mask = k_ref[r, ps, :], v_ref[r, ps, :], band
            kk = jnp.concatenate([kprev, k_ref[r, rs, :]], axis=0).astype(BF16)
            vv = jnp.concatenate([vprev, v_ref[r, rs, :]], axis=0).astype(BF16)
            s = jnp.where(mask, _dot_nt(q, kk) * scale, NEG_BIG)
            m = jnp.max(s, axis=-1, keepdims=True)
            p = jnp.exp(s - m)
            lsum = jnp.sum(p, axis=-1, keepdims=True)
            o = _dot(p.astype(BF16), vv) / lsum
            lse = jnp.broadcast_to(m + jnp.log(lsum), (blk, ATT_HD))
            rows = rs if dil == 1 else pl.ds(sb * blk * dil + r, blk, stride=dil)
            o_ref[rows, :] = o
            lse_ref[rows, :] = lse


def _band_attention(qa, ka, va, cq, ck, cv, n, l, dil):
    lq = l // dil
    hd = ATT_HD
    tokens = min(ATT_TOKENS, l)
    qb = tokens // dil
    nqb = lq // qb
    sub = qb // ATT_BLOCK

    def main(c0):
        return pl.BlockSpec((None, dil, qb, hd), lambda b, i, h: (b, 0, i, c0 + h))

    def prev(c0):
        return pl.BlockSpec((None, dil, ATT_BLOCK, hd), lambda b, i, h: (b, 0, jnp.maximum(i * sub - 1, 0), c0 + h))

    out = pl.BlockSpec((tokens, hd), lambda b, i, h: (b * nqb + i, h))
    shp = jax.ShapeDtypeStruct((n * l, ATT_HEADS * hd), F32)
    return pl.pallas_call(
        functools.partial(_band_kernel, dil=dil, qb=qb),
        grid=(n, nqb, ATT_HEADS),
        in_specs=[main(cq), main(ck), prev(ck), main(cv), prev(cv)],
        out_specs=[out, out],
        out_shape=[shp, shp],
        compiler_params=_cparams("arbitrary", "arbitrary", "arbitrary"),
        name=f"band_attention_d{dil}",
    )(qa, ka, ka, va, va)


def _att_step_kernel(q_ref, kn_ref, vn_ref, c0_ref, c1_ref, c2_ref, o_ref):
    scale = ATT_HD ** -0.5
    outs, lses = [], []
    for g, c_ref in enumerate((c0_ref, c1_ref, c2_ref)):
        q, kn, vn = q_ref[g], kn_ref[g], vn_ref[g]
        s = jnp.sum(c_ref[:, 0] * q[None], axis=-1, keepdims=True) * scale
        sn = jnp.sum(kn * q, axis=-1, keepdims=True) * scale
        m = jnp.maximum(jnp.max(s, axis=0), sn)
        p = jnp.exp(s - m[None])
        pn = jnp.exp(sn - m)
        lsum = jnp.sum(p, axis=0) + pn
        outs.append((jnp.sum(p * c_ref[:, 1], axis=0) + pn * vn) / lsum)
        lses.append(m + jnp.log(lsum))
    mx = functools.reduce(jnp.maximum, lses)
    es = [jnp.exp(x - mx) for x in lses]
    inv = 1.0 / functools.reduce(lambda a, b: a + b, es)
    o_ref[...] = functools.reduce(lambda a, b: a + b, [e * inv * o for e, o in zip(es, outs)]).astype(o_ref.dtype)


def _att_step(qs, ks, vs, caches):
    nb = qs.shape[0]
    views, cspecs = [], []
    for (window, dil), cch in zip(ATT_GROUPS, caches):
        assert cch.shape[1] == window, "sample step expects full caches"
        views.append(cch.reshape((nb, window // dil, dil) + cch.shape[2:]))
        cspecs.append(pl.BlockSpec((None, ATT_BLOCK, None) + cch.shape[2:], lambda b: (b, 0, 0, 0, 0, 0)))
    row = pl.BlockSpec((None,) + qs.shape[1:], lambda b: (b, 0, 0, 0))
    o = pl.pallas_call(
        _att_step_kernel,
        grid=(nb,),
        in_specs=[row] * 3 + cspecs,
        out_specs=pl.BlockSpec((None, ATT_HEADS, ATT_HD), lambda b: (b, 0, 0)),
        out_shape=jax.ShapeDtypeStruct((nb, ATT_HEADS, ATT_HD), BF16),
        compiler_params=_cparams("arbitrary"),
        name="att_step",
    )(qs, ks, vs, *views)
    return o.reshape(nb, ATT_HEADS * ATT_HD)


def _kv_shift_kernel(a_ref, nxt_ref, new_ref, o_ref, *, rows):
    i = pl.program_id(1)
    last = pl.num_programs(1) - 1
    o_ref[0:rows - 1] = a_ref[1:rows]

    @pl.when(i == last)
    def _():
        o_ref[rows - 1:rows] = new_ref[...]

    @pl.when(i != last)
    def _():
        o_ref[rows - 1:rows] = nxt_ref[...]


def _kv_shift(cache, new_row):
    nb, wb = cache.shape[0], cache.shape[1]
    rw = cache.shape[2] * cache.shape[3] * cache.shape[4]
    sub = rw // LANES
    rows = min(wb, 512)
    cv = cache.reshape(nb, wb, sub, LANES)
    out = pl.pallas_call(
        functools.partial(_kv_shift_kernel, rows=rows),
        grid=(nb, wb // rows),
        in_specs=[pl.BlockSpec((None, rows, sub, LANES), lambda b, i: (b, i, 0, 0)),
                  pl.BlockSpec((None, 1, sub, LANES), lambda b, i: (b, jnp.minimum((i + 1) * rows, wb - 1), 0, 0)),
                  pl.BlockSpec((None, 1, sub, LANES), lambda b, i: (b, 0, 0, 0))],
        out_specs=pl.BlockSpec((None, rows, sub, LANES), lambda b, i: (b, i, 0, 0)),
        out_shape=jax.ShapeDtypeStruct(cv.shape, cache.dtype),
        compiler_params=_cparams("arbitrary", "arbitrary"),
        name=f"kv_shift_w{wb}",
    )(cv, cv, new_row.reshape(nb, 1, sub, LANES))
    return out.reshape(cache.shape)


def _merge_kernel(*refs, n_att, n_alias):
    oa_ref = refs[0]
    att = refs[1:1 + n_att]
    (gza_ref, gzb_ref, x_ref, g1_ref, sc2_ref, sh2_ref, n2_ref, wa_ref, wb_ref, wo_ref,
     wq_ref) = refs[1 + n_att:12 + n_att]
    x1_ref, h2_ref, pq_ref = refs[12 + n_att + n_alias:]
    tm = x_ref.shape[0]
    if n_att == 1:
        ob = att[0][...]
    else:
        ng = n_att // 2
        ls = [r[...] for r in att[ng:]]
        mx = functools.reduce(jnp.maximum, ls)
        es = [jnp.exp(x - mx) for x in ls]
        inv = 1.0 / functools.reduce(lambda a, b: a + b, es)
        ob = functools.reduce(lambda a, b: a + b, [e * inv * r[...] for e, r in zip(es, att[:ng])]).astype(BF16)
    ma = _dot(oa_ref[...], wa_ref[...])
    mb = _dot(ob, wb_ref[...])
    merged = _sigmoid(gza_ref[...]) * ma + _sigmoid(gzb_ref[...]) * mb
    x1 = x_ref[...] + g1_ref[...] * _dot(merged.astype(BF16), wo_ref[...])
    x1_ref[...] = x1
    h2 = (_rms(x1, n2_ref[...]) * (1.0 + sc2_ref[...]) + sh2_ref[...]).astype(BF16)
    h2_ref[0:tm] = h2
    pq_ref[0:tm] = _dot(h2, wq_ref[...])
    if h2_ref.shape[0] > tm:
        h2_ref[tm:] = jnp.zeros((h2_ref.shape[0] - tm, h2_ref.shape[1]), h2_ref.dtype)
        pq_ref[tm:] = jnp.zeros((pq_ref.shape[0] - tm, pq_ref.shape[1]), pq_ref.dtype)


def _merge(oa, att, z, x2, g1, sc2, sh2, n2g, wa, wb, wo, wq, tm, rows_per_mod, gza_blk, gzb_blk,
           peer_rows, peer_tile, peer_blk0, alias=()):
    t, d = x2.shape
    r = g1.shape[1]
    wqn = wq.shape[1]
    tiles_per_mod = rows_per_mod // tm
    tok = lambda w: pl.BlockSpec((tm, w), lambda i: (i, 0))
    mod = pl.BlockSpec((None, r, d), lambda i: (i // tiles_per_mod, 0, 0))
    const = lambda a: pl.BlockSpec(a.shape, lambda i: (0, 0))
    peer = lambda w: pl.BlockSpec((peer_tile, w), lambda i: (peer_blk0 + i, 0))
    n_in = 12 + len(att)
    return pl.pallas_call(
        functools.partial(_merge_kernel, n_att=len(att), n_alias=len(alias)),
        grid=(t // tm,),
        in_specs=[tok(oa.shape[1])] + [tok(a.shape[1]) for a in att]
                 + [pl.BlockSpec((tm, d), lambda i: (i, gza_blk)), pl.BlockSpec((tm, d), lambda i: (i, gzb_blk)),
                    tok(d), mod, mod, mod, const(n2g), const(wa), const(wb), const(wo), const(wq)]
                 + [pl.BlockSpec(memory_space=pl.ANY)] * len(alias),
        out_specs=[tok(d), peer(d), peer(wqn)],
        out_shape=[jax.ShapeDtypeStruct((t, d), F32), jax.ShapeDtypeStruct((peer_rows, d), BF16),
                   jax.ShapeDtypeStruct((peer_rows, wqn), F32)],
        input_output_aliases={n_in + k: 1 + k for k in range(len(alias))},
        compiler_params=_cparams("arbitrary"),
        name="merge",
    )(oa, *att, z, z, x2, g1, sc2, sh2, n2g, wa, wb, wo, wq, *alias)


def _strict_max_below(rows, prev):
    m = None
    for x in rows:
        y = jnp.where(x < prev, x, NEG_BIG) if prev is not None else x
        m = y if m is None else jnp.maximum(m, y)
    return m


def _topk_desc(s, k):
    vals = []
    prev = None
    for _ in range(k):
        y = s if prev is None else jnp.where(s < prev, s, NEG_BIG)
        prev = jnp.max(y, axis=0, keepdims=True)
        vals.append(prev)
    return vals


def _route_kernel(pq_ref, k1_ref, k2_ref, s1_ref, s2_ref, aux_ref):
    nk = PEER_NKEYS
    pq = pq_ref[...].astype(BF16)
    s1 = _dot_nt(k1_ref[...], pq)
    s2 = _dot_nt(k2_ref[...], pq)
    v1h, v2h = [], []
    for h in range(PEER_HEADS):
        a = s1[h * nk:(h + 1) * nk]
        b = s2[h * nk:(h + 1) * nk]
        s1_ref[h] = a
        s2_ref[h] = b
        v1h.append(_topk_desc(a, PEER_TOPK))
        v2h.append(_topk_desc(b, PEER_TOPK))
    v1 = [jnp.concatenate([v1h[h][k] for h in range(PEER_HEADS)], axis=0) for k in range(PEER_TOPK)]
    v2 = [jnp.concatenate([v2h[h][k] for h in range(PEER_HEADS)], axis=0) for k in range(PEER_TOPK)]
    cands = [v1[i] + v2[j] for i in range(PEER_TOPK) for j in range(PEER_TOPK)
             if (i + 1) * (j + 1) <= PEER_TOPK]
    tau = None
    for _ in range(PEER_TOPK):
        tau = _strict_max_below(cands, tau)
    cmax = v1[0] + v2[0]
    zsum = None
    for cnd in cands:
        e = jnp.where(cnd >= tau, jnp.exp(cnd - cmax), 0.0)
        zsum = e if zsum is None else zsum + e
    aux_ref[0] = tau
    aux_ref[1] = v1[0]
    aux_ref[2] = v2[0]
    aux_ref[3] = 1.0 / zsum


def _peer_route(pq, k1big, k2big, tt):
    t, w = pq.shape
    nh, nk = PEER_HEADS, PEER_NKEYS
    sspec = pl.BlockSpec((nh, nk, tt), lambda i: (0, 0, i))
    return pl.pallas_call(
        _route_kernel,
        grid=(t // tt,),
        in_specs=[pl.BlockSpec((tt, w), lambda i: (i, 0)),
                  pl.BlockSpec(k1big.shape, lambda i: (0, 0)), pl.BlockSpec(k2big.shape, lambda i: (0, 0))],
        out_specs=[sspec, sspec, pl.BlockSpec((4, nh, tt), lambda i: (0, 0, i))],
        out_shape=[jax.ShapeDtypeStruct((nh, nk, t), F32), jax.ShapeDtypeStruct((nh, nk, t), F32),
                   jax.ShapeDtypeStruct((4, nh, t), F32)],
        compiler_params=_cparams("arbitrary"),
        name="peer_route",
    )(pq, k1big, k2big)


def _gelu_tanh(x):
    return 0.5 * x * (1.0 + jnp.tanh(GELU_C * (x + 0.044715 * (x * x * x))))


def _peer_dense_kernel(h2_ref, u_ref, vt_ref, s1_ref, s2_ref, aux_ref, o_ref, e2_scr, acc_scr, wg_scr, *, et):
    e = pl.program_id(1)
    nk = PEER_NKEYS

    @pl.when(e == 0)
    def _():
        for h in range(PEER_HEADS):
            e2_scr[h] = jnp.exp(s2_ref[h] - aux_ref[2, h:h + 1, :])
        acc_scr[...] = jnp.zeros_like(acc_scr)

    st = _dot_nt(u_ref[...], h2_ref[...])
    tau = aux_ref[0]
    m1 = aux_ref[1]
    zinv = aux_ref[3]
    for ii in range(et // nk):
        w = None
        for h in range(PEER_HEADS):
            s1row = s1_ref[h, ii:ii + 1, :]
            e1 = jnp.exp(s1row - m1[h:h + 1, :]) * zinv[h:h + 1, :]
            sel = (s1row + s2_ref[h]) >= tau[h:h + 1, :]
            c = jnp.where(sel, e2_scr[h], 0.0) * e1
            w = c if w is None else w + c
        wg_scr[ii * nk:(ii + 1) * nk, :] = (w * _gelu_tanh(st[ii * nk:(ii + 1) * nk, :])).astype(BF16)
    acc_scr[...] += _dot(vt_ref[...], wg_scr[...])

    @pl.when(e == pl.num_programs(1) - 1)
    def _():
        o_ref[...] = acc_scr[...].T


def _peer_dense(h2, u_b, vt_b, s1, s2, aux, tt, et):
    t, d = h2.shape
    ne = u_b.shape[0]
    nh, nk = PEER_HEADS, PEER_NKEYS
    return pl.pallas_call(
        functools.partial(_peer_dense_kernel, et=et),
        grid=(t // tt, ne // et),
        in_specs=[pl.BlockSpec((tt, d), lambda i, e: (i, 0)),
                  pl.BlockSpec((et, d), lambda i, e: (e, 0)),
                  pl.BlockSpec((d, et), lambda i, e: (0, e)),
                  pl.BlockSpec((nh, et // nk, tt), lambda i, e: (0, e, i)),
                  pl.BlockSpec((nh, nk, tt), lambda i, e: (0, 0, i)),
                  pl.BlockSpec((4, nh, tt), lambda i, e: (0, 0, i))],
        out_specs=pl.BlockSpec((tt, d), lambda i, e: (i, 0)),
        out_shape=jax.ShapeDtypeStruct((t, d), F32),
        scratch_shapes=[pltpu.VMEM((nh, nk, tt), F32), pltpu.VMEM((d, tt), F32), pltpu.VMEM((et, tt), BF16)],
        compiler_params=_cparams("arbitrary", "arbitrary"),
        name="peer_dense",
    )(h2, u_b, vt_b, s1, s2, aux)


def _final_kernel(x1_ref, p_ref, g2_ref, fg_ref, y_ref):
    y_ref[...] = _rms(x1_ref[...] + g2_ref[...] * p_ref[...], fg_ref[...])


def _final(x1, p, g2, fg, tm, rows_per_mod, p_blk0):
    t, d = x1.shape
    r = g2.shape[1]
    tiles_per_mod = rows_per_mod // tm
    tok = pl.BlockSpec((tm, d), lambda i: (i, 0))
    return pl.pallas_call(
        _final_kernel,
        grid=(t // tm,),
        in_specs=[tok, pl.BlockSpec((tm, d), lambda i: (p_blk0 + i, 0)),
                  pl.BlockSpec((None, r, d), lambda i: (i // tiles_per_mod, 0, 0)),
                  pl.BlockSpec((1, d), lambda i: (0, 0))],
        out_specs=tok,
        out_shape=jax.ShapeDtypeStruct((t, d), F32),
        compiler_params=_cparams("arbitrary"),
        name="final_norm",
    )(x1, p, g2, fg)


def _rope_tables(pos):
    half = ROPE_DIM // 2
    inv_freq = ROPE_THETA ** (-jnp.arange(half, dtype=F32) / half)
    ang = pos.astype(F32)[:, None] * inv_freq[None, :]
    cos, sin = jnp.cos(ang), jnp.sin(ang)
    n = pos.shape[0]
    ones = jnp.ones((n, LANES - ROPE_DIM), F32)
    zeros = jnp.zeros((n, LANES - ROPE_DIM), F32)
    zh = jnp.zeros((n, half), F32)
    return (jnp.concatenate([cos, cos, ones], axis=1),
            jnp.concatenate([-sin, zh, zeros], axis=1),
            jnp.concatenate([zh, sin, zeros], axis=1))


def _block_diag_keys(keys, half):
    nk, dh = keys.shape
    eye = jnp.eye(PEER_HEADS, dtype=keys.dtype)
    blk = jnp.zeros((PEER_HEADS, nk, PEER_HEADS, 2, dh), keys.dtype)
    blk = blk.at[:, :, :, half, :].set(eye[:, None, :, None] * keys[None, :, None, :])
    return blk.reshape(PEER_HEADS * nk, PEER_HEADS * 2 * dh).astype(BF16)


def _pick_tile(n, pref):
    t = min(pref, n)
    while n % t:
        t //= 2
    return t


def kernel(x_prompt, x_sample, c_prompt, c_sample, state_gla, cache_kv_w128, cache_kv_w512, cache_kv_w2048,
           ada_w, ada_b, norm1_g, w_in, gla_a_w2, gla_a_b, gla_gn_g, w_branch_a, w_branch_b, w_out,
           norm2_g, peer_wq, peer_k1, peer_k2, peer_u, peer_v, final_g):
    depth = ada_w.shape[0]
    assert depth == 1, "single-layer trunk"
    n_p, l_p, d = x_prompt.shape
    n_s, l_s, _ = x_sample.shape
    assert l_s == 1, "sample group decodes one token per sequence"
    caches = (cache_kv_w128[0], cache_kv_w512[0], cache_kv_w2048[0])
    n_groups = len(ATT_GROUPS)

    w_qa = gla_a_w2.shape[2]
    w_va = gla_gn_g.shape[1]
    w_ob = ATT_HEADS * ATT_HD
    w_qb = n_groups * w_ob
    assert w_qa == TN_IN and w_va == 2 * TN_IN and d == 2 * TN_IN and w_ob == TN_IN
    widths = (w_qa, w_qa, w_va, w_va, GLA_RANK, w_qb, w_qb, w_qb, d, d)
    offs = [0]
    for w in widths:
        offs.append(offs[-1] + w)
    seg = lambda i: w_in[0][:, offs[i]:offs[i + 1]]
    att_cols = [seg(k)[:, g * w_ob:(g + 1) * w_ob] for g in range(n_groups) for k in (5, 6, 7)]
    w_main = jnp.concatenate([seg(0), seg(1), seg(2), seg(3), seg(8), seg(9)] + att_cols, axis=1).astype(BF16)
    n_tiles = w_main.shape[1] // TN_IN
    gza_blk, gzb_blk = 3, 4
    w_alr = jnp.pad(seg(4), ((0, 0), (0, LANES - GLA_RANK))).astype(BF16)
    w2p = jnp.pad(gla_a_w2[0], ((0, LANES - GLA_RANK), (0, 0))).astype(BF16)
    a_b = gla_a_b[0].reshape(1, w_qa)
    g1n = norm1_g[0].reshape(1, d)
    gn_g = gla_gn_g[0].reshape(1, w_va)
    n2g = norm2_g[0].reshape(1, d)
    wa, wb = w_branch_a[0].astype(BF16), w_branch_b[0].astype(BF16)
    wo, wq = w_out[0].astype(BF16), peer_wq[0].astype(BF16)
    q_tile = lambda g: ATT_TILE0 + 3 * g
    rope_tiles = [q_tile(g) + k for g in range(n_groups) for k in (0, 1)]

    n_mod = n_p + n_s
    n_pad = -(-n_mod // SUBLANES) * SUBLANES
    c_all = jnp.concatenate([c_prompt, c_sample, jnp.zeros((n_pad - n_mod, d), F32)], axis=0)
    mod = _modulation(c_all, ada_w[0], ada_b[0])
    mods_p = [mod[:n_p, i * d:(i + 1) * d].reshape(n_p, 1, d) for i in range(6)]
    mods_s = [mod[n_p:n_mod, i * d:(i + 1) * d].reshape(1, n_s, d) for i in range(6)]

    t_p = n_p * l_p
    tt = _pick_tile(t_p, 512)
    t_all = t_p + (-(-n_s // tt)) * tt
    assert t_all - t_p == tt

    xp2 = x_prompt.reshape(t_p, d)
    tm_p = _pick_tile(l_p, 512)
    tabs_p = _rope_tables(jnp.arange(l_p, dtype=jnp.int32))
    split = []
    for g, (window, dil) in enumerate(ATT_GROUPS):
        assert window // dil == ATT_BLOCK
        if dil > 1:
            split += [(q_tile(g) + k, len(split) + k, dil) for k in range(3)]
    n_main = q_tile(1) if split else n_tiles
    outs = _in_proj(xp2, g1n, mods_p[1], mods_p[0], w_main, w_alr, w2p, a_b, tabs_p, tm_p, l_p,
                    n_main, rope_tiles, split)
    z_p, la_p, split_p = outs[0], outs[1], outs[2:]
    s0_p = jnp.zeros((n_p,) + state_gla.shape[2:], F32)
    oa_p, sfin_p = _gla_prompt(z_p, la_p, s0_p, gn_g, n_p, l_p, _pick_tile(l_p, 256))
    z4 = z_p.reshape(n_p, 1, l_p, z_p.shape[1])
    att_o, att_l, kv_p = [], [], []
    for g, (window, dil) in enumerate(ATT_GROUPS):
        keep = min(window, l_p)
        if dil > 1:
            qa, ka, va = split_p[3 * (g - 1):3 * g]
            o_g, lse_g = _band_attention(qa, ka, va, 0, 0, 0, n_p, l_p, dil)
            tail = lambda a: a[:, :, l_p // dil - keep // dil:].transpose(0, 2, 1, 3).reshape(
                n_p, keep, ATT_HEADS, ATT_HD)
            kk, vv = tail(ka), tail(va)
        else:
            cpt = TN_IN // ATT_HD
            o_g, lse_g = _band_attention(z4, z4, z4, q_tile(g) * cpt, (q_tile(g) + 1) * cpt, (q_tile(g) + 2) * cpt,
                                         n_p, l_p, dil)
            z3 = z_p.reshape(n_p, l_p, -1)[:, l_p - keep:]
            kk = z3[:, :, (q_tile(g) + 1) * TN_IN:(q_tile(g) + 2) * TN_IN].reshape(n_p, keep, ATT_HEADS, ATT_HD)
            vv = z3[:, :, (q_tile(g) + 2) * TN_IN:(q_tile(g) + 3) * TN_IN].reshape(n_p, keep, ATT_HEADS, ATT_HD)
        att_o.append(o_g)
        att_l.append(lse_g)
        kv_p.append(jnp.stack([kk, vv], axis=2)[None])
    tm_m = _pick_tile(l_p, 256)
    x1_p, h2_all, pq_all = _merge(oa_p, att_o + att_l, z_p, xp2, mods_p[2], mods_p[4], mods_p[3], n2g,
                                  wa, wb, wo, wq, tm_m, l_p, gza_blk, gzb_blk, t_all, tm_m, 0)

    xs2 = x_sample.reshape(n_s, d)
    tabs_s = _rope_tables(jnp.full((n_s,), PAST_LEN, dtype=jnp.int32))
    z_s, la_s = _in_proj(xs2, g1n, mods_s[1], mods_s[0], w_main, w_alr, w2p, a_b, tabs_s, n_s, n_s,
                         n_tiles, rope_tiles, [])
    oa_s, state_s = _gla_step(z_s[:, 0:w_qa], z_s[:, w_qa:2 * w_qa], la_s,
                              z_s[:, 2 * w_qa:2 * w_qa + w_va], z_s[:, 2 * w_qa + w_va:2 * w_qa + 2 * w_va],
                              state_gla[0], gn_g)
    col = lambda tile: z_s[:, tile * TN_IN:(tile + 1) * TN_IN]
    stack = lambda k: jnp.stack([col(q_tile(g) + k) for g in range(n_groups)], axis=1).reshape(
        n_s, n_groups, ATT_HEADS, ATT_HD)
    ob_s = _att_step(stack(0), stack(1), stack(2), caches)
    x1_s, h2_all, pq_all = _merge(oa_s, [ob_s], z_s, xs2, mods_s[2], mods_s[4], mods_s[3], n2g,
                                  wa, wb, wo, wq, n_s, n_s, gza_blk, gzb_blk, t_all, tt, t_p // tt,
                                  alias=(h2_all, pq_all))
    kv_s = []
    for g, cch in enumerate(caches):
        new_row = jnp.concatenate([col(q_tile(g) + 1), col(q_tile(g) + 2)], axis=1)
        kv_s.append(_kv_shift(cch, new_row)[None])

    k1big = _block_diag_keys(peer_k1[0], 0)
    k2big = _block_diag_keys(peer_k2[0], 1)
    s1, s2, aux = _peer_route(pq_all, k1big, k2big, _pick_tile(tt, 256))
    u_b = peer_u[0].astype(BF16)
    vt_b = peer_v[0].astype(BF16).T
    p_all = _peer_dense(h2_all, u_b, vt_b, s1, s2, aux, tt, PEER_NKEYS * SUBLANES)

    fg = final_g.reshape(1, d)
    y_p = _final(x1_p, p_all, mods_p[5], fg, tm_m, l_p, 0)
    y_s = _final(x1_s, p_all, mods_s[5], fg, n_s, n_s, t_p // n_s)

    return (y_p.reshape(n_p, l_p, d), y_s.reshape(n_s, l_s, d), sfin_p[None], state_s[None],
            kv_p[0], kv_s[0], kv_p[1], kv_s[1], kv_p[2], kv_s[2])
```

```python
import functools
import math

import jax
import jax.numpy as jnp
from jax import lax
from jax.experimental import pallas as pl
from jax.experimental.pallas import tpu as pltpu

F32 = jnp.float32
BF16 = jnp.bfloat16

PAST_LEN = 16384
GLA_HEADS = 4
GLA_RANK = 16
GLA_TAU = 16.0
GLA_CHUNK = 64
ATT_GROUPS = ((128, 1), (512, 4), (2048, 16))
ATT_HEADS = 4
ATT_HD = 128
ATT_BLOCK = 128
ATT_TOKENS = 2048
ROPE_DIM = ATT_HD // 4
ROPE_THETA = 500000.0
PEER_HEADS = 8
PEER_NKEYS = 128
PEER_DQ = 128
PEER_TOPK = 16
NORM_EPS = 1e-6
GELU_C = math.sqrt(2.0 / math.pi)

LANES = 128
SUBLANES = 8
VMEM_LIMIT = 56 * 1024 * 1024
NEG_BIG = -1e30


def _cparams(*sem):
    return pltpu.CompilerParams(dimension_semantics=sem, vmem_limit_bytes=VMEM_LIMIT)


def _sigmoid(x):
    return 1.0 / (1.0 + jnp.exp(-x))


def _silu(x):
    return x * _sigmoid(x)


def _log_sigmoid(x):
    return jnp.minimum(x, 0.0) - jnp.log(1.0 + jnp.exp(-jnp.abs(x)))


def _rms(x, g):
    return x * lax.rsqrt(jnp.mean(x * x, axis=-1, keepdims=True) + NORM_EPS) * g


def _dot(a, b):
    return jnp.dot(a, b, preferred_element_type=F32)


def _dot_nt(a, b):
    return lax.dot_general(a, b, (((1,), (1,)), ((), ())), preferred_element_type=F32)


def _dot_tn(a, b):
    return lax.dot_general(a, b, (((0,), (0,)), ((), ())), preferred_element_type=F32)


def _mod_kernel(c_ref, w_ref, b_ref, o_ref):
    s = _silu(c_ref[...]).astype(BF16)
    o_ref[...] = _dot(s, w_ref[...].astype(BF16)) + b_ref[...]


def _modulation(c, ada_w, ada_b):
    n, d = c.shape
    n6 = ada_w.shape[1]
    tn = n6 // 4
    return pl.pallas_call(
        _mod_kernel,
        grid=(n6 // tn,),
        in_specs=[pl.BlockSpec((n, d), lambda j: (0, 0)),
                  pl.BlockSpec((d, tn), lambda j: (0, j)),
                  pl.BlockSpec((1, tn), lambda j: (0, j))],
        out_specs=pl.BlockSpec((n, tn), lambda j: (0, j)),
        out_shape=jax.ShapeDtypeStruct((n, n6), F32),
        compiler_params=_cparams("arbitrary"),
        name="modulation",
    )(c, ada_w, ada_b.reshape(1, n6))


TN_IN = 512
ATT_TILE0 = 10


def _rope_tile(z, c, s1, s2):
    reps = z.shape[1] // LANES
    cc = jnp.concatenate([c] * reps, axis=1)
    a = jnp.concatenate([s1] * reps, axis=1)
    b = jnp.concatenate([s2] * reps, axis=1)
    n = z.shape[1]
    return z * cc + pltpu.roll(z, n - ROPE_DIM // 2, 1) * a + pltpu.roll(z, ROPE_DIM // 2, 1) * b


def _inproj_kernel(x_ref, g_ref, sc_ref, sh_ref, w_ref, walr_ref, w2_ref, ab_ref, c_ref, s1_ref, s2_ref,
                   z_ref, la_ref, *rest, n_main, rope_tiles, split_tiles):
    split_refs, (h_scr, zt_scr) = rest[:-2], rest[-2:]
    j = pl.program_id(1)
    tm = x_ref.shape[0]

    @pl.when(j == 0)
    def _():
        h = _rms(x_ref[...], g_ref[...]) * (1.0 + sc_ref[...]) + sh_ref[...]
        hb = h.astype(BF16)
        h_scr[...] = hb
        alr = _dot(hb, walr_ref[...])
        p = _dot(alr.astype(BF16), w2_ref[...]) + ab_ref[...]
        la_ref[...] = _log_sigmoid(p) * (1.0 / GLA_TAU)

    z = _dot(h_scr[...], w_ref[...])
    is_rope = functools.reduce(jnp.logical_or, [j == t for t in rope_tiles])
    is_main = j < n_main

    @pl.when(jnp.logical_and(is_main, is_rope))
    def _():
        z_ref[...] = _rope_tile(z, c_ref[...], s1_ref[...], s2_ref[...])

    @pl.when(jnp.logical_and(is_main, jnp.logical_not(is_rope)))
    def _():
        z_ref[...] = z

    for t, slot, dil in split_tiles:
        @pl.when(j == t)
        def _(t=t, slot=slot, dil=dil):
            zr = _rope_tile(z, c_ref[...], s1_ref[...], s2_ref[...]) if t in rope_tiles else z
            for c in range(TN_IN // LANES):
                cs = slice(c * LANES, (c + 1) * LANES)
                zt_scr[c] = zr[:, cs]
                for r in range(dil):
                    split_refs[slot][r, :, cs] = zt_scr[c, pl.ds(r, tm // dil, stride=dil), :]


def _in_proj(x2, g, sc, sh, w_main, w_alr, w2p, a_b, tabs, tm, rows_per_mod, n_main, rope_tiles, split_tiles):
    t, d = x2.shape
    r = sc.shape[1]
    wq = a_b.shape[1]
    n_tiles = w_main.shape[1] // TN_IN
    tiles_per_mod = rows_per_mod // tm
    tab_tiles = tabs[0].shape[0] // tm
    nmod = t // rows_per_mod
    mod_spec = pl.BlockSpec((None, r, d), lambda i, j: (i // tiles_per_mod, 0, 0))
    tab_spec = pl.BlockSpec((tm, LANES), lambda i, j: (i % tab_tiles, 0))
    const = lambda shape: pl.BlockSpec(shape, lambda i, j: (0, 0))
    out_specs = [pl.BlockSpec((tm, TN_IN), lambda i, j: (i, jnp.minimum(j, n_main - 1))),
                 pl.BlockSpec((tm, wq), lambda i, j: (i, 0))]
    out_shape = [jax.ShapeDtypeStruct((t, n_main * TN_IN), F32), jax.ShapeDtypeStruct((t, wq), F32)]
    for _, _, dil in split_tiles:
        out_specs.append(pl.BlockSpec((None, dil, tm // dil, TN_IN),
                                      lambda i, j: (i // tiles_per_mod, 0, i % tiles_per_mod, 0)))
        out_shape.append(jax.ShapeDtypeStruct((nmod, dil, rows_per_mod // dil, TN_IN), F32))
    return pl.pallas_call(
        functools.partial(_inproj_kernel, n_main=n_main, rope_tiles=tuple(rope_tiles),
                          split_tiles=tuple(split_tiles)),
        grid=(t // tm, n_tiles),
        in_specs=[pl.BlockSpec((tm, d), lambda i, j: (i, 0)), const((1, d)), mod_spec, mod_spec,
                  pl.BlockSpec((d, TN_IN), lambda i, j: (0, j)), const((d, LANES)), const((LANES, wq)),
                  const((1, wq)), tab_spec, tab_spec, tab_spec],
        out_specs=out_specs,
        out_shape=out_shape,
        scratch_shapes=[pltpu.VMEM((tm, d), BF16), pltpu.VMEM((TN_IN // LANES, tm, LANES), F32)],
        compiler_params=_cparams("arbitrary", "arbitrary"),
        name="in_proj",
    )(x2, g, sc, sh, w_main, w_alr, w2p, a_b, *tabs)


GLA_EXP_CLAMP = 80.0


def _gla_kernel(q_ref, k_ref, v_ref, ra_ref, la_ref, s0_ref, gn_ref, o_ref, sfin_ref, st_scr, *, rows, dk, dv):
    c = pl.program_id(1)
    nc = pl.num_programs(1)
    ch = GLA_CHUNK

    @pl.when(c == 0)
    def _():
        for h in range(GLA_HEADS):
            st_scr[h] = s0_ref[h].T

    ti = lax.broadcasted_iota(jnp.int32, (ch, ch), 0)
    si = lax.broadcasted_iota(jnp.int32, (ch, ch), 1)
    causal = si <= ti
    tri = causal.astype(F32)
    scale = dk ** -0.5

    def chunk(ci, carry):
        r0 = pl.multiple_of(ci * ch, ch)
        for h in range(GLA_HEADS):
            la = la_ref[pl.ds(r0, ch), h * dk:(h + 1) * dk]
            b = jnp.dot(tri, la, preferred_element_type=F32, precision=lax.Precision.HIGHEST)
            bl = b[ch - 1:ch, :]
            q = q_ref[pl.ds(r0, ch), h * dk:(h + 1) * dk] * scale
            k = k_ref[pl.ds(r0, ch), h * dk:(h + 1) * dk]
            v = v_ref[pl.ds(r0, ch), h * dv:(h + 1) * dv].astype(BF16)
            qe = (q * jnp.exp(b)).astype(BF16)
            ke = (k * jnp.exp(jnp.minimum(-b, GLA_EXP_CLAMP))).astype(BF16)
            kh = (k * jnp.exp(bl - b)).astype(BF16)
            att = jnp.where(causal, _dot_nt(qe, ke), 0.0)
            st = st_scr[h]
            o = _dot_nt(qe, st.astype(BF16)) + _dot(att.astype(BF16), v)
            st_scr[h] = st * jnp.exp(bl) + _dot_tn(v, kh)
            on = _rms(o, gn_ref[:, h * dv:(h + 1) * dv])
            gate = _silu(ra_ref[pl.ds(r0, ch), h * dv:(h + 1) * dv])
            o_ref[pl.ds(r0, ch), h * dv:(h + 1) * dv] = (on * gate).astype(o_ref.dtype)
        return carry

    lax.fori_loop(0, rows // ch, chunk, 0)

    @pl.when(c == nc - 1)
    def _():
        for h in range(GLA_HEADS):
            sfin_ref[h] = st_scr[h].T


def _gla_prompt(z, la, s0, gn_g, n, l, rows):
    t = n * l
    dk = la.shape[1] // GLA_HEADS
    dv = gn_g.shape[1] // GLA_HEADS
    wk, wv = GLA_HEADS * dk, GLA_HEADS * dv
    cpb = l // rows
    row = lambda b, c: b * cpb + c
    return pl.pallas_call(
        functools.partial(_gla_kernel, rows=rows, dk=dk, dv=dv),
        grid=(n, cpb),
        in_specs=[pl.BlockSpec((rows, wk), lambda b, c: (row(b, c), 0)),
                  pl.BlockSpec((rows, wk), lambda b, c: (row(b, c), 1)),
                  pl.BlockSpec((rows, wv), lambda b, c: (row(b, c), 1)),
                  pl.BlockSpec((rows, wv), lambda b, c: (row(b, c), 2)),
                  pl.BlockSpec((rows, wk), lambda b, c: (row(b, c), 0)),
                  pl.BlockSpec((None, GLA_HEADS, dk, dv), lambda b, c: (b, 0, 0, 0)),
                  pl.BlockSpec((1, wv), lambda b, c: (0, 0))],
        out_specs=[pl.BlockSpec((rows, wv), lambda b, c: (row(b, c), 0)),
                   pl.BlockSpec((None, GLA_HEADS, dk, dv), lambda b, c: (b, 0, 0, 0))],
        out_shape=[jax.ShapeDtypeStruct((t, wv), BF16), jax.ShapeDtypeStruct((n, GLA_HEADS, dk, dv), F32)],
        scratch_shapes=[pltpu.VMEM((GLA_HEADS, dv, dk), F32)],
        compiler_params=_cparams("arbitrary", "arbitrary"),
        name="gla_prompt",
    )(z, z, z, z, la, s0, gn_g)


def _gla_step_kernel(q_ref, k_ref, la_ref, v_ref, ra_ref, s_ref, gn_ref, o_ref, so_ref, *, dk):
    sn = jnp.exp(la_ref[...]) * s_ref[...] + k_ref[...] * v_ref[...]
    so_ref[...] = sn
    o = jnp.sum((q_ref[...] * dk ** -0.5) * sn, axis=1, keepdims=True)
    on = _rms(o, gn_ref[...])
    o_ref[...] = (on * _silu(ra_ref[...])).astype(o_ref.dtype)


def _gla_step(q, k, la, v, ra, s, gn_g):
    nb, nh, dk, dv = s.shape
    col = pl.BlockSpec((None, nh, dk, 1), lambda b: (b, 0, 0, 0))
    rowv = pl.BlockSpec((None, nh, 1, dv), lambda b: (b, 0, 0, 0))
    st = pl.BlockSpec((None, nh, dk, dv), lambda b: (b, 0, 0, 0))
    o, so = pl.pallas_call(
        functools.partial(_gla_step_kernel, dk=dk),
        grid=(nb,),
        in_specs=[col, col, col, rowv, rowv, st, pl.BlockSpec((nh, 1, dv), lambda b: (0, 0, 0))],
        out_specs=[rowv, st],
        out_shape=[jax.ShapeDtypeStruct((nb, nh, 1, dv), BF16), jax.ShapeDtypeStruct(s.shape, F32)],
        compiler_params=_cparams("arbitrary"),
        name="gla_step",
    )(q.reshape(nb, nh, dk, 1), k.reshape(nb, nh, dk, 1), la.reshape(nb, nh, dk, 1),
      v.reshape(nb, nh, 1, dv), ra.reshape(nb, nh, 1, dv), s, gn_g.reshape(nh, 1, dv))
    return o.reshape(nb, nh * dv), so


def _band_kernel(q_ref, k_ref, kp_ref, v_ref, vp_ref, o_ref, lse_ref, *, dil, qb):
    i = pl.program_id(1)
    blk = ATT_BLOCK
    qi = lax.broadcasted_iota(jnp.int32, (blk, 2 * blk), 0)
    ki = lax.broadcasted_iota(jnp.int32, (blk, 2 * blk), 1)
    dist = blk + qi - ki
    band = jnp.logical_and(dist >= 0, dist <= ATT_BLOCK)
    band_first = jnp.logical_and(band, ki >= jnp.where(i > 0, 0, blk))
    scale = ATT_HD ** -0.5
    for r in range(dil):
        for sb in range(qb // blk):
            rs = slice(sb * blk, (sb + 1) * blk)
            q = q_ref[r, rs, :].astype(BF16)
            if sb == 0:
                kprev, vprev, mask = kp_ref[r], vp_ref[r], band_first
            else:
                ps = slice((sb - 1) * blk, sb * blk)
                kprev, vprev, mask = k_ref[r, ps, :], v_ref[r, ps, :], band
            kk = jnp.concatenate([kprev, k_ref[r, rs, :]], axis=0).astype(BF16)
            vv = jnp.concatenate([vprev, v_ref[r, rs, :]], axis=0).astype(BF16)
            s = jnp.where(mask, _dot_nt(q, kk) * scale, NEG_BIG)
            m = jnp.max(s, axis=-1, keepdims=True)
            p = jnp.exp(s - m)
            lsum = jnp.sum(p, axis=-1, keepdims=True)
            o = _dot(p.astype(BF16), vv) / lsum
            lse = jnp.broadcast_to(m + jnp.log(lsum), (blk, ATT_HD))
            rows = rs if dil == 1 else pl.ds(sb * blk * dil + r, blk, stride=dil)
            o_ref[rows, :] = o
            lse_ref[rows, :] = lse


def _band_attention(qa, ka, va, cq, ck, cv, n, l, dil):
    lq = l // dil
    hd = ATT_HD
    tokens = min(ATT_TOKENS, l)
    qb = tokens // dil
    nqb = lq // qb
    sub = qb // ATT_BLOCK

    def main(c0):
        return pl.BlockSpec((None, dil, qb, hd), lambda b, i, h: (b, 0, i, c0 + h))

    def prev(c0):
        return pl.BlockSpec((None, dil, ATT_BLOCK, hd), lambda b, i, h: (b, 0, jnp.maximum(i * sub - 1, 0), c0 + h))

    out = pl.BlockSpec((tokens, hd), lambda b, i, h: (b * nqb + i, h))
    shp = jax.ShapeDtypeStruct((n * l, ATT_HEADS * hd), F32)
    return pl.pallas_call(
        functools.partial(_band_kernel, dil=dil, qb=qb),
        grid=(n, nqb, ATT_HEADS),
        in_specs=[main(cq), main(ck), prev(ck), main(cv), prev(cv)],
        out_specs=[out, out],
        out_shape=[shp, shp],
        compiler_params=_cparams("arbitrary", "arbitrary", "arbitrary"),
        name=f"band_attention_d{dil}",
    )(qa, ka, ka, va, va)


def _att_step_kernel(q_ref, kn_ref, vn_ref, c0_ref, c1_ref, c2_ref, o_ref):
    scale = ATT_HD ** -0.5
    outs, lses = [], []
    for g, c_ref in enumerate((c0_ref, c1_ref, c2_ref)):
        q, kn, vn = q_ref[g], kn_ref[g], vn_ref[g]
        s = jnp.sum(c_ref[:, 0] * q[None], axis=-1, keepdims=True) * scale
        sn = jnp.sum(kn * q, axis=-1, keepdims=True) * scale
        m = jnp.maximum(jnp.max(s, axis=0), sn)
        p = jnp.exp(s - m[None])
        pn = jnp.exp(sn - m)
        lsum = jnp.sum(p, axis=0) + pn
        outs.append((jnp.sum(p * c_ref[:, 1], axis=0) + pn * vn) / lsum)
        lses.append(m + jnp.log(lsum))
    mx = functools.reduce(jnp.maximum, lses)
    es = [jnp.exp(x - mx) for x in lses]
    inv = 1.0 / functools.reduce(lambda a, b: a + b, es)
    o_ref[...] = functools.reduce(lambda a, b: a + b, [e * inv * o for e, o in zip(es, outs)]).astype(o_ref.dtype)


def _att_step(qs, ks, vs, caches):
    nb = qs.shape[0]
    views, cspecs = [], []
    for (window, dil), cch in zip(ATT_GROUPS, caches):
        assert cch.shape[1] == window, "sample step expects full caches"
        views.append(cch.reshape((nb, window // dil, dil) + cch.shape[2:]))
        cspecs.append(pl.BlockSpec((None, ATT_BLOCK, None) + cch.shape[2:], lambda b: (b, 0, 0, 0, 0, 0)))
    row = pl.BlockSpec((None,) + qs.shape[1:], lambda b: (b, 0, 0, 0))
    o = pl.pallas_call(
        _att_step_kernel,
        grid=(nb,),
        in_specs=[row] * 3 + cspecs,
        out_specs=pl.BlockSpec((None, ATT_HEADS, ATT_HD), lambda b: (b, 0, 0)),
        out_shape=jax.ShapeDtypeStruct((nb, ATT_HEADS, ATT_HD), BF16),
        compiler_params=_cparams("arbitrary"),
        name="att_step",
    )(qs, ks, vs, *views)
    return o.reshape(nb, ATT_HEADS * ATT_HD)


def _kv_shift_kernel(a_ref, nxt_ref, new_ref, o_ref, *, rows):
    i = pl.program_id(1)
    last = pl.num_programs(1) - 1
    o_ref[0:rows - 1] = a_ref[1:rows]

    @pl.when(i == last)
    def _():
        o_ref[rows - 1:rows] = new_ref[...]

    @pl.when(i != last)
    def _():
        o_ref[rows - 1:rows] = nxt_ref[...]


def _kv_shift(cache, new_row):
    nb, wb = cache.shape[0], cache.shape[1]
    rw = cache.shape[2] * cache.shape[3] * cache.shape[4]
    sub = rw // LANES
    rows = min(wb, 512)
    cv = cache.reshape(nb, wb, sub, LANES)
    out = pl.pallas_call(
        functools.partial(_kv_shift_kernel, rows=rows),
        grid=(nb, wb // rows),
        in_specs=[pl.BlockSpec((None, rows, sub, LANES), lambda b, i: (b, i, 0, 0)),
                  pl.BlockSpec((None, 1, sub, LANES), lambda b, i: (b, jnp.minimum((i + 1) * rows, wb - 1), 0, 0)),
                  pl.BlockSpec((None, 1, sub, LANES), lambda b, i: (b, 0, 0, 0))],
        out_specs=pl.BlockSpec((None, rows, sub, LANES), lambda b, i: (b, i, 0, 0)),
        out_shape=jax.ShapeDtypeStruct(cv.shape, cache.dtype),
        compiler_params=_cparams("arbitrary", "arbitrary"),
        name=f"kv_shift_w{wb}",
    )(cv, cv, new_row.reshape(nb, 1, sub, LANES))
    return out.reshape(cache.shape)


def _merge_kernel(*refs, n_att, n_alias):
    oa_ref = refs[0]
    att = refs[1:1 + n_att]
    (gza_ref, gzb_ref, x_ref, g1_ref, sc2_ref, sh2_ref, n2_ref, wa_ref, wb_ref, wo_ref,
     wq_ref) = refs[1 + n_att:12 + n_att]
    x1_ref, h2_ref, pq_ref = refs[12 + n_att + n_alias:]
    tm = x_ref.shape[0]
    if n_att == 1:
        ob = att[0][...]
    else:
        ng = n_att // 2
        ls = [r[...] for r in att[ng:]]
        mx = functools.reduce(jnp.maximum, ls)
        es = [jnp.exp(x - mx) for x in ls]
        inv = 1.0 / functools.reduce(lambda a, b: a + b, es)
        ob = functools.reduce(lambda a, b: a + b, [e * inv * r[...] for e, r in zip(es, att[:ng])]).astype(BF16)
    ma = _dot(oa_ref[...], wa_ref[...])
    mb = _dot(ob, wb_ref[...])
    merged = _sigmoid(gza_ref[...]) * ma + _sigmoid(gzb_ref[...]) * mb
    x1 = x_ref[...] + g1_ref[...] * _dot(merged.astype(BF16), wo_ref[...])
    x1_ref[...] = x1
    h2 = (_rms(x1, n2_ref[...]) * (1.0 + sc2_ref[...]) + sh2_ref[...]).astype(BF16)
    h2_ref[0:tm] = h2
    pq_ref[0:tm] = _dot(h2, wq_ref[...])
    if h2_ref.shape[0] > tm:
        h2_ref[tm:] = jnp.zeros((h2_ref.shape[0] - tm, h2_ref.shape[1]), h2_ref.dtype)
        pq_ref[tm:] = jnp.zeros((pq_ref.shape[0] - tm, pq_ref.shape[1]), pq_ref.dtype)


def _merge(oa, att, z, x2, g1, sc2, sh2, n2g, wa, wb, wo, wq, tm, rows_per_mod, gza_blk, gzb_blk,
           peer_rows, peer_tile, peer_blk0, alias=()):
    t, d = x2.shape
    r = g1.shape[1]
    wqn = wq.shape[1]
    tiles_per_mod = rows_per_mod // tm
    tok = lambda w: pl.BlockSpec((tm, w), lambda i: (i, 0))
    mod = pl.BlockSpec((None, r, d), lambda i: (i // tiles_per_mod, 0, 0))
    const = lambda a: pl.BlockSpec(a.shape, lambda i: (0, 0))
    peer = lambda w: pl.BlockSpec((peer_tile, w), lambda i: (peer_blk0 + i, 0))
    n_in = 12 + len(att)
    return pl.pallas_call(
        functools.partial(_merge_kernel, n_att=len(att), n_alias=len(alias)),
        grid=(t // tm,),
        in_specs=[tok(oa.shape[1])] + [tok(a.shape[1]) for a in att]
                 + [pl.BlockSpec((tm, d), lambda i: (i, gza_blk)), pl.BlockSpec((tm, d), lambda i: (i, gzb_blk)),
                    tok(d), mod, mod, mod, const(n2g), const(wa), const(wb), const(wo), const(wq)]
                 + [pl.BlockSpec(memory_space=pl.ANY)] * len(alias),
        out_specs=[tok(d), peer(d), peer(wqn)],
        out_shape=[jax.ShapeDtypeStruct((t, d), F32), jax.ShapeDtypeStruct((peer_rows, d), BF16),
                   jax.ShapeDtypeStruct((peer_rows, wqn), F32)],
        input_output_aliases={n_in + k: 1 + k for k in range(len(alias))},
        compiler_params=_cparams("arbitrary"),
        name="merge",
    )(oa, *att, z, z, x2, g1, sc2, sh2, n2g, wa, wb, wo, wq, *alias)


def _strict_max_below(rows, prev):
    m = None
    for x in rows:
        y = jnp.where(x < prev, x, NEG_BIG) if prev is not None else x
        m = y if m is None else jnp.maximum(m, y)
    return m


def _topk_desc(s, k):
    vals = []
    prev = None
    for _ in range(k):
        y = s if prev is None else jnp.where(s < prev, s, NEG_BIG)
        prev = jnp.max(y, axis=0, keepdims=True)
        vals.append(prev)
    return vals


def _route_kernel(pq_ref, k1_ref, k2_ref, n1_ref, e1_ref, r2_ref, e2_ref, s1_scr, s2_scr):
    nk = PEER_NKEYS
    pq = pq_ref[...].astype(BF16)
    s1_scr[...] = _dot_nt(k1_ref[...], pq)
    s2_scr[...] = _dot_nt(k2_ref[...], pq)
    v1h, v2h = [], []
    for h in range(PEER_HEADS):
        v1h.append(_topk_desc(s1_scr[h * nk:(h + 1) * nk], PEER_TOPK))
        v2h.append(_topk_desc(s2_scr[h * nk:(h + 1) * nk], PEER_TOPK))
    v1 = [jnp.concatenate([v1h[h][k] for h in range(PEER_HEADS)], axis=0) for k in range(PEER_TOPK)]
    v2 = [jnp.concatenate([v2h[h][k] for h in range(PEER_HEADS)], axis=0) for k in range(PEER_TOPK)]
    pairs = [(i, j) for i in range(PEER_TOPK) for j in range(PEER_TOPK) if (i + 1) * (j + 1) <= PEER_TOPK]
    cands = [v1[i] + v2[j] for i, j in pairs]
    tau = None
    for _ in range(PEER_TOPK):
        tau = _strict_max_below(cands, tau)
    cmax = v1[0] + v2[0]
    zsum = None
    counts = [None] * PEER_TOPK
    for (i, _), cnd in zip(pairs, cands):
        sel = cnd >= tau
        e = jnp.where(sel, jnp.exp(cnd - cmax), 0.0)
        zsum = e if zsum is None else zsum + e
        one = jnp.where(sel, 1.0, 0.0)
        counts[i] = one if counts[i] is None else counts[i] + one
    zinv = 1.0 / zsum
    for h in range(PEER_HEADS):
        a = s1_scr[h * nk:(h + 1) * nk]
        b = s2_scr[h * nk:(h + 1) * nk]
        n1 = jnp.zeros_like(a)
        r2 = jnp.full_like(b, float(PEER_TOPK))
        for k in range(PEER_TOPK):
            n1 = jnp.where(a == v1h[h][k], counts[k][h:h + 1, :], n1)
            r2 = jnp.where(b == v2h[h][k], float(k), r2)
        n1_ref[h] = n1
        e1_ref[h] = jnp.exp(a - v1h[h][0]) * zinv[h:h + 1, :]
        r2_ref[h] = r2.astype(r2_ref.dtype)
        e2_ref[h] = jnp.exp(b - v2h[h][0]).astype(e2_ref.dtype)


def _peer_route(pq, k1big, k2big, tt):
    t, w = pq.shape
    nh, nk = PEER_HEADS, PEER_NKEYS
    sspec = pl.BlockSpec((nh, nk, tt), lambda i: (0, 0, i))
    shp = lambda dt: jax.ShapeDtypeStruct((nh, nk, t), dt)
    return pl.pallas_call(
        _route_kernel,
        grid=(t // tt,),
        in_specs=[pl.BlockSpec((tt, w), lambda i: (i, 0)),
                  pl.BlockSpec(k1big.shape, lambda i: (0, 0)), pl.BlockSpec(k2big.shape, lambda i: (0, 0))],
        out_specs=[sspec] * 4,
        out_shape=[shp(F32), shp(F32), shp(BF16), shp(BF16)],
        scratch_shapes=[pltpu.VMEM((nh * nk, tt), F32)] * 2,
        compiler_params=_cparams("arbitrary"),
        name="peer_route",
    )(pq, k1big, k2big)


def _gelu_tanh(x):
    return 0.5 * x * (1.0 + jnp.tanh(GELU_C * (x + 0.044715 * (x * x * x))))


BF16_ROWS = 16


def _peer_dense_kernel(h2_ref, u_ref, vt_ref, n1_ref, e1_ref, r2_ref, e2_ref, o_ref, acc_scr, wg_scr, *, et):
    e = pl.program_id(1)
    nk = PEER_NKEYS
    tt = h2_ref.shape[0]

    @pl.when(e == 0)
    def _():
        acc_scr[...] = jnp.zeros_like(acc_scr)

    st = _dot_nt(u_ref[...], h2_ref[...])
    zero = jnp.zeros((), BF16)
    for ii in range(et // nk):
        w = None
        for h in range(PEER_HEADS):
            n1 = jnp.broadcast_to(n1_ref[h, ii:ii + 1, :], (BF16_ROWS, tt)).astype(BF16)
            e1 = jnp.broadcast_to(e1_ref[h, ii:ii + 1, :], (BF16_ROWS, tt)).astype(BF16)
            r2 = r2_ref[h].reshape(nk // BF16_ROWS, BF16_ROWS, tt)
            e2 = e2_ref[h].reshape(nk // BF16_ROWS, BF16_ROWS, tt)
            c = jnp.where(r2 < n1[None], e2, zero) * e1[None]
            w = c if w is None else w + c
        g = _gelu_tanh(st[ii * nk:(ii + 1) * nk, :]).astype(BF16)
        wg_scr[ii * nk:(ii + 1) * nk, :] = w.reshape(nk, tt) * g
    acc_scr[...] += _dot(vt_ref[...], wg_scr[...])

    @pl.when(e == pl.num_programs(1) - 1)
    def _():
        o_ref[...] = acc_scr[...].T


def _peer_dense(h2, u_b, vt_b, n1, e1, r2, e2, tt, et):
    t, d = h2.shape
    ne = u_b.shape[0]
    nh, nk = PEER_HEADS, PEER_NKEYS
    rowside = pl.BlockSpec((nh, et // nk, tt), lambda i, e: (0, e, i))
    colside = pl.BlockSpec((nh, nk, tt), lambda i, e: (0, 0, i))
    return pl.pallas_call(
        functools.partial(_peer_dense_kernel, et=et),
        grid=(t // tt, ne // et),
        in_specs=[pl.BlockSpec((tt, d), lambda i, e: (i, 0)),
                  pl.BlockSpec((et, d), lambda i, e: (e, 0)),
                  pl.BlockSpec((d, et), lambda i, e: (0, e)),
                  rowside, rowside, colside, colside],
        out_specs=pl.BlockSpec((tt, d), lambda i, e: (i, 0)),
        out_shape=jax.ShapeDtypeStruct((t, d), F32),
        scratch_shapes=[pltpu.VMEM((d, tt), F32), pltpu.VMEM((et, tt), BF16)],
        compiler_params=_cparams("arbitrary", "arbitrary"),
        name="peer_dense",
    )(h2, u_b, vt_b, n1, e1, r2, e2)


def _final_kernel(x1_ref, p_ref, g2_ref, fg_ref, y_ref):
    y_ref[...] = _rms(x1_ref[...] + g2_ref[...] * p_ref[...], fg_ref[...])


def _final(x1, p, g2, fg, tm, rows_per_mod, p_blk0):
    t, d = x1.shape
    r = g2.shape[1]
    tiles_per_mod = rows_per_mod // tm
    tok = pl.BlockSpec((tm, d), lambda i: (i, 0))
    return pl.pallas_call(
        _final_kernel,
        grid=(t // tm,),
        in_specs=[tok, pl.BlockSpec((tm, d), lambda i: (p_blk0 + i, 0)),
                  pl.BlockSpec((None, r, d), lambda i: (i // tiles_per_mod, 0, 0)),
                  pl.BlockSpec((1, d), lambda i: (0, 0))],
        out_specs=tok,
        out_shape=jax.ShapeDtypeStruct((t, d), F32),
        compiler_params=_cparams("arbitrary"),
        name="final_norm",
    )(x1, p, g2, fg)


def _rope_tables(pos):
    half = ROPE_DIM // 2
    inv_freq = ROPE_THETA ** (-jnp.arange(half, dtype=F32) / half)
    ang = pos.astype(F32)[:, None] * inv_freq[None, :]
    cos, sin = jnp.cos(ang), jnp.sin(ang)
    n = pos.shape[0]
    ones = jnp.ones((n, LANES - ROPE_DIM), F32)
    zeros = jnp.zeros((n, LANES - ROPE_DIM), F32)
    zh = jnp.zeros((n, half), F32)
    return (jnp.concatenate([cos, cos, ones], axis=1),
            jnp.concatenate([-sin, zh, zeros], axis=1),
            jnp.concatenate([zh, sin, zeros], axis=1))


def _block_diag_keys(keys, half):
    nk, dh = keys.shape
    eye = jnp.eye(PEER_HEADS, dtype=keys.dtype)
    blk = jnp.zeros((PEER_HEADS, nk, PEER_HEADS, 2, dh), keys.dtype)
    blk = blk.at[:, :, :, half, :].set(eye[:, None, :, None] * keys[None, :, None, :])
    return blk.reshape(PEER_HEADS * nk, PEER_HEADS * 2 * dh).astype(BF16)


def _pick_tile(n, pref):
    t = min(pref, n)
    while n % t:
        t //= 2
    return t


def kernel(x_prompt, x_sample, c_prompt, c_sample, state_gla, cache_kv_w128, cache_kv_w512, cache_kv_w2048,
           ada_w, ada_b, norm1_g, w_in, gla_a_w2, gla_a_b, gla_gn_g, w_branch_a, w_branch_b, w_out,
           norm2_g, peer_wq, peer_k1, peer_k2, peer_u, peer_v, final_g):
    depth = ada_w.shape[0]
    assert depth == 1, "single-layer trunk"
    n_p, l_p, d = x_prompt.shape
    n_s, l_s, _ = x_sample.shape
    assert l_s == 1, "sample group decodes one token per sequence"
    caches = (cache_kv_w128[0], cache_kv_w512[0], cache_kv_w2048[0])
    n_groups = len(ATT_GROUPS)

    w_qa = gla_a_w2.shape[2]
    w_va = gla_gn_g.shape[1]
    w_ob = ATT_HEADS * ATT_HD
    w_qb = n_groups * w_ob
    assert w_qa == TN_IN and w_va == 2 * TN_IN and d == 2 * TN_IN and w_ob == TN_IN
    widths = (w_qa, w_qa, w_va, w_va, GLA_RANK, w_qb, w_qb, w_qb, d, d)
    offs = [0]
    for w in widths:
        offs.append(offs[-1] + w)
    seg = lambda i: w_in[0][:, offs[i]:offs[i + 1]]
    att_cols = [seg(k)[:, g * w_ob:(g + 1) * w_ob] for g in range(n_groups) for k in (5, 6, 7)]
    w_main = jnp.concatenate([seg(0), seg(1), seg(2), seg(3), seg(8), seg(9)] + att_cols, axis=1).astype(BF16)
    n_tiles = w_main.shape[1] // TN_IN
    gza_blk, gzb_blk = 3, 4
    w_alr = jnp.pad(seg(4), ((0, 0), (0, LANES - GLA_RANK))).astype(BF16)
    w2p = jnp.pad(gla_a_w2[0], ((0, LANES - GLA_RANK), (0, 0))).astype(BF16)
    a_b = gla_a_b[0].reshape(1, w_qa)
    g1n = norm1_g[0].reshape(1, d)
    gn_g = gla_gn_g[0].reshape(1, w_va)
    n2g = norm2_g[0].reshape(1, d)
    wa, wb = w_branch_a[0].astype(BF16), w_branch_b[0].astype(BF16)
    wo, wq = w_out[0].astype(BF16), peer_wq[0].astype(BF16)
    q_tile = lambda g: ATT_TILE0 + 3 * g
    rope_tiles = [q_tile(g) + k for g in range(n_groups) for k in (0, 1)]

    n_mod = n_p + n_s
    n_pad = -(-n_mod // SUBLANES) * SUBLANES
    c_all = jnp.concatenate([c_prompt, c_sample, jnp.zeros((n_pad - n_mod, d), F32)], axis=0)
    mod = _modulation(c_all, ada_w[0], ada_b[0])
    mods_p = [mod[:n_p, i * d:(i + 1) * d].reshape(n_p, 1, d) for i in range(6)]
    mods_s = [mod[n_p:n_mod, i * d:(i + 1) * d].reshape(1, n_s, d) for i in range(6)]

    t_p = n_p * l_p
    tt = _pick_tile(t_p, 512)
    t_all = t_p + (-(-n_s // tt)) * tt
    assert t_all - t_p == tt

    xp2 = x_prompt.reshape(t_p, d)
    tm_p = _pick_tile(l_p, 512)
    tabs_p = _rope_tables(jnp.arange(l_p, dtype=jnp.int32))
    split = []
    for g, (window, dil) in enumerate(ATT_GROUPS):
        assert window // dil == ATT_BLOCK
        if dil > 1:
            split += [(q_tile(g) + k, len(split) + k, dil) for k in range(3)]
    n_main = q_tile(1) if split else n_tiles
    outs = _in_proj(xp2, g1n, mods_p[1], mods_p[0], w_main, w_alr, w2p, a_b, tabs_p, tm_p, l_p,
                    n_main, rope_tiles, split)
    z_p, la_p, split_p = outs[0], outs[1], outs[2:]
    s0_p = jnp.zeros((n_p,) + state_gla.shape[2:], F32)
    oa_p, sfin_p = _gla_prompt(z_p, la_p, s0_p, gn_g, n_p, l_p, _pick_tile(l_p, 256))
    z4 = z_p.reshape(n_p, 1, l_p, z_p.shape[1])
    att_o, att_l, kv_p = [], [], []
    for g, (window, dil) in enumerate(ATT_GROUPS):
        keep = min(window, l_p)
        if dil > 1:
            qa, ka, va = split_p[3 * (g - 1):3 * g]
            o_g, lse_g = _band_attention(qa, ka, va, 0, 0, 0, n_p, l_p, dil)
            tail = lambda a: a[:, :, l_p // dil - keep // dil:].transpose(0, 2, 1, 3).reshape(
                n_p, keep, ATT_HEADS, ATT_HD)
            kk, vv = tail(ka), tail(va)
        else:
            cpt = TN_IN // ATT_HD
            o_g, lse_g = _band_attention(z4, z4, z4, q_tile(g) * cpt, (q_tile(g) + 1) * cpt, (q_tile(g) + 2) * cpt,
                                         n_p, l_p, dil)
            z3 = z_p.reshape(n_p, l_p, -1)[:, l_p - keep:]
            kk = z3[:, :, (q_tile(g) + 1) * TN_IN:(q_tile(g) + 2) * TN_IN].reshape(n_p, keep, ATT_HEADS, ATT_HD)
            vv = z3[:, :, (q_tile(g) + 2) * TN_IN:(q_tile(g) + 3) * TN_IN].reshape(n_p, keep, ATT_HEADS, ATT_HD)
        att_o.append(o_g)
        att_l.append(lse_g)
        kv_p.append(jnp.stack([kk, vv], axis=2)[None])
    tm_m = _pick_tile(l_p, 256)
    x1_p, h2_all, pq_all = _merge(oa_p, att_o + att_l, z_p, xp2, mods_p[2], mods_p[4], mods_p[3], n2g,
                                  wa, wb, wo, wq, tm_m, l_p, gza_blk, gzb_blk, t_all, tm_m, 0)

    xs2 = x_sample.reshape(n_s, d)
    tabs_s = _rope_tables(jnp.full((n_s,), PAST_LEN, dtype=jnp.int32))
    z_s, la_s = _in_proj(xs2, g1n, mods_s[1], mods_s[0], w_main, w_alr, w2p, a_b, tabs_s, n_s, n_s,
                         n_tiles, rope_tiles, [])
    oa_s, state_s = _gla_step(z_s[:, 0:w_qa], z_s[:, w_qa:2 * w_qa], la_s,
                              z_s[:, 2 * w_qa:2 * w_qa + w_va], z_s[:, 2 * w_qa + w_va:2 * w_qa + 2 * w_va],
                              state_gla[0], gn_g)
    col = lambda tile: z_s[:, tile * TN_IN:(tile + 1) * TN_IN]
    stack = lambda k: jnp.stack([col(q_tile(g) + k) for g in range(n_groups)], axis=1).reshape(
        n_s, n_groups, ATT_HEADS, ATT_HD)
    ob_s = _att_step(stack(0), stack(1), stack(2), caches)
    x1_s, h2_all, pq_all = _merge(oa_s, [ob_s], z_s, xs2, mods_s[2], mods_s[4], mods_s[3], n2g,
                                  wa, wb, wo, wq, n_s, n_s, gza_blk, gzb_blk, t_all, tt, t_p // tt,
                                  alias=(h2_all, pq_all))
    kv_s = []
    for g, cch in enumerate(caches):
        new_row = jnp.concatenate([col(q_tile(g) + 1), col(q_tile(g) + 2)], axis=1)
        kv_s.append(_kv_shift(cch, new_row)[None])

    k1big = _block_diag_keys(peer_k1[0], 0)
    k2big = _block_diag_keys(peer_k2[0], 1)
    n1, e1, r2, e2 = _peer_route(pq_all, k1big, k2big, _pick_tile(tt, 256))
    u_b = peer_u[0].astype(BF16)
    vt_b = peer_v[0].astype(BF16).T
    p_all = _peer_dense(h2_all, u_b, vt_b, n1, e1, r2, e2, tt, PEER_NKEYS * SUBLANES)

    fg = final_g.reshape(1, d)
    y_p = _final(x1_p, p_all, mods_p[5], fg, tm_m, l_p, 0)
    y_s = _final(x1_s, p_all, mods_s[5], fg, n_s, n_s, t_p // n_s)

    return (y_p.reshape(n_p, l_p, d), y_s.reshape(n_s, l_s, d), sfin_p[None], state_s[None],
            kv_p[0], kv_s[0], kv_p[1], kv_s[1], kv_p[2], kv_s[2])
```

```python
import functools
import math

import jax
import jax.numpy as jnp
from jax import lax
from jax.experimental import pallas as pl
from jax.experimental.pallas import tpu as pltpu

F32 = jnp.float32
BF16 = jnp.bfloat16

PAST_LEN = 16384
GLA_HEADS = 4
GLA_RANK = 16
GLA_TAU = 16.0
GLA_CHUNK = 64
ATT_GROUPS = ((128, 1), (512, 4), (2048, 16))
ATT_HEADS = 4
ATT_HD = 128
ATT_BLOCK = 128
ATT_TOKENS = 2048
ROPE_DIM = ATT_HD // 4
ROPE_THETA = 500000.0
PEER_HEADS = 8
PEER_NKEYS = 128
PEER_DQ = 128
PEER_TOPK = 16
NORM_EPS = 1e-6
GELU_C = math.sqrt(2.0 / math.pi)

LANES = 128
SUBLANES = 8
VMEM_LIMIT = 56 * 1024 * 1024
NEG_BIG = -1e30


def _cparams(*sem):
    return pltpu.CompilerParams(dimension_semantics=sem, vmem_limit_bytes=VMEM_LIMIT)


def _sigmoid(x):
    return 1.0 / (1.0 + jnp.exp(-x))


def _silu(x):
    return x * _sigmoid(x)


def _log_sigmoid(x):
    return jnp.minimum(x, 0.0) - jnp.log(1.0 + jnp.exp(-jnp.abs(x)))


def _rms(x, g):
    return x * lax.rsqrt(jnp.mean(x * x, axis=-1, keepdims=True) + NORM_EPS) * g


def _dot(a, b):
    return jnp.dot(a, b, preferred_element_type=F32)


def _dot_nt(a, b):
    return lax.dot_general(a, b, (((1,), (1,)), ((), ())), preferred_element_type=F32)


def _dot_tn(a, b):
    return lax.dot_general(a, b, (((0,), (0,)), ((), ())), preferred_element_type=F32)


def _mod_kernel(c_ref, w_ref, b_ref, o_ref):
    s = _silu(c_ref[...]).astype(BF16)
    o_ref[...] = _dot(s, w_ref[...].astype(BF16)) + b_ref[...]


def _modulation(c, ada_w, ada_b):
    n, d = c.shape
    n6 = ada_w.shape[1]
    tn = n6 // 4
    return pl.pallas_call(
        _mod_kernel,
        grid=(n6 // tn,),
        in_specs=[pl.BlockSpec((n, d), lambda j: (0, 0)),
                  pl.BlockSpec((d, tn), lambda j: (0, j)),
                  pl.BlockSpec((1, tn), lambda j: (0, j))],
        out_specs=pl.BlockSpec((n, tn), lambda j: (0, j)),
        out_shape=jax.ShapeDtypeStruct((n, n6), F32),
        compiler_params=_cparams("arbitrary"),
        name="modulation",
    )(c, ada_w, ada_b.reshape(1, n6))


TN_IN = 512
VA_TILE0 = 8
ATT_TILE0 = 10


def _rope_tile(z, c, s1, s2):
    reps = z.shape[1] // LANES
    cc = jnp.concatenate([c] * reps, axis=1)
    a = jnp.concatenate([s1] * reps, axis=1)
    b = jnp.concatenate([s2] * reps, axis=1)
    n = z.shape[1]
    return z * cc + pltpu.roll(z, n - ROPE_DIM // 2, 1) * a + pltpu.roll(z, ROPE_DIM // 2, 1) * b


def _inproj_kernel(x_ref, g_ref, sc_ref, sh_ref, w_ref, walr_ref, w2_ref, ab_ref, c_ref, s1_ref, s2_ref,
                   *rest, segs, rope_tiles, split_tiles):
    n_seg = len(segs)
    seg_refs, la_ref = rest[:n_seg], rest[n_seg]
    split_refs, (h_scr, zt_scr) = rest[n_seg + 1:-2], rest[-2:]
    j = pl.program_id(1)
    tm = x_ref.shape[0]

    @pl.when(j == 0)
    def _():
        h = _rms(x_ref[...], g_ref[...]) * (1.0 + sc_ref[...]) + sh_ref[...]
        hb = h.astype(BF16)
        h_scr[...] = hb
        alr = _dot(hb, walr_ref[...])
        p = _dot(alr.astype(BF16), w2_ref[...]) + ab_ref[...]
        la_ref[...] = _log_sigmoid(p) * (1.0 / GLA_TAU)

    z = _dot(h_scr[...], w_ref[...])
    rope = lambda: _rope_tile(z, c_ref[...], s1_ref[...], s2_ref[...])

    for (lo, hi), ref in zip(segs, seg_refs):
        ropes = [t for t in rope_tiles if lo <= t < hi]
        in_seg = jnp.logical_and(j >= lo, j < hi)
        is_rope = functools.reduce(jnp.logical_or, [j == t for t in ropes], False)
        if ropes:
            @pl.when(jnp.logical_and(in_seg, is_rope))
            def _(ref=ref):
                ref[...] = rope().astype(ref.dtype)

        @pl.when(jnp.logical_and(in_seg, jnp.logical_not(is_rope)))
        def _(ref=ref):
            ref[...] = z.astype(ref.dtype)

    for t, slot, dil in split_tiles:
        @pl.when(j == t)
        def _(t=t, slot=slot, dil=dil):
            zr = rope() if t in rope_tiles else z
            for c in range(TN_IN // LANES):
                cs = slice(c * LANES, (c + 1) * LANES)
                zt_scr[c] = zr[:, cs]
                for r in range(dil):
                    rows = zt_scr[c, pl.ds(r, tm // dil, stride=dil), :]
                    split_refs[slot][r, :, cs] = rows.astype(split_refs[slot].dtype)


def _in_proj(x2, g, sc, sh, w_main, w_alr, w2p, a_b, tabs, tm, rows_per_mod, segs, rope_tiles, split_tiles):
    t, d = x2.shape
    r = sc.shape[1]
    wq = a_b.shape[1]
    n_tiles = w_main.shape[1] // TN_IN
    tiles_per_mod = rows_per_mod // tm
    tab_tiles = tabs[0].shape[0] // tm
    nmod = t // rows_per_mod
    mod_spec = pl.BlockSpec((None, r, d), lambda i, j: (i // tiles_per_mod, 0, 0))
    tab_spec = pl.BlockSpec((tm, LANES), lambda i, j: (i % tab_tiles, 0))
    const = lambda shape: pl.BlockSpec(shape, lambda i, j: (0, 0))
    out_specs, out_shape = [], []
    for lo, hi, dt in segs:
        out_specs.append(pl.BlockSpec((tm, TN_IN), lambda i, j, lo=lo, hi=hi: (i, jnp.clip(j - lo, 0, hi - lo - 1))))
        out_shape.append(jax.ShapeDtypeStruct((t, (hi - lo) * TN_IN), dt))
    out_specs.append(pl.BlockSpec((tm, wq), lambda i, j: (i, 0)))
    out_shape.append(jax.ShapeDtypeStruct((t, wq), F32))
    for _, _, dil in split_tiles:
        out_specs.append(pl.BlockSpec((None, dil, tm // dil, TN_IN),
                                      lambda i, j: (i // tiles_per_mod, 0, i % tiles_per_mod, 0)))
        out_shape.append(jax.ShapeDtypeStruct((nmod, dil, rows_per_mod // dil, TN_IN), BF16))
    return pl.pallas_call(
        functools.partial(_inproj_kernel, segs=tuple((lo, hi) for lo, hi, _ in segs),
                          rope_tiles=tuple(rope_tiles), split_tiles=tuple(split_tiles)),
        grid=(t // tm, n_tiles),
        in_specs=[pl.BlockSpec((tm, d), lambda i, j: (i, 0)), const((1, d)), mod_spec, mod_spec,
                  pl.BlockSpec((d, TN_IN), lambda i, j: (0, j)), const((d, LANES)), const((LANES, wq)),
                  const((1, wq)), tab_spec, tab_spec, tab_spec],
        out_specs=out_specs,
        out_shape=out_shape,
        scratch_shapes=[pltpu.VMEM((tm, d), BF16), pltpu.VMEM((TN_IN // LANES, tm, LANES), F32)],
        compiler_params=_cparams("arbitrary", "arbitrary"),
        name="in_proj",
    )(x2, g, sc, sh, w_main, w_alr, w2p, a_b, *tabs)


GLA_EXP_CLAMP = 80.0


def _gla_kernel(q_ref, k_ref, v_ref, ra_ref, la_ref, s0_ref, gn_ref, o_ref, sfin_ref, st_scr, *, rows, dk, dv):
    c = pl.program_id(1)
    nc = pl.num_programs(1)
    ch = GLA_CHUNK

    @pl.when(c == 0)
    def _():
        for h in range(GLA_HEADS):
            st_scr[h] = s0_ref[h].T

    ti = lax.broadcasted_iota(jnp.int32, (ch, ch), 0)
    si = lax.broadcasted_iota(jnp.int32, (ch, ch), 1)
    causal = si <= ti
    tri = causal.astype(F32)
    scale = dk ** -0.5

    def chunk(ci, carry):
        r0 = pl.multiple_of(ci * ch, ch)
        for h in range(GLA_HEADS):
            la = la_ref[pl.ds(r0, ch), h * dk:(h + 1) * dk]
            b = jnp.dot(tri, la, preferred_element_type=F32, precision=lax.Precision.HIGHEST)
            bl = b[ch - 1:ch, :]
            q = q_ref[pl.ds(r0, ch), h * dk:(h + 1) * dk] * scale
            k = k_ref[pl.ds(r0, ch), h * dk:(h + 1) * dk]
            v = v_ref[pl.ds(r0, ch), h * dv:(h + 1) * dv].astype(BF16)
            qe = (q * jnp.exp(b)).astype(BF16)
            ke = (k * jnp.exp(jnp.minimum(-b, GLA_EXP_CLAMP))).astype(BF16)
            kh = (k * jnp.exp(bl - b)).astype(BF16)
            att = jnp.where(causal, _dot_nt(qe, ke), 0.0)
            st = st_scr[h]
            o = _dot_nt(qe, st.astype(BF16)) + _dot(att.astype(BF16), v)
            st_scr[h] = st * jnp.exp(bl) + _dot_tn(v, kh)
            on = _rms(o, gn_ref[:, h * dv:(h + 1) * dv])
            gate = _silu(ra_ref[pl.ds(r0, ch), h * dv:(h + 1) * dv])
            o_ref[pl.ds(r0, ch), h * dv:(h + 1) * dv] = (on * gate).astype(o_ref.dtype)
        return carry

    lax.fori_loop(0, rows // ch, chunk, 0)

    @pl.when(c == nc - 1)
    def _():
        for h in range(GLA_HEADS):
            sfin_ref[h] = st_scr[h].T


def _gla_prompt(z, zv, la, s0, gn_g, n, l, rows):
    t = n * l
    dk = la.shape[1] // GLA_HEADS
    dv = gn_g.shape[1] // GLA_HEADS
    wk, wv = GLA_HEADS * dk, GLA_HEADS * dv
    cpb = l // rows
    row = lambda b, c: b * cpb + c
    return pl.pallas_call(
        functools.partial(_gla_kernel, rows=rows, dk=dk, dv=dv),
        grid=(n, cpb),
        in_specs=[pl.BlockSpec((rows, wk), lambda b, c: (row(b, c), 0)),
                  pl.BlockSpec((rows, wk), lambda b, c: (row(b, c), 1)),
                  pl.BlockSpec((rows, wv), lambda b, c: (row(b, c), 0)),
                  pl.BlockSpec((rows, wv), lambda b, c: (row(b, c), 1)),
                  pl.BlockSpec((rows, wk), lambda b, c: (row(b, c), 0)),
                  pl.BlockSpec((None, GLA_HEADS, dk, dv), lambda b, c: (b, 0, 0, 0)),
                  pl.BlockSpec((1, wv), lambda b, c: (0, 0))],
        out_specs=[pl.BlockSpec((rows, wv), lambda b, c: (row(b, c), 0)),
                   pl.BlockSpec((None, GLA_HEADS, dk, dv), lambda b, c: (b, 0, 0, 0))],
        out_shape=[jax.ShapeDtypeStruct((t, wv), BF16), jax.ShapeDtypeStruct((n, GLA_HEADS, dk, dv), F32)],
        scratch_shapes=[pltpu.VMEM((GLA_HEADS, dv, dk), F32)],
        compiler_params=_cparams("arbitrary", "arbitrary"),
        name="gla_prompt",
    )(z, z, zv, z, la, s0, gn_g)


def _gla_step_kernel(q_ref, k_ref, la_ref, v_ref, ra_ref, s_ref, gn_ref, o_ref, so_ref, *, dk):
    sn = jnp.exp(la_ref[...]) * s_ref[...] + k_ref[...] * v_ref[...]
    so_ref[...] = sn
    o = jnp.sum((q_ref[...] * dk ** -0.5) * sn, axis=1, keepdims=True)
    on = _rms(o, gn_ref[...])
    o_ref[...] = (on * _silu(ra_ref[...])).astype(o_ref.dtype)


def _gla_step(q, k, la, v, ra, s, gn_g):
    nb, nh, dk, dv = s.shape
    col = pl.BlockSpec((None, nh, dk, 1), lambda b: (b, 0, 0, 0))
    rowv = pl.BlockSpec((None, nh, 1, dv), lambda b: (b, 0, 0, 0))
    st = pl.BlockSpec((None, nh, dk, dv), lambda b: (b, 0, 0, 0))
    o, so = pl.pallas_call(
        functools.partial(_gla_step_kernel, dk=dk),
        grid=(nb,),
        in_specs=[col, col, col, rowv, rowv, st, pl.BlockSpec((nh, 1, dv), lambda b: (0, 0, 0))],
        out_specs=[rowv, st],
        out_shape=[jax.ShapeDtypeStruct((nb, nh, 1, dv), BF16), jax.ShapeDtypeStruct(s.shape, F32)],
        compiler_params=_cparams("arbitrary"),
        name="gla_step",
    )(q.reshape(nb, nh, dk, 1), k.reshape(nb, nh, dk, 1), la.reshape(nb, nh, dk, 1),
      v.reshape(nb, nh, 1, dv), ra.reshape(nb, nh, 1, dv), s, gn_g.reshape(nh, 1, dv))
    return o.reshape(nb, nh * dv), so


def _band_kernel(q_ref, k_ref, kp_ref, v_ref, vp_ref, o_ref, lse_ref, *, dil, qb):
    i = pl.program_id(1)
    blk = ATT_BLOCK
    qi = lax.broadcasted_iota(jnp.int32, (blk, 2 * blk), 0)
    ki = lax.broadcasted_iota(jnp.int32, (blk, 2 * blk), 1)
    dist = blk + qi - ki
    band = jnp.logical_and(dist >= 0, dist <= ATT_BLOCK)
    band_first = jnp.logical_and(band, ki >= jnp.where(i > 0, 0, blk))
    scale = ATT_HD ** -0.5
    for r in range(dil):
        for sb in range(qb // blk):
            rs = slice(sb * blk, (sb + 1) * blk)
            q = q_ref[r, rs, :].astype(BF16)
            if sb == 0:
                kprev, vprev, mask = kp_ref[r], vp_ref[r], band_first
            else:
                ps = slice((sb - 1) * blk, sb * blk)
                kprev, vprev, mask = k_ref[r, ps, :], v_ref[r, ps, :], band
            kk = jnp.concatenate([kprev, k_ref[r, rs, :]], axis=0).astype(BF16)
            vv = jnp.concatenate([vprev, v_ref[r, rs, :]], axis=0).astype(BF16)
            s = jnp.where(mask, _dot_nt(q, kk) * scale, NEG_BIG)
            m = jnp.max(s, axis=-1, keepdims=True)
            p = jnp.exp(s - m)
            lsum = jnp.sum(p, axis=-1, keepdims=True)
            o = _dot(p.astype(BF16), vv) / lsum
            lse = jnp.broadcast_to(m + jnp.log(lsum), (blk, ATT_HD))
            rows = rs if dil == 1 else pl.ds(sb * blk * dil + r, blk, stride=dil)
            o_ref[rows, :] = o
            lse_ref[rows, :] = lse


def _band_attention(qa, ka, va, cq, ck, cv, n, l, dil):
    lq = l // dil
    hd = ATT_HD
    tokens = min(ATT_TOKENS, l)
    qb = tokens // dil
    nqb = lq // qb
    sub = qb // ATT_BLOCK

    def main(c0):
        return pl.BlockSpec((None, dil, qb, hd), lambda b, i, h: (b, 0, i, c0 + h))

    def prev(c0):
        return pl.BlockSpec((None, dil, ATT_BLOCK, hd), lambda b, i, h: (b, 0, jnp.maximum(i * sub - 1, 0), c0 + h))

    out = pl.BlockSpec((tokens, hd), lambda b, i, h: (b * nqb + i, h))
    shp = jax.ShapeDtypeStruct((n * l, ATT_HEADS * hd), F32)
    return pl.pallas_call(
        functools.partial(_band_kernel, dil=dil, qb=qb),
        grid=(n, nqb, ATT_HEADS),
        in_specs=[main(cq), main(ck), prev(ck), main(cv), prev(cv)],
        out_specs=[out, out],
        out_shape=[shp, shp],
        compiler_params=_cparams("arbitrary", "arbitrary", "arbitrary"),
        name=f"band_attention_d{dil}",
    )(qa, ka, ka, va, va)


def _att_step_kernel(q_ref, kn_ref, vn_ref, c0_ref, c1_ref, c2_ref, o_ref):
    scale = ATT_HD ** -0.5
    outs, lses = [], []
    for g, c_ref in enumerate((c0_ref, c1_ref, c2_ref)):
        q, kn, vn = q_ref[g], kn_ref[g], vn_ref[g]
        s = jnp.sum(c_ref[:, 0] * q[None], axis=-1, keepdims=True) * scale
        sn = jnp.sum(kn * q, axis=-1, keepdims=True) * scale
        m = jnp.maximum(jnp.max(s, axis=0), sn)
        p = jnp.exp(s - m[None])
        pn = jnp.exp(sn - m)
        lsum = jnp.sum(p, axis=0) + pn
        outs.append((jnp.sum(p * c_ref[:, 1], axis=0) + pn * vn) / lsum)
        lses.append(m + jnp.log(lsum))
    mx = functools.reduce(jnp.maximum, lses)
    es = [jnp.exp(x - mx) for x in lses]
    inv = 1.0 / functools.reduce(lambda a, b: a + b, es)
    o_ref[...] = functools.reduce(lambda a, b: a + b, [e * inv * o for e, o in zip(es, outs)]).astype(o_ref.dtype)


def _att_step(qs, ks, vs, caches):
    nb = qs.shape[0]
    views, cspecs = [], []
    for (window, dil), cch in zip(ATT_GROUPS, caches):
        assert cch.shape[1] == window, "sample step expects full caches"
        views.append(cch.reshape((nb, window // dil, dil) + cch.shape[2:]))
        cspecs.append(pl.BlockSpec((None, ATT_BLOCK, None) + cch.shape[2:], lambda b: (b, 0, 0, 0, 0, 0)))
    row = pl.BlockSpec((None,) + qs.shape[1:], lambda b: (b, 0, 0, 0))
    o = pl.pallas_call(
        _att_step_kernel,
        grid=(nb,),
        in_specs=[row] * 3 + cspecs,
        out_specs=pl.BlockSpec((None, ATT_HEADS, ATT_HD), lambda b: (b, 0, 0)),
        out_shape=jax.ShapeDtypeStruct((nb, ATT_HEADS, ATT_HD), BF16),
        compiler_params=_cparams("arbitrary"),
        name="att_step",
    )(qs, ks, vs, *views)
    return o.reshape(nb, ATT_HEADS * ATT_HD)


def _kv_shift_kernel(a_ref, nxt_ref, new_ref, o_ref, *, rows):
    i = pl.program_id(1)
    last = pl.num_programs(1) - 1
    o_ref[0:rows - 1] = a_ref[1:rows]

    @pl.when(i == last)
    def _():
        o_ref[rows - 1:rows] = new_ref[...]

    @pl.when(i != last)
    def _():
        o_ref[rows - 1:rows] = nxt_ref[...]


def _kv_shift(cache, new_row):
    nb, wb = cache.shape[0], cache.shape[1]
    rw = cache.shape[2] * cache.shape[3] * cache.shape[4]
    sub = rw // LANES
    rows = min(wb, 512)
    cv = cache.reshape(nb, wb, sub, LANES)
    out = pl.pallas_call(
        functools.partial(_kv_shift_kernel, rows=rows),
        grid=(nb, wb // rows),
        in_specs=[pl.BlockSpec((None, rows, sub, LANES), lambda b, i: (b, i, 0, 0)),
                  pl.BlockSpec((None, 1, sub, LANES), lambda b, i: (b, jnp.minimum((i + 1) * rows, wb - 1), 0, 0)),
                  pl.BlockSpec((None, 1, sub, LANES), lambda b, i: (b, 0, 0, 0))],
        out_specs=pl.BlockSpec((None, rows, sub, LANES), lambda b, i: (b, i, 0, 0)),
        out_shape=jax.ShapeDtypeStruct(cv.shape, cache.dtype),
        compiler_params=_cparams("arbitrary", "arbitrary"),
        name=f"kv_shift_w{wb}",
    )(cv, cv, new_row.reshape(nb, 1, sub, LANES))
    return out.reshape(cache.shape)


def _merge_kernel(*refs, n_att, n_alias):
    oa_ref = refs[0]
    att = refs[1:1 + n_att]
    (gza_ref, gzb_ref, x_ref, g1_ref, sc2_ref, sh2_ref, n2_ref, wa_ref, wb_ref, wo_ref,
     wq_ref) = refs[1 + n_att:12 + n_att]
    x1_ref, h2_ref, pq_ref = refs[12 + n_att + n_alias:]
    tm = x_ref.shape[0]
    if n_att == 1:
        ob = att[0][...]
    else:
        ng = n_att // 2
        ls = [r[...] for r in att[ng:]]
        mx = functools.reduce(jnp.maximum, ls)
        es = [jnp.exp(x - mx) for x in ls]
        inv = 1.0 / functools.reduce(lambda a, b: a + b, es)
        ob = functools.reduce(lambda a, b: a + b, [e * inv * r[...] for e, r in zip(es, att[:ng])]).astype(BF16)
    ma = _dot(oa_ref[...], wa_ref[...])
    mb = _dot(ob, wb_ref[...])
    merged = _sigmoid(gza_ref[...]) * ma + _sigmoid(gzb_ref[...]) * mb
    x1 = x_ref[...] + g1_ref[...] * _dot(merged.astype(BF16), wo_ref[...])
    x1_ref[...] = x1
    h2 = (_rms(x1, n2_ref[...]) * (1.0 + sc2_ref[...]) + sh2_ref[...]).astype(BF16)
    h2_ref[0:tm] = h2
    pq_ref[0:tm] = _dot(h2, wq_ref[...])
    if h2_ref.shape[0] > tm:
        h2_ref[tm:] = jnp.zeros((h2_ref.shape[0] - tm, h2_ref.shape[1]), h2_ref.dtype)
        pq_ref[tm:] = jnp.zeros((pq_ref.shape[0] - tm, pq_ref.shape[1]), pq_ref.dtype)


def _merge(oa, att, z, x2, g1, sc2, sh2, n2g, wa, wb, wo, wq, tm, rows_per_mod, gza_blk, gzb_blk,
           peer_rows, peer_tile, peer_blk0, alias=()):
    t, d = x2.shape
    r = g1.shape[1]
    wqn = wq.shape[1]
    tiles_per_mod = rows_per_mod // tm
    tok = lambda w: pl.BlockSpec((tm, w), lambda i: (i, 0))
    mod = pl.BlockSpec((None, r, d), lambda i: (i // tiles_per_mod, 0, 0))
    const = lambda a: pl.BlockSpec(a.shape, lambda i: (0, 0))
    peer = lambda w: pl.BlockSpec((peer_tile, w), lambda i: (peer_blk0 + i, 0))
    n_in = 12 + len(att)
    return pl.pallas_call(
        functools.partial(_merge_kernel, n_att=len(att), n_alias=len(alias)),
        grid=(t // tm,),
        in_specs=[tok(oa.shape[1])] + [tok(a.shape[1]) for a in att]
                 + [pl.BlockSpec((tm, d), lambda i: (i, gza_blk)), pl.BlockSpec((tm, d), lambda i: (i, gzb_blk)),
                    tok(d), mod, mod, mod, const(n2g), const(wa), const(wb), const(wo), const(wq)]
                 + [pl.BlockSpec(memory_space=pl.ANY)] * len(alias),
        out_specs=[tok(d), peer(d), peer(wqn)],
        out_shape=[jax.ShapeDtypeStruct((t, d), F32), jax.ShapeDtypeStruct((peer_rows, d), BF16),
                   jax.ShapeDtypeStruct((peer_rows, wqn), F32)],
        input_output_aliases={n_in + k: 1 + k for k in range(len(alias))},
        compiler_params=_cparams("arbitrary"),
        name="merge",
    )(oa, *att, z, z, x2, g1, sc2, sh2, n2g, wa, wb, wo, wq, *alias)


def _strict_max_below(rows, prev):
    m = None
    for x in rows:
        y = jnp.where(x < prev, x, NEG_BIG) if prev is not None else x
        m = y if m is None else jnp.maximum(m, y)
    return m


def _topk_desc(s, k):
    vals = []
    prev = None
    for _ in range(k):
        y = s if prev is None else jnp.where(s < prev, s, NEG_BIG)
        prev = jnp.max(y, axis=0, keepdims=True)
        vals.append(prev)
    return vals


def _route_kernel(pq_ref, k1_ref, k2_ref, n1_ref, e1_ref, r2_ref, e2_ref, s1_scr, s2_scr):
    nk = PEER_NKEYS
    pq = pq_ref[...].astype(BF16)
    s1_scr[...] = _dot_nt(k1_ref[...], pq)
    s2_scr[...] = _dot_nt(k2_ref[...], pq)
    v1h, v2h = [], []
    for h in range(PEER_HEADS):
        v1h.append(_topk_desc(s1_scr[h * nk:(h + 1) * nk], PEER_TOPK))
        v2h.append(_topk_desc(s2_scr[h * nk:(h + 1) * nk], PEER_TOPK))
    v1 = [jnp.concatenate([v1h[h][k] for h in range(PEER_HEADS)], axis=0) for k in range(PEER_TOPK)]
    v2 = [jnp.concatenate([v2h[h][k] for h in range(PEER_HEADS)], axis=0) for k in range(PEER_TOPK)]
    pairs = [(i, j) for i in range(PEER_TOPK) for j in range(PEER_TOPK) if (i + 1) * (j + 1) <= PEER_TOPK]
    cands = [v1[i] + v2[j] for i, j in pairs]
    tau = None
    for _ in range(PEER_TOPK):
        tau = _strict_max_below(cands, tau)
    cmax = v1[0] + v2[0]
    zsum = None
    counts = [None] * PEER_TOPK
    for (i, _), cnd in zip(pairs, cands):
        sel = cnd >= tau
        e = jnp.where(sel, jnp.exp(cnd - cmax), 0.0)
        zsum = e if zsum is None else zsum + e
        one = jnp.where(sel, 1.0, 0.0)
        counts[i] = one if counts[i] is None else counts[i] + one
    zinv = 1.0 / zsum
    for h in range(PEER_HEADS):
        a = s1_scr[h * nk:(h + 1) * nk]
        b = s2_scr[h * nk:(h + 1) * nk]
        n1 = jnp.zeros_like(a)
        r2 = jnp.full_like(b, float(PEER_TOPK))
        for k in range(PEER_TOPK):
            n1 = jnp.where(a == v1h[h][k], counts[k][h:h + 1, :], n1)
            r2 = jnp.where(b == v2h[h][k], float(k), r2)
        n1_ref[h] = _bf16_pair_words(n1)
        e1_ref[h] = _bf16_pair_words(jnp.exp(a - v1h[h][0]) * zinv[h:h + 1, :])
        r2_ref[h] = r2.astype(r2_ref.dtype)
        e2_ref[h] = jnp.exp(b - v2h[h][0]).astype(e2_ref.dtype)


def _peer_route(pq, k1big, k2big, tt):
    t, w = pq.shape
    nh, nk = PEER_HEADS, PEER_NKEYS
    sspec = pl.BlockSpec((nh, nk, tt), lambda i: (0, 0, i))
    shp = lambda dt: jax.ShapeDtypeStruct((nh, nk, t), dt)
    return pl.pallas_call(
        _route_kernel,
        grid=(t // tt,),
        in_specs=[pl.BlockSpec((tt, w), lambda i: (i, 0)),
                  pl.BlockSpec(k1big.shape, lambda i: (0, 0)), pl.BlockSpec(k2big.shape, lambda i: (0, 0))],
        out_specs=[sspec] * 4,
        out_shape=[shp(jnp.uint32), shp(jnp.uint32), shp(BF16), shp(BF16)],
        scratch_shapes=[pltpu.VMEM((nh * nk, tt), F32)] * 2,
        compiler_params=_cparams("arbitrary"),
        name="peer_route",
    )(pq, k1big, k2big)


def _gelu_tanh(x):
    return 0.5 * x * (1.0 + jnp.tanh(GELU_C * (x + 0.044715 * (x * x * x))))


BF16_ROWS = 16
GATE_LANES = 256


def _gelu_tanh_bf16(x):
    hx = 0.5 * x
    return hx + hx * jnp.tanh(x * (GELU_C + (GELU_C * 0.044715) * (x * x)))


def _bf16_pair_words(x):
    bits = lax.bitcast_convert_type(x.astype(BF16).astype(F32), jnp.uint32)
    return bits | (bits >> 16)


def _rows_from_words(w):
    return pltpu.bitcast(jnp.broadcast_to(w, (SUBLANES, w.shape[1])), BF16)


def _peer_dense_kernel(h2_ref, u_ref, vt_ref, n1_ref, e1_ref, r2_ref, e2_ref, o_ref, acc_scr, wg_scr, st_scr,
                       *, et):
    e = pl.program_id(1)
    nk = PEER_NKEYS
    tt = h2_ref.shape[0]

    @pl.when(e == 0)
    def _():
        acc_scr[...] = jnp.zeros_like(acc_scr)

    st_scr[...] = _dot_nt(u_ref[...], h2_ref[...]).astype(BF16)
    zero = jnp.zeros((), BF16)
    groups = nk // BF16_ROWS
    for ii in range(et // nk):
        for lc in range(tt // GATE_LANES):
            ls = slice(lc * GATE_LANES, (lc + 1) * GATE_LANES)
            w = None
            for h in range(PEER_HEADS):
                n1 = _rows_from_words(n1_ref[h, ii:ii + 1, ls])
                e1 = _rows_from_words(e1_ref[h, ii:ii + 1, ls])
                r2 = r2_ref[h, :, ls].reshape(groups, BF16_ROWS, GATE_LANES)
                e2 = e2_ref[h, :, ls].reshape(groups, BF16_ROWS, GATE_LANES)
                c = jnp.where(r2 < n1[None], e2, zero) * e1[None]
                w = c if w is None else w + c
            g = _gelu_tanh_bf16(st_scr[ii * nk:(ii + 1) * nk, ls])
            wg_scr[ii * nk:(ii + 1) * nk, ls] = w.reshape(nk, GATE_LANES) * g
    acc_scr[...] += _dot(vt_ref[...], wg_scr[...])

    @pl.when(e == pl.num_programs(1) - 1)
    def _():
        o_ref[...] = acc_scr[...].T


def _peer_dense(h2, u_b, vt_b, n1, e1, r2, e2, tt, et):
    t, d = h2.shape
    ne = u_b.shape[0]
    nh, nk = PEER_HEADS, PEER_NKEYS
    rowside = pl.BlockSpec((nh, et // nk, tt), lambda i, e: (0, e, i))
    colside = pl.BlockSpec((nh, nk, tt), lambda i, e: (0, 0, i))
    return pl.pallas_call(
        functools.partial(_peer_dense_kernel, et=et),
        grid=(t // tt, ne // et),
        in_specs=[pl.BlockSpec((tt, d), lambda i, e: (i, 0)),
                  pl.BlockSpec((et, d), lambda i, e: (e, 0)),
                  pl.BlockSpec((d, et), lambda i, e: (0, e)),
                  rowside, rowside, colside, colside],
        out_specs=pl.BlockSpec((tt, d), lambda i, e: (i, 0)),
        out_shape=jax.ShapeDtypeStruct((t, d), F32),
        scratch_shapes=[pltpu.VMEM((d, tt), F32), pltpu.VMEM((et, tt), BF16), pltpu.VMEM((et, tt), BF16)],
        compiler_params=_cparams("arbitrary", "arbitrary"),
        name="peer_dense",
    )(h2, u_b, vt_b, n1, e1, r2, e2)


def _final_kernel(x1_ref, p_ref, g2_ref, fg_ref, y_ref):
    y_ref[...] = _rms(x1_ref[...] + g2_ref[...] * p_ref[...], fg_ref[...])


def _final(x1, p, g2, fg, tm, rows_per_mod, p_blk0):
    t, d = x1.shape
    r = g2.shape[1]
    tiles_per_mod = rows_per_mod // tm
    tok = pl.BlockSpec((tm, d), lambda i: (i, 0))
    return pl.pallas_call(
        _final_kernel,
        grid=(t // tm,),
        in_specs=[tok, pl.BlockSpec((tm, d), lambda i: (p_blk0 + i, 0)),
                  pl.BlockSpec((None, r, d), lambda i: (i // tiles_per_mod, 0, 0)),
                  pl.BlockSpec((1, d), lambda i: (0, 0))],
        out_specs=tok,
        out_shape=jax.ShapeDtypeStruct((t, d), F32),
        compiler_params=_cparams("arbitrary"),
        name="final_norm",
    )(x1, p, g2, fg)


def _rope_tables(pos):
    half = ROPE_DIM // 2
    inv_freq = ROPE_THETA ** (-jnp.arange(half, dtype=F32) / half)
    ang = pos.astype(F32)[:, None] * inv_freq[None, :]
    cos, sin = jnp.cos(ang), jnp.sin(ang)
    n = pos.shape[0]
    ones = jnp.ones((n, LANES - ROPE_DIM), F32)
    zeros = jnp.zeros((n, LANES - ROPE_DIM), F32)
    zh = jnp.zeros((n, half), F32)
    return (jnp.concatenate([cos, cos, ones], axis=1),
            jnp.concatenate([-sin, zh, zeros], axis=1),
            jnp.concatenate([zh, sin, zeros], axis=1))


def _block_diag_keys(keys, half):
    nk, dh = keys.shape
    eye = jnp.eye(PEER_HEADS, dtype=keys.dtype)
    blk = jnp.zeros((PEER_HEADS, nk, PEER_HEADS, 2, dh), keys.dtype)
    blk = blk.at[:, :, :, half, :].set(eye[:, None, :, None] * keys[None, :, None, :])
    return blk.reshape(PEER_HEADS * nk, PEER_HEADS * 2 * dh).astype(BF16)


def _pick_tile(n, pref):
    t = min(pref, n)
    while n % t:
        t //= 2
    return t


def kernel(x_prompt, x_sample, c_prompt, c_sample, state_gla, cache_kv_w128, cache_kv_w512, cache_kv_w2048,
           ada_w, ada_b, norm1_g, w_in, gla_a_w2, gla_a_b, gla_gn_g, w_branch_a, w_branch_b, w_out,
           norm2_g, peer_wq, peer_k1, peer_k2, peer_u, peer_v, final_g):
    depth = ada_w.shape[0]
    assert depth == 1, "single-layer trunk"
    n_p, l_p, d = x_prompt.shape
    n_s, l_s, _ = x_sample.shape
    assert l_s == 1, "sample group decodes one token per sequence"
    caches = (cache_kv_w128[0], cache_kv_w512[0], cache_kv_w2048[0])
    n_groups = len(ATT_GROUPS)

    w_qa = gla_a_w2.shape[2]
    w_va = gla_gn_g.shape[1]
    w_ob = ATT_HEADS * ATT_HD
    w_qb = n_groups * w_ob
    assert w_qa == TN_IN and w_va == 2 * TN_IN and d == 2 * TN_IN and w_ob == TN_IN
    widths = (w_qa, w_qa, w_va, w_va, GLA_RANK, w_qb, w_qb, w_qb, d, d)
    offs = [0]
    for w in widths:
        offs.append(offs[-1] + w)
    seg = lambda i: w_in[0][:, offs[i]:offs[i + 1]]
    att_col = lambda g, k: seg(5 + k)[:, g * w_ob:(g + 1) * w_ob]
    att_cols = [att_col(g, k) for g in range(n_groups) for k in range(3)]
    w_main = jnp.concatenate([seg(0), seg(1), seg(3), seg(8), seg(9), seg(2)] + att_cols, axis=1).astype(BF16)
    w_kv = jnp.concatenate([att_col(g, k) for g in range(n_groups) for k in (1, 2)], axis=1).astype(BF16)
    n_tiles = w_main.shape[1] // TN_IN
    ra_off, gza_blk, gzb_blk = 2 * w_qa, 2, 3
    w_alr = jnp.pad(seg(4), ((0, 0), (0, LANES - GLA_RANK))).astype(BF16)
    w2p = jnp.pad(gla_a_w2[0], ((0, LANES - GLA_RANK), (0, 0))).astype(BF16)
    a_b = gla_a_b[0].reshape(1, w_qa)
    g1n = norm1_g[0].reshape(1, d)
    gn_g = gla_gn_g[0].reshape(1, w_va)
    n2g = norm2_g[0].reshape(1, d)
    wa, wb = w_branch_a[0].astype(BF16), w_branch_b[0].astype(BF16)
    wo, wq = w_out[0].astype(BF16), peer_wq[0].astype(BF16)
    q_tile = lambda g: ATT_TILE0 + 3 * g
    rope_tiles = [q_tile(g) + k for g in range(n_groups) for k in (0, 1)]

    n_mod = n_p + n_s
    n_pad = -(-n_mod // SUBLANES) * SUBLANES
    c_all = jnp.concatenate([c_prompt, c_sample, jnp.zeros((n_pad - n_mod, d), F32)], axis=0)
    mod = _modulation(c_all, ada_w[0], ada_b[0])
    mods_p = [mod[:n_p, i * d:(i + 1) * d].reshape(n_p, 1, d) for i in range(6)]
    mods_s = [mod[n_p:n_mod, i * d:(i + 1) * d].reshape(1, n_s, d) for i in range(6)]

    t_p = n_p * l_p
    tt = _pick_tile(t_p, 512)
    t_all = t_p + (-(-n_s // tt)) * tt
    assert t_all - t_p == tt

    xp2 = x_prompt.reshape(t_p, d)
    tm_p = _pick_tile(l_p, 1024)
    pos_p = jnp.arange(l_p, dtype=jnp.int32)
    split = []
    for g, (window, dil) in enumerate(ATT_GROUPS):
        assert window // dil == ATT_BLOCK
        if dil > 1:
            split += [(q_tile(g) + k, len(split) + k, dil) for k in range(3)]
    first_split = min([t for t, _, _ in split], default=n_tiles)
    outs = _in_proj(xp2, g1n, mods_p[1], mods_p[0], w_main, w_alr, w2p, a_b, _rope_tables(pos_p), tm_p, l_p,
                    ((0, VA_TILE0, F32), (VA_TILE0, first_split, BF16)), rope_tiles, split)
    z_p, zb_p, la_p, split_p = outs[0], outs[1], outs[2], outs[3:]
    s0_p = jnp.zeros((n_p,) + state_gla.shape[2:], F32)
    oa_p, sfin_p = _gla_prompt(z_p, zb_p, la_p, s0_p, gn_g, n_p, l_p, _pick_tile(l_p, 256))
    zb4 = zb_p.reshape(n_p, 1, l_p, zb_p.shape[1])
    att_o, att_l = [], []
    for g, (window, dil) in enumerate(ATT_GROUPS):
        if dil > 1:
            qa, ka, va = [split_p[slot] for t, slot, _ in split if q_tile(g) <= t < q_tile(g) + 3]
            o_g, lse_g = _band_attention(qa, ka, va, 0, 0, 0, n_p, l_p, dil)
        else:
            c0 = lambda k: (q_tile(g) + k - VA_TILE0) * (TN_IN // ATT_HD)
            o_g, lse_g = _band_attention(zb4, zb4, zb4, c0(0), c0(1), c0(2), n_p, l_p, dil)
        att_o.append(o_g)
        att_l.append(lse_g)
    keep_max = min(max(w for w, _ in ATT_GROUPS), l_p)
    x_tail = x_prompt[:, l_p - keep_max:].reshape(n_p * keep_max, d)
    kv_tiles = 2 * n_groups
    z_kv = _in_proj(x_tail, g1n, mods_p[1], mods_p[0], w_kv, w_alr, w2p, a_b, _rope_tables(pos_p[l_p - keep_max:]),
                    _pick_tile(keep_max, 1024), keep_max, ((0, kv_tiles, F32),), range(0, kv_tiles, 2), [])[0]
    z_kv = z_kv.reshape(n_p, keep_max, kv_tiles, ATT_HEADS, ATT_HD)
    kv_p = []
    for g, (window, dil) in enumerate(ATT_GROUPS):
        keep = min(window, l_p)
        kv_p.append(z_kv[:, keep_max - keep:, 2 * g:2 * g + 2][None])
    tm_m = _pick_tile(l_p, 256)
    x1_p, h2_all, pq_all = _merge(oa_p, att_o + att_l, z_p, xp2, mods_p[2], mods_p[4], mods_p[3], n2g,
                                  wa, wb, wo, wq, tm_m, l_p, gza_blk, gzb_blk, t_all, tm_m, 0)

    xs2 = x_sample.reshape(n_s, d)
    tabs_s = _rope_tables(jnp.full((n_s,), PAST_LEN, dtype=jnp.int32))
    z_s, la_s = _in_proj(xs2, g1n, mods_s[1], mods_s[0], w_main, w_alr, w2p, a_b, tabs_s, n_s, n_s,
                         ((0, n_tiles, F32),), rope_tiles, [])
    col = lambda tile: z_s[:, tile * TN_IN:(tile + 1) * TN_IN]
    oa_s, state_s = _gla_step(col(0), col(1), la_s, z_s[:, VA_TILE0 * TN_IN:VA_TILE0 * TN_IN + w_va],
                              z_s[:, ra_off:ra_off + w_va], state_gla[0], gn_g)
    stack = lambda k: jnp.stack([col(q_tile(g) + k) for g in range(n_groups)], axis=1).reshape(
        n_s, n_groups, ATT_HEADS, ATT_HD)
    ob_s = _att_step(stack(0), stack(1), stack(2), caches)
    x1_s, h2_all, pq_all = _merge(oa_s, [ob_s], z_s, xs2, mods_s[2], mods_s[4], mods_s[3], n2g,
                                  wa, wb, wo, wq, n_s, n_s, gza_blk, gzb_blk, t_all, tt, t_p // tt,
                                  alias=(h2_all, pq_all))
    kv_s = []
    for g, cch in enumerate(caches):
        new_row = jnp.concatenate([col(q_tile(g) + 1), col(q_tile(g) + 2)], axis=1)
        kv_s.append(_kv_shift(cch, new_row)[None])

    k1big = _block_diag_keys(peer_k1[0], 0)
    k2big = _block_diag_keys(peer_k2[0], 1)
    n1, e1, r2, e2 = _peer_route(pq_all, k1big, k2big, _pick_tile(tt, 256))
    u_b = peer_u[0].astype(BF16)
    vt_b = peer_v[0].astype(BF16).T
    p_all = _peer_dense(h2_all, u_b, vt_b, n1, e1, r2, e2, tt, PEER_NKEYS * SUBLANES)

    fg = final_g.reshape(1, d)
    y_p = _final(x1_p, p_all, mods_p[5], fg, tm_m, l_p, 0)
    y_s = _final(x1_s, p_all, mods_s[5], fg, n_s, n_s, t_p // n_s)

    return (y_p.reshape(n_p, l_p, d), y_s.reshape(n_s, l_s, d), sfin_p[None], state_s[None],
            kv_p[0], kv_s[0], kv_p[1], kv_s[1], kv_p[2], kv_s[2])
```

```python
import functools
import math

import jax
import jax.numpy as jnp
from jax import lax
from jax.experimental import pallas as pl
from jax.experimental.pallas import tpu as pltpu

F32 = jnp.float32
BF16 = jnp.bfloat16

PAST_LEN = 16384
GLA_HEADS = 4
GLA_RANK = 16
GLA_TAU = 16.0
GLA_CHUNK = 64
ATT_GROUPS = ((128, 1), (512, 4), (2048, 16))
ATT_HEADS = 4
ATT_HD = 128
ATT_BLOCK = 128
ATT_TOKENS = 2048
ROPE_DIM = ATT_HD // 4
ROPE_THETA = 500000.0
PEER_HEADS = 8
PEER_NKEYS = 128
PEER_DQ = 128
PEER_TOPK = 16
NORM_EPS = 1e-6
GELU_C = math.sqrt(2.0 / math.pi)

LANES = 128
SUBLANES = 8
VMEM_LIMIT = 56 * 1024 * 1024
NEG_BIG = -1e30


def _cparams(*sem):
    return pltpu.CompilerParams(dimension_semantics=sem, vmem_limit_bytes=VMEM_LIMIT)


def _sigmoid(x):
    return 1.0 / (1.0 + jnp.exp(-x))


def _silu(x):
    return x * _sigmoid(x)


def _log_sigmoid(x):
    return jnp.minimum(x, 0.0) - jnp.log(1.0 + jnp.exp(-jnp.abs(x)))


def _rms(x, g):
    return x * lax.rsqrt(jnp.mean(x * x, axis=-1, keepdims=True) + NORM_EPS) * g


def _dot(a, b):
    return jnp.dot(a, b, preferred_element_type=F32)


def _dot_nt(a, b):
    return lax.dot_general(a, b, (((1,), (1,)), ((), ())), preferred_element_type=F32)


def _dot_tn(a, b):
    return lax.dot_general(a, b, (((0,), (0,)), ((), ())), preferred_element_type=F32)


def _mod_kernel(c_ref, w_ref, b_ref, o_ref):
    s = _silu(c_ref[...]).astype(BF16)
    o_ref[...] = _dot(s, w_ref[...].astype(BF16)) + b_ref[...]


def _modulation(c, ada_w, ada_b):
    n, d = c.shape
    n6 = ada_w.shape[1]
    tn = n6 // 4
    return pl.pallas_call(
        _mod_kernel,
        grid=(n6 // tn,),
        in_specs=[pl.BlockSpec((n, d), lambda j: (0, 0)),
                  pl.BlockSpec((d, tn), lambda j: (0, j)),
                  pl.BlockSpec((1, tn), lambda j: (0, j))],
        out_specs=pl.BlockSpec((n, tn), lambda j: (0, j)),
        out_shape=jax.ShapeDtypeStruct((n, n6), F32),
        compiler_params=_cparams("arbitrary"),
        name="modulation",
    )(c, ada_w, ada_b.reshape(1, n6))


TN_IN = 512
VA_TILE0 = 8
ATT_TILE0 = 10


def _rope_tile(z, c, s1, s2):
    reps = z.shape[1] // LANES
    cc = jnp.concatenate([c] * reps, axis=1)
    a = jnp.concatenate([s1] * reps, axis=1)
    b = jnp.concatenate([s2] * reps, axis=1)
    n = z.shape[1]
    return z * cc + pltpu.roll(z, n - ROPE_DIM // 2, 1) * a + pltpu.roll(z, ROPE_DIM // 2, 1) * b


def _inproj_kernel(x_ref, g_ref, sc_ref, sh_ref, w_ref, walr_ref, w2_ref, ab_ref, c_ref, s1_ref, s2_ref,
                   *rest, segs, rope_tiles, split_tiles):
    n_seg = len(segs)
    seg_refs, la_ref = rest[:n_seg], rest[n_seg]
    split_refs, (h_scr, zt_scr) = rest[n_seg + 1:-2], rest[-2:]
    j = pl.program_id(1)
    tm = x_ref.shape[0]

    @pl.when(j == 0)
    def _():
        h = _rms(x_ref[...], g_ref[...]) * (1.0 + sc_ref[...]) + sh_ref[...]
        hb = h.astype(BF16)
        h_scr[...] = hb
        alr = _dot(hb, walr_ref[...])
        p = _dot(alr.astype(BF16), w2_ref[...]) + ab_ref[...]
        la_ref[...] = _log_sigmoid(p) * (1.0 / GLA_TAU)

    z = _dot(h_scr[...], w_ref[...])
    rope = lambda: _rope_tile(z, c_ref[...], s1_ref[...], s2_ref[...])

    for (lo, hi), ref in zip(segs, seg_refs):
        ropes = [t for t in rope_tiles if lo <= t < hi]
        in_seg = jnp.logical_and(j >= lo, j < hi)
        is_rope = functools.reduce(jnp.logical_or, [j == t for t in ropes], False)
        if ropes:
            @pl.when(jnp.logical_and(in_seg, is_rope))
            def _(ref=ref):
                ref[...] = rope().astype(ref.dtype)

        @pl.when(jnp.logical_and(in_seg, jnp.logical_not(is_rope)))
        def _(ref=ref):
            ref[...] = z.astype(ref.dtype)

    for t, slot, dil in split_tiles:
        @pl.when(j == t)
        def _(t=t, slot=slot, dil=dil):
            zr = rope() if t in rope_tiles else z
            for c in range(TN_IN // LANES):
                cs = slice(c * LANES, (c + 1) * LANES)
                zt_scr[c] = zr[:, cs]
                for r in range(dil):
                    rows = zt_scr[c, pl.ds(r, tm // dil, stride=dil), :]
                    split_refs[slot][r, :, cs] = rows.astype(split_refs[slot].dtype)


def _in_proj(x2, g, sc, sh, w_main, w_alr, w2p, a_b, tabs, tm, rows_per_mod, segs, rope_tiles, split_tiles):
    t, d = x2.shape
    r = sc.shape[1]
    wq = a_b.shape[1]
    n_tiles = w_main.shape[1] // TN_IN
    tiles_per_mod = rows_per_mod // tm
    tab_tiles = tabs[0].shape[0] // tm
    nmod = t // rows_per_mod
    mod_spec = pl.BlockSpec((None, r, d), lambda i, j: (i // tiles_per_mod, 0, 0))
    tab_spec = pl.BlockSpec((tm, LANES), lambda i, j: (i % tab_tiles, 0))
    const = lambda shape: pl.BlockSpec(shape, lambda i, j: (0, 0))
    out_specs, out_shape = [], []
    for lo, hi, dt in segs:
        out_specs.append(pl.BlockSpec((tm, TN_IN), lambda i, j, lo=lo, hi=hi: (i, jnp.clip(j - lo, 0, hi - lo - 1))))
        out_shape.append(jax.ShapeDtypeStruct((t, (hi - lo) * TN_IN), dt))
    out_specs.append(pl.BlockSpec((tm, wq), lambda i, j: (i, 0)))
    out_shape.append(jax.ShapeDtypeStruct((t, wq), F32))
    for _, _, dil in split_tiles:
        out_specs.append(pl.BlockSpec((None, dil, tm // dil, TN_IN),
                                      lambda i, j: (i // tiles_per_mod, 0, i % tiles_per_mod, 0)))
        out_shape.append(jax.ShapeDtypeStruct((nmod, dil, rows_per_mod // dil, TN_IN), BF16))
    return pl.pallas_call(
        functools.partial(_inproj_kernel, segs=tuple((lo, hi) for lo, hi, _ in segs),
                          rope_tiles=tuple(rope_tiles), split_tiles=tuple(split_tiles)),
        grid=(t // tm, n_tiles),
        in_specs=[pl.BlockSpec((tm, d), lambda i, j: (i, 0)), const((1, d)), mod_spec, mod_spec,
                  pl.BlockSpec((d, TN_IN), lambda i, j: (0, j)), const((d, LANES)), const((LANES, wq)),
                  const((1, wq)), tab_spec, tab_spec, tab_spec],
        out_specs=out_specs,
        out_shape=out_shape,
        scratch_shapes=[pltpu.VMEM((tm, d), BF16), pltpu.VMEM((TN_IN // LANES, tm, LANES), F32)],
        compiler_params=_cparams("arbitrary", "arbitrary"),
        name="in_proj",
    )(x2, g, sc, sh, w_main, w_alr, w2p, a_b, *tabs)


GLA_EXP_CLAMP = 80.0


def _gla_kernel(q_ref, k_ref, v_ref, ra_ref, la_ref, s0_ref, gn_ref, o_ref, sfin_ref, st_scr, *, rows, dk, dv):
    c = pl.program_id(1)
    nc = pl.num_programs(1)
    ch = GLA_CHUNK

    @pl.when(c == 0)
    def _():
        for h in range(GLA_HEADS):
            st_scr[h] = s0_ref[h].T

    ti = lax.broadcasted_iota(jnp.int32, (ch, ch), 0)
    si = lax.broadcasted_iota(jnp.int32, (ch, ch), 1)
    causal = si <= ti
    tri = causal.astype(BF16)
    scale = dk ** -0.5
    wk = GLA_HEADS * dk
    heads = range(GLA_HEADS)
    ks = [slice(h * dk, (h + 1) * dk) for h in heads]
    vs = [slice(h * dv, (h + 1) * dv) for h in heads]

    def chunk(ci, carry):
        rows = pl.ds(pl.multiple_of(ci * ch, ch), ch)
        la = la_ref[rows, :]
        hi = la.astype(BF16)
        r1 = la - hi.astype(F32)
        mid = r1.astype(BF16)
        lo = (r1 - mid.astype(F32)).astype(BF16)
        b3 = _dot(tri, jnp.concatenate([hi, mid, lo], axis=1))
        b = b3[:, :wk] + b3[:, wk:2 * wk] + b3[:, 2 * wk:]
        bl = b[ch - 1:ch, :]
        q = q_ref[rows, :] * scale
        k = k_ref[rows, :]
        qe = (q * jnp.exp(b)).astype(BF16)
        ke = (k * jnp.exp(jnp.minimum(-b, GLA_EXP_CLAMP))).astype(BF16)
        kh = (k * jnp.exp(bl - b)).astype(BF16)
        dec = jnp.exp(bl)
        v = v_ref[rows, :].astype(BF16)
        att = [jnp.where(causal, _dot_nt(qe[:, ks[h]], ke[:, ks[h]]), 0.0).astype(BF16) for h in heads]
        st = [st_scr[h] for h in heads]
        o = [_dot_nt(qe[:, ks[h]], st[h].astype(BF16)) + _dot(att[h], v[:, vs[h]]) for h in heads]
        for h in heads:
            st_scr[h] = st[h] * dec[:, ks[h]] + _dot_tn(v[:, vs[h]], kh[:, ks[h]])
        for h in heads:
            on = _rms(o[h], gn_ref[:, vs[h]])
            o_ref[rows, vs[h]] = (on * _silu(ra_ref[rows, vs[h]])).astype(o_ref.dtype)
        return carry

    lax.fori_loop(0, rows // ch, chunk, 0)

    @pl.when(c == nc - 1)
    def _():
        for h in range(GLA_HEADS):
            sfin_ref[h] = st_scr[h].T


def _gla_prompt(z, zv, la, s0, gn_g, n, l, rows):
    t = n * l
    dk = la.shape[1] // GLA_HEADS
    dv = gn_g.shape[1] // GLA_HEADS
    wk, wv = GLA_HEADS * dk, GLA_HEADS * dv
    cpb = l // rows
    row = lambda b, c: b * cpb + c
    return pl.pallas_call(
        functools.partial(_gla_kernel, rows=rows, dk=dk, dv=dv),
        grid=(n, cpb),
        in_specs=[pl.BlockSpec((rows, wk), lambda b, c: (row(b, c), 0)),
                  pl.BlockSpec((rows, wk), lambda b, c: (row(b, c), 1)),
                  pl.BlockSpec((rows, wv), lambda b, c: (row(b, c), 0)),
                  pl.BlockSpec((rows, wv), lambda b, c: (row(b, c), 1)),
                  pl.BlockSpec((rows, wk), lambda b, c: (row(b, c), 0)),
                  pl.BlockSpec((None, GLA_HEADS, dk, dv), lambda b, c: (b, 0, 0, 0)),
                  pl.BlockSpec((1, wv), lambda b, c: (0, 0))],
        out_specs=[pl.BlockSpec((rows, wv), lambda b, c: (row(b, c), 0)),
                   pl.BlockSpec((None, GLA_HEADS, dk, dv), lambda b, c: (b, 0, 0, 0))],
        out_shape=[jax.ShapeDtypeStruct((t, wv), BF16), jax.ShapeDtypeStruct((n, GLA_HEADS, dk, dv), F32)],
        scratch_shapes=[pltpu.VMEM((GLA_HEADS, dv, dk), F32)],
        compiler_params=_cparams("arbitrary", "arbitrary"),
        name="gla_prompt",
    )(z, z, zv, z, la, s0, gn_g)


def _gla_step_kernel(q_ref, k_ref, la_ref, v_ref, ra_ref, s_ref, gn_ref, o_ref, so_ref, *, dk):
    sn = jnp.exp(la_ref[...]) * s_ref[...] + k_ref[...] * v_ref[...]
    so_ref[...] = sn
    o = jnp.sum((q_ref[...] * dk ** -0.5) * sn, axis=1, keepdims=True)
    on = _rms(o, gn_ref[...])
    o_ref[...] = (on * _silu(ra_ref[...])).astype(o_ref.dtype)


def _gla_step(q, k, la, v, ra, s, gn_g):
    nb, nh, dk, dv = s.shape
    col = pl.BlockSpec((None, nh, dk, 1), lambda b: (b, 0, 0, 0))
    rowv = pl.BlockSpec((None, nh, 1, dv), lambda b: (b, 0, 0, 0))
    st = pl.BlockSpec((None, nh, dk, dv), lambda b: (b, 0, 0, 0))
    o, so = pl.pallas_call(
        functools.partial(_gla_step_kernel, dk=dk),
        grid=(nb,),
        in_specs=[col, col, col, rowv, rowv, st, pl.BlockSpec((nh, 1, dv), lambda b: (0, 0, 0))],
        out_specs=[rowv, st],
        out_shape=[jax.ShapeDtypeStruct((nb, nh, 1, dv), BF16), jax.ShapeDtypeStruct(s.shape, F32)],
        compiler_params=_cparams("arbitrary"),
        name="gla_step",
    )(q.reshape(nb, nh, dk, 1), k.reshape(nb, nh, dk, 1), la.reshape(nb, nh, dk, 1),
      v.reshape(nb, nh, 1, dv), ra.reshape(nb, nh, 1, dv), s, gn_g.reshape(nh, 1, dv))
    return o.reshape(nb, nh * dv), so


def _band_kernel(q_ref, k_ref, kp_ref, v_ref, vp_ref, o_ref, lse_ref, *, dil, qb):
    i = pl.program_id(1)
    blk = ATT_BLOCK
    qi = lax.broadcasted_iota(jnp.int32, (blk, 2 * blk), 0)
    ki = lax.broadcasted_iota(jnp.int32, (blk, 2 * blk), 1)
    dist = blk + qi - ki
    band = jnp.logical_and(dist >= 0, dist <= ATT_BLOCK)
    band_first = jnp.logical_and(band, ki >= jnp.where(i > 0, 0, blk))
    scale = ATT_HD ** -0.5
    for r in range(dil):
        for sb in range(qb // blk):
            rs = slice(sb * blk, (sb + 1) * blk)
            q = q_ref[r, rs, :].astype(BF16)
            if sb == 0:
                kprev, vprev, mask = kp_ref[r], vp_ref[r], band_first
            else:
                ps = slice((sb - 1) * blk, sb * blk)
                kprev, vprev, mask = k_ref[r, ps, :], v_ref[r, ps, :], band
            kk = jnp.concatenate([kprev, k_ref[r, rs, :]], axis=0).astype(BF16)
            vv = jnp.concatenate([vprev, v_ref[r, rs, :]], axis=0).astype(BF16)
            s = jnp.where(mask, _dot_nt(q, kk) * scale, NEG_BIG)
            m = jnp.max(s, axis=-1, keepdims=True)
            p = jnp.exp(s - m)
            lsum = jnp.sum(p, axis=-1, keepdims=True)
            o = _dot(p.astype(BF16), vv) / lsum
            lse = jnp.broadcast_to(m + jnp.log(lsum), (blk, ATT_HD))
            rows = rs if dil == 1 else pl.ds(sb * blk * dil + r, blk, stride=dil)
            o_ref[rows, :] = o
            lse_ref[rows, :] = lse


def _band_attention(qa, ka, va, cq, ck, cv, n, l, dil):
    lq = l // dil
    hd = ATT_HD
    tokens = min(ATT_TOKENS, l)
    qb = tokens // dil
    nqb = lq // qb
    sub = qb // ATT_BLOCK

    def main(c0):
        return pl.BlockSpec((None, dil, qb, hd), lambda b, i, h: (b, 0, i, c0 + h))

    def prev(c0):
        return pl.BlockSpec((None, dil, ATT_BLOCK, hd), lambda b, i, h: (b, 0, jnp.maximum(i * sub - 1, 0), c0 + h))

    out = pl.BlockSpec((tokens, hd), lambda b, i, h: (b * nqb + i, h))
    shp = jax.ShapeDtypeStruct((n * l, ATT_HEADS * hd), F32)
    return pl.pallas_call(
        functools.partial(_band_kernel, dil=dil, qb=qb),
        grid=(n, nqb, ATT_HEADS),
        in_specs=[main(cq), main(ck), prev(ck), main(cv), prev(cv)],
        out_specs=[out, out],
        out_shape=[shp, shp],
        compiler_params=_cparams("arbitrary", "arbitrary", "arbitrary"),
        name=f"band_attention_d{dil}",
    )(qa, ka, ka, va, va)


def _att_step_kernel(q_ref, kn_ref, vn_ref, c0_ref, c1_ref, c2_ref, o_ref):
    scale = ATT_HD ** -0.5
    outs, lses = [], []
    for g, c_ref in enumerate((c0_ref, c1_ref, c2_ref)):
        q, kn, vn = q_ref[g], kn_ref[g], vn_ref[g]
        s = jnp.sum(c_ref[:, 0] * q[None], axis=-1, keepdims=True) * scale
        sn = jnp.sum(kn * q, axis=-1, keepdims=True) * scale
        m = jnp.maximum(jnp.max(s, axis=0), sn)
        p = jnp.exp(s - m[None])
        pn = jnp.exp(sn - m)
        lsum = jnp.sum(p, axis=0) + pn
        outs.append((jnp.sum(p * c_ref[:, 1], axis=0) + pn * vn) / lsum)
        lses.append(m + jnp.log(lsum))
    mx = functools.reduce(jnp.maximum, lses)
    es = [jnp.exp(x - mx) for x in lses]
    inv = 1.0 / functools.reduce(lambda a, b: a + b, es)
    o_ref[...] = functools.reduce(lambda a, b: a + b, [e * inv * o for e, o in zip(es, outs)]).astype(o_ref.dtype)


def _att_step(qs, ks, vs, caches):
    nb = qs.shape[0]
    views, cspecs = [], []
    for (window, dil), cch in zip(ATT_GROUPS, caches):
        assert cch.shape[1] == window, "sample step expects full caches"
        views.append(cch.reshape((nb, window // dil, dil) + cch.shape[2:]))
        cspecs.append(pl.BlockSpec((None, ATT_BLOCK, None) + cch.shape[2:], lambda b: (b, 0, 0, 0, 0, 0)))
    row = pl.BlockSpec((None,) + qs.shape[1:], lambda b: (b, 0, 0, 0))
    o = pl.pallas_call(
        _att_step_kernel,
        grid=(nb,),
        in_specs=[row] * 3 + cspecs,
        out_specs=pl.BlockSpec((None, ATT_HEADS, ATT_HD), lambda b: (b, 0, 0)),
        out_shape=jax.ShapeDtypeStruct((nb, ATT_HEADS, ATT_HD), BF16),
        compiler_params=_cparams("arbitrary"),
        name="att_step",
    )(qs, ks, vs, *views)
    return o.reshape(nb, ATT_HEADS * ATT_HD)


def _kv_shift_kernel(a_ref, nxt_ref, new_ref, o_ref, *, rows):
    i = pl.program_id(1)
    last = pl.num_programs(1) - 1
    o_ref[0:rows - 1] = a_ref[1:rows]

    @pl.when(i == last)
    def _():
        o_ref[rows - 1:rows] = new_ref[...]

    @pl.when(i != last)
    def _():
        o_ref[rows - 1:rows] = nxt_ref[...]


def _kv_shift(cache, new_row):
    nb, wb = cache.shape[0], cache.shape[1]
    rw = cache.shape[2] * cache.shape[3] * cache.shape[4]
    sub = rw // LANES
    rows = min(wb, 512)
    cv = cache.reshape(nb, wb, sub, LANES)
    out = pl.pallas_call(
        functools.partial(_kv_shift_kernel, rows=rows),
        grid=(nb, wb // rows),
        in_specs=[pl.BlockSpec((None, rows, sub, LANES), lambda b, i: (b, i, 0, 0)),
                  pl.BlockSpec((None, 1, sub, LANES), lambda b, i: (b, jnp.minimum((i + 1) * rows, wb - 1), 0, 0)),
                  pl.BlockSpec((None, 1, sub, LANES), lambda b, i: (b, 0, 0, 0))],
        out_specs=pl.BlockSpec((None, rows, sub, LANES), lambda b, i: (b, i, 0, 0)),
        out_shape=jax.ShapeDtypeStruct(cv.shape, cache.dtype),
        compiler_params=_cparams("arbitrary", "arbitrary"),
        name=f"kv_shift_w{wb}",
    )(cv, cv, new_row.reshape(nb, 1, sub, LANES))
    return out.reshape(cache.shape)


def _merge_kernel(*refs, n_att, n_alias):
    oa_ref = refs[0]
    att = refs[1:1 + n_att]
    (gza_ref, gzb_ref, x_ref, g1_ref, sc2_ref, sh2_ref, n2_ref, wa_ref, wb_ref, wo_ref,
     wq_ref) = refs[1 + n_att:12 + n_att]
    x1_ref, h2_ref, pq_ref = refs[12 + n_att + n_alias:]
    tm = x_ref.shape[0]
    if n_att == 1:
        ob = att[0][...]
    else:
        ng = n_att // 2
        ls = [r[...] for r in att[ng:]]
        mx = functools.reduce(jnp.maximum, ls)
        es = [jnp.exp(x - mx) for x in ls]
        inv = 1.0 / functools.reduce(lambda a, b: a + b, es)
        ob = functools.reduce(lambda a, b: a + b, [e * inv * r[...] for e, r in zip(es, att[:ng])]).astype(BF16)
    ma = _dot(oa_ref[...], wa_ref[...])
    mb = _dot(ob, wb_ref[...])
    merged = _sigmoid(gza_ref[...]) * ma + _sigmoid(gzb_ref[...]) * mb
    x1 = x_ref[...] + g1_ref[...] * _dot(merged.astype(BF16), wo_ref[...])
    x1_ref[...] = x1
    h2 = (_rms(x1, n2_ref[...]) * (1.0 + sc2_ref[...]) + sh2_ref[...]).astype(BF16)
    h2_ref[0:tm] = h2
    pq_ref[0:tm] = _dot(h2, wq_ref[...])
    if h2_ref.shape[0] > tm:
        h2_ref[tm:] = jnp.zeros((h2_ref.shape[0] - tm, h2_ref.shape[1]), h2_ref.dtype)
        pq_ref[tm:] = jnp.zeros((pq_ref.shape[0] - tm, pq_ref.shape[1]), pq_ref.dtype)


def _merge(oa, att, z, x2, g1, sc2, sh2, n2g, wa, wb, wo, wq, tm, rows_per_mod, gza_blk, gzb_blk,
           peer_rows, peer_tile, peer_blk0, alias=()):
    t, d = x2.shape
    r = g1.shape[1]
    wqn = wq.shape[1]
    tiles_per_mod = rows_per_mod // tm
    tok = lambda w: pl.BlockSpec((tm, w), lambda i: (i, 0))
    mod = pl.BlockSpec((None, r, d), lambda i: (i // tiles_per_mod, 0, 0))
    const = lambda a: pl.BlockSpec(a.shape, lambda i: (0, 0))
    peer = lambda w: pl.BlockSpec((peer_tile, w), lambda i: (peer_blk0 + i, 0))
    n_in = 12 + len(att)
    return pl.pallas_call(
        functools.partial(_merge_kernel, n_att=len(att), n_alias=len(alias)),
        grid=(t // tm,),
        in_specs=[tok(oa.shape[1])] + [tok(a.shape[1]) for a in att]
                 + [pl.BlockSpec((tm, d), lambda i: (i, gza_blk)), pl.BlockSpec((tm, d), lambda i: (i, gzb_blk)),
                    tok(d), mod, mod, mod, const(n2g), const(wa), const(wb), const(wo), const(wq)]
                 + [pl.BlockSpec(memory_space=pl.ANY)] * len(alias),
        out_specs=[tok(d), peer(d), peer(wqn)],
        out_shape=[jax.ShapeDtypeStruct((t, d), F32), jax.ShapeDtypeStruct((peer_rows, d), BF16),
                   jax.ShapeDtypeStruct((peer_rows, wqn), F32)],
        input_output_aliases={n_in + k: 1 + k for k in range(len(alias))},
        compiler_params=_cparams("arbitrary"),
        name="merge",
    )(oa, *att, z, z, x2, g1, sc2, sh2, n2g, wa, wb, wo, wq, *alias)


def _strict_max_below(rows, prev):
    m = None
    for x in rows:
        y = jnp.where(x < prev, x, NEG_BIG) if prev is not None else x
        m = y if m is None else jnp.maximum(m, y)
    return m


def _topk_desc(s, k):
    vals = []
    prev = None
    for _ in range(k):
        y = s if prev is None else jnp.where(s < prev, s, NEG_BIG)
        prev = jnp.max(y, axis=0, keepdims=True)
        vals.append(prev)
    return vals


def _route_kernel(pq_ref, k1_ref, k2_ref, n1_ref, e1_ref, r2_ref, e2_ref, s1_scr, s2_scr):
    nk = PEER_NKEYS
    pq = pq_ref[...].astype(BF16)
    s1_scr[...] = _dot_nt(k1_ref[...], pq)
    s2_scr[...] = _dot_nt(k2_ref[...], pq)
    v1h, v2h = [], []
    for h in range(PEER_HEADS):
        v1h.append(_topk_desc(s1_scr[h * nk:(h + 1) * nk], PEER_TOPK))
        v2h.append(_topk_desc(s2_scr[h * nk:(h + 1) * nk], PEER_TOPK))
    v1 = [jnp.concatenate([v1h[h][k] for h in range(PEER_HEADS)], axis=0) for k in range(PEER_TOPK)]
    v2 = [jnp.concatenate([v2h[h][k] for h in range(PEER_HEADS)], axis=0) for k in range(PEER_TOPK)]
    pairs = [(i, j) for i in range(PEER_TOPK) for j in range(PEER_TOPK) if (i + 1) * (j + 1) <= PEER_TOPK]
    cands = [v1[i] + v2[j] for i, j in pairs]
    tau = None
    for _ in range(PEER_TOPK):
        tau = _strict_max_below(cands, tau)
    cmax = v1[0] + v2[0]
    zsum = None
    counts = [None] * PEER_TOPK
    for (i, _), cnd in zip(pairs, cands):
        sel = cnd >= tau
        e = jnp.where(sel, jnp.exp(cnd - cmax), 0.0)
        zsum = e if zsum is None else zsum + e
        one = jnp.where(sel, 1.0, 0.0)
        counts[i] = one if counts[i] is None else counts[i] + one
    zinv = 1.0 / zsum
    for h in range(PEER_HEADS):
        a = s1_scr[h * nk:(h + 1) * nk]
        b = s2_scr[h * nk:(h + 1) * nk]
        n1 = jnp.zeros_like(a)
        r2 = jnp.full_like(b, float(PEER_TOPK))
        for k in range(PEER_TOPK):
            n1 = jnp.where(a == v1h[h][k], counts[k][h:h + 1, :], n1)
            r2 = jnp.where(b == v2h[h][k], float(k), r2)
        n1_ref[h] = n1
        e1_ref[h] = jnp.exp(a - v1h[h][0]) * zinv[h:h + 1, :]
        r2_ref[h] = r2.astype(r2_ref.dtype)
        e2_ref[h] = jnp.exp(b - v2h[h][0]).astype(e2_ref.dtype)


def _peer_route(pq, k1big, k2big, tt):
    t, w = pq.shape
    nh, nk = PEER_HEADS, PEER_NKEYS
    sspec = pl.BlockSpec((nh, nk, tt), lambda i: (0, 0, i))
    shp = lambda dt: jax.ShapeDtypeStruct((nh, nk, t), dt)
    return pl.pallas_call(
        _route_kernel,
        grid=(t // tt,),
        in_specs=[pl.BlockSpec((tt, w), lambda i: (i, 0)),
                  pl.BlockSpec(k1big.shape, lambda i: (0, 0)), pl.BlockSpec(k2big.shape, lambda i: (0, 0))],
        out_specs=[sspec] * 4,
        out_shape=[shp(F32), shp(F32), shp(BF16), shp(BF16)],
        scratch_shapes=[pltpu.VMEM((nh * nk, tt), F32)] * 2,
        compiler_params=_cparams("arbitrary"),
        name="peer_route",
    )(pq, k1big, k2big)


def _gelu_tanh(x):
    return 0.5 * x * (1.0 + jnp.tanh(GELU_C * (x + 0.044715 * (x * x * x))))


BF16_ROWS = 16
GATE_LANES = 256
PEER_PART = 256
PEER_TILE = 2048


def _gelu_tanh_bf16(x):
    hx = 0.5 * x
    return hx + hx * jnp.tanh(x * (GELU_C + (GELU_C * 0.044715) * (x * x)))


def _bf16_rows(row):
    return jnp.broadcast_to(row, (BF16_ROWS, row.shape[1])).astype(BF16)


def _peer_dense_kernel(h2_ref, u_ref, vt_ref, n1_ref, e1_ref, r2_ref, e2_ref, o_ref, acc_scr, wg_scr, st_scr,
                       *, et):
    e = pl.program_id(1)
    nk = PEER_NKEYS
    tt = h2_ref.shape[0]

    @pl.when(e == 0)
    def _():
        acc_scr[...] = jnp.zeros_like(acc_scr)

    zero = jnp.zeros((), BF16)
    groups = nk // BF16_ROWS

    def scores(part):
        st_scr[part, :] = _dot_nt(u_ref[part, :], h2_ref[...]).astype(BF16)

    def gates(ii):
        rs = slice(ii * nk, (ii + 1) * nk)
        for lc in range(tt // GATE_LANES):
            ls = slice(lc * GATE_LANES, (lc + 1) * GATE_LANES)
            w = None
            for h in range(PEER_HEADS):
                n1 = _bf16_rows(n1_ref[h, ii:ii + 1, ls])
                e1 = _bf16_rows(e1_ref[h, ii:ii + 1, ls])
                r2 = r2_ref[h, :, ls].reshape(groups, BF16_ROWS, GATE_LANES)
                e2 = e2_ref[h, :, ls].reshape(groups, BF16_ROWS, GATE_LANES)
                c = jnp.where(r2 < n1[None], e2, zero) * e1[None]
                w = c if w is None else w + c
            wg_scr[rs, ls] = w.reshape(nk, GATE_LANES) * _gelu_tanh_bf16(st_scr[rs, ls])

    def mix(part):
        return _dot(vt_ref[:, part], wg_scr[part, :])

    per = PEER_PART // nk
    parts = [slice(p * PEER_PART, (p + 1) * PEER_PART) for p in range(et // PEER_PART)]
    out = None
    scores(parts[0])
    for p, part in enumerate(parts):
        if p + 1 < len(parts):
            scores(parts[p + 1])
        for ii in range(p * per, (p + 1) * per):
            gates(ii)
        if p > 0:
            d = mix(parts[p - 1])
            out = d if out is None else out + d
    d = mix(parts[-1])
    acc_scr[...] += d if out is None else out + d

    @pl.when(e == pl.num_programs(1) - 1)
    def _():
        o_ref[...] = acc_scr[...].T


def _peer_dense(h2, u_b, vt_b, n1, e1, r2, e2, tt, et):
    t, d = h2.shape
    ne = u_b.shape[0]
    nh, nk = PEER_HEADS, PEER_NKEYS
    rowside = pl.BlockSpec((nh, et // nk, tt), lambda i, e: (0, e, i))
    colside = pl.BlockSpec((nh, nk, tt), lambda i, e: (0, 0, i))
    return pl.pallas_call(
        functools.partial(_peer_dense_kernel, et=et),
        grid=(t // tt, ne // et),
        in_specs=[pl.BlockSpec((tt, d), lambda i, e: (i, 0)),
                  pl.BlockSpec((et, d), lambda i, e: (e, 0)),
                  pl.BlockSpec((d, et), lambda i, e: (0, e)),
                  rowside, rowside, colside, colside],
        out_specs=pl.BlockSpec((tt, d), lambda i, e: (i, 0)),
        out_shape=jax.ShapeDtypeStruct((t, d), F32),
        scratch_shapes=[pltpu.VMEM((d, tt), F32), pltpu.VMEM((et, tt), BF16), pltpu.VMEM((et, tt), BF16)],
        compiler_params=_cparams("arbitrary", "arbitrary"),
        name="peer_dense",
    )(h2, u_b, vt_b, n1, e1, r2, e2)


def _final_kernel(x1_ref, p_ref, g2_ref, fg_ref, y_ref):
    y_ref[...] = _rms(x1_ref[...] + g2_ref[...] * p_ref[...], fg_ref[...])


def _final(x1, p, g2, fg, tm, rows_per_mod, p_blk0):
    t, d = x1.shape
    r = g2.shape[1]
    tiles_per_mod = rows_per_mod // tm
    tok = pl.BlockSpec((tm, d), lambda i: (i, 0))
    return pl.pallas_call(
        _final_kernel,
        grid=(t // tm,),
        in_specs=[tok, pl.BlockSpec((tm, d), lambda i: (p_blk0 + i, 0)),
                  pl.BlockSpec((None, r, d), lambda i: (i // tiles_per_mod, 0, 0)),
                  pl.BlockSpec((1, d), lambda i: (0, 0))],
        out_specs=tok,
        out_shape=jax.ShapeDtypeStruct((t, d), F32),
        compiler_params=_cparams("arbitrary"),
        name="final_norm",
    )(x1, p, g2, fg)


def _rope_tables(pos):
    half = ROPE_DIM // 2
    inv_freq = ROPE_THETA ** (-jnp.arange(half, dtype=F32) / half)
    ang = pos.astype(F32)[:, None] * inv_freq[None, :]
    cos, sin = jnp.cos(ang), jnp.sin(ang)
    n = pos.shape[0]
    ones = jnp.ones((n, LANES - ROPE_DIM), F32)
    zeros = jnp.zeros((n, LANES - ROPE_DIM), F32)
    zh = jnp.zeros((n, half), F32)
    return (jnp.concatenate([cos, cos, ones], axis=1),
            jnp.concatenate([-sin, zh, zeros], axis=1),
            jnp.concatenate([zh, sin, zeros], axis=1))


def _block_diag_keys(keys, half):
    nk, dh = keys.shape
    eye = jnp.eye(PEER_HEADS, dtype=keys.dtype)
    blk = jnp.zeros((PEER_HEADS, nk, PEER_HEADS, 2, dh), keys.dtype)
    blk = blk.at[:, :, :, half, :].set(eye[:, None, :, None] * keys[None, :, None, :])
    return blk.reshape(PEER_HEADS * nk, PEER_HEADS * 2 * dh).astype(BF16)


def _pick_tile(n, pref):
    t = min(pref, n)
    while n % t:
        t //= 2
    return t


def kernel(x_prompt, x_sample, c_prompt, c_sample, state_gla, cache_kv_w128, cache_kv_w512, cache_kv_w2048,
           ada_w, ada_b, norm1_g, w_in, gla_a_w2, gla_a_b, gla_gn_g, w_branch_a, w_branch_b, w_out,
           norm2_g, peer_wq, peer_k1, peer_k2, peer_u, peer_v, final_g):
    depth = ada_w.shape[0]
    assert depth == 1, "single-layer trunk"
    n_p, l_p, d = x_prompt.shape
    n_s, l_s, _ = x_sample.shape
    assert l_s == 1, "sample group decodes one token per sequence"
    caches = (cache_kv_w128[0], cache_kv_w512[0], cache_kv_w2048[0])
    n_groups = len(ATT_GROUPS)

    w_qa = gla_a_w2.shape[2]
    w_va = gla_gn_g.shape[1]
    w_ob = ATT_HEADS * ATT_HD
    w_qb = n_groups * w_ob
    assert w_qa == TN_IN and w_va == 2 * TN_IN and d == 2 * TN_IN and w_ob == TN_IN
    widths = (w_qa, w_qa, w_va, w_va, GLA_RANK, w_qb, w_qb, w_qb, d, d)
    offs = [0]
    for w in widths:
        offs.append(offs[-1] + w)
    seg = lambda i: w_in[0][:, offs[i]:offs[i + 1]]
    att_col = lambda g, k: seg(5 + k)[:, g * w_ob:(g + 1) * w_ob]
    att_cols = [att_col(g, k) for g in range(n_groups) for k in range(3)]
    w_main = jnp.concatenate([seg(0), seg(1), seg(3), seg(8), seg(9), seg(2)] + att_cols, axis=1).astype(BF16)
    w_kv = jnp.concatenate([att_col(g, k) for g in range(n_groups) for k in (1, 2)], axis=1).astype(BF16)
    n_tiles = w_main.shape[1] // TN_IN
    ra_off, gza_blk, gzb_blk = 2 * w_qa, 2, 3
    w_alr = jnp.pad(seg(4), ((0, 0), (0, LANES - GLA_RANK))).astype(BF16)
    w2p = jnp.pad(gla_a_w2[0], ((0, LANES - GLA_RANK), (0, 0))).astype(BF16)
    a_b = gla_a_b[0].reshape(1, w_qa)
    g1n = norm1_g[0].reshape(1, d)
    gn_g = gla_gn_g[0].reshape(1, w_va)
    n2g = norm2_g[0].reshape(1, d)
    wa, wb = w_branch_a[0].astype(BF16), w_branch_b[0].astype(BF16)
    wo, wq = w_out[0].astype(BF16), peer_wq[0].astype(BF16)
    q_tile = lambda g: ATT_TILE0 + 3 * g
    rope_tiles = [q_tile(g) + k for g in range(n_groups) for k in (0, 1)]

    n_mod = n_p + n_s
    n_pad = -(-n_mod // SUBLANES) * SUBLANES
    c_all = jnp.concatenate([c_prompt, c_sample, jnp.zeros((n_pad - n_mod, d), F32)], axis=0)
    mod = _modulation(c_all, ada_w[0], ada_b[0])
    mods_p = [mod[:n_p, i * d:(i + 1) * d].reshape(n_p, 1, d) for i in range(6)]
    mods_s = [mod[n_p:n_mod, i * d:(i + 1) * d].reshape(1, n_s, d) for i in range(6)]

    t_p = n_p * l_p
    tt = _pick_tile(t_p, 512)
    t_all = t_p + (-(-n_s // tt)) * tt
    assert t_all - t_p == tt

    xp2 = x_prompt.reshape(t_p, d)
    tm_p = _pick_tile(l_p, 1024)
    pos_p = jnp.arange(l_p, dtype=jnp.int32)
    split = []
    for g, (window, dil) in enumerate(ATT_GROUPS):
        assert window // dil == ATT_BLOCK
        if dil > 1:
            split += [(q_tile(g) + k, len(split) + k, dil) for k in range(3)]
    first_split = min([t for t, _, _ in split], default=n_tiles)
    outs = _in_proj(xp2, g1n, mods_p[1], mods_p[0], w_main, w_alr, w2p, a_b, _rope_tables(pos_p), tm_p, l_p,
                    ((0, VA_TILE0, F32), (VA_TILE0, first_split, BF16)), rope_tiles, split)
    z_p, zb_p, la_p, split_p = outs[0], outs[1], outs[2], outs[3:]
    s0_p = jnp.zeros((n_p,) + state_gla.shape[2:], F32)
    oa_p, sfin_p = _gla_prompt(z_p, zb_p, la_p, s0_p, gn_g, n_p, l_p, _pick_tile(l_p, 256))
    zb4 = zb_p.reshape(n_p, 1, l_p, zb_p.shape[1])
    att_o, att_l = [], []
    for g, (window, dil) in enumerate(ATT_GROUPS):
        if dil > 1:
            qa, ka, va = [split_p[slot] for t, slot, _ in split if q_tile(g) <= t < q_tile(g) + 3]
            o_g, lse_g = _band_attention(qa, ka, va, 0, 0, 0, n_p, l_p, dil)
        else:
            c0 = lambda k: (q_tile(g) + k - VA_TILE0) * (TN_IN // ATT_HD)
            o_g, lse_g = _band_attention(zb4, zb4, zb4, c0(0), c0(1), c0(2), n_p, l_p, dil)
        att_o.append(o_g)
        att_l.append(lse_g)
    keep_max = min(max(w for w, _ in ATT_GROUPS), l_p)
    x_tail = x_prompt[:, l_p - keep_max:].reshape(n_p * keep_max, d)
    kv_tiles = 2 * n_groups
    z_kv = _in_proj(x_tail, g1n, mods_p[1], mods_p[0], w_kv, w_alr, w2p, a_b, _rope_tables(pos_p[l_p - keep_max:]),
                    _pick_tile(keep_max, 1024), keep_max, ((0, kv_tiles, F32),), range(0, kv_tiles, 2), [])[0]
    z_kv = z_kv.reshape(n_p, keep_max, kv_tiles, ATT_HEADS, ATT_HD)
    kv_p = []
    for g, (window, dil) in enumerate(ATT_GROUPS):
        keep = min(window, l_p)
        kv_p.append(z_kv[:, keep_max - keep:, 2 * g:2 * g + 2][None])
    tm_m = _pick_tile(l_p, 256)
    x1_p, h2_all, pq_all = _merge(oa_p, att_o + att_l, z_p, xp2, mods_p[2], mods_p[4], mods_p[3], n2g,
                                  wa, wb, wo, wq, tm_m, l_p, gza_blk, gzb_blk, t_all, tm_m, 0)

    xs2 = x_sample.reshape(n_s, d)
    tabs_s = _rope_tables(jnp.full((n_s,), PAST_LEN, dtype=jnp.int32))
    z_s, la_s = _in_proj(xs2, g1n, mods_s[1], mods_s[0], w_main, w_alr, w2p, a_b, tabs_s, n_s, n_s,
                         ((0, n_tiles, F32),), rope_tiles, [])
    col = lambda tile: z_s[:, tile * TN_IN:(tile + 1) * TN_IN]
    oa_s, state_s = _gla_step(col(0), col(1), la_s, z_s[:, VA_TILE0 * TN_IN:VA_TILE0 * TN_IN + w_va],
                              z_s[:, ra_off:ra_off + w_va], state_gla[0], gn_g)
    stack = lambda k: jnp.stack([col(q_tile(g) + k) for g in range(n_groups)], axis=1).reshape(
        n_s, n_groups, ATT_HEADS, ATT_HD)
    ob_s = _att_step(stack(0), stack(1), stack(2), caches)
    x1_s, h2_all, pq_all = _merge(oa_s, [ob_s], z_s, xs2, mods_s[2], mods_s[4], mods_s[3], n2g,
                                  wa, wb, wo, wq, n_s, n_s, gza_blk, gzb_blk, t_all, tt, t_p // tt,
                                  alias=(h2_all, pq_all))
    kv_s = []
    for g, cch in enumerate(caches):
        new_row = jnp.concatenate([col(q_tile(g) + 1), col(q_tile(g) + 2)], axis=1)
        kv_s.append(_kv_shift(cch, new_row)[None])

    k1big = _block_diag_keys(peer_k1[0], 0)
    k2big = _block_diag_keys(peer_k2[0], 1)
    n1, e1, r2, e2 = _peer_route(pq_all, k1big, k2big, _pick_tile(tt, 256))
    u_b = peer_u[0].astype(BF16)
    vt_b = peer_v[0].astype(BF16).T
    p_all = _peer_dense(h2_all, u_b, vt_b, n1, e1, r2, e2, tt, PEER_TILE)

    fg = final_g.reshape(1, d)
    y_p = _final(x1_p, p_all, mods_p[5], fg, tm_m, l_p, 0)
    y_s = _final(x1_s, p_all, mods_s[5], fg, n_s, n_s, t_p // n_s)

    return (y_p.reshape(n_p, l_p, d), y_s.reshape(n_s, l_s, d), sfin_p[None], state_s[None],
            kv_p[0], kv_s[0], kv_p[1], kv_s[1], kv_p[2], kv_s[2])
```

```python
import functools
import math

import jax
import jax.numpy as jnp
from jax import lax
from jax.experimental import pallas as pl
from jax.experimental.pallas import tpu as pltpu

F32 = jnp.float32
BF16 = jnp.bfloat16

PAST_LEN = 16384
GLA_HEADS = 4
GLA_RANK = 16
GLA_TAU = 16.0
GLA_CHUNK = 64
ATT_GROUPS = ((128, 1), (512, 4), (2048, 16))
ATT_HEADS = 4
ATT_HD = 128
ATT_BLOCK = 128
ATT_TOKENS = 2048
ROPE_DIM = ATT_HD // 4
ROPE_THETA = 500000.0
PEER_HEADS = 8
PEER_NKEYS = 128
PEER_DQ = 128
PEER_TOPK = 16
NORM_EPS = 1e-6
GELU_C = math.sqrt(2.0 / math.pi)

LANES = 128
SUBLANES = 8
VMEM_LIMIT = 56 * 1024 * 1024
NEG_BIG = -1e30


def _cparams(*sem):
    return pltpu.CompilerParams(dimension_semantics=sem, vmem_limit_bytes=VMEM_LIMIT)


def _sigmoid(x):
    return 1.0 / (1.0 + jnp.exp(-x))


def _silu(x):
    return x * _sigmoid(x)


def _log_sigmoid(x):
    return jnp.minimum(x, 0.0) - jnp.log(1.0 + jnp.exp(-jnp.abs(x)))


def _rms(x, g):
    return x * lax.rsqrt(jnp.mean(x * x, axis=-1, keepdims=True) + NORM_EPS) * g


def _dot(a, b):
    return jnp.dot(a, b, preferred_element_type=F32)


def _dot_nt(a, b):
    return lax.dot_general(a, b, (((1,), (1,)), ((), ())), preferred_element_type=F32)


def _dot_tn(a, b):
    return lax.dot_general(a, b, (((0,), (0,)), ((), ())), preferred_element_type=F32)


def _mod_kernel(c_ref, w_ref, b_ref, o_ref):
    s = _silu(c_ref[...]).astype(BF16)
    o_ref[...] = _dot(s, w_ref[...].astype(BF16)) + b_ref[...]


def _modulation(c, ada_w, ada_b):
    n, d = c.shape
    n6 = ada_w.shape[1]
    tn = n6 // 4
    return pl.pallas_call(
        _mod_kernel,
        grid=(n6 // tn,),
        in_specs=[pl.BlockSpec((n, d), lambda j: (0, 0)),
                  pl.BlockSpec((d, tn), lambda j: (0, j)),
                  pl.BlockSpec((1, tn), lambda j: (0, j))],
        out_specs=pl.BlockSpec((n, tn), lambda j: (0, j)),
        out_shape=jax.ShapeDtypeStruct((n, n6), F32),
        compiler_params=_cparams("arbitrary"),
        name="modulation",
    )(c, ada_w, ada_b.reshape(1, n6))


TN_IN = 512
VA_TILE0 = 8
ATT_TILE0 = 10


def _rope_tile(z, c, s1, s2):
    reps = z.shape[1] // LANES
    cc = jnp.concatenate([c] * reps, axis=1)
    a = jnp.concatenate([s1] * reps, axis=1)
    b = jnp.concatenate([s2] * reps, axis=1)
    n = z.shape[1]
    return z * cc + pltpu.roll(z, n - ROPE_DIM // 2, 1) * a + pltpu.roll(z, ROPE_DIM // 2, 1) * b


def _inproj_kernel(x_ref, g_ref, sc_ref, sh_ref, w_ref, walr_ref, w2_ref, ab_ref, c_ref, s1_ref, s2_ref,
                   *rest, segs, rope_tiles, split_tiles):
    n_seg = len(segs)
    seg_refs, la_ref = rest[:n_seg], rest[n_seg]
    split_refs, (h_scr, zt_scr) = rest[n_seg + 1:-2], rest[-2:]
    j = pl.program_id(1)
    tm = x_ref.shape[0]

    @pl.when(j == 0)
    def _():
        h = _rms(x_ref[...], g_ref[...]) * (1.0 + sc_ref[...]) + sh_ref[...]
        hb = h.astype(BF16)
        h_scr[...] = hb
        alr = _dot(hb, walr_ref[...])
        p = _dot(alr.astype(BF16), w2_ref[...]) + ab_ref[...]
        la_ref[...] = _log_sigmoid(p) * (1.0 / GLA_TAU)

    z = _dot(h_scr[...], w_ref[...])
    rope = lambda: _rope_tile(z, c_ref[...], s1_ref[...], s2_ref[...])

    for (lo, hi), ref in zip(segs, seg_refs):
        ropes = [t for t in rope_tiles if lo <= t < hi]
        in_seg = jnp.logical_and(j >= lo, j < hi)
        is_rope = functools.reduce(jnp.logical_or, [j == t for t in ropes], False)
        if ropes:
            @pl.when(jnp.logical_and(in_seg, is_rope))
            def _(ref=ref):
                ref[...] = rope().astype(ref.dtype)

        @pl.when(jnp.logical_and(in_seg, jnp.logical_not(is_rope)))
        def _(ref=ref):
            ref[...] = z.astype(ref.dtype)

    for t, slot, dil in split_tiles:
        @pl.when(j == t)
        def _(t=t, slot=slot, dil=dil):
            zr = rope() if t in rope_tiles else z
            for c in range(TN_IN // LANES):
                cs = slice(c * LANES, (c + 1) * LANES)
                zt_scr[c] = zr[:, cs]
                for r in range(dil):
                    rows = zt_scr[c, pl.ds(r, tm // dil, stride=dil), :]
                    split_refs[slot][r, :, cs] = rows.astype(split_refs[slot].dtype)


def _in_proj(x2, g, sc, sh, w_main, w_alr, w2p, a_b, tabs, tm, rows_per_mod, segs, rope_tiles, split_tiles):
    t, d = x2.shape
    r = sc.shape[1]
    wq = a_b.shape[1]
    n_tiles = w_main.shape[1] // TN_IN
    tiles_per_mod = rows_per_mod // tm
    tab_tiles = tabs[0].shape[0] // tm
    nmod = t // rows_per_mod
    mod_spec = pl.BlockSpec((None, r, d), lambda i, j: (i // tiles_per_mod, 0, 0))
    tab_spec = pl.BlockSpec((tm, LANES), lambda i, j: (i % tab_tiles, 0))
    const = lambda shape: pl.BlockSpec(shape, lambda i, j: (0, 0))
    out_specs, out_shape = [], []
    for lo, hi, dt in segs:
        out_specs.append(pl.BlockSpec((tm, TN_IN), lambda i, j, lo=lo, hi=hi: (i, jnp.clip(j - lo, 0, hi - lo - 1))))
        out_shape.append(jax.ShapeDtypeStruct((t, (hi - lo) * TN_IN), dt))
    out_specs.append(pl.BlockSpec((tm, wq), lambda i, j: (i, 0)))
    out_shape.append(jax.ShapeDtypeStruct((t, wq), F32))
    for _, _, dil in split_tiles:
        out_specs.append(pl.BlockSpec((None, dil, tm // dil, TN_IN),
                                      lambda i, j: (i // tiles_per_mod, 0, i % tiles_per_mod, 0)))
        out_shape.append(jax.ShapeDtypeStruct((nmod, dil, rows_per_mod // dil, TN_IN), BF16))
    return pl.pallas_call(
        functools.partial(_inproj_kernel, segs=tuple((lo, hi) for lo, hi, _ in segs),
                          rope_tiles=tuple(rope_tiles), split_tiles=tuple(split_tiles)),
        grid=(t // tm, n_tiles),
        in_specs=[pl.BlockSpec((tm, d), lambda i, j: (i, 0)), const((1, d)), mod_spec, mod_spec,
                  pl.BlockSpec((d, TN_IN), lambda i, j: (0, j)), const((d, LANES)), const((LANES, wq)),
                  const((1, wq)), tab_spec, tab_spec, tab_spec],
        out_specs=out_specs,
        out_shape=out_shape,
        scratch_shapes=[pltpu.VMEM((tm, d), BF16), pltpu.VMEM((TN_IN // LANES, tm, LANES), F32)],
        compiler_params=_cparams("arbitrary", "arbitrary"),
        name="in_proj",
    )(x2, g, sc, sh, w_main, w_alr, w2p, a_b, *tabs)


GLA_EXP_CLAMP = 80.0


def _gla_kernel(q_ref, k_ref, v_ref, ra_ref, la_ref, s0_ref, gn_ref, o_ref, sfin_ref, st_scr, *, rows, dk, dv):
    c = pl.program_id(1)
    nc = pl.num_programs(1)
    ch = GLA_CHUNK

    @pl.when(c == 0)
    def _():
        for h in range(GLA_HEADS):
            st_scr[h] = s0_ref[h].T

    ti = lax.broadcasted_iota(jnp.int32, (ch, ch), 0)
    si = lax.broadcasted_iota(jnp.int32, (ch, ch), 1)
    causal = si <= ti
    tri = causal.astype(BF16)
    scale = dk ** -0.5
    wk = GLA_HEADS * dk
    heads = range(GLA_HEADS)
    ks = [slice(h * dk, (h + 1) * dk) for h in heads]
    vs = [slice(h * dv, (h + 1) * dv) for h in heads]

    def chunk(ci, carry):
        rows = pl.ds(pl.multiple_of(ci * ch, ch), ch)
        la = la_ref[rows, :]
        hi = la.astype(BF16)
        r1 = la - hi.astype(F32)
        mid = r1.astype(BF16)
        lo = (r1 - mid.astype(F32)).astype(BF16)
        b3 = _dot(tri, jnp.concatenate([hi, mid, lo], axis=1))
        b = b3[:, :wk] + b3[:, wk:2 * wk] + b3[:, 2 * wk:]
        bl = b[ch - 1:ch, :]
        q = q_ref[rows, :] * scale
        k = k_ref[rows, :]
        qe = (q * jnp.exp(b)).astype(BF16)
        ke = (k * jnp.exp(jnp.minimum(-b, GLA_EXP_CLAMP))).astype(BF16)
        kh = (k * jnp.exp(bl - b)).astype(BF16)
        dec = jnp.exp(bl)
        v = v_ref[rows, :].astype(BF16)
        att = [jnp.where(causal, _dot_nt(qe[:, ks[h]], ke[:, ks[h]]), 0.0).astype(BF16) for h in heads]
        st = [st_scr[h] for h in heads]
        o = [_dot_nt(qe[:, ks[h]], st[h].astype(BF16)) + _dot(att[h], v[:, vs[h]]) for h in heads]
        for h in heads:
            st_scr[h] = st[h] * dec[:, ks[h]] + _dot_tn(v[:, vs[h]], kh[:, ks[h]])
        for h in heads:
            on = _rms(o[h], gn_ref[:, vs[h]])
            o_ref[rows, vs[h]] = (on * _silu(ra_ref[rows, vs[h]])).astype(o_ref.dtype)
        return carry

    lax.fori_loop(0, rows // ch, chunk, 0)

    @pl.when(c == nc - 1)
    def _():
        for h in range(GLA_HEADS):
            sfin_ref[h] = st_scr[h].T


def _gla_prompt(z, zv, la, s0, gn_g, n, l, rows):
    t = n * l
    dk = la.shape[1] // GLA_HEADS
    dv = gn_g.shape[1] // GLA_HEADS
    wk, wv = GLA_HEADS * dk, GLA_HEADS * dv
    cpb = l // rows
    row = lambda b, c: b * cpb + c
    return pl.pallas_call(
        functools.partial(_gla_kernel, rows=rows, dk=dk, dv=dv),
        grid=(n, cpb),
        in_specs=[pl.BlockSpec((rows, wk), lambda b, c: (row(b, c), 0)),
                  pl.BlockSpec((rows, wk), lambda b, c: (row(b, c), 1)),
                  pl.BlockSpec((rows, wv), lambda b, c: (row(b, c), 0)),
                  pl.BlockSpec((rows, wv), lambda b, c: (row(b, c), 1)),
                  pl.BlockSpec((rows, wk), lambda b, c: (row(b, c), 0)),
                  pl.BlockSpec((None, GLA_HEADS, dk, dv), lambda b, c: (b, 0, 0, 0)),
                  pl.BlockSpec((1, wv), lambda b, c: (0, 0))],
        out_specs=[pl.BlockSpec((rows, wv), lambda b, c: (row(b, c), 0)),
                   pl.BlockSpec((None, GLA_HEADS, dk, dv), lambda b, c: (b, 0, 0, 0))],
        out_shape=[jax.ShapeDtypeStruct((t, wv), BF16), jax.ShapeDtypeStruct((n, GLA_HEADS, dk, dv), F32)],
        scratch_shapes=[pltpu.VMEM((GLA_HEADS, dv, dk), F32)],
        compiler_params=_cparams("arbitrary", "arbitrary"),
        name="gla_prompt",
    )(z, z, zv, z, la, s0, gn_g)


def _gla_step_kernel(q_ref, k_ref, la_ref, v_ref, ra_ref, s_ref, gn_ref, o_ref, so_ref, *, dk):
    sn = jnp.exp(la_ref[...]) * s_ref[...] + k_ref[...] * v_ref[...]
    so_ref[...] = sn
    o = jnp.sum((q_ref[...] * dk ** -0.5) * sn, axis=1, keepdims=True)
    on = _rms(o, gn_ref[...])
    o_ref[...] = (on * _silu(ra_ref[...])).astype(o_ref.dtype)


def _gla_step(q, k, la, v, ra, s, gn_g):
    nb, nh, dk, dv = s.shape
    col = pl.BlockSpec((None, nh, dk, 1), lambda b: (b, 0, 0, 0))
    rowv = pl.BlockSpec((None, nh, 1, dv), lambda b: (b, 0, 0, 0))
    st = pl.BlockSpec((None, nh, dk, dv), lambda b: (b, 0, 0, 0))
    o, so = pl.pallas_call(
        functools.partial(_gla_step_kernel, dk=dk),
        grid=(nb,),
        in_specs=[col, col, col, rowv, rowv, st, pl.BlockSpec((nh, 1, dv), lambda b: (0, 0, 0))],
        out_specs=[rowv, st],
        out_shape=[jax.ShapeDtypeStruct((nb, nh, 1, dv), BF16), jax.ShapeDtypeStruct(s.shape, F32)],
        compiler_params=_cparams("arbitrary"),
        name="gla_step",
    )(q.reshape(nb, nh, dk, 1), k.reshape(nb, nh, dk, 1), la.reshape(nb, nh, dk, 1),
      v.reshape(nb, nh, 1, dv), ra.reshape(nb, nh, 1, dv), s, gn_g.reshape(nh, 1, dv))
    return o.reshape(nb, nh * dv), so


def _band_kernel(q_ref, k_ref, kp_ref, v_ref, vp_ref, o_ref, lse_ref, *, dil, qb):
    i = pl.program_id(1)
    blk = ATT_BLOCK
    qi = lax.broadcasted_iota(jnp.int32, (blk, 2 * blk), 0)
    ki = lax.broadcasted_iota(jnp.int32, (blk, 2 * blk), 1)
    dist = blk + qi - ki
    band = jnp.logical_and(dist >= 0, dist <= ATT_BLOCK)
    band_first = jnp.logical_and(band, ki >= jnp.where(i > 0, 0, blk))
    scale = ATT_HD ** -0.5
    for r in range(dil):
        for sb in range(qb // blk):
            rs = slice(sb * blk, (sb + 1) * blk)
            q = q_ref[r, rs, :].astype(BF16)
            if sb == 0:
                kprev, vprev, mask = kp_ref[r], vp_ref[r], band_first
            else:
                ps = slice((sb - 1) * blk, sb * blk)
                kprev, vprev, mask = k_ref[r, ps, :], v_ref[r, ps, :], band
            kk = jnp.concatenate([kprev, k_ref[r, rs, :]], axis=0).astype(BF16)
            vv = jnp.concatenate([vprev, v_ref[r, rs, :]], axis=0).astype(BF16)
            s = jnp.where(mask, _dot_nt(q, kk) * scale, NEG_BIG)
            m = jnp.max(s, axis=-1, keepdims=True)
            p = jnp.exp(s - m)
            lsum = jnp.sum(p, axis=-1, keepdims=True)
            o = _dot(p.astype(BF16), vv) / lsum
            lse = jnp.broadcast_to(m + jnp.log(lsum), (blk, ATT_HD))
            rows = rs if dil == 1 else pl.ds(sb * blk * dil + r, blk, stride=dil)
            o_ref[rows, :] = o
            lse_ref[rows, :] = lse


def _band_attention(qa, ka, va, cq, ck, cv, n, l, dil):
    lq = l // dil
    hd = ATT_HD
    tokens = min(ATT_TOKENS, l)
    qb = tokens // dil
    nqb = lq // qb
    sub = qb // ATT_BLOCK

    def main(c0):
        return pl.BlockSpec((None, dil, qb, hd), lambda b, i, h: (b, 0, i, c0 + h))

    def prev(c0):
        return pl.BlockSpec((None, dil, ATT_BLOCK, hd), lambda b, i, h: (b, 0, jnp.maximum(i * sub - 1, 0), c0 + h))

    out = pl.BlockSpec((tokens, hd), lambda b, i, h: (b * nqb + i, h))
    shp = jax.ShapeDtypeStruct((n * l, ATT_HEADS * hd), F32)
    return pl.pallas_call(
        functools.partial(_band_kernel, dil=dil, qb=qb),
        grid=(n, nqb, ATT_HEADS),
        in_specs=[main(cq), main(ck), prev(ck), main(cv), prev(cv)],
        out_specs=[out, out],
        out_shape=[shp, shp],
        compiler_params=_cparams("arbitrary", "arbitrary", "arbitrary"),
        name=f"band_attention_d{dil}",
    )(qa, ka, ka, va, va)


def _att_step_kernel(q_ref, kn_ref, vn_ref, c0_ref, c1_ref, c2_ref, o_ref):
    scale = ATT_HD ** -0.5
    outs, lses = [], []
    for g, c_ref in enumerate((c0_ref, c1_ref, c2_ref)):
        q, kn, vn = q_ref[g], kn_ref[g], vn_ref[g]
        s = jnp.sum(c_ref[:, 0] * q[None], axis=-1, keepdims=True) * scale
        sn = jnp.sum(kn * q, axis=-1, keepdims=True) * scale
        m = jnp.maximum(jnp.max(s, axis=0), sn)
        p = jnp.exp(s - m[None])
        pn = jnp.exp(sn - m)
        lsum = jnp.sum(p, axis=0) + pn
        outs.append((jnp.sum(p * c_ref[:, 1], axis=0) + pn * vn) / lsum)
        lses.append(m + jnp.log(lsum))
    mx = functools.reduce(jnp.maximum, lses)
    es = [jnp.exp(x - mx) for x in lses]
    inv = 1.0 / functools.reduce(lambda a, b: a + b, es)
    o_ref[...] = functools.reduce(lambda a, b: a + b, [e * inv * o for e, o in zip(es, outs)]).astype(o_ref.dtype)


def _att_step(qs, ks, vs, caches):
    nb = qs.shape[0]
    views, cspecs = [], []
    for (window, dil), cch in zip(ATT_GROUPS, caches):
        assert cch.shape[1] == window, "sample step expects full caches"
        views.append(cch.reshape((nb, window // dil, dil) + cch.shape[2:]))
        cspecs.append(pl.BlockSpec((None, ATT_BLOCK, None) + cch.shape[2:], lambda b: (b, 0, 0, 0, 0, 0)))
    row = pl.BlockSpec((None,) + qs.shape[1:], lambda b: (b, 0, 0, 0))
    o = pl.pallas_call(
        _att_step_kernel,
        grid=(nb,),
        in_specs=[row] * 3 + cspecs,
        out_specs=pl.BlockSpec((None, ATT_HEADS, ATT_HD), lambda b: (b, 0, 0)),
        out_shape=jax.ShapeDtypeStruct((nb, ATT_HEADS, ATT_HD), BF16),
        compiler_params=_cparams("arbitrary"),
        name="att_step",
    )(qs, ks, vs, *views)
    return o.reshape(nb, ATT_HEADS * ATT_HD)


def _kv_shift_kernel(a_ref, nxt_ref, new_ref, o_ref, *, rows):
    i = pl.program_id(1)
    last = pl.num_programs(1) - 1
    o_ref[0:rows - 1] = a_ref[1:rows]

    @pl.when(i == last)
    def _():
        o_ref[rows - 1:rows] = new_ref[...]

    @pl.when(i != last)
    def _():
        o_ref[rows - 1:rows] = nxt_ref[...]


def _kv_shift(cache, new_row):
    nb, wb = cache.shape[0], cache.shape[1]
    rw = cache.shape[2] * cache.shape[3] * cache.shape[4]
    sub = rw // LANES
    rows = min(wb, 512)
    cv = cache.reshape(nb, wb, sub, LANES)
    out = pl.pallas_call(
        functools.partial(_kv_shift_kernel, rows=rows),
        grid=(nb, wb // rows),
        in_specs=[pl.BlockSpec((None, rows, sub, LANES), lambda b, i: (b, i, 0, 0)),
                  pl.BlockSpec((None, 1, sub, LANES), lambda b, i: (b, jnp.minimum((i + 1) * rows, wb - 1), 0, 0)),
                  pl.BlockSpec((None, 1, sub, LANES), lambda b, i: (b, 0, 0, 0))],
        out_specs=pl.BlockSpec((None, rows, sub, LANES), lambda b, i: (b, i, 0, 0)),
        out_shape=jax.ShapeDtypeStruct(cv.shape, cache.dtype),
        compiler_params=_cparams("arbitrary", "arbitrary"),
        name=f"kv_shift_w{wb}",
    )(cv, cv, new_row.reshape(nb, 1, sub, LANES))
    return out.reshape(cache.shape)


MERGE_PART = 256


def _merge_kernel(*refs, n_att, n_alias):
    oa_ref = refs[0]
    att = refs[1:1 + n_att]
    (gza_ref, gzb_ref, x_ref, g1_ref, sc2_ref, sh2_ref, n2_ref, wa_ref, wb_ref, wo_ref,
     wq_ref) = refs[1 + n_att:12 + n_att]
    x1_ref, h2_ref, pq_ref = refs[12 + n_att + n_alias:]
    tm = x_ref.shape[0]
    pr = min(MERGE_PART, tm)
    parts = [slice(k * pr, (k + 1) * pr) for k in range(tm // pr)]
    mod = lambda ref, part: ref[...] if ref.shape[0] == 1 else ref[part, :]
    add = lambda a, b: a + b

    def attention_out(part):
        if n_att == 1:
            return att[0][part, :]
        ng = n_att // 2
        ls = [r[part, :] for r in att[ng:]]
        mx = functools.reduce(jnp.maximum, ls)
        es = [jnp.exp(x - mx) for x in ls]
        inv = 1.0 / functools.reduce(add, es)
        return functools.reduce(add, [e * inv * r[part, :] for e, r in zip(es, att[:ng])]).astype(BF16)

    ob = [attention_out(p) for p in parts]
    ma = [_dot(oa_ref[p, :], wa_ref[...]) for p in parts]
    mb = [_dot(o, wb_ref[...]) for o in ob]
    merged = [(_sigmoid(gza_ref[p, :]) * a + _sigmoid(gzb_ref[p, :]) * b).astype(BF16)
              for p, a, b in zip(parts, ma, mb)]
    y = [_dot(m, wo_ref[...]) for m in merged]
    h2 = []
    for p, yy in zip(parts, y):
        x1 = x_ref[p, :] + mod(g1_ref, p) * yy
        x1_ref[p, :] = x1
        h2.append((_rms(x1, n2_ref[...]) * (1.0 + mod(sc2_ref, p)) + mod(sh2_ref, p)).astype(BF16))
    for p, h in zip(parts, h2):
        h2_ref[p, :] = h
        pq_ref[p, :] = _dot(h, wq_ref[...])
    if h2_ref.shape[0] > tm:
        h2_ref[tm:] = jnp.zeros((h2_ref.shape[0] - tm, h2_ref.shape[1]), h2_ref.dtype)
        pq_ref[tm:] = jnp.zeros((pq_ref.shape[0] - tm, pq_ref.shape[1]), pq_ref.dtype)


def _merge(oa, att, z, x2, g1, sc2, sh2, n2g, wa, wb, wo, wq, tm, rows_per_mod, gza_blk, gzb_blk,
           peer_rows, peer_tile, peer_blk0, alias=()):
    t, d = x2.shape
    r = g1.shape[1]
    wqn = wq.shape[1]
    tiles_per_mod = rows_per_mod // tm
    tok = lambda w: pl.BlockSpec((tm, w), lambda i: (i, 0))
    mod = pl.BlockSpec((None, r, d), lambda i: (i // tiles_per_mod, 0, 0))
    const = lambda a: pl.BlockSpec(a.shape, lambda i: (0, 0), pipeline_mode=pl.Buffered(1))
    peer = lambda w: pl.BlockSpec((peer_tile, w), lambda i: (peer_blk0 + i, 0))
    n_in = 12 + len(att)
    return pl.pallas_call(
        functools.partial(_merge_kernel, n_att=len(att), n_alias=len(alias)),
        grid=(t // tm,),
        in_specs=[tok(oa.shape[1])] + [tok(a.shape[1]) for a in att]
                 + [pl.BlockSpec((tm, d), lambda i: (i, gza_blk)), pl.BlockSpec((tm, d), lambda i: (i, gzb_blk)),
                    tok(d), mod, mod, mod, const(n2g), const(wa), const(wb), const(wo), const(wq)]
                 + [pl.BlockSpec(memory_space=pl.ANY)] * len(alias),
        out_specs=[tok(d), peer(d), peer(wqn)],
        out_shape=[jax.ShapeDtypeStruct((t, d), F32), jax.ShapeDtypeStruct((peer_rows, d), BF16),
                   jax.ShapeDtypeStruct((peer_rows, wqn), F32)],
        input_output_aliases={n_in + k: 1 + k for k in range(len(alias))},
        compiler_params=_cparams("arbitrary"),
        name="merge",
    )(oa, *att, z, z, x2, g1, sc2, sh2, n2g, wa, wb, wo, wq, *alias)


def _strict_max_below(rows, prev):
    m = None
    for x in rows:
        y = jnp.where(x < prev, x, NEG_BIG) if prev is not None else x
        m = y if m is None else jnp.maximum(m, y)
    return m


def _oddeven_merge(lo, hi, r):
    step = r * 2
    if step < hi - lo:
        yield from _oddeven_merge(lo, hi, step)
        yield from _oddeven_merge(lo + r, hi, step)
        yield from [(i, i + r) for i in range(lo + r, hi - r, step)]
    else:
        yield (lo, lo + r)


def _oddeven_merge_sort(lo, hi):
    if hi - lo >= 1:
        mid = lo + (hi - lo) // 2
        yield from _oddeven_merge_sort(lo, mid)
        yield from _oddeven_merge_sort(mid + 1, hi)
        yield from _oddeven_merge(lo, hi, 1)


def _exchange(p, i, j):
    p[i], p[j] = jnp.maximum(p[i], p[j]), jnp.minimum(p[i], p[j])


def _topk_desc(s, k):
    assert s.shape[0] == k * SUBLANES and k & (k - 1) == 0
    p = [s[i * SUBLANES:(i + 1) * SUBLANES] for i in range(k)]
    for i, j in _oddeven_merge_sort(0, k - 1):
        _exchange(p, i, j)
    shift = SUBLANES // 2
    while shift:
        q = [pltpu.roll(v, shift, 0) for v in p]
        p = [jnp.maximum(p[i], q[k - 1 - i]) for i in range(k)]
        stride = k // 2
        while stride:
            for i in range(k):
                if not i & stride:
                    _exchange(p, i, i + stride)
            stride //= 2
        shift //= 2
    return [v[0:1] for v in p]


def _route_kernel(pq_ref, k1_ref, k2_ref, n1_ref, e1_ref, r2_ref, e2_ref, s1_scr, s2_scr):
    nk = PEER_NKEYS
    pq = pq_ref[...].astype(BF16)
    s1_scr[...] = _dot_nt(k1_ref[...], pq)
    s2_scr[...] = _dot_nt(k2_ref[...], pq)
    v1h, v2h = [], []
    for h in range(PEER_HEADS):
        v1h.append(_topk_desc(s1_scr[h * nk:(h + 1) * nk], PEER_TOPK))
        v2h.append(_topk_desc(s2_scr[h * nk:(h + 1) * nk], PEER_TOPK))
    v1 = [jnp.concatenate([v1h[h][k] for h in range(PEER_HEADS)], axis=0) for k in range(PEER_TOPK)]
    v2 = [jnp.concatenate([v2h[h][k] for h in range(PEER_HEADS)], axis=0) for k in range(PEER_TOPK)]
    pairs = [(i, j) for i in range(PEER_TOPK) for j in range(PEER_TOPK) if (i + 1) * (j + 1) <= PEER_TOPK]
    cands = [v1[i] + v2[j] for i, j in pairs]
    tau = None
    for _ in range(PEER_TOPK):
        tau = _strict_max_below(cands, tau)
    cmax = v1[0] + v2[0]
    zsum = None
    counts = [None] * PEER_TOPK
    for (i, _), cnd in zip(pairs, cands):
        sel = cnd >= tau
        e = jnp.where(sel, jnp.exp(cnd - cmax), 0.0)
        zsum = e if zsum is None else zsum + e
        one = jnp.where(sel, 1.0, 0.0)
        counts[i] = one if counts[i] is None else counts[i] + one
    zinv = 1.0 / zsum
    for h in range(PEER_HEADS):
        a = s1_scr[h * nk:(h + 1) * nk]
        b = s2_scr[h * nk:(h + 1) * nk]
        n1 = jnp.zeros_like(a)
        r2 = jnp.full_like(b, float(PEER_TOPK))
        for k in range(PEER_TOPK):
            n1 = jnp.where(a == v1h[h][k], counts[k][h:h + 1, :], n1)
            r2 = jnp.where(b == v2h[h][k], float(k), r2)
        n1_ref[h] = n1
        e1_ref[h] = jnp.exp(a - v1h[h][0]) * zinv[h:h + 1, :]
        r2_ref[h] = r2.astype(r2_ref.dtype)
        e2_ref[h] = jnp.exp(b - v2h[h][0]).astype(e2_ref.dtype)


def _peer_route(pq, k1big, k2big, tt):
    t, w = pq.shape
    nh, nk = PEER_HEADS, PEER_NKEYS
    sspec = pl.BlockSpec((nh, nk, tt), lambda i: (0, 0, i))
    shp = lambda dt: jax.ShapeDtypeStruct((nh, nk, t), dt)
    return pl.pallas_call(
        _route_kernel,
        grid=(t // tt,),
        in_specs=[pl.BlockSpec((tt, w), lambda i: (i, 0)),
                  pl.BlockSpec(k1big.shape, lambda i: (0, 0)), pl.BlockSpec(k2big.shape, lambda i: (0, 0))],
        out_specs=[sspec] * 4,
        out_shape=[shp(F32), shp(F32), shp(BF16), shp(BF16)],
        scratch_shapes=[pltpu.VMEM((nh * nk, tt), F32)] * 2,
        compiler_params=_cparams("arbitrary"),
        name="peer_route",
    )(pq, k1big, k2big)


def _gelu_tanh(x):
    return 0.5 * x * (1.0 + jnp.tanh(GELU_C * (x + 0.044715 * (x * x * x))))


BF16_ROWS = 16
GATE_LANES = 256
PEER_PART = 256
PEER_TILE = 2048


def _gelu_tanh_bf16(x):
    hx = 0.5 * x
    return hx + hx * jnp.tanh(x * (GELU_C + (GELU_C * 0.044715) * (x * x)))


def _bf16_rows(row):
    return jnp.broadcast_to(row, (BF16_ROWS, row.shape[1])).astype(BF16)


def _peer_dense_kernel(h2_ref, u_ref, vt_ref, n1_ref, e1_ref, r2_ref, e2_ref, o_ref, acc_scr, wg_scr, st_scr,
                       *, et):
    e = pl.program_id(1)
    nk = PEER_NKEYS
    tt = h2_ref.shape[0]

    @pl.when(e == 0)
    def _():
        acc_scr[...] = jnp.zeros_like(acc_scr)

    zero = jnp.zeros((), BF16)
    groups = nk // BF16_ROWS

    def scores(part):
        st_scr[part, :] = _dot_nt(u_ref[part, :], h2_ref[...]).astype(BF16)

    def gates(ii):
        rs = slice(ii * nk, (ii + 1) * nk)
        for lc in range(tt // GATE_LANES):
            ls = slice(lc * GATE_LANES, (lc + 1) * GATE_LANES)
            w = None
            for h in range(PEER_HEADS):
                n1 = _bf16_rows(n1_ref[h, ii:ii + 1, ls])
                e1 = _bf16_rows(e1_ref[h, ii:ii + 1, ls])
                r2 = r2_ref[h, :, ls].reshape(groups, BF16_ROWS, GATE_LANES)
                e2 = e2_ref[h, :, ls].reshape(groups, BF16_ROWS, GATE_LANES)
                c = jnp.where(r2 < n1[None], e2, zero) * e1[None]
                w = c if w is None else w + c
            wg_scr[rs, ls] = w.reshape(nk, GATE_LANES) * _gelu_tanh_bf16(st_scr[rs, ls])

    def mix(part):
        return _dot(vt_ref[:, part], wg_scr[part, :])

    per = PEER_PART // nk
    parts = [slice(p * PEER_PART, (p + 1) * PEER_PART) for p in range(et // PEER_PART)]
    out = None
    scores(parts[0])
    for p, part in enumerate(parts):
        if p + 1 < len(parts):
            scores(parts[p + 1])
        for ii in range(p * per, (p + 1) * per):
            gates(ii)
        if p > 0:
            d = mix(parts[p - 1])
            out = d if out is None else out + d
    d = mix(parts[-1])
    acc_scr[...] += d if out is None else out + d

    @pl.when(e == pl.num_programs(1) - 1)
    def _():
        o_ref[...] = acc_scr[...].T


def _peer_dense(h2, u_b, vt_b, n1, e1, r2, e2, tt, et):
    t, d = h2.shape
    ne = u_b.shape[0]
    nh, nk = PEER_HEADS, PEER_NKEYS
    rowside = pl.BlockSpec((nh, et // nk, tt), lambda i, e: (0, e, i))
    colside = pl.BlockSpec((nh, nk, tt), lambda i, e: (0, 0, i))
    return pl.pallas_call(
        functools.partial(_peer_dense_kernel, et=et),
        grid=(t // tt, ne // et),
        in_specs=[pl.BlockSpec((tt, d), lambda i, e: (i, 0)),
                  pl.BlockSpec((et, d), lambda i, e: (e, 0)),
                  pl.BlockSpec((d, et), lambda i, e: (0, e)),
                  rowside, rowside, colside, colside],
        out_specs=pl.BlockSpec((tt, d), lambda i, e: (i, 0)),
        out_shape=jax.ShapeDtypeStruct((t, d), F32),
        scratch_shapes=[pltpu.VMEM((d, tt), F32), pltpu.VMEM((et, tt), BF16), pltpu.VMEM((et, tt), BF16)],
        compiler_params=_cparams("arbitrary", "arbitrary"),
        name="peer_dense",
    )(h2, u_b, vt_b, n1, e1, r2, e2)


def _final_kernel(x1_ref, p_ref, g2_ref, fg_ref, y_ref):
    y_ref[...] = _rms(x1_ref[...] + g2_ref[...] * p_ref[...], fg_ref[...])


def _final(x1, p, g2, fg, tm, rows_per_mod, p_blk0):
    t, d = x1.shape
    r = g2.shape[1]
    tiles_per_mod = rows_per_mod // tm
    tok = pl.BlockSpec((tm, d), lambda i: (i, 0))
    return pl.pallas_call(
        _final_kernel,
        grid=(t // tm,),
        in_specs=[tok, pl.BlockSpec((tm, d), lambda i: (p_blk0 + i, 0)),
                  pl.BlockSpec((None, r, d), lambda i: (i // tiles_per_mod, 0, 0)),
                  pl.BlockSpec((1, d), lambda i: (0, 0))],
        out_specs=tok,
        out_shape=jax.ShapeDtypeStruct((t, d), F32),
        compiler_params=_cparams("arbitrary"),
        name="final_norm",
    )(x1, p, g2, fg)


def _rope_tables(pos):
    half = ROPE_DIM // 2
    inv_freq = ROPE_THETA ** (-jnp.arange(half, dtype=F32) / half)
    ang = pos.astype(F32)[:, None] * inv_freq[None, :]
    cos, sin = jnp.cos(ang), jnp.sin(ang)
    n = pos.shape[0]
    ones = jnp.ones((n, LANES - ROPE_DIM), F32)
    zeros = jnp.zeros((n, LANES - ROPE_DIM), F32)
    zh = jnp.zeros((n, half), F32)
    return (jnp.concatenate([cos, cos, ones], axis=1),
            jnp.concatenate([-sin, zh, zeros], axis=1),
            jnp.concatenate([zh, sin, zeros], axis=1))


def _block_diag_keys(keys, half):
    nk, dh = keys.shape
    eye = jnp.eye(PEER_HEADS, dtype=keys.dtype)
    blk = jnp.zeros((PEER_HEADS, nk, PEER_HEADS, 2, dh), keys.dtype)
    blk = blk.at[:, :, :, half, :].set(eye[:, None, :, None] * keys[None, :, None, :])
    return blk.reshape(PEER_HEADS * nk, PEER_HEADS * 2 * dh).astype(BF16)


def _pick_tile(n, pref):
    t = min(pref, n)
    while n % t:
        t //= 2
    return t


def kernel(x_prompt, x_sample, c_prompt, c_sample, state_gla, cache_kv_w128, cache_kv_w512, cache_kv_w2048,
           ada_w, ada_b, norm1_g, w_in, gla_a_w2, gla_a_b, gla_gn_g, w_branch_a, w_branch_b, w_out,
           norm2_g, peer_wq, peer_k1, peer_k2, peer_u, peer_v, final_g):
    depth = ada_w.shape[0]
    assert depth == 1, "single-layer trunk"
    n_p, l_p, d = x_prompt.shape
    n_s, l_s, _ = x_sample.shape
    assert l_s == 1, "sample group decodes one token per sequence"
    caches = (cache_kv_w128[0], cache_kv_w512[0], cache_kv_w2048[0])
    n_groups = len(ATT_GROUPS)

    w_qa = gla_a_w2.shape[2]
    w_va = gla_gn_g.shape[1]
    w_ob = ATT_HEADS * ATT_HD
    w_qb = n_groups * w_ob
    assert w_qa == TN_IN and w_va == 2 * TN_IN and d == 2 * TN_IN and w_ob == TN_IN
    widths = (w_qa, w_qa, w_va, w_va, GLA_RANK, w_qb, w_qb, w_qb, d, d)
    offs = [0]
    for w in widths:
        offs.append(offs[-1] + w)
    seg = lambda i: w_in[0][:, offs[i]:offs[i + 1]]
    att_col = lambda g, k: seg(5 + k)[:, g * w_ob:(g + 1) * w_ob]
    att_cols = [att_col(g, k) for g in range(n_groups) for k in range(3)]
    w_main = jnp.concatenate([seg(0), seg(1), seg(3), seg(8), seg(9), seg(2)] + att_cols, axis=1).astype(BF16)
    w_kv = jnp.concatenate([att_col(g, k) for g in range(n_groups) for k in (1, 2)], axis=1).astype(BF16)
    n_tiles = w_main.shape[1] // TN_IN
    ra_off, gza_blk, gzb_blk = 2 * w_qa, 2, 3
    w_alr = jnp.pad(seg(4), ((0, 0), (0, LANES - GLA_RANK))).astype(BF16)
    w2p = jnp.pad(gla_a_w2[0], ((0, LANES - GLA_RANK), (0, 0))).astype(BF16)
    a_b = gla_a_b[0].reshape(1, w_qa)
    g1n = norm1_g[0].reshape(1, d)
    gn_g = gla_gn_g[0].reshape(1, w_va)
    n2g = norm2_g[0].reshape(1, d)
    wa, wb = w_branch_a[0].astype(BF16), w_branch_b[0].astype(BF16)
    wo, wq = w_out[0].astype(BF16), peer_wq[0].astype(BF16)
    q_tile = lambda g: ATT_TILE0 + 3 * g
    rope_tiles = [q_tile(g) + k for g in range(n_groups) for k in (0, 1)]

    n_mod = n_p + n_s
    n_pad = -(-n_mod // SUBLANES) * SUBLANES
    c_all = jnp.concatenate([c_prompt, c_sample, jnp.zeros((n_pad - n_mod, d), F32)], axis=0)
    mod = _modulation(c_all, ada_w[0], ada_b[0])
    mods_p = [mod[:n_p, i * d:(i + 1) * d].reshape(n_p, 1, d) for i in range(6)]
    mods_s = [mod[n_p:n_mod, i * d:(i + 1) * d].reshape(1, n_s, d) for i in range(6)]

    t_p = n_p * l_p
    tt = _pick_tile(t_p, 512)
    t_all = t_p + (-(-n_s // tt)) * tt
    assert t_all - t_p == tt

    xp2 = x_prompt.reshape(t_p, d)
    tm_p = _pick_tile(l_p, 1024)
    pos_p = jnp.arange(l_p, dtype=jnp.int32)
    split = []
    for g, (window, dil) in enumerate(ATT_GROUPS):
        assert window // dil == ATT_BLOCK
        if dil > 1:
            split += [(q_tile(g) + k, len(split) + k, dil) for k in range(3)]
    first_split = min([t for t, _, _ in split], default=n_tiles)
    outs = _in_proj(xp2, g1n, mods_p[1], mods_p[0], w_main, w_alr, w2p, a_b, _rope_tables(pos_p), tm_p, l_p,
                    ((0, VA_TILE0, F32), (VA_TILE0, first_split, BF16)), rope_tiles, split)
    z_p, zb_p, la_p, split_p = outs[0], outs[1], outs[2], outs[3:]
    s0_p = jnp.zeros((n_p,) + state_gla.shape[2:], F32)
    oa_p, sfin_p = _gla_prompt(z_p, zb_p, la_p, s0_p, gn_g, n_p, l_p, _pick_tile(l_p, 256))
    zb4 = zb_p.reshape(n_p, 1, l_p, zb_p.shape[1])
    att_o, att_l = [], []
    for g, (window, dil) in enumerate(ATT_GROUPS):
        if dil > 1:
            qa, ka, va = [split_p[slot] for t, slot, _ in split if q_tile(g) <= t < q_tile(g) + 3]
            o_g, lse_g = _band_attention(qa, ka, va, 0, 0, 0, n_p, l_p, dil)
        else:
            c0 = lambda k: (q_tile(g) + k - VA_TILE0) * (TN_IN // ATT_HD)
            o_g, lse_g = _band_attention(zb4, zb4, zb4, c0(0), c0(1), c0(2), n_p, l_p, dil)
        att_o.append(o_g)
        att_l.append(lse_g)
    keep_max = min(max(w for w, _ in ATT_GROUPS), l_p)
    x_tail = x_prompt[:, l_p - keep_max:].reshape(n_p * keep_max, d)
    kv_tiles = 2 * n_groups
    z_kv = _in_proj(x_tail, g1n, mods_p[1], mods_p[0], w_kv, w_alr, w2p, a_b, _rope_tables(pos_p[l_p - keep_max:]),
                    _pick_tile(keep_max, 1024), keep_max, ((0, kv_tiles, F32),), range(0, kv_tiles, 2), [])[0]
    z_kv = z_kv.reshape(n_p, keep_max, kv_tiles, ATT_HEADS, ATT_HD)
    kv_p = []
    for g, (window, dil) in enumerate(ATT_GROUPS):
        keep = min(window, l_p)
        kv_p.append(z_kv[:, keep_max - keep:, 2 * g:2 * g + 2][None])
    tm_m = _pick_tile(l_p, 512)
    x1_p, h2_all, pq_all = _merge(oa_p, att_o + att_l, z_p, xp2, mods_p[2], mods_p[4], mods_p[3], n2g,
                                  wa, wb, wo, wq, tm_m, l_p, gza_blk, gzb_blk, t_all, tm_m, 0)

    xs2 = x_sample.reshape(n_s, d)
    tabs_s = _rope_tables(jnp.full((n_s,), PAST_LEN, dtype=jnp.int32))
    z_s, la_s = _in_proj(xs2, g1n, mods_s[1], mods_s[0], w_main, w_alr, w2p, a_b, tabs_s, n_s, n_s,
                         ((0, n_tiles, F32),), rope_tiles, [])
    col = lambda tile: z_s[:, tile * TN_IN:(tile + 1) * TN_IN]
    oa_s, state_s = _gla_step(col(0), col(1), la_s, z_s[:, VA_TILE0 * TN_IN:VA_TILE0 * TN_IN + w_va],
                              z_s[:, ra_off:ra_off + w_va], state_gla[0], gn_g)
    stack = lambda k: jnp.stack([col(q_tile(g) + k) for g in range(n_groups)], axis=1).reshape(
        n_s, n_groups, ATT_HEADS, ATT_HD)
    ob_s = _att_step(stack(0), stack(1), stack(2), caches)
    x1_s, h2_all, pq_all = _merge(oa_s, [ob_s], z_s, xs2, mods_s[2], mods_s[4], mods_s[3], n2g,
                                  wa, wb, wo, wq, n_s, n_s, gza_blk, gzb_blk, t_all, tt, t_p // tt,
                                  alias=(h2_all, pq_all))
    kv_s = []
    for g, cch in enumerate(caches):
        new_row = jnp.concatenate([col(q_tile(g) + 1), col(q_tile(g) + 2)], axis=1)
        kv_s.append(_kv_shift(cch, new_row)[None])

    k1big = _block_diag_keys(peer_k1[0], 0)
    k2big = _block_diag_keys(peer_k2[0], 1)
    n1, e1, r2, e2 = _peer_route(pq_all, k1big, k2big, _pick_tile(tt, 256))
    u_b = peer_u[0].astype(BF16)
    vt_b = peer_v[0].astype(BF16).T
    p_all = _peer_dense(h2_all, u_b, vt_b, n1, e1, r2, e2, tt, PEER_TILE)

    fg = final_g.reshape(1, d)
    y_p = _final(x1_p, p_all, mods_p[5], fg, tm_m, l_p, 0)
    y_s = _final(x1_s, p_all, mods_s[5], fg, n_s, n_s, t_p // n_s)

    return (y_p.reshape(n_p, l_p, d), y_s.reshape(n_s, l_s, d), sfin_p[None], state_s[None],
            kv_p[0], kv_s[0], kv_p[1], kv_s[1], kv_p[2], kv_s[2])
```

```python
import functools
import math

import jax
import jax.numpy as jnp
from jax import lax
from jax.experimental import pallas as pl
from jax.experimental.pallas import tpu as pltpu

F32 = jnp.float32
BF16 = jnp.bfloat16

PAST_LEN = 16384
GLA_HEADS = 4
GLA_RANK = 16
GLA_TAU = 16.0
GLA_CHUNK = 64
ATT_GROUPS = ((128, 1), (512, 4), (2048, 16))
ATT_HEADS = 4
ATT_HD = 128
ATT_BLOCK = 128
ATT_TOKENS = 2048
ROPE_DIM = ATT_HD // 4
ROPE_THETA = 500000.0
PEER_HEADS = 8
PEER_NKEYS = 128
PEER_DQ = 128
PEER_TOPK = 16
NORM_EPS = 1e-6
GELU_C = math.sqrt(2.0 / math.pi)

LANES = 128
SUBLANES = 8
VMEM_LIMIT = 56 * 1024 * 1024
NEG_BIG = -1e30


def _cparams(*sem):
    return pltpu.CompilerParams(dimension_semantics=sem, vmem_limit_bytes=VMEM_LIMIT)


def _sigmoid(x):
    return 1.0 / (1.0 + jnp.exp(-x))


def _silu(x):
    return x * _sigmoid(x)


def _log_sigmoid(x):
    return jnp.minimum(x, 0.0) - jnp.log(1.0 + jnp.exp(-jnp.abs(x)))


def _rms(x, g):
    return x * lax.rsqrt(jnp.mean(x * x, axis=-1, keepdims=True) + NORM_EPS) * g


def _dot(a, b):
    return jnp.dot(a, b, preferred_element_type=F32)


def _dot_nt(a, b):
    return lax.dot_general(a, b, (((1,), (1,)), ((), ())), preferred_element_type=F32)


def _dot_tn(a, b):
    return lax.dot_general(a, b, (((0,), (0,)), ((), ())), preferred_element_type=F32)


def _mod_kernel(c_ref, w_ref, b_ref, o_ref):
    s = _silu(c_ref[...]).astype(BF16)
    o_ref[...] = _dot(s, w_ref[...].astype(BF16)) + b_ref[...]


def _modulation(c, ada_w, ada_b):
    n, d = c.shape
    n6 = ada_w.shape[1]
    tn = n6 // 4
    return pl.pallas_call(
        _mod_kernel,
        grid=(n6 // tn,),
        in_specs=[pl.BlockSpec((n, d), lambda j: (0, 0)),
                  pl.BlockSpec((d, tn), lambda j: (0, j)),
                  pl.BlockSpec((1, tn), lambda j: (0, j))],
        out_specs=pl.BlockSpec((n, tn), lambda j: (0, j)),
        out_shape=jax.ShapeDtypeStruct((n, n6), F32),
        compiler_params=_cparams("arbitrary"),
        name="modulation",
    )(c, ada_w, ada_b.reshape(1, n6))


TN_IN = 512
VA_TILE0 = 8
ATT_TILE0 = 10


def _rope_tile(z, c, s):
    reps = z.shape[1] // LANES
    n = z.shape[1]
    half = ROPE_DIM // 2
    lane = lax.broadcasted_iota(jnp.int32, (1, LANES), 1)
    first = jnp.concatenate([lane < half] * reps, axis=1)
    partner = jnp.where(first, pltpu.roll(z, n - half, 1), pltpu.roll(z, half, 1))
    return z * jnp.concatenate([c] * reps, axis=1) + partner * jnp.concatenate([s] * reps, axis=1)


def _inproj_kernel(x_ref, g_ref, sc_ref, sh_ref, w_ref, walr_ref, w2_ref, ab_ref, *rest,
                   segs, rope_tiles, split_tiles):
    dils = sorted({dil for _, _, dil in split_tiles})
    n_seg, n_tab = len(segs), 2 * (1 + len(dils))
    tabs = [(rest[2 * v], rest[2 * v + 1]) for v in range(1 + len(dils))]
    seg_refs, la_ref = rest[n_tab:n_tab + n_seg], rest[n_tab + n_seg]
    split_refs, (h_scr, hf_scr) = rest[n_tab + n_seg + 1:-2], rest[-2:]
    j = pl.program_id(1)
    tm, d = x_ref.shape

    @pl.when(j == 0)
    def _():
        h = _rms(x_ref[...], g_ref[...]) * (1.0 + sc_ref[...]) + sh_ref[...]
        hb = h.astype(BF16)
        h_scr[0] = hb
        alr = _dot(hb, walr_ref[...])
        p = _dot(alr.astype(BF16), w2_ref[...]) + ab_ref[...]
        la_ref[...] = _log_sigmoid(p) * (1.0 / GLA_TAU)
        for c in range((d // LANES) if dils else 0):
            cs = slice(c * LANES, (c + 1) * LANES)
            hf_scr[c] = h[:, cs]
            for v, dil in enumerate(dils, 1):
                per = tm // dil
                for r in range(dil):
                    h_scr[v, r * per:(r + 1) * per, cs] = hf_scr[c, pl.ds(r, per, stride=dil), :].astype(BF16)

    def tile(v, roped):
        z = _dot(h_scr[v], w_ref[...])
        return _rope_tile(z, tabs[v][0][...], tabs[v][1][...]) if roped else z

    for (lo, hi), ref in zip(segs, seg_refs):
        ropes = [t for t in rope_tiles if lo <= t < hi]
        in_seg = jnp.logical_and(j >= lo, j < hi)
        is_rope = functools.reduce(jnp.logical_or, [j == t for t in ropes], False)
        if ropes:
            @pl.when(jnp.logical_and(in_seg, is_rope))
            def _(ref=ref):
                ref[...] = tile(0, True).astype(ref.dtype)

        @pl.when(jnp.logical_and(in_seg, jnp.logical_not(is_rope)))
        def _(ref=ref):
            ref[...] = tile(0, False).astype(ref.dtype)

    for t, slot, dil in split_tiles:
        @pl.when(j == t)
        def _(t=t, slot=slot, dil=dil):
            z = tile(1 + dils.index(dil), t in rope_tiles)
            split_refs[slot][...] = z.reshape(dil, tm // dil, z.shape[1]).astype(split_refs[slot].dtype)


def _in_proj(x2, g, sc, sh, w_main, w_alr, w2p, a_b, pos, tm, rows_per_mod, segs, rope_tiles, split_tiles):
    t, d = x2.shape
    r = sc.shape[1]
    wq = a_b.shape[1]
    n_tiles = w_main.shape[1] // TN_IN
    tiles_per_mod = rows_per_mod // tm
    tab_tiles = pos.shape[0] // tm
    nmod = t // rows_per_mod
    dils = sorted({dil for _, _, dil in split_tiles})
    tabs = list(_rope_tables(pos))
    for dil in dils:
        tabs += _rope_tables(pos.reshape(tab_tiles, tm // dil, dil).transpose(0, 2, 1).reshape(-1))
    mod_spec = pl.BlockSpec((None, r, d), lambda i, j: (i // tiles_per_mod, 0, 0))
    tab_spec = pl.BlockSpec((tm, LANES), lambda i, j: (i % tab_tiles, 0))
    const = lambda shape: pl.BlockSpec(shape, lambda i, j: (0, 0))
    out_specs, out_shape = [], []
    for lo, hi, dt in segs:
        out_specs.append(pl.BlockSpec((tm, TN_IN), lambda i, j, lo=lo, hi=hi: (i, jnp.clip(j - lo, 0, hi - lo - 1))))
        out_shape.append(jax.ShapeDtypeStruct((t, (hi - lo) * TN_IN), dt))
    out_specs.append(pl.BlockSpec((tm, wq), lambda i, j: (i, 0)))
    out_shape.append(jax.ShapeDtypeStruct((t, wq), F32))
    for _, _, dil in split_tiles:
        out_specs.append(pl.BlockSpec((None, dil, tm // dil, TN_IN),
                                      lambda i, j: (i // tiles_per_mod, 0, i % tiles_per_mod, 0)))
        out_shape.append(jax.ShapeDtypeStruct((nmod, dil, rows_per_mod // dil, TN_IN), BF16))
    return pl.pallas_call(
        functools.partial(_inproj_kernel, segs=tuple((lo, hi) for lo, hi, _ in segs),
                          rope_tiles=tuple(rope_tiles), split_tiles=tuple(split_tiles)),
        grid=(t // tm, n_tiles),
        in_specs=[pl.BlockSpec((tm, d), lambda i, j: (i, 0)), const((1, d)), mod_spec, mod_spec,
                  pl.BlockSpec((d, TN_IN), lambda i, j: (0, j)), const((d, LANES)), const((LANES, wq)),
                  const((1, wq))] + [tab_spec] * len(tabs),
        out_specs=out_specs,
        out_shape=out_shape,
        scratch_shapes=[pltpu.VMEM((1 + len(dils), tm, d), BF16),
                        pltpu.VMEM((d // LANES, tm if dils else SUBLANES, LANES), F32)],
        compiler_params=_cparams("arbitrary", "arbitrary"),
        name="in_proj",
    )(x2, g, sc, sh, w_main, w_alr, w2p, a_b, *tabs)


GLA_EXP_CLAMP = 80.0


def _gla_kernel(q_ref, k_ref, v_ref, ra_ref, la_ref, s0_ref, gn_ref, o_ref, sfin_ref, st_scr, *, rows, dk, dv):
    c = pl.program_id(1)
    nc = pl.num_programs(1)
    ch = GLA_CHUNK

    @pl.when(c == 0)
    def _():
        for h in range(GLA_HEADS):
            st_scr[h] = s0_ref[h].T

    ti = lax.broadcasted_iota(jnp.int32, (ch, ch), 0)
    si = lax.broadcasted_iota(jnp.int32, (ch, ch), 1)
    causal = si <= ti
    tri = causal.astype(BF16)
    scale = dk ** -0.5
    wk = GLA_HEADS * dk
    heads = range(GLA_HEADS)
    ks = [slice(h * dk, (h + 1) * dk) for h in heads]
    vs = [slice(h * dv, (h + 1) * dv) for h in heads]

    def chunk(ci, carry):
        rows = pl.ds(pl.multiple_of(ci * ch, ch), ch)
        la = la_ref[rows, :]
        hi = la.astype(BF16)
        r1 = la - hi.astype(F32)
        mid = r1.astype(BF16)
        lo = (r1 - mid.astype(F32)).astype(BF16)
        b3 = _dot(tri, jnp.concatenate([hi, mid, lo], axis=1))
        b = b3[:, :wk] + b3[:, wk:2 * wk] + b3[:, 2 * wk:]
        bl = b[ch - 1:ch, :]
        q = q_ref[rows, :] * scale
        k = k_ref[rows, :]
        qe = (q * jnp.exp(b)).astype(BF16)
        ke = (k * jnp.exp(jnp.minimum(-b, GLA_EXP_CLAMP))).astype(BF16)
        kh = (k * jnp.exp(bl - b)).astype(BF16)
        dec = jnp.exp(bl)
        v = v_ref[rows, :].astype(BF16)
        att = [jnp.where(causal, _dot_nt(qe[:, ks[h]], ke[:, ks[h]]), 0.0).astype(BF16) for h in heads]
        st = [st_scr[h] for h in heads]
        o = [_dot_nt(qe[:, ks[h]], st[h].astype(BF16)) + _dot(att[h], v[:, vs[h]]) for h in heads]
        for h in heads:
            st_scr[h] = st[h] * dec[:, ks[h]] + _dot_tn(v[:, vs[h]], kh[:, ks[h]])
        for h in heads:
            on = _rms(o[h], gn_ref[:, vs[h]])
            o_ref[rows, vs[h]] = (on * _silu(ra_ref[rows, vs[h]])).astype(o_ref.dtype)
        return carry

    lax.fori_loop(0, rows // ch, chunk, 0)

    @pl.when(c == nc - 1)
    def _():
        for h in range(GLA_HEADS):
            sfin_ref[h] = st_scr[h].T


def _gla_prompt(z, zv, la, s0, gn_g, n, l, rows):
    t = n * l
    dk = la.shape[1] // GLA_HEADS
    dv = gn_g.shape[1] // GLA_HEADS
    wk, wv = GLA_HEADS * dk, GLA_HEADS * dv
    cpb = l // rows
    row = lambda b, c: b * cpb + c
    return pl.pallas_call(
        functools.partial(_gla_kernel, rows=rows, dk=dk, dv=dv),
        grid=(n, cpb),
        in_specs=[pl.BlockSpec((rows, wk), lambda b, c: (row(b, c), 0)),
                  pl.BlockSpec((rows, wk), lambda b, c: (row(b, c), 1)),
                  pl.BlockSpec((rows, wv), lambda b, c: (row(b, c), 0)),
                  pl.BlockSpec((rows, wv), lambda b, c: (row(b, c), 1)),
                  pl.BlockSpec((rows, wk), lambda b, c: (row(b, c), 0)),
                  pl.BlockSpec((None, GLA_HEADS, dk, dv), lambda b, c: (b, 0, 0, 0)),
                  pl.BlockSpec((1, wv), lambda b, c: (0, 0))],
        out_specs=[pl.BlockSpec((rows, wv), lambda b, c: (row(b, c), 0)),
                   pl.BlockSpec((None, GLA_HEADS, dk, dv), lambda b, c: (b, 0, 0, 0))],
        out_shape=[jax.ShapeDtypeStruct((t, wv), BF16), jax.ShapeDtypeStruct((n, GLA_HEADS, dk, dv), F32)],
        scratch_shapes=[pltpu.VMEM((GLA_HEADS, dv, dk), F32)],
        compiler_params=_cparams("arbitrary", "arbitrary"),
        name="gla_prompt",
    )(z, z, zv, z, la, s0, gn_g)


def _gla_step_kernel(q_ref, k_ref, la_ref, v_ref, ra_ref, s_ref, gn_ref, o_ref, so_ref, *, dk):
    sn = jnp.exp(la_ref[...]) * s_ref[...] + k_ref[...] * v_ref[...]
    so_ref[...] = sn
    o = jnp.sum((q_ref[...] * dk ** -0.5) * sn, axis=1, keepdims=True)
    on = _rms(o, gn_ref[...])
    o_ref[...] = (on * _silu(ra_ref[...])).astype(o_ref.dtype)


def _gla_step(q, k, la, v, ra, s, gn_g):
    nb, nh, dk, dv = s.shape
    col = pl.BlockSpec((None, nh, dk, 1), lambda b: (b, 0, 0, 0))
    rowv = pl.BlockSpec((None, nh, 1, dv), lambda b: (b, 0, 0, 0))
    st = pl.BlockSpec((None, nh, dk, dv), lambda b: (b, 0, 0, 0))
    o, so = pl.pallas_call(
        functools.partial(_gla_step_kernel, dk=dk),
        grid=(nb,),
        in_specs=[col, col, col, rowv, rowv, st, pl.BlockSpec((nh, 1, dv), lambda b: (0, 0, 0))],
        out_specs=[rowv, st],
        out_shape=[jax.ShapeDtypeStruct((nb, nh, 1, dv), BF16), jax.ShapeDtypeStruct(s.shape, F32)],
        compiler_params=_cparams("arbitrary"),
        name="gla_step",
    )(q.reshape(nb, nh, dk, 1), k.reshape(nb, nh, dk, 1), la.reshape(nb, nh, dk, 1),
      v.reshape(nb, nh, 1, dv), ra.reshape(nb, nh, 1, dv), s, gn_g.reshape(nh, 1, dv))
    return o.reshape(nb, nh * dv), so


def _band_kernel(q_ref, k_ref, kp_ref, v_ref, vp_ref, o_ref, lse_ref, *, dil, qb):
    i = pl.program_id(1)
    blk = ATT_BLOCK
    qi = lax.broadcasted_iota(jnp.int32, (blk, 2 * blk), 0)
    ki = lax.broadcasted_iota(jnp.int32, (blk, 2 * blk), 1)
    dist = blk + qi - ki
    band = jnp.logical_and(dist >= 0, dist <= ATT_BLOCK)
    band_first = jnp.logical_and(band, ki >= jnp.where(i > 0, 0, blk))
    scale = ATT_HD ** -0.5
    for r in range(dil):
        for sb in range(qb // blk):
            rs = slice(sb * blk, (sb + 1) * blk)
            q = q_ref[r, rs, :].astype(BF16)
            if sb == 0:
                kprev, vprev, mask = kp_ref[r], vp_ref[r], band_first
            else:
                ps = slice((sb - 1) * blk, sb * blk)
                kprev, vprev, mask = k_ref[r, ps, :], v_ref[r, ps, :], band
            kk = jnp.concatenate([kprev, k_ref[r, rs, :]], axis=0).astype(BF16)
            vv = jnp.concatenate([vprev, v_ref[r, rs, :]], axis=0).astype(BF16)
            s = jnp.where(mask, _dot_nt(q, kk) * scale, NEG_BIG)
            m = jnp.max(s, axis=-1, keepdims=True)
            p = jnp.exp(s - m)
            lsum = jnp.sum(p, axis=-1, keepdims=True)
            o = _dot(p.astype(BF16), vv) / lsum
            lse = jnp.broadcast_to(m + jnp.log(lsum), (blk, ATT_HD))
            rows = rs if dil == 1 else pl.ds(sb * blk * dil + r, blk, stride=dil)
            o_ref[rows, :] = o
            lse_ref[rows, :] = lse


def _band_attention(qa, ka, va, cq, ck, cv, n, l, dil):
    lq = l // dil
    hd = ATT_HD
    tokens = min(ATT_TOKENS, l)
    qb = tokens // dil
    nqb = lq // qb
    sub = qb // ATT_BLOCK

    def main(c0):
        return pl.BlockSpec((None, dil, qb, hd), lambda b, i, h: (b, 0, i, c0 + h))

    def prev(c0):
        return pl.BlockSpec((None, dil, ATT_BLOCK, hd), lambda b, i, h: (b, 0, jnp.maximum(i * sub - 1, 0), c0 + h))

    out = pl.BlockSpec((tokens, hd), lambda b, i, h: (b * nqb + i, h))
    shp = jax.ShapeDtypeStruct((n * l, ATT_HEADS * hd), F32)
    return pl.pallas_call(
        functools.partial(_band_kernel, dil=dil, qb=qb),
        grid=(n, nqb, ATT_HEADS),
        in_specs=[main(cq), main(ck), prev(ck), main(cv), prev(cv)],
        out_specs=[out, out],
        out_shape=[shp, shp],
        compiler_params=_cparams("arbitrary", "arbitrary", "arbitrary"),
        name=f"band_attention_d{dil}",
    )(qa, ka, ka, va, va)


def _att_step_kernel(q_ref, kn_ref, vn_ref, c0_ref, c1_ref, c2_ref, o_ref):
    scale = ATT_HD ** -0.5
    outs, lses = [], []
    for g, c_ref in enumerate((c0_ref, c1_ref, c2_ref)):
        q, kn, vn = q_ref[g], kn_ref[g], vn_ref[g]
        s = jnp.sum(c_ref[:, 0] * q[None], axis=-1, keepdims=True) * scale
        sn = jnp.sum(kn * q, axis=-1, keepdims=True) * scale
        m = jnp.maximum(jnp.max(s, axis=0), sn)
        p = jnp.exp(s - m[None])
        pn = jnp.exp(sn - m)
        lsum = jnp.sum(p, axis=0) + pn
        outs.append((jnp.sum(p * c_ref[:, 1], axis=0) + pn * vn) / lsum)
        lses.append(m + jnp.log(lsum))
    mx = functools.reduce(jnp.maximum, lses)
    es = [jnp.exp(x - mx) for x in lses]
    inv = 1.0 / functools.reduce(lambda a, b: a + b, es)
    o_ref[...] = functools.reduce(lambda a, b: a + b, [e * inv * o for e, o in zip(es, outs)]).astype(o_ref.dtype)


def _att_step(qs, ks, vs, caches):
    nb = qs.shape[0]
    views, cspecs = [], []
    for (window, dil), cch in zip(ATT_GROUPS, caches):
        assert cch.shape[1] == window, "sample step expects full caches"
        views.append(cch.reshape((nb, window // dil, dil) + cch.shape[2:]))
        cspecs.append(pl.BlockSpec((None, ATT_BLOCK, None) + cch.shape[2:], lambda b: (b, 0, 0, 0, 0, 0)))
    row = pl.BlockSpec((None,) + qs.shape[1:], lambda b: (b, 0, 0, 0))
    o = pl.pallas_call(
        _att_step_kernel,
        grid=(nb,),
        in_specs=[row] * 3 + cspecs,
        out_specs=pl.BlockSpec((None, ATT_HEADS, ATT_HD), lambda b: (b, 0, 0)),
        out_shape=jax.ShapeDtypeStruct((nb, ATT_HEADS, ATT_HD), BF16),
        compiler_params=_cparams("arbitrary"),
        name="att_step",
    )(qs, ks, vs, *views)
    return o.reshape(nb, ATT_HEADS * ATT_HD)


def _kv_shift_kernel(a_ref, nxt_ref, new_ref, o_ref, *, rows):
    i = pl.program_id(1)
    last = pl.num_programs(1) - 1
    o_ref[0:rows - 1] = a_ref[1:rows]

    @pl.when(i == last)
    def _():
        o_ref[rows - 1:rows] = new_ref[...]

    @pl.when(i != last)
    def _():
        o_ref[rows - 1:rows] = nxt_ref[...]


def _kv_shift(cache, new_row):
    nb, wb = cache.shape[0], cache.shape[1]
    rw = cache.shape[2] * cache.shape[3] * cache.shape[4]
    sub = rw // LANES
    rows = min(wb, 512)
    cv = cache.reshape(nb, wb, sub, LANES)
    out = pl.pallas_call(
        functools.partial(_kv_shift_kernel, rows=rows),
        grid=(nb, wb // rows),
        in_specs=[pl.BlockSpec((None, rows, sub, LANES), lambda b, i: (b, i, 0, 0)),
                  pl.BlockSpec((None, 1, sub, LANES), lambda b, i: (b, jnp.minimum((i + 1) * rows, wb - 1), 0, 0)),
                  pl.BlockSpec((None, 1, sub, LANES), lambda b, i: (b, 0, 0, 0))],
        out_specs=pl.BlockSpec((None, rows, sub, LANES), lambda b, i: (b, i, 0, 0)),
        out_shape=jax.ShapeDtypeStruct(cv.shape, cache.dtype),
        compiler_params=_cparams("arbitrary", "arbitrary"),
        name=f"kv_shift_w{wb}",
    )(cv, cv, new_row.reshape(nb, 1, sub, LANES))
    return out.reshape(cache.shape)


MERGE_PART = 256


def _merge_kernel(*refs, n_att, n_alias):
    oa_ref = refs[0]
    att = refs[1:1 + n_att]
    (gza_ref, gzb_ref, x_ref, g1_ref, sc2_ref, sh2_ref, n2_ref, wa_ref, wb_ref, wo_ref,
     wq_ref) = refs[1 + n_att:12 + n_att]
    x1_ref, h2_ref, pq_ref = refs[12 + n_att + n_alias:]
    tm = x_ref.shape[0]
    pr = min(MERGE_PART, tm)
    parts = [slice(k * pr, (k + 1) * pr) for k in range(tm // pr)]
    mod = lambda ref, part: ref[...] if ref.shape[0] == 1 else ref[part, :]
    add = lambda a, b: a + b

    def attention_out(part):
        if n_att == 1:
            return att[0][part, :]
        ng = n_att // 2
        ls = [r[part, :] for r in att[ng:]]
        mx = functools.reduce(jnp.maximum, ls)
        es = [jnp.exp(x - mx) for x in ls]
        inv = 1.0 / functools.reduce(add, es)
        return functools.reduce(add, [e * inv * r[part, :] for e, r in zip(es, att[:ng])]).astype(BF16)

    ob = [attention_out(p) for p in parts]
    ma = [_dot(oa_ref[p, :], wa_ref[...]) for p in parts]
    mb = [_dot(o, wb_ref[...]) for o in ob]
    merged = [(_sigmoid(gza_ref[p, :]) * a + _sigmoid(gzb_ref[p, :]) * b).astype(BF16)
              for p, a, b in zip(parts, ma, mb)]
    y = [_dot(m, wo_ref[...]) for m in merged]
    h2 = []
    for p, yy in zip(parts, y):
        x1 = x_ref[p, :] + mod(g1_ref, p) * yy
        x1_ref[p, :] = x1
        h2.append((_rms(x1, n2_ref[...]) * (1.0 + mod(sc2_ref, p)) + mod(sh2_ref, p)).astype(BF16))
    for p, h in zip(parts, h2):
        h2_ref[p, :] = h
        pq_ref[p, :] = _dot(h, wq_ref[...])
    if h2_ref.shape[0] > tm:
        h2_ref[tm:] = jnp.zeros((h2_ref.shape[0] - tm, h2_ref.shape[1]), h2_ref.dtype)
        pq_ref[tm:] = jnp.zeros((pq_ref.shape[0] - tm, pq_ref.shape[1]), pq_ref.dtype)


def _merge(oa, att, z, x2, g1, sc2, sh2, n2g, wa, wb, wo, wq, tm, rows_per_mod, gza_blk, gzb_blk,
           peer_rows, peer_tile, peer_blk0, alias=()):
    t, d = x2.shape
    r = g1.shape[1]
    wqn = wq.shape[1]
    tiles_per_mod = rows_per_mod // tm
    tok = lambda w: pl.BlockSpec((tm, w), lambda i: (i, 0))
    mod = pl.BlockSpec((None, r, d), lambda i: (i // tiles_per_mod, 0, 0))
    const = lambda a: pl.BlockSpec(a.shape, lambda i: (0, 0), pipeline_mode=pl.Buffered(1))
    peer = lambda w: pl.BlockSpec((peer_tile, w), lambda i: (peer_blk0 + i, 0))
    n_in = 12 + len(att)
    return pl.pallas_call(
        functools.partial(_merge_kernel, n_att=len(att), n_alias=len(alias)),
        grid=(t // tm,),
        in_specs=[tok(oa.shape[1])] + [tok(a.shape[1]) for a in att]
                 + [pl.BlockSpec((tm, d), lambda i: (i, gza_blk)), pl.BlockSpec((tm, d), lambda i: (i, gzb_blk)),
                    tok(d), mod, mod, mod, const(n2g), const(wa), const(wb), const(wo), const(wq)]
                 + [pl.BlockSpec(memory_space=pl.ANY)] * len(alias),
        out_specs=[tok(d), peer(d), peer(wqn)],
        out_shape=[jax.ShapeDtypeStruct((t, d), F32), jax.ShapeDtypeStruct((peer_rows, d), BF16),
                   jax.ShapeDtypeStruct((peer_rows, wqn), F32)],
        input_output_aliases={n_in + k: 1 + k for k in range(len(alias))},
        compiler_params=_cparams("arbitrary"),
        name="merge",
    )(oa, *att, z, z, x2, g1, sc2, sh2, n2g, wa, wb, wo, wq, *alias)


def _strict_max_below(rows, prev):
    m = None
    for x in rows:
        y = jnp.where(x < prev, x, NEG_BIG) if prev is not None else x
        m = y if m is None else jnp.maximum(m, y)
    return m


def _oddeven_merge(lo, hi, r):
    step = r * 2
    if step < hi - lo:
        yield from _oddeven_merge(lo, hi, step)
        yield from _oddeven_merge(lo + r, hi, step)
        yield from [(i, i + r) for i in range(lo + r, hi - r, step)]
    else:
        yield (lo, lo + r)


def _oddeven_merge_sort(lo, hi):
    if hi - lo >= 1:
        mid = lo + (hi - lo) // 2
        yield from _oddeven_merge_sort(lo, mid)
        yield from _oddeven_merge_sort(mid + 1, hi)
        yield from _oddeven_merge(lo, hi, 1)


def _exchange(p, i, j):
    p[i], p[j] = jnp.maximum(p[i], p[j]), jnp.minimum(p[i], p[j])


def _topk_desc(s, k):
    assert s.shape[0] == k * SUBLANES and k & (k - 1) == 0
    p = [s[i * SUBLANES:(i + 1) * SUBLANES] for i in range(k)]
    for i, j in _oddeven_merge_sort(0, k - 1):
        _exchange(p, i, j)
    shift = SUBLANES // 2
    while shift:
        q = [pltpu.roll(v, shift, 0) for v in p]
        p = [jnp.maximum(p[i], q[k - 1 - i]) for i in range(k)]
        stride = k // 2
        while stride:
            for i in range(k):
                if not i & stride:
                    _exchange(p, i, i + stride)
            stride //= 2
        shift //= 2
    return [v[0:1] for v in p]


def _route_kernel(pq_ref, k1_ref, k2_ref, n1_ref, e1_ref, r2_ref, e2_ref, s1_scr, s2_scr):
    nk = PEER_NKEYS
    pq = pq_ref[...].astype(BF16)
    s1_scr[...] = _dot_nt(k1_ref[...], pq)
    s2_scr[...] = _dot_nt(k2_ref[...], pq)
    v1h, v2h = [], []
    for h in range(PEER_HEADS):
        v1h.append(_topk_desc(s1_scr[h * nk:(h + 1) * nk], PEER_TOPK))
        v2h.append(_topk_desc(s2_scr[h * nk:(h + 1) * nk], PEER_TOPK))
    v1 = [jnp.concatenate([v1h[h][k] for h in range(PEER_HEADS)], axis=0) for k in range(PEER_TOPK)]
    v2 = [jnp.concatenate([v2h[h][k] for h in range(PEER_HEADS)], axis=0) for k in range(PEER_TOPK)]
    pairs = [(i, j) for i in range(PEER_TOPK) for j in range(PEER_TOPK) if (i + 1) * (j + 1) <= PEER_TOPK]
    cands = [v1[i] + v2[j] for i, j in pairs]
    tau = None
    for _ in range(PEER_TOPK):
        tau = _strict_max_below(cands, tau)
    cmax = v1[0] + v2[0]
    zsum = None
    counts = [None] * PEER_TOPK
    for (i, _), cnd in zip(pairs, cands):
        sel = cnd >= tau
        e = jnp.where(sel, jnp.exp(cnd - cmax), 0.0)
        zsum = e if zsum is None else zsum + e
        one = jnp.where(sel, 1.0, 0.0)
        counts[i] = one if counts[i] is None else counts[i] + one
    zinv = 1.0 / zsum
    for h in range(PEER_HEADS):
        a = s1_scr[h * nk:(h + 1) * nk]
        b = s2_scr[h * nk:(h + 1) * nk]
        n1 = jnp.zeros_like(a)
        r2 = jnp.full_like(b, float(PEER_TOPK))
        for k in range(PEER_TOPK):
            n1 = jnp.where(a == v1h[h][k], counts[k][h:h + 1, :], n1)
            r2 = jnp.where(b == v2h[h][k], float(k), r2)
        n1_ref[h] = n1
        e1_ref[h] = jnp.exp(a - v1h[h][0]) * zinv[h:h + 1, :]
        r2_ref[h] = r2.astype(r2_ref.dtype)
        e2_ref[h] = jnp.exp(b - v2h[h][0]).astype(e2_ref.dtype)


def _peer_route(pq, k1big, k2big, tt):
    t, w = pq.shape
    nh, nk = PEER_HEADS, PEER_NKEYS
    sspec = pl.BlockSpec((nh, nk, tt), lambda i: (0, 0, i))
    shp = lambda dt: jax.ShapeDtypeStruct((nh, nk, t), dt)
    return pl.pallas_call(
        _route_kernel,
        grid=(t // tt,),
        in_specs=[pl.BlockSpec((tt, w), lambda i: (i, 0)),
                  pl.BlockSpec(k1big.shape, lambda i: (0, 0)), pl.BlockSpec(k2big.shape, lambda i: (0, 0))],
        out_specs=[sspec] * 4,
        out_shape=[shp(F32), shp(F32), shp(BF16), shp(BF16)],
        scratch_shapes=[pltpu.VMEM((nh * nk, tt), F32)] * 2,
        compiler_params=_cparams("arbitrary"),
        name="peer_route",
    )(pq, k1big, k2big)


def _gelu_tanh(x):
    return 0.5 * x * (1.0 + jnp.tanh(GELU_C * (x + 0.044715 * (x * x * x))))


BF16_ROWS = 16
GATE_LANES = 256
PEER_PART = 256
PEER_TILE = 2048


def _gelu_tanh_bf16(x):
    hx = 0.5 * x
    return hx + hx * jnp.tanh(x * (GELU_C + (GELU_C * 0.044715) * (x * x)))


def _bf16_rows(row):
    return jnp.broadcast_to(row, (BF16_ROWS, row.shape[1])).astype(BF16)


def _peer_dense_kernel(h2_ref, u_ref, vt_ref, n1_ref, e1_ref, r2_ref, e2_ref, o_ref, acc_scr, wg_scr, st_scr,
                       *, et):
    e = pl.program_id(1)
    nk = PEER_NKEYS
    tt = h2_ref.shape[0]

    @pl.when(e == 0)
    def _():
        acc_scr[...] = jnp.zeros_like(acc_scr)

    zero = jnp.zeros((), BF16)
    groups = nk // BF16_ROWS

    def scores(part):
        st_scr[part, :] = _dot_nt(u_ref[part, :], h2_ref[...]).astype(BF16)

    def gates(ii):
        rs = slice(ii * nk, (ii + 1) * nk)
        for lc in range(tt // GATE_LANES):
            ls = slice(lc * GATE_LANES, (lc + 1) * GATE_LANES)
            w = None
            for h in range(PEER_HEADS):
                n1 = _bf16_rows(n1_ref[h, ii:ii + 1, ls])
                e1 = _bf16_rows(e1_ref[h, ii:ii + 1, ls])
                r2 = r2_ref[h, :, ls].reshape(groups, BF16_ROWS, GATE_LANES)
                e2 = e2_ref[h, :, ls].reshape(groups, BF16_ROWS, GATE_LANES)
                c = jnp.where(r2 < n1[None], e2, zero) * e1[None]
                w = c if w is None else w + c
            wg_scr[rs, ls] = w.reshape(nk, GATE_LANES) * _gelu_tanh_bf16(st_scr[rs, ls])

    def mix(part):
        return _dot(vt_ref[:, part], wg_scr[part, :])

    per = PEER_PART // nk
    parts = [slice(p * PEER_PART, (p + 1) * PEER_PART) for p in range(et // PEER_PART)]
    out = None
    scores(parts[0])
    for p, part in enumerate(parts):
        if p + 1 < len(parts):
            scores(parts[p + 1])
        for ii in range(p * per, (p + 1) * per):
            gates(ii)
        if p > 0:
            d = mix(parts[p - 1])
            out = d if out is None else out + d
    d = mix(parts[-1])
    acc_scr[...] += d if out is None else out + d

    @pl.when(e == pl.num_programs(1) - 1)
    def _():
        o_ref[...] = acc_scr[...].T


def _peer_dense(h2, u_b, vt_b, n1, e1, r2, e2, tt, et):
    t, d = h2.shape
    ne = u_b.shape[0]
    nh, nk = PEER_HEADS, PEER_NKEYS
    rowside = pl.BlockSpec((nh, et // nk, tt), lambda i, e: (0, e, i))
    colside = pl.BlockSpec((nh, nk, tt), lambda i, e: (0, 0, i))
    return pl.pallas_call(
        functools.partial(_peer_dense_kernel, et=et),
        grid=(t // tt, ne // et),
        in_specs=[pl.BlockSpec((tt, d), lambda i, e: (i, 0)),
                  pl.BlockSpec((et, d), lambda i, e: (e, 0)),
                  pl.BlockSpec((d, et), lambda i, e: (0, e)),
                  rowside, rowside, colside, colside],
        out_specs=pl.BlockSpec((tt, d), lambda i, e: (i, 0)),
        out_shape=jax.ShapeDtypeStruct((t, d), F32),
        scratch_shapes=[pltpu.VMEM((d, tt), F32), pltpu.VMEM((et, tt), BF16), pltpu.VMEM((et, tt), BF16)],
        compiler_params=_cparams("arbitrary", "arbitrary"),
        name="peer_dense",
    )(h2, u_b, vt_b, n1, e1, r2, e2)


def _final_kernel(x1_ref, p_ref, g2_ref, fg_ref, y_ref):
    y_ref[...] = _rms(x1_ref[...] + g2_ref[...] * p_ref[...], fg_ref[...])


def _final(x1, p, g2, fg, tm, rows_per_mod, p_blk0):
    t, d = x1.shape
    r = g2.shape[1]
    tiles_per_mod = rows_per_mod // tm
    tok = pl.BlockSpec((tm, d), lambda i: (i, 0))
    return pl.pallas_call(
        _final_kernel,
        grid=(t // tm,),
        in_specs=[tok, pl.BlockSpec((tm, d), lambda i: (p_blk0 + i, 0)),
                  pl.BlockSpec((None, r, d), lambda i: (i // tiles_per_mod, 0, 0)),
                  pl.BlockSpec((1, d), lambda i: (0, 0))],
        out_specs=tok,
        out_shape=jax.ShapeDtypeStruct((t, d), F32),
        compiler_params=_cparams("arbitrary"),
        name="final_norm",
    )(x1, p, g2, fg)


def _rope_tables(pos):
    half = ROPE_DIM // 2
    inv_freq = ROPE_THETA ** (-jnp.arange(half, dtype=F32) / half)
    ang = pos.astype(F32)[:, None] * inv_freq[None, :]
    cos, sin = jnp.cos(ang), jnp.sin(ang)
    n = pos.shape[0]
    ones = jnp.ones((n, LANES - ROPE_DIM), F32)
    zeros = jnp.zeros((n, LANES - ROPE_DIM), F32)
    return (jnp.concatenate([cos, cos, ones], axis=1), jnp.concatenate([-sin, sin, zeros], axis=1))


def _block_diag_keys(keys, half):
    nk, dh = keys.shape
    eye = jnp.eye(PEER_HEADS, dtype=keys.dtype)
    blk = jnp.zeros((PEER_HEADS, nk, PEER_HEADS, 2, dh), keys.dtype)
    blk = blk.at[:, :, :, half, :].set(eye[:, None, :, None] * keys[None, :, None, :])
    return blk.reshape(PEER_HEADS * nk, PEER_HEADS * 2 * dh).astype(BF16)


def _pick_tile(n, pref):
    t = min(pref, n)
    while n % t:
        t //= 2
    return t


def kernel(x_prompt, x_sample, c_prompt, c_sample, state_gla, cache_kv_w128, cache_kv_w512, cache_kv_w2048,
           ada_w, ada_b, norm1_g, w_in, gla_a_w2, gla_a_b, gla_gn_g, w_branch_a, w_branch_b, w_out,
           norm2_g, peer_wq, peer_k1, peer_k2, peer_u, peer_v, final_g):
    depth = ada_w.shape[0]
    assert depth == 1, "single-layer trunk"
    n_p, l_p, d = x_prompt.shape
    n_s, l_s, _ = x_sample.shape
    assert l_s == 1, "sample group decodes one token per sequence"
    caches = (cache_kv_w128[0], cache_kv_w512[0], cache_kv_w2048[0])
    n_groups = len(ATT_GROUPS)

    w_qa = gla_a_w2.shape[2]
    w_va = gla_gn_g.shape[1]
    w_ob = ATT_HEADS * ATT_HD
    w_qb = n_groups * w_ob
    assert w_qa == TN_IN and w_va == 2 * TN_IN and d == 2 * TN_IN and w_ob == TN_IN
    widths = (w_qa, w_qa, w_va, w_va, GLA_RANK, w_qb, w_qb, w_qb, d, d)
    offs = [0]
    for w in widths:
        offs.append(offs[-1] + w)
    seg = lambda i: w_in[0][:, offs[i]:offs[i + 1]]
    att_col = lambda g, k: seg(5 + k)[:, g * w_ob:(g + 1) * w_ob]
    att_cols = [att_col(g, k) for g in range(n_groups) for k in range(3)]
    w_main = jnp.concatenate([seg(0), seg(1), seg(3), seg(8), seg(9), seg(2)] + att_cols, axis=1).astype(BF16)
    w_kv = jnp.concatenate([att_col(g, k) for g in range(n_groups) for k in (1, 2)], axis=1).astype(BF16)
    n_tiles = w_main.shape[1] // TN_IN
    ra_off, gza_blk, gzb_blk = 2 * w_qa, 2, 3
    w_alr = jnp.pad(seg(4), ((0, 0), (0, LANES - GLA_RANK))).astype(BF16)
    w2p = jnp.pad(gla_a_w2[0], ((0, LANES - GLA_RANK), (0, 0))).astype(BF16)
    a_b = gla_a_b[0].reshape(1, w_qa)
    g1n = norm1_g[0].reshape(1, d)
    gn_g = gla_gn_g[0].reshape(1, w_va)
    n2g = norm2_g[0].reshape(1, d)
    wa, wb = w_branch_a[0].astype(BF16), w_branch_b[0].astype(BF16)
    wo, wq = w_out[0].astype(BF16), peer_wq[0].astype(BF16)
    q_tile = lambda g: ATT_TILE0 + 3 * g
    rope_tiles = [q_tile(g) + k for g in range(n_groups) for k in (0, 1)]

    n_mod = n_p + n_s
    n_pad = -(-n_mod // SUBLANES) * SUBLANES
    c_all = jnp.concatenate([c_prompt, c_sample, jnp.zeros((n_pad - n_mod, d), F32)], axis=0)
    mod = _modulation(c_all, ada_w[0], ada_b[0])
    mods_p = [mod[:n_p, i * d:(i + 1) * d].reshape(n_p, 1, d) for i in range(6)]
    mods_s = [mod[n_p:n_mod, i * d:(i + 1) * d].reshape(1, n_s, d) for i in range(6)]

    t_p = n_p * l_p
    tt = _pick_tile(t_p, 512)
    t_all = t_p + (-(-n_s // tt)) * tt
    assert t_all - t_p == tt

    xp2 = x_prompt.reshape(t_p, d)
    tm_p = _pick_tile(l_p, 1024)
    pos_p = jnp.arange(l_p, dtype=jnp.int32)
    split = []
    for g, (window, dil) in enumerate(ATT_GROUPS):
        assert window // dil == ATT_BLOCK
        if dil > 1:
            split += [(q_tile(g) + k, len(split) + k, dil) for k in range(3)]
    first_split = min([t for t, _, _ in split], default=n_tiles)
    outs = _in_proj(xp2, g1n, mods_p[1], mods_p[0], w_main, w_alr, w2p, a_b, pos_p, tm_p, l_p,
                    ((0, VA_TILE0, F32), (VA_TILE0, first_split, BF16)), rope_tiles, split)
    z_p, zb_p, la_p, split_p = outs[0], outs[1], outs[2], outs[3:]
    s0_p = jnp.zeros((n_p,) + state_gla.shape[2:], F32)
    oa_p, sfin_p = _gla_prompt(z_p, zb_p, la_p, s0_p, gn_g, n_p, l_p, _pick_tile(l_p, 256))
    zb4 = zb_p.reshape(n_p, 1, l_p, zb_p.shape[1])
    att_o, att_l = [], []
    for g, (window, dil) in enumerate(ATT_GROUPS):
        if dil > 1:
            qa, ka, va = [split_p[slot] for t, slot, _ in split if q_tile(g) <= t < q_tile(g) + 3]
            o_g, lse_g = _band_attention(qa, ka, va, 0, 0, 0, n_p, l_p, dil)
        else:
            c0 = lambda k: (q_tile(g) + k - VA_TILE0) * (TN_IN // ATT_HD)
            o_g, lse_g = _band_attention(zb4, zb4, zb4, c0(0), c0(1), c0(2), n_p, l_p, dil)
        att_o.append(o_g)
        att_l.append(lse_g)
    keep_max = min(max(w for w, _ in ATT_GROUPS), l_p)
    x_tail = x_prompt[:, l_p - keep_max:].reshape(n_p * keep_max, d)
    kv_tiles = 2 * n_groups
    z_kv = _in_proj(x_tail, g1n, mods_p[1], mods_p[0], w_kv, w_alr, w2p, a_b, pos_p[l_p - keep_max:],
                    _pick_tile(keep_max, 1024), keep_max, ((0, kv_tiles, F32),), range(0, kv_tiles, 2), [])[0]
    z_kv = z_kv.reshape(n_p, keep_max, kv_tiles, ATT_HEADS, ATT_HD)
    kv_p = []
    for g, (window, dil) in enumerate(ATT_GROUPS):
        keep = min(window, l_p)
        kv_p.append(z_kv[:, keep_max - keep:, 2 * g:2 * g + 2][None])
    tm_m = _pick_tile(l_p, 512)
    x1_p, h2_all, pq_all = _merge(oa_p, att_o + att_l, z_p, xp2, mods_p[2], mods_p[4], mods_p[3], n2g,
                                  wa, wb, wo, wq, tm_m, l_p, gza_blk, gzb_blk, t_all, tm_m, 0)

    xs2 = x_sample.reshape(n_s, d)
    pos_s = jnp.full((n_s,), PAST_LEN, dtype=jnp.int32)
    z_s, la_s = _in_proj(xs2, g1n, mods_s[1], mods_s[0], w_main, w_alr, w2p, a_b, pos_s, n_s, n_s,
                         ((0, n_tiles, F32),), rope_tiles, [])
    col = lambda tile: z_s[:, tile * TN_IN:(tile + 1) * TN_IN]
    oa_s, state_s = _gla_step(col(0), col(1), la_s, z_s[:, VA_TILE0 * TN_IN:VA_TILE0 * TN_IN + w_va],
                              z_s[:, ra_off:ra_off + w_va], state_gla[0], gn_g)
    stack = lambda k: jnp.stack([col(q_tile(g) + k) for g in range(n_groups)], axis=1).reshape(
        n_s, n_groups, ATT_HEADS, ATT_HD)
    ob_s = _att_step(stack(0), stack(1), stack(2), caches)
    x1_s, h2_all, pq_all = _merge(oa_s, [ob_s], z_s, xs2, mods_s[2], mods_s[4], mods_s[3], n2g,
                                  wa, wb, wo, wq, n_s, n_s, gza_blk, gzb_blk, t_all, tt, t_p // tt,
                                  alias=(h2_all, pq_all))
    kv_s = []
    for g, cch in enumerate(caches):
        new_row = jnp.concatenate([col(q_tile(g) + 1), col(q_tile(g) + 2)], axis=1)
        kv_s.append(_kv_shift(cch, new_row)[None])

    k1big = _block_diag_keys(peer_k1[0], 0)
    k2big = _block_diag_keys(peer_k2[0], 1)
    n1, e1, r2, e2 = _peer_route(pq_all, k1big, k2big, _pick_tile(tt, 256))
    u_b = peer_u[0].astype(BF16)
    vt_b = peer_v[0].astype(BF16).T
    p_all = _peer_dense(h2_all, u_b, vt_b, n1, e1, r2, e2, tt, PEER_TILE)

    fg = final_g.reshape(1, d)
    y_p = _final(x1_p, p_all, mods_p[5], fg, tm_m, l_p, 0)
    y_s = _final(x1_s, p_all, mods_s[5], fg, n_s, n_s, t_p // n_s)

    return (y_p.reshape(n_p, l_p, d), y_s.reshape(n_s, l_s, d), sfin_p[None], state_s[None],
            kv_p[0], kv_s[0], kv_p[1], kv_s[1], kv_p[2], kv_s[2])
```

```python
import functools
import math

import jax
import jax.numpy as jnp
from jax import lax
from jax.experimental import pallas as pl
from jax.experimental.pallas import tpu as pltpu

F32 = jnp.float32
BF16 = jnp.bfloat16

PAST_LEN = 16384
GLA_HEADS = 4
GLA_RANK = 16
GLA_TAU = 16.0
GLA_CHUNK = 64
ATT_GROUPS = ((128, 1), (512, 4), (2048, 16))
ATT_HEADS = 4
ATT_HD = 128
ATT_BLOCK = 128
ATT_TOKENS = 2048
ROPE_DIM = ATT_HD // 4
ROPE_THETA = 500000.0
PEER_HEADS = 8
PEER_NKEYS = 128
PEER_DQ = 128
PEER_TOPK = 16
NORM_EPS = 1e-6
GELU_C = math.sqrt(2.0 / math.pi)

LANES = 128
SUBLANES = 8
VMEM_LIMIT = 56 * 1024 * 1024
NEG_BIG = -1e30


def _cparams(*sem):
    return pltpu.CompilerParams(dimension_semantics=sem, vmem_limit_bytes=VMEM_LIMIT)


def _sigmoid(x):
    return 1.0 / (1.0 + jnp.exp(-x))


def _silu(x):
    return x * _sigmoid(x)


def _log_sigmoid(x):
    return jnp.minimum(x, 0.0) - jnp.log(1.0 + jnp.exp(-jnp.abs(x)))


def _rms(x, g):
    return x * lax.rsqrt(jnp.mean(x * x, axis=-1, keepdims=True) + NORM_EPS) * g


def _dot(a, b):
    return jnp.dot(a, b, preferred_element_type=F32)


def _dot_nt(a, b):
    return lax.dot_general(a, b, (((1,), (1,)), ((), ())), preferred_element_type=F32)


def _dot_tn(a, b):
    return lax.dot_general(a, b, (((0,), (0,)), ((), ())), preferred_element_type=F32)


def _mod_kernel(c_ref, w_ref, b_ref, o_ref):
    s = _silu(c_ref[...]).astype(BF16)
    o_ref[...] = _dot(s, w_ref[...].astype(BF16)) + b_ref[...]


def _modulation(c, ada_w, ada_b):
    n, d = c.shape
    n6 = ada_w.shape[1]
    tn = n6 // 4
    return pl.pallas_call(
        _mod_kernel,
        grid=(n6 // tn,),
        in_specs=[pl.BlockSpec((n, d), lambda j: (0, 0)),
                  pl.BlockSpec((d, tn), lambda j: (0, j)),
                  pl.BlockSpec((1, tn), lambda j: (0, j))],
        out_specs=pl.BlockSpec((n, tn), lambda j: (0, j)),
        out_shape=jax.ShapeDtypeStruct((n, n6), F32),
        compiler_params=_cparams("arbitrary"),
        name="modulation",
    )(c, ada_w, ada_b.reshape(1, n6))


TN_IN = 512
VA_TILE0 = 8
ATT_TILE0 = 10


def _rope_tile(z, c, s):
    reps = z.shape[1] // LANES
    n = z.shape[1]
    half = ROPE_DIM // 2
    lane = lax.broadcasted_iota(jnp.int32, (1, LANES), 1)
    first = jnp.concatenate([lane < half] * reps, axis=1)
    partner = jnp.where(first, pltpu.roll(z, n - half, 1), pltpu.roll(z, half, 1))
    return z * jnp.concatenate([c] * reps, axis=1) + partner * jnp.concatenate([s] * reps, axis=1)


def _inproj_kernel(x_ref, g_ref, sc_ref, sh_ref, w_ref, walr_ref, w2_ref, ab_ref, *rest,
                   segs, rope_tiles, split_tiles):
    dils = sorted({dil for _, _, dil in split_tiles})
    n_seg, n_tab = len(segs), 2 * (1 + len(dils))
    tabs = [(rest[2 * v], rest[2 * v + 1]) for v in range(1 + len(dils))]
    seg_refs, la_ref = rest[n_tab:n_tab + n_seg], rest[n_tab + n_seg]
    split_refs, (h_scr, hf_scr) = rest[n_tab + n_seg + 1:-2], rest[-2:]
    j = pl.program_id(1)
    tm, d = x_ref.shape

    @pl.when(j == 0)
    def _():
        h = _rms(x_ref[...], g_ref[...]) * (1.0 + sc_ref[...]) + sh_ref[...]
        hb = h.astype(BF16)
        h_scr[0] = hb
        alr = _dot(hb, walr_ref[...])
        p = _dot(alr.astype(BF16), w2_ref[...]) + ab_ref[...]
        la_ref[...] = _log_sigmoid(p) * (1.0 / GLA_TAU)
        for c in range((d // LANES) if dils else 0):
            cs = slice(c * LANES, (c + 1) * LANES)
            hf_scr[c] = h[:, cs]
            for v, dil in enumerate(dils, 1):
                per = tm // dil
                for r in range(dil):
                    h_scr[v, r * per:(r + 1) * per, cs] = hf_scr[c, pl.ds(r, per, stride=dil), :].astype(BF16)

    def tile(v, roped):
        z = _dot(h_scr[v], w_ref[...])
        return _rope_tile(z, tabs[v][0][...], tabs[v][1][...]) if roped else z

    for (lo, hi), ref in zip(segs, seg_refs):
        ropes = [t for t in rope_tiles if lo <= t < hi]
        in_seg = jnp.logical_and(j >= lo, j < hi)
        is_rope = functools.reduce(jnp.logical_or, [j == t for t in ropes], False)
        if ropes:
            @pl.when(jnp.logical_and(in_seg, is_rope))
            def _(ref=ref):
                ref[...] = tile(0, True).astype(ref.dtype)

        @pl.when(jnp.logical_and(in_seg, jnp.logical_not(is_rope)))
        def _(ref=ref):
            ref[...] = tile(0, False).astype(ref.dtype)

    for t, slot, dil in split_tiles:
        @pl.when(j == t)
        def _(t=t, slot=slot, dil=dil):
            z = tile(1 + dils.index(dil), t in rope_tiles)
            split_refs[slot][...] = z.reshape(dil, tm // dil, z.shape[1]).astype(split_refs[slot].dtype)


def _in_proj(x2, g, sc, sh, w_main, w_alr, w2p, a_b, pos, tm, rows_per_mod, segs, rope_tiles, split_tiles):
    t, d = x2.shape
    r = sc.shape[1]
    wq = a_b.shape[1]
    n_tiles = w_main.shape[1] // TN_IN
    tiles_per_mod = rows_per_mod // tm
    tab_tiles = pos.shape[0] // tm
    nmod = t // rows_per_mod
    dils = sorted({dil for _, _, dil in split_tiles})
    tabs = list(_rope_tables(pos))
    for dil in dils:
        tabs += [tb.reshape(tab_tiles, tm // dil, dil, LANES).transpose(0, 2, 1, 3).reshape(-1, LANES)
                 for tb in tabs[:2]]
    mod_spec = pl.BlockSpec((None, r, d), lambda i, j: (i // tiles_per_mod, 0, 0))
    tab_spec = pl.BlockSpec((tm, LANES), lambda i, j: (i % tab_tiles, 0))
    const = lambda shape: pl.BlockSpec(shape, lambda i, j: (0, 0))
    out_specs, out_shape = [], []
    for lo, hi, dt in segs:
        out_specs.append(pl.BlockSpec((tm, TN_IN), lambda i, j, lo=lo, hi=hi: (i, jnp.clip(j - lo, 0, hi - lo - 1))))
        out_shape.append(jax.ShapeDtypeStruct((t, (hi - lo) * TN_IN), dt))
    out_specs.append(pl.BlockSpec((tm, wq), lambda i, j: (i, 0)))
    out_shape.append(jax.ShapeDtypeStruct((t, wq), F32))
    for _, _, dil in split_tiles:
        out_specs.append(pl.BlockSpec((None, dil, tm // dil, TN_IN),
                                      lambda i, j: (i // tiles_per_mod, 0, i % tiles_per_mod, 0)))
        out_shape.append(jax.ShapeDtypeStruct((nmod, dil, rows_per_mod // dil, TN_IN), BF16))
    return pl.pallas_call(
        functools.partial(_inproj_kernel, segs=tuple((lo, hi) for lo, hi, _ in segs),
                          rope_tiles=tuple(rope_tiles), split_tiles=tuple(split_tiles)),
        grid=(t // tm, n_tiles),
        in_specs=[pl.BlockSpec((tm, d), lambda i, j: (i, 0)), const((1, d)), mod_spec, mod_spec,
                  pl.BlockSpec((d, TN_IN), lambda i, j: (0, j)), const((d, LANES)), const((LANES, wq)),
                  const((1, wq))] + [tab_spec] * len(tabs),
        out_specs=out_specs,
        out_shape=out_shape,
        scratch_shapes=[pltpu.VMEM((1 + len(dils), tm, d), BF16),
                        pltpu.VMEM((d // LANES, tm if dils else SUBLANES, LANES), F32)],
        compiler_params=_cparams("arbitrary", "arbitrary"),
        name="in_proj",
    )(x2, g, sc, sh, w_main, w_alr, w2p, a_b, *tabs)


GLA_EXP_CLAMP = 80.0


def _gla_kernel(q_ref, k_ref, v_ref, ra_ref, la_ref, s0_ref, gn_ref, o_ref, sfin_ref, st_scr, *, rows, dk, dv):
    c = pl.program_id(1)
    nc = pl.num_programs(1)
    ch = GLA_CHUNK

    @pl.when(c == 0)
    def _():
        for h in range(GLA_HEADS):
            st_scr[h] = s0_ref[h].T

    ti = lax.broadcasted_iota(jnp.int32, (ch, ch), 0)
    si = lax.broadcasted_iota(jnp.int32, (ch, ch), 1)
    causal = si <= ti
    tri = causal.astype(BF16)
    scale = dk ** -0.5
    wk = GLA_HEADS * dk
    heads = range(GLA_HEADS)
    ks = [slice(h * dk, (h + 1) * dk) for h in heads]
    vs = [slice(h * dv, (h + 1) * dv) for h in heads]

    def chunk(ci, carry):
        rows = pl.ds(pl.multiple_of(ci * ch, ch), ch)
        la = la_ref[rows, :]
        hi = la.astype(BF16)
        r1 = la - hi.astype(F32)
        mid = r1.astype(BF16)
        lo = (r1 - mid.astype(F32)).astype(BF16)
        b3 = _dot(tri, jnp.concatenate([hi, mid, lo], axis=1))
        b = b3[:, :wk] + b3[:, wk:2 * wk] + b3[:, 2 * wk:]
        bl = b[ch - 1:ch, :]
        q = q_ref[rows, :] * scale
        k = k_ref[rows, :]
        qe = (q * jnp.exp(b)).astype(BF16)
        ke = (k * jnp.exp(jnp.minimum(-b, GLA_EXP_CLAMP))).astype(BF16)
        kh = (k * jnp.exp(bl - b)).astype(BF16)
        dec = jnp.exp(bl)
        v = v_ref[rows, :].astype(BF16)
        att = [jnp.where(causal, _dot_nt(qe[:, ks[h]], ke[:, ks[h]]), 0.0).astype(BF16) for h in heads]
        st = [st_scr[h] for h in heads]
        o = [_dot_nt(qe[:, ks[h]], st[h].astype(BF16)) + _dot(att[h], v[:, vs[h]]) for h in heads]
        for h in heads:
            st_scr[h] = st[h] * dec[:, ks[h]] + _dot_tn(v[:, vs[h]], kh[:, ks[h]])
        for h in heads:
            on = _rms(o[h], gn_ref[:, vs[h]])
            o_ref[rows, vs[h]] = (on * _silu(ra_ref[rows, vs[h]])).astype(o_ref.dtype)
        return carry

    lax.fori_loop(0, rows // ch, chunk, 0)

    @pl.when(c == nc - 1)
    def _():
        for h in range(GLA_HEADS):
            sfin_ref[h] = st_scr[h].T


def _gla_prompt(z, zv, la, s0, gn_g, n, l, rows):
    t = n * l
    dk = la.shape[1] // GLA_HEADS
    dv = gn_g.shape[1] // GLA_HEADS
    wk, wv = GLA_HEADS * dk, GLA_HEADS * dv
    cpb = l // rows
    row = lambda b, c: b * cpb + c
    return pl.pallas_call(
        functools.partial(_gla_kernel, rows=rows, dk=dk, dv=dv),
        grid=(n, cpb),
        in_specs=[pl.BlockSpec((rows, wk), lambda b, c: (row(b, c), 0)),
                  pl.BlockSpec((rows, wk), lambda b, c: (row(b, c), 1)),
                  pl.BlockSpec((rows, wv), lambda b, c: (row(b, c), 0)),
                  pl.BlockSpec((rows, wv), lambda b, c: (row(b, c), 1)),
                  pl.BlockSpec((rows, wk), lambda b, c: (row(b, c), 0)),
                  pl.BlockSpec((None, GLA_HEADS, dk, dv), lambda b, c: (b, 0, 0, 0)),
                  pl.BlockSpec((1, wv), lambda b, c: (0, 0))],
        out_specs=[pl.BlockSpec((rows, wv), lambda b, c: (row(b, c), 0)),
                   pl.BlockSpec((None, GLA_HEADS, dk, dv), lambda b, c: (b, 0, 0, 0))],
        out_shape=[jax.ShapeDtypeStruct((t, wv), BF16), jax.ShapeDtypeStruct((n, GLA_HEADS, dk, dv), F32)],
        scratch_shapes=[pltpu.VMEM((GLA_HEADS, dv, dk), F32)],
        compiler_params=_cparams("arbitrary", "arbitrary"),
        name="gla_prompt",
    )(z, z, zv, z, la, s0, gn_g)


def _gla_step_kernel(q_ref, k_ref, la_ref, v_ref, ra_ref, s_ref, gn_ref, o_ref, so_ref, *, dk):
    sn = jnp.exp(la_ref[...]) * s_ref[...] + k_ref[...] * v_ref[...]
    so_ref[...] = sn
    o = jnp.sum((q_ref[...] * dk ** -0.5) * sn, axis=1, keepdims=True)
    on = _rms(o, gn_ref[...])
    o_ref[...] = (on * _silu(ra_ref[...])).astype(o_ref.dtype)


def _gla_step(q, k, la, v, ra, s, gn_g):
    nb, nh, dk, dv = s.shape
    col = pl.BlockSpec((None, nh, dk, 1), lambda b: (b, 0, 0, 0))
    rowv = pl.BlockSpec((None, nh, 1, dv), lambda b: (b, 0, 0, 0))
    st = pl.BlockSpec((None, nh, dk, dv), lambda b: (b, 0, 0, 0))
    o, so = pl.pallas_call(
        functools.partial(_gla_step_kernel, dk=dk),
        grid=(nb,),
        in_specs=[col, col, col, rowv, rowv, st, pl.BlockSpec((nh, 1, dv), lambda b: (0, 0, 0))],
        out_specs=[rowv, st],
        out_shape=[jax.ShapeDtypeStruct((nb, nh, 1, dv), BF16), jax.ShapeDtypeStruct(s.shape, F32)],
        compiler_params=_cparams("arbitrary"),
        name="gla_step",
    )(q.reshape(nb, nh, dk, 1), k.reshape(nb, nh, dk, 1), la.reshape(nb, nh, dk, 1),
      v.reshape(nb, nh, 1, dv), ra.reshape(nb, nh, 1, dv), s, gn_g.reshape(nh, 1, dv))
    return o.reshape(nb, nh * dv), so


def _band_kernel(q_ref, k_ref, kp_ref, v_ref, vp_ref, o_ref, lse_ref, *, dil, qb):
    i = pl.program_id(1)
    blk = ATT_BLOCK
    qi = lax.broadcasted_iota(jnp.int32, (blk, 2 * blk), 0)
    ki = lax.broadcasted_iota(jnp.int32, (blk, 2 * blk), 1)
    dist = blk + qi - ki
    band = jnp.logical_and(dist >= 0, dist <= ATT_BLOCK)
    band_first = jnp.logical_and(band, ki >= jnp.where(i > 0, 0, blk))
    scale = ATT_HD ** -0.5
    for r in range(dil):
        for sb in range(qb // blk):
            rs = slice(sb * blk, (sb + 1) * blk)
            q = q_ref[r, rs, :].astype(BF16)
            if sb == 0:
                kprev, vprev, mask = kp_ref[r], vp_ref[r], band_first
            else:
                ps = slice((sb - 1) * blk, sb * blk)
                kprev, vprev, mask = k_ref[r, ps, :], v_ref[r, ps, :], band
            kk = jnp.concatenate([kprev, k_ref[r, rs, :]], axis=0).astype(BF16)
            vv = jnp.concatenate([vprev, v_ref[r, rs, :]], axis=0).astype(BF16)
            s = jnp.where(mask, _dot_nt(q, kk) * scale, NEG_BIG)
            m = jnp.max(s, axis=-1, keepdims=True)
            p = jnp.exp(s - m)
            lsum = jnp.sum(p, axis=-1, keepdims=True)
            o = _dot(p.astype(BF16), vv) / lsum
            lse = jnp.broadcast_to(m + jnp.log(lsum), (blk, ATT_HD))
            rows = rs if dil == 1 else pl.ds(sb * blk * dil + r, blk, stride=dil)
            o_ref[rows, :] = o
            lse_ref[rows, :] = lse


def _band_attention(qa, ka, va, cq, ck, cv, n, l, dil):
    lq = l // dil
    hd = ATT_HD
    tokens = min(ATT_TOKENS, l)
    qb = tokens // dil
    nqb = lq // qb
    sub = qb // ATT_BLOCK

    def main(c0):
        return pl.BlockSpec((None, dil, qb, hd), lambda b, i, h: (b, 0, i, c0 + h))

    def prev(c0):
        return pl.BlockSpec((None, dil, ATT_BLOCK, hd), lambda b, i, h: (b, 0, jnp.maximum(i * sub - 1, 0), c0 + h))

    out = pl.BlockSpec((tokens, hd), lambda b, i, h: (b * nqb + i, h))
    shp = jax.ShapeDtypeStruct((n * l, ATT_HEADS * hd), F32)
    return pl.pallas_call(
        functools.partial(_band_kernel, dil=dil, qb=qb),
        grid=(n, nqb, ATT_HEADS),
        in_specs=[main(cq), main(ck), prev(ck), main(cv), prev(cv)],
        out_specs=[out, out],
        out_shape=[shp, shp],
        compiler_params=_cparams("arbitrary", "arbitrary", "arbitrary"),
        name=f"band_attention_d{dil}",
    )(qa, ka, ka, va, va)


def _att_step_kernel(q_ref, kn_ref, vn_ref, c0_ref, c1_ref, c2_ref, o_ref):
    scale = ATT_HD ** -0.5
    outs, lses = [], []
    for g, c_ref in enumerate((c0_ref, c1_ref, c2_ref)):
        q, kn, vn = q_ref[g], kn_ref[g], vn_ref[g]
        s = jnp.sum(c_ref[:, 0] * q[None], axis=-1, keepdims=True) * scale
        sn = jnp.sum(kn * q, axis=-1, keepdims=True) * scale
        m = jnp.maximum(jnp.max(s, axis=0), sn)
        p = jnp.exp(s - m[None])
        pn = jnp.exp(sn - m)
        lsum = jnp.sum(p, axis=0) + pn
        outs.append((jnp.sum(p * c_ref[:, 1], axis=0) + pn * vn) / lsum)
        lses.append(m + jnp.log(lsum))
    mx = functools.reduce(jnp.maximum, lses)
    es = [jnp.exp(x - mx) for x in lses]
    inv = 1.0 / functools.reduce(lambda a, b: a + b, es)
    o_ref[...] = functools.reduce(lambda a, b: a + b, [e * inv * o for e, o in zip(es, outs)]).astype(o_ref.dtype)


def _att_step(qs, ks, vs, caches):
    nb = qs.shape[0]
    views, cspecs = [], []
    for (window, dil), cch in zip(ATT_GROUPS, caches):
        assert cch.shape[1] == window, "sample step expects full caches"
        views.append(cch.reshape((nb, window // dil, dil) + cch.shape[2:]))
        cspecs.append(pl.BlockSpec((None, ATT_BLOCK, None) + cch.shape[2:], lambda b: (b, 0, 0, 0, 0, 0)))
    row = pl.BlockSpec((None,) + qs.shape[1:], lambda b: (b, 0, 0, 0))
    o = pl.pallas_call(
        _att_step_kernel,
        grid=(nb,),
        in_specs=[row] * 3 + cspecs,
        out_specs=pl.BlockSpec((None, ATT_HEADS, ATT_HD), lambda b: (b, 0, 0)),
        out_shape=jax.ShapeDtypeStruct((nb, ATT_HEADS, ATT_HD), BF16),
        compiler_params=_cparams("arbitrary"),
        name="att_step",
    )(qs, ks, vs, *views)
    return o.reshape(nb, ATT_HEADS * ATT_HD)


MERGE_PART = 256


def _merge_kernel(*refs, n_att, n_alias, n_tiles):
    oa_ref = refs[0]
    att = refs[1:1 + n_att]
    (gza_ref, gzb_ref, x_ref, g1_ref, sc2_ref, sh2_ref, n2_ref, wa_ref, wb_ref, wo_ref,
     wq_ref) = refs[1 + n_att:12 + n_att]
    x1_ref, h2_ref, pq_ref = refs[12 + n_att + n_alias:]
    tm = x_ref.shape[0]
    pr = min(MERGE_PART, tm)
    parts = [slice(k * pr, (k + 1) * pr) for k in range(tm // pr)]
    mod = lambda ref, part: ref[...] if ref.shape[0] == 1 else ref[part, :]
    add = lambda a, b: a + b

    def attention_out(part):
        if n_att == 1:
            return att[0][part, :]
        ng = n_att // 2
        ls = [r[part, :] for r in att[ng:]]
        mx = functools.reduce(jnp.maximum, ls)
        es = [jnp.exp(x - mx) for x in ls]
        inv = 1.0 / functools.reduce(add, es)
        return functools.reduce(add, [e * inv * r[part, :] for e, r in zip(es, att[:ng])]).astype(BF16)

    @pl.when(pl.program_id(0) >= n_tiles)
    def _():
        h2_ref[...] = jnp.zeros_like(h2_ref)
        pq_ref[...] = jnp.zeros_like(pq_ref)

    @pl.when(pl.program_id(0) < n_tiles)
    def _():
        _merge_rows(parts, attention_out, mod, oa_ref, gza_ref, gzb_ref, x_ref, g1_ref, sc2_ref, sh2_ref, n2_ref,
                    wa_ref, wb_ref, wo_ref, wq_ref, x1_ref, h2_ref, pq_ref)
        if h2_ref.shape[0] > tm:
            h2_ref[tm:] = jnp.zeros((h2_ref.shape[0] - tm, h2_ref.shape[1]), h2_ref.dtype)
            pq_ref[tm:] = jnp.zeros((pq_ref.shape[0] - tm, pq_ref.shape[1]), pq_ref.dtype)


def _merge_rows(parts, attention_out, mod, oa_ref, gza_ref, gzb_ref, x_ref, g1_ref, sc2_ref, sh2_ref, n2_ref,
                wa_ref, wb_ref, wo_ref, wq_ref, x1_ref, h2_ref, pq_ref):
    ob = [attention_out(p) for p in parts]
    ma = [_dot(oa_ref[p, :], wa_ref[...]) for p in parts]
    mb = [_dot(o, wb_ref[...]) for o in ob]
    merged = [(_sigmoid(gza_ref[p, :]) * a + _sigmoid(gzb_ref[p, :]) * b).astype(BF16)
              for p, a, b in zip(parts, ma, mb)]
    y = [_dot(m, wo_ref[...]) for m in merged]
    h2 = []
    for p, yy in zip(parts, y):
        x1 = x_ref[p, :] + mod(g1_ref, p) * yy
        x1_ref[p, :] = x1
        h2.append((_rms(x1, n2_ref[...]) * (1.0 + mod(sc2_ref, p)) + mod(sh2_ref, p)).astype(BF16))
    for p, h in zip(parts, h2):
        h2_ref[p, :] = h
        pq_ref[p, :] = _dot(h, wq_ref[...])


def _merge(oa, att, z, x2, g1, sc2, sh2, n2g, wa, wb, wo, wq, tm, rows_per_mod, gza_blk, gzb_blk,
           peer_rows, peer_tile, peer_blk0, alias=(), zero_tiles=0):
    t, d = x2.shape
    r = g1.shape[1]
    wqn = wq.shape[1]
    tiles_per_mod = rows_per_mod // tm
    n_tiles = t // tm
    row = lambda i: jnp.minimum(i, n_tiles - 1)
    tok = lambda w: pl.BlockSpec((tm, w), lambda i: (row(i), 0))
    mod = pl.BlockSpec((None, r, d), lambda i: (row(i) // tiles_per_mod, 0, 0))
    const = lambda a: pl.BlockSpec(a.shape, lambda i: (0, 0), pipeline_mode=pl.Buffered(1))
    peer = lambda w: pl.BlockSpec((peer_tile, w), lambda i: (peer_blk0 + i, 0))
    n_in = 12 + len(att)
    return pl.pallas_call(
        functools.partial(_merge_kernel, n_att=len(att), n_alias=len(alias), n_tiles=n_tiles),
        grid=(n_tiles + zero_tiles,),
        in_specs=[tok(oa.shape[1])] + [tok(a.shape[1]) for a in att]
                 + [pl.BlockSpec((tm, d), lambda i: (row(i), gza_blk)),
                    pl.BlockSpec((tm, d), lambda i: (row(i), gzb_blk)),
                    tok(d), mod, mod, mod, const(n2g), const(wa), const(wb), const(wo), const(wq)]
                 + [pl.BlockSpec(memory_space=pl.ANY)] * len(alias),
        out_specs=[tok(d), peer(d), peer(wqn)],
        out_shape=[jax.ShapeDtypeStruct((t, d), F32), jax.ShapeDtypeStruct((peer_rows, d), BF16),
                   jax.ShapeDtypeStruct((peer_rows, wqn), F32)],
        input_output_aliases={n_in + k: 1 + k for k in range(len(alias))},
        compiler_params=_cparams("arbitrary"),
        name="merge",
    )(oa, *att, z, z, x2, g1, sc2, sh2, n2g, wa, wb, wo, wq, *alias)


def _strict_max_below(rows, prev):
    m = None
    for x in rows:
        y = jnp.where(x < prev, x, NEG_BIG) if prev is not None else x
        m = y if m is None else jnp.maximum(m, y)
    return m


def _oddeven_merge(lo, hi, r):
    step = r * 2
    if step < hi - lo:
        yield from _oddeven_merge(lo, hi, step)
        yield from _oddeven_merge(lo + r, hi, step)
        yield from [(i, i + r) for i in range(lo + r, hi - r, step)]
    else:
        yield (lo, lo + r)


def _oddeven_merge_sort(lo, hi):
    if hi - lo >= 1:
        mid = lo + (hi - lo) // 2
        yield from _oddeven_merge_sort(lo, mid)
        yield from _oddeven_merge_sort(mid + 1, hi)
        yield from _oddeven_merge(lo, hi, 1)


def _exchange(p, i, j):
    p[i], p[j] = jnp.maximum(p[i], p[j]), jnp.minimum(p[i], p[j])


def _topk_desc(s, k):
    assert s.shape[0] == k * SUBLANES and k & (k - 1) == 0
    p = [s[i * SUBLANES:(i + 1) * SUBLANES] for i in range(k)]
    for i, j in _oddeven_merge_sort(0, k - 1):
        _exchange(p, i, j)
    shift = SUBLANES // 2
    while shift:
        q = [pltpu.roll(v, shift, 0) for v in p]
        p = [jnp.maximum(p[i], q[k - 1 - i]) for i in range(k)]
        stride = k // 2
        while stride:
            for i in range(k):
                if not i & stride:
                    _exchange(p, i, i + stride)
            stride //= 2
        shift //= 2
    return [v[0:1] for v in p]


def _route_kernel(pq_ref, k1_ref, k2_ref, n1_ref, e1_ref, r2_ref, e2_ref, s1_scr, s2_scr):
    nk = PEER_NKEYS
    pq = pq_ref[...].astype(BF16)
    s1_scr[...] = _dot_nt(k1_ref[...], pq)
    s2_scr[...] = _dot_nt(k2_ref[...], pq)
    v1h, v2h = [], []
    for h in range(PEER_HEADS):
        v1h.append(_topk_desc(s1_scr[h * nk:(h + 1) * nk], PEER_TOPK))
        v2h.append(_topk_desc(s2_scr[h * nk:(h + 1) * nk], PEER_TOPK))
    v1 = [jnp.concatenate([v1h[h][k] for h in range(PEER_HEADS)], axis=0) for k in range(PEER_TOPK)]
    v2 = [jnp.concatenate([v2h[h][k] for h in range(PEER_HEADS)], axis=0) for k in range(PEER_TOPK)]
    pairs = [(i, j) for i in range(PEER_TOPK) for j in range(PEER_TOPK) if (i + 1) * (j + 1) <= PEER_TOPK]
    cands = [v1[i] + v2[j] for i, j in pairs]
    tau = None
    for _ in range(PEER_TOPK):
        tau = _strict_max_below(cands, tau)
    cmax = v1[0] + v2[0]
    zsum = None
    counts = [None] * PEER_TOPK
    for (i, _), cnd in zip(pairs, cands):
        sel = cnd >= tau
        e = jnp.where(sel, jnp.exp(cnd - cmax), 0.0)
        zsum = e if zsum is None else zsum + e
        one = jnp.where(sel, 1.0, 0.0)
        counts[i] = one if counts[i] is None else counts[i] + one
    zinv = 1.0 / zsum
    for h in range(PEER_HEADS):
        a = s1_scr[h * nk:(h + 1) * nk]
        b = s2_scr[h * nk:(h + 1) * nk]
        n1 = jnp.zeros_like(a)
        r2 = jnp.full_like(b, float(PEER_TOPK))
        for k in range(PEER_TOPK):
            n1 = jnp.where(a == v1h[h][k], counts[k][h:h + 1, :], n1)
            r2 = jnp.where(b == v2h[h][k], float(k), r2)
        n1_ref[h] = n1
        e1_ref[h] = jnp.exp(a - v1h[h][0]) * zinv[h:h + 1, :]
        r2_ref[h] = r2.astype(r2_ref.dtype)
        e2_ref[h] = jnp.exp(b - v2h[h][0]).astype(e2_ref.dtype)


def _peer_route(pq, k1big, k2big, tt):
    t, w = pq.shape
    nh, nk = PEER_HEADS, PEER_NKEYS
    sspec = pl.BlockSpec((nh, nk, tt), lambda i: (0, 0, i))
    shp = lambda dt: jax.ShapeDtypeStruct((nh, nk, t), dt)
    return pl.pallas_call(
        _route_kernel,
        grid=(t // tt,),
        in_specs=[pl.BlockSpec((tt, w), lambda i: (i, 0)),
                  pl.BlockSpec(k1big.shape, lambda i: (0, 0)), pl.BlockSpec(k2big.shape, lambda i: (0, 0))],
        out_specs=[sspec] * 4,
        out_shape=[shp(F32), shp(F32), shp(BF16), shp(BF16)],
        scratch_shapes=[pltpu.VMEM((nh * nk, tt), F32)] * 2,
        compiler_params=_cparams("arbitrary"),
        name="peer_route",
    )(pq, k1big, k2big)


def _gelu_tanh(x):
    return 0.5 * x * (1.0 + jnp.tanh(GELU_C * (x + 0.044715 * (x * x * x))))


BF16_ROWS = 16
GATE_LANES = 256
PEER_PART = 256
PEER_TILE = 2048


def _gelu_tanh_bf16(x):
    hx = 0.5 * x
    return hx + hx * jnp.tanh(x * (GELU_C + (GELU_C * 0.044715) * (x * x)))


def _bf16_rows(row):
    return jnp.broadcast_to(row, (BF16_ROWS, row.shape[1])).astype(BF16)


def _cache_shift_step(step, caches, new_rows, outs, sems):
    nb = caches[0].shape[0]
    jobs = len(caches) * nb

    def for_job(job, act):
        b, slot = job % nb, job % 2
        for c, (cache, new, out) in enumerate(zip(caches, new_rows, outs)):
            @pl.when(job // nb == c)
            def _(cache=cache, new=new, out=out):
                w = cache.shape[1]
                act(pltpu.make_async_copy(cache.at[b, pl.ds(1, w - 1)], out.at[b, pl.ds(0, w - 1)], sems.at[slot, 0]))
                act(pltpu.make_async_copy(new.at[b], out.at[b, pl.ds(w - 1, 1)], sems.at[slot, 1]))

    @pl.when(jnp.logical_and(step >= 1, step <= jobs))
    def _():
        for_job(step - 1, lambda cp: cp.wait())

    @pl.when(step < jobs)
    def _():
        for_job(step, lambda cp: cp.start())


def _peer_dense_kernel(h2_ref, u_ref, vt_ref, n1_ref, e1_ref, r2_ref, e2_ref, *rest, et, n_kv):
    caches, new_rows = rest[:n_kv], rest[n_kv:2 * n_kv]
    o_ref, kv_outs = rest[2 * n_kv], rest[2 * n_kv + 1:3 * n_kv + 1]
    acc_scr, wg_scr, st_scr = rest[3 * n_kv + 1:3 * n_kv + 4]
    e = pl.program_id(1)
    nk = PEER_NKEYS
    tt = h2_ref.shape[0]
    if n_kv:
        _cache_shift_step(pl.program_id(0) * pl.num_programs(1) + e, caches, new_rows, kv_outs, rest[-1])

    @pl.when(e == 0)
    def _():
        acc_scr[...] = jnp.zeros_like(acc_scr)

    zero = jnp.zeros((), BF16)
    groups = nk // BF16_ROWS

    def scores(part):
        st_scr[part, :] = _dot_nt(u_ref[part, :], h2_ref[...]).astype(BF16)

    def gates(ii):
        rs = slice(ii * nk, (ii + 1) * nk)
        for lc in range(tt // GATE_LANES):
            ls = slice(lc * GATE_LANES, (lc + 1) * GATE_LANES)
            w = None
            for h in range(PEER_HEADS):
                n1 = _bf16_rows(n1_ref[h, ii:ii + 1, ls])
                e1 = _bf16_rows(e1_ref[h, ii:ii + 1, ls])
                r2 = r2_ref[h, :, ls].reshape(groups, BF16_ROWS, GATE_LANES)
                e2 = e2_ref[h, :, ls].reshape(groups, BF16_ROWS, GATE_LANES)
                c = jnp.where(r2 < n1[None], e2, zero) * e1[None]
                w = c if w is None else w + c
            wg_scr[rs, ls] = w.reshape(nk, GATE_LANES) * _gelu_tanh_bf16(st_scr[rs, ls])

    def mix(part):
        return _dot(vt_ref[:, part], wg_scr[part, :])

    per = PEER_PART // nk
    parts = [slice(p * PEER_PART, (p + 1) * PEER_PART) for p in range(et // PEER_PART)]
    out = None
    scores(parts[0])
    for p, part in enumerate(parts):
        if p + 1 < len(parts):
            scores(parts[p + 1])
        for ii in range(p * per, (p + 1) * per):
            gates(ii)
        if p > 0:
            d = mix(parts[p - 1])
            out = d if out is None else out + d
    d = mix(parts[-1])
    acc_scr[...] += d if out is None else out + d

    @pl.when(e == pl.num_programs(1) - 1)
    def _():
        o_ref[...] = acc_scr[...].T


def _peer_dense(h2, u_b, vt_b, n1, e1, r2, e2, tt, et, caches=(), new_rows=()):
    t, d = h2.shape
    ne = u_b.shape[0]
    nh, nk = PEER_HEADS, PEER_NKEYS
    n_kv = len(caches)
    grid = (t // tt, ne // et)
    assert grid[0] * grid[1] > n_kv * (caches[0].shape[0] if n_kv else 0), "one cache-shift job per grid step"
    rowside = pl.BlockSpec((nh, et // nk, tt), lambda i, e: (0, e, i))
    colside = pl.BlockSpec((nh, nk, tt), lambda i, e: (0, 0, i))
    anywhere = pl.BlockSpec(memory_space=pl.ANY)
    outs = pl.pallas_call(
        functools.partial(_peer_dense_kernel, et=et, n_kv=n_kv),
        grid=grid,
        in_specs=[pl.BlockSpec((tt, d), lambda i, e: (i, 0)),
                  pl.BlockSpec((et, d), lambda i, e: (e, 0)),
                  pl.BlockSpec((d, et), lambda i, e: (0, e)),
                  rowside, rowside, colside, colside] + [anywhere] * (2 * n_kv),
        out_specs=[pl.BlockSpec((tt, d), lambda i, e: (i, 0))] + [anywhere] * n_kv,
        out_shape=[jax.ShapeDtypeStruct((t, d), F32)] + [jax.ShapeDtypeStruct(c.shape, c.dtype) for c in caches],
        scratch_shapes=[pltpu.VMEM((d, tt), F32), pltpu.VMEM((et, tt), BF16), pltpu.VMEM((et, tt), BF16)]
                       + ([pltpu.SemaphoreType.DMA((2, 2))] if n_kv else []),
        compiler_params=_cparams("arbitrary", "arbitrary"),
        name="peer_dense",
    )(h2, u_b, vt_b, n1, e1, r2, e2, *caches, *new_rows)
    return outs[0], outs[1:]


def _final_kernel(x1_ref, p_ref, g2_ref, fg_ref, y_ref):
    y_ref[...] = _rms(x1_ref[...] + g2_ref[...] * p_ref[...], fg_ref[...])


def _final(x1, p, g2, fg, tm, rows_per_mod, p_blk0):
    t, d = x1.shape
    r = g2.shape[1]
    tiles_per_mod = rows_per_mod // tm
    tok = pl.BlockSpec((tm, d), lambda i: (i, 0))
    return pl.pallas_call(
        _final_kernel,
        grid=(t // tm,),
        in_specs=[tok, pl.BlockSpec((tm, d), lambda i: (p_blk0 + i, 0)),
                  pl.BlockSpec((None, r, d), lambda i: (i // tiles_per_mod, 0, 0)),
                  pl.BlockSpec((1, d), lambda i: (0, 0))],
        out_specs=tok,
        out_shape=jax.ShapeDtypeStruct((t, d), F32),
        compiler_params=_cparams("arbitrary"),
        name="final_norm",
    )(x1, p, g2, fg)


def _rope_tables(pos):
    half = ROPE_DIM // 2
    inv_freq = ROPE_THETA ** (-jnp.arange(half, dtype=F32) / half)
    ang = pos.astype(F32)[:, None] * inv_freq[None, :]
    cos, sin = jnp.cos(ang), jnp.sin(ang)
    n = pos.shape[0]
    ones = jnp.ones((n, LANES - ROPE_DIM), F32)
    zeros = jnp.zeros((n, LANES - ROPE_DIM), F32)
    return (jnp.concatenate([cos, cos, ones], axis=1), jnp.concatenate([-sin, sin, zeros], axis=1))


def _block_diag_keys(keys, half):
    nk, dh = keys.shape
    eye = jnp.eye(PEER_HEADS, dtype=keys.dtype)
    blk = jnp.zeros((PEER_HEADS, nk, PEER_HEADS, 2, dh), keys.dtype)
    blk = blk.at[:, :, :, half, :].set(eye[:, None, :, None] * keys[None, :, None, :])
    return blk.reshape(PEER_HEADS * nk, PEER_HEADS * 2 * dh).astype(BF16)


def _pick_tile(n, pref):
    t = min(pref, n)
    while n % t:
        t //= 2
    return t


def kernel(x_prompt, x_sample, c_prompt, c_sample, state_gla, cache_kv_w128, cache_kv_w512, cache_kv_w2048,
           ada_w, ada_b, norm1_g, w_in, gla_a_w2, gla_a_b, gla_gn_g, w_branch_a, w_branch_b, w_out,
           norm2_g, peer_wq, peer_k1, peer_k2, peer_u, peer_v, final_g):
    depth = ada_w.shape[0]
    assert depth == 1, "single-layer trunk"
    n_p, l_p, d = x_prompt.shape
    n_s, l_s, _ = x_sample.shape
    assert l_s == 1, "sample group decodes one token per sequence"
    caches = (cache_kv_w128[0], cache_kv_w512[0], cache_kv_w2048[0])
    n_groups = len(ATT_GROUPS)

    w_qa = gla_a_w2.shape[2]
    w_va = gla_gn_g.shape[1]
    w_ob = ATT_HEADS * ATT_HD
    w_qb = n_groups * w_ob
    assert w_qa == TN_IN and w_va == 2 * TN_IN and d == 2 * TN_IN and w_ob == TN_IN
    widths = (w_qa, w_qa, w_va, w_va, GLA_RANK, w_qb, w_qb, w_qb, d, d)
    offs = [0]
    for w in widths:
        offs.append(offs[-1] + w)
    seg = lambda i: w_in[0][:, offs[i]:offs[i + 1]]
    att_col = lambda g, k: seg(5 + k)[:, g * w_ob:(g + 1) * w_ob]
    att_cols = [att_col(g, k) for g in range(n_groups) for k in range(3)]
    w_main = jnp.concatenate([seg(0), seg(1), seg(3), seg(8), seg(9), seg(2)] + att_cols, axis=1).astype(BF16)
    w_kv = jnp.concatenate([att_col(g, k) for g in range(n_groups) for k in (1, 2)], axis=1).astype(BF16)
    n_tiles = w_main.shape[1] // TN_IN
    ra_off, gza_blk, gzb_blk = 2 * w_qa, 2, 3
    w_alr = jnp.pad(seg(4), ((0, 0), (0, LANES - GLA_RANK))).astype(BF16)
    w2p = jnp.pad(gla_a_w2[0], ((0, LANES - GLA_RANK), (0, 0))).astype(BF16)
    a_b = gla_a_b[0].reshape(1, w_qa)
    g1n = norm1_g[0].reshape(1, d)
    gn_g = gla_gn_g[0].reshape(1, w_va)
    n2g = norm2_g[0].reshape(1, d)
    wa, wb = w_branch_a[0].astype(BF16), w_branch_b[0].astype(BF16)
    wo, wq = w_out[0].astype(BF16), peer_wq[0].astype(BF16)
    q_tile = lambda g: ATT_TILE0 + 3 * g
    rope_tiles = [q_tile(g) + k for g in range(n_groups) for k in (0, 1)]

    n_mod = n_p + n_s
    n_pad = -(-n_mod // SUBLANES) * SUBLANES
    c_all = jnp.concatenate([c_prompt, c_sample, jnp.zeros((n_pad - n_mod, d), F32)], axis=0)
    mod = _modulation(c_all, ada_w[0], ada_b[0])
    mods_p = [mod[:n_p, i * d:(i + 1) * d].reshape(n_p, 1, d) for i in range(6)]
    mods_s = [mod[n_p:n_mod, i * d:(i + 1) * d].reshape(1, n_s, d) for i in range(6)]

    t_p = n_p * l_p
    tt = _pick_tile(t_p, 512)
    t_all = t_p + (-(-n_s // tt)) * tt
    assert t_all - t_p == tt

    xp2 = x_prompt.reshape(t_p, d)
    tm_p = _pick_tile(l_p, 1024)
    pos_p = jnp.arange(l_p, dtype=jnp.int32)
    split = []
    for g, (window, dil) in enumerate(ATT_GROUPS):
        assert window // dil == ATT_BLOCK
        if dil > 1:
            split += [(q_tile(g) + k, len(split) + k, dil) for k in range(3)]
    first_split = min([t for t, _, _ in split], default=n_tiles)
    outs = _in_proj(xp2, g1n, mods_p[1], mods_p[0], w_main, w_alr, w2p, a_b, pos_p, tm_p, l_p,
                    ((0, VA_TILE0, F32), (VA_TILE0, first_split, BF16)), rope_tiles, split)
    z_p, zb_p, la_p, split_p = outs[0], outs[1], outs[2], outs[3:]
    s0_p = jnp.zeros((n_p,) + state_gla.shape[2:], F32)
    oa_p, sfin_p = _gla_prompt(z_p, zb_p, la_p, s0_p, gn_g, n_p, l_p, _pick_tile(l_p, 256))
    zb4 = zb_p.reshape(n_p, 1, l_p, zb_p.shape[1])
    att_o, att_l = [], []
    for g, (window, dil) in enumerate(ATT_GROUPS):
        if dil > 1:
            qa, ka, va = [split_p[slot] for t, slot, _ in split if q_tile(g) <= t < q_tile(g) + 3]
            o_g, lse_g = _band_attention(qa, ka, va, 0, 0, 0, n_p, l_p, dil)
        else:
            c0 = lambda k: (q_tile(g) + k - VA_TILE0) * (TN_IN // ATT_HD)
            o_g, lse_g = _band_attention(zb4, zb4, zb4, c0(0), c0(1), c0(2), n_p, l_p, dil)
        att_o.append(o_g)
        att_l.append(lse_g)
    keep_max = min(max(w for w, _ in ATT_GROUPS), l_p)
    x_tail = x_prompt[:, l_p - keep_max:].reshape(n_p * keep_max, d)
    kv_tiles = 2 * n_groups
    z_kv = _in_proj(x_tail, g1n, mods_p[1], mods_p[0], w_kv, w_alr, w2p, a_b, pos_p[l_p - keep_max:],
                    _pick_tile(keep_max, 1024), keep_max, ((0, kv_tiles, F32),), range(0, kv_tiles, 2), [])[0]
    z_kv = z_kv.reshape(n_p, keep_max, kv_tiles, ATT_HEADS, ATT_HD)
    kv_p = []
    for g, (window, dil) in enumerate(ATT_GROUPS):
        keep = min(window, l_p)
        kv_p.append(z_kv[:, keep_max - keep:, 2 * g:2 * g + 2][None])
    tm_m = _pick_tile(l_p, 512)
    x1_p, h2_all, pq_all = _merge(oa_p, att_o + att_l, z_p, xp2, mods_p[2], mods_p[4], mods_p[3], n2g,
                                  wa, wb, wo, wq, tm_m, l_p, gza_blk, gzb_blk, t_all, tm_m, 0,
                                  zero_tiles=(t_all - t_p) // tm_m)

    xs2 = x_sample.reshape(n_s, d)
    pos_s = jnp.full((n_s,), PAST_LEN, dtype=jnp.int32)
    z_s, la_s = _in_proj(xs2, g1n, mods_s[1], mods_s[0], w_main, w_alr, w2p, a_b, pos_s, n_s, n_s,
                         ((0, n_tiles, F32),), rope_tiles, [])
    col = lambda tile: z_s[:, tile * TN_IN:(tile + 1) * TN_IN]
    oa_s, state_s = _gla_step(col(0), col(1), la_s, z_s[:, VA_TILE0 * TN_IN:VA_TILE0 * TN_IN + w_va],
                              z_s[:, ra_off:ra_off + w_va], state_gla[0], gn_g)
    stack = lambda k: jnp.stack([col(q_tile(g) + k) for g in range(n_groups)], axis=1).reshape(
        n_s, n_groups, ATT_HEADS, ATT_HD)
    ob_s = _att_step(stack(0), stack(1), stack(2), caches)
    x1_s, h2_all, pq_all = _merge(oa_s, [ob_s], z_s, xs2, mods_s[2], mods_s[4], mods_s[3], n2g,
                                  wa, wb, wo, wq, n_s, n_s, gza_blk, gzb_blk, t_all, tt, t_p // tt,
                                  alias=(h2_all, pq_all))
    new_rows = [jnp.concatenate([col(q_tile(g) + 1), col(q_tile(g) + 2)], axis=1).reshape(
        (n_s, 1) + cch.shape[2:]) for g, cch in enumerate(caches)]

    k1big = _block_diag_keys(peer_k1[0], 0)
    k2big = _block_diag_keys(peer_k2[0], 1)
    n1, e1, r2, e2 = _peer_route(pq_all, k1big, k2big, _pick_tile(tt, 256))
    u_b = peer_u[0].astype(BF16)
    vt_b = peer_v[0].astype(BF16).T
    p_all, kv_s = _peer_dense(h2_all, u_b, vt_b, n1, e1, r2, e2, tt, PEER_TILE, caches, new_rows)
    kv_s = [a[None] for a in kv_s]

    fg = final_g.reshape(1, d)
    y_p = _final(x1_p, p_all, mods_p[5], fg, tm_m, l_p, 0)
    y_s = _final(x1_s, p_all, mods_s[5], fg, n_s, n_s, t_p // n_s)

    return (y_p.reshape(n_p, l_p, d), y_s.reshape(n_s, l_s, d), sfin_p[None], state_s[None],
            kv_p[0], kv_s[0], kv_p[1], kv_s[1], kv_p[2], kv_s[2])
```

```python
import functools
import math

import jax
import jax.numpy as jnp
from jax import lax
from jax.experimental import pallas as pl
from jax.experimental.pallas import tpu as pltpu

F32 = jnp.float32
BF16 = jnp.bfloat16

PAST_LEN = 16384
GLA_HEADS = 4
GLA_RANK = 16
GLA_TAU = 16.0
GLA_CHUNK = 64
ATT_GROUPS = ((128, 1), (512, 4), (2048, 16))
ATT_HEADS = 4
ATT_HD = 128
ATT_BLOCK = 128
ATT_TOKENS = 2048
ROPE_DIM = ATT_HD // 4
ROPE_THETA = 500000.0
PEER_HEADS = 8
PEER_NKEYS = 128
PEER_DQ = 128
PEER_TOPK = 16
NORM_EPS = 1e-6
GELU_C = math.sqrt(2.0 / math.pi)

LANES = 128
SUBLANES = 8
VMEM_LIMIT = 56 * 1024 * 1024
NEG_BIG = -1e30


def _cparams(*sem):
    return pltpu.CompilerParams(dimension_semantics=sem, vmem_limit_bytes=VMEM_LIMIT)


def _sigmoid(x):
    return 1.0 / (1.0 + jnp.exp(-x))


def _silu(x):
    return x * _sigmoid(x)


def _log_sigmoid(x):
    return jnp.minimum(x, 0.0) - jnp.log(1.0 + jnp.exp(-jnp.abs(x)))


def _rms(x, g):
    return x * lax.rsqrt(jnp.mean(x * x, axis=-1, keepdims=True) + NORM_EPS) * g


def _dot(a, b):
    return jnp.dot(a, b, preferred_element_type=F32)


def _dot_nt(a, b):
    return lax.dot_general(a, b, (((1,), (1,)), ((), ())), preferred_element_type=F32)


def _dot_tn(a, b):
    return lax.dot_general(a, b, (((0,), (0,)), ((), ())), preferred_element_type=F32)


def _mod_kernel(c_ref, w_ref, b_ref, o_ref):
    s = _silu(c_ref[...]).astype(BF16)
    o_ref[...] = _dot(s, w_ref[...].astype(BF16)) + b_ref[...]


def _modulation(c, ada_w, ada_b):
    n, d = c.shape
    n6 = ada_w.shape[1]
    tn = n6 // 4
    return pl.pallas_call(
        _mod_kernel,
        grid=(n6 // tn,),
        in_specs=[pl.BlockSpec((n, d), lambda j: (0, 0)),
                  pl.BlockSpec((d, tn), lambda j: (0, j)),
                  pl.BlockSpec((1, tn), lambda j: (0, j))],
        out_specs=pl.BlockSpec((n, tn), lambda j: (0, j)),
        out_shape=jax.ShapeDtypeStruct((n, n6), F32),
        compiler_params=_cparams("arbitrary"),
        name="modulation",
    )(c, ada_w, ada_b.reshape(1, n6))


TN_IN = 512
VA_TILE0 = 8
ATT_TILE0 = 10


def _rope_tile(z, c, s):
    reps = z.shape[1] // LANES
    n = z.shape[1]
    half = ROPE_DIM // 2
    lane = lax.broadcasted_iota(jnp.int32, (1, LANES), 1)
    first = jnp.concatenate([lane < half] * reps, axis=1)
    partner = jnp.where(first, pltpu.roll(z, n - half, 1), pltpu.roll(z, half, 1))
    return z * jnp.concatenate([c] * reps, axis=1) + partner * jnp.concatenate([s] * reps, axis=1)


def _inproj_kernel(x_ref, g_ref, sc_ref, sh_ref, w_ref, walr_ref, w2_ref, ab_ref, *rest,
                   segs, rope_tiles, split_tiles):
    dils = sorted({dil for _, _, dil in split_tiles})
    n_seg, n_tab = len(segs), 2 * (1 + len(dils))
    tabs = [(rest[2 * v], rest[2 * v + 1]) for v in range(1 + len(dils))]
    seg_refs, la_ref = rest[n_tab:n_tab + n_seg], rest[n_tab + n_seg]
    split_refs, (h_scr, hf_scr) = rest[n_tab + n_seg + 1:-2], rest[-2:]
    j = pl.program_id(1)
    tm, d = x_ref.shape

    @pl.when(j == 0)
    def _():
        h = _rms(x_ref[...], g_ref[...]) * (1.0 + sc_ref[...]) + sh_ref[...]
        hb = h.astype(BF16)
        h_scr[0] = hb
        alr = _dot(hb, walr_ref[...])
        p = _dot(alr.astype(BF16), w2_ref[...]) + ab_ref[...]
        la_ref[...] = _log_sigmoid(p) * (1.0 / GLA_TAU)
        for c in range((d // LANES) if dils else 0):
            cs = slice(c * LANES, (c + 1) * LANES)
            hf_scr[c] = h[:, cs]
            for v, dil in enumerate(dils, 1):
                per = tm // dil
                for r in range(dil):
                    h_scr[v, r * per:(r + 1) * per, cs] = hf_scr[c, pl.ds(r, per, stride=dil), :].astype(BF16)

    def tile(v, roped):
        z = _dot(h_scr[v], w_ref[...])
        return _rope_tile(z, tabs[v][0][...], tabs[v][1][...]) if roped else z

    for (lo, hi), ref in zip(segs, seg_refs):
        ropes = [t for t in rope_tiles if lo <= t < hi]
        in_seg = jnp.logical_and(j >= lo, j < hi)
        is_rope = functools.reduce(jnp.logical_or, [j == t for t in ropes], False)
        if ropes:
            @pl.when(jnp.logical_and(in_seg, is_rope))
            def _(ref=ref):
                ref[...] = tile(0, True).astype(ref.dtype)

        @pl.when(jnp.logical_and(in_seg, jnp.logical_not(is_rope)))
        def _(ref=ref):
            ref[...] = tile(0, False).astype(ref.dtype)

    for t, slot, dil in split_tiles:
        @pl.when(j == t)
        def _(t=t, slot=slot, dil=dil):
            z = tile(1 + dils.index(dil), t in rope_tiles)
            split_refs[slot][...] = z.reshape(dil, tm // dil, z.shape[1]).astype(split_refs[slot].dtype)


def _in_proj(x2, g, sc, sh, w_main, w_alr, w2p, a_b, pos, tm, rows_per_mod, segs, rope_tiles, split_tiles):
    t, d = x2.shape
    r = sc.shape[1]
    wq = a_b.shape[1]
    n_tiles = w_main.shape[1] // TN_IN
    tiles_per_mod = rows_per_mod // tm
    tab_tiles = pos.shape[0] // tm
    nmod = t // rows_per_mod
    dils = sorted({dil for _, _, dil in split_tiles})
    tabs = list(_rope_tables(pos))
    for dil in dils:
        tabs += [tb.reshape(tab_tiles, tm // dil, dil, LANES).transpose(0, 2, 1, 3).reshape(-1, LANES)
                 for tb in tabs[:2]]
    mod_spec = pl.BlockSpec((None, r, d), lambda i, j: (i // tiles_per_mod, 0, 0))
    tab_spec = pl.BlockSpec((tm, LANES), lambda i, j: (i % tab_tiles, 0))
    const = lambda shape: pl.BlockSpec(shape, lambda i, j: (0, 0))
    out_specs, out_shape = [], []
    for lo, hi, dt in segs:
        out_specs.append(pl.BlockSpec((tm, TN_IN), lambda i, j, lo=lo, hi=hi: (i, jnp.clip(j - lo, 0, hi - lo - 1))))
        out_shape.append(jax.ShapeDtypeStruct((t, (hi - lo) * TN_IN), dt))
    out_specs.append(pl.BlockSpec((tm, wq), lambda i, j: (i, 0)))
    out_shape.append(jax.ShapeDtypeStruct((t, wq), F32))
    for _, _, dil in split_tiles:
        out_specs.append(pl.BlockSpec((None, dil, tm // dil, TN_IN),
                                      lambda i, j: (i // tiles_per_mod, 0, i % tiles_per_mod, 0)))
        out_shape.append(jax.ShapeDtypeStruct((nmod, dil, rows_per_mod // dil, TN_IN), BF16))
    return pl.pallas_call(
        functools.partial(_inproj_kernel, segs=tuple((lo, hi) for lo, hi, _ in segs),
                          rope_tiles=tuple(rope_tiles), split_tiles=tuple(split_tiles)),
        grid=(t // tm, n_tiles),
        in_specs=[pl.BlockSpec((tm, d), lambda i, j: (i, 0)), const((1, d)), mod_spec, mod_spec,
                  pl.BlockSpec((d, TN_IN), lambda i, j: (0, j)), const((d, LANES)), const((LANES, wq)),
                  const((1, wq))] + [tab_spec] * len(tabs),
        out_specs=out_specs,
        out_shape=out_shape,
        scratch_shapes=[pltpu.VMEM((1 + len(dils), tm, d), BF16),
                        pltpu.VMEM((d // LANES, tm if dils else SUBLANES, LANES), F32)],
        compiler_params=_cparams("arbitrary", "arbitrary"),
        name="in_proj",
    )(x2, g, sc, sh, w_main, w_alr, w2p, a_b, *tabs)


GLA_EXP_CLAMP = 80.0


def _gla_kernel(q_ref, k_ref, v_ref, ra_ref, la_ref, s0_ref, gn_ref, o_ref, sfin_ref, st_scr, *, rows, dk, dv):
    c = pl.program_id(1)
    nc = pl.num_programs(1)
    ch = GLA_CHUNK

    @pl.when(c == 0)
    def _():
        for h in range(GLA_HEADS):
            st_scr[h] = s0_ref[h].T

    ti = lax.broadcasted_iota(jnp.int32, (ch, ch), 0)
    si = lax.broadcasted_iota(jnp.int32, (ch, ch), 1)
    causal = si <= ti
    tri = causal.astype(BF16)
    scale = dk ** -0.5
    wk = GLA_HEADS * dk
    heads = range(GLA_HEADS)
    ks = [slice(h * dk, (h + 1) * dk) for h in heads]
    vs = [slice(h * dv, (h + 1) * dv) for h in heads]

    def chunk(ci, carry):
        rows = pl.ds(pl.multiple_of(ci * ch, ch), ch)
        la = la_ref[rows, :]
        hi = la.astype(BF16)
        r1 = la - hi.astype(F32)
        mid = r1.astype(BF16)
        lo = (r1 - mid.astype(F32)).astype(BF16)
        b3 = _dot(tri, jnp.concatenate([hi, mid, lo], axis=1))
        b = b3[:, :wk] + b3[:, wk:2 * wk] + b3[:, 2 * wk:]
        bl = b[ch - 1:ch, :]
        q = q_ref[rows, :] * scale
        k = k_ref[rows, :]
        qe = (q * jnp.exp(b)).astype(BF16)
        ke = (k * jnp.exp(jnp.minimum(-b, GLA_EXP_CLAMP))).astype(BF16)
        kh = (k * jnp.exp(bl - b)).astype(BF16)
        dec = jnp.exp(bl)
        v = v_ref[rows, :].astype(BF16)
        att = [jnp.where(causal, _dot_nt(qe[:, ks[h]], ke[:, ks[h]]), 0.0).astype(BF16) for h in heads]
        st = [st_scr[h] for h in heads]
        o = [_dot_nt(qe[:, ks[h]], st[h].astype(BF16)) + _dot(att[h], v[:, vs[h]]) for h in heads]
        for h in heads:
            st_scr[h] = st[h] * dec[:, ks[h]] + _dot_tn(v[:, vs[h]], kh[:, ks[h]])
        for h in heads:
            on = _rms(o[h], gn_ref[:, vs[h]])
            o_ref[rows, vs[h]] = (on * _silu(ra_ref[rows, vs[h]])).astype(o_ref.dtype)
        return carry

    lax.fori_loop(0, rows // ch, chunk, 0)

    @pl.when(c == nc - 1)
    def _():
        for h in range(GLA_HEADS):
            sfin_ref[h] = st_scr[h].T


def _gla_prompt(z, zv, la, s0, gn_g, n, l, rows):
    t = n * l
    dk = la.shape[1] // GLA_HEADS
    dv = gn_g.shape[1] // GLA_HEADS
    wk, wv = GLA_HEADS * dk, GLA_HEADS * dv
    cpb = l // rows
    row = lambda b, c: b * cpb + c
    return pl.pallas_call(
        functools.partial(_gla_kernel, rows=rows, dk=dk, dv=dv),
        grid=(n, cpb),
        in_specs=[pl.BlockSpec((rows, wk), lambda b, c: (row(b, c), 0)),
                  pl.BlockSpec((rows, wk), lambda b, c: (row(b, c), 1)),
                  pl.BlockSpec((rows, wv), lambda b, c: (row(b, c), 0)),
                  pl.BlockSpec((rows, wv), lambda b, c: (row(b, c), 1)),
                  pl.BlockSpec((rows, wk), lambda b, c: (row(b, c), 0)),
                  pl.BlockSpec((None, GLA_HEADS, dk, dv), lambda b, c: (b, 0, 0, 0)),
                  pl.BlockSpec((1, wv), lambda b, c: (0, 0))],
        out_specs=[pl.BlockSpec((rows, wv), lambda b, c: (row(b, c), 0)),
                   pl.BlockSpec((None, GLA_HEADS, dk, dv), lambda b, c: (b, 0, 0, 0))],
        out_shape=[jax.ShapeDtypeStruct((t, wv), BF16), jax.ShapeDtypeStruct((n, GLA_HEADS, dk, dv), F32)],
        scratch_shapes=[pltpu.VMEM((GLA_HEADS, dv, dk), F32)],
        compiler_params=_cparams("arbitrary", "arbitrary"),
        name="gla_prompt",
    )(z, z, zv, z, la, s0, gn_g)


def _gla_step_kernel(q_ref, k_ref, la_ref, v_ref, ra_ref, s_ref, gn_ref, o_ref, so_ref, *, dk):
    sn = jnp.exp(la_ref[...]) * s_ref[...] + k_ref[...] * v_ref[...]
    so_ref[...] = sn
    o = jnp.sum((q_ref[...] * dk ** -0.5) * sn, axis=1, keepdims=True)
    on = _rms(o, gn_ref[...])
    o_ref[...] = (on * _silu(ra_ref[...])).astype(o_ref.dtype)


def _gla_step(q, k, la, v, ra, s, gn_g):
    nb, nh, dk, dv = s.shape
    col = pl.BlockSpec((None, nh, dk, 1), lambda b: (b, 0, 0, 0))
    rowv = pl.BlockSpec((None, nh, 1, dv), lambda b: (b, 0, 0, 0))
    st = pl.BlockSpec((None, nh, dk, dv), lambda b: (b, 0, 0, 0))
    o, so = pl.pallas_call(
        functools.partial(_gla_step_kernel, dk=dk),
        grid=(nb,),
        in_specs=[col, col, col, rowv, rowv, st, pl.BlockSpec((nh, 1, dv), lambda b: (0, 0, 0))],
        out_specs=[rowv, st],
        out_shape=[jax.ShapeDtypeStruct((nb, nh, 1, dv), BF16), jax.ShapeDtypeStruct(s.shape, F32)],
        compiler_params=_cparams("arbitrary"),
        name="gla_step",
    )(q.reshape(nb, nh, dk, 1), k.reshape(nb, nh, dk, 1), la.reshape(nb, nh, dk, 1),
      v.reshape(nb, nh, 1, dv), ra.reshape(nb, nh, 1, dv), s, gn_g.reshape(nh, 1, dv))
    return o.reshape(nb, nh * dv), so


def _band_kernel(q_ref, k_ref, kp_ref, v_ref, vp_ref, o_ref, lse_ref, *, dil, qb):
    i = pl.program_id(1)
    blk = ATT_BLOCK
    qi = lax.broadcasted_iota(jnp.int32, (blk, 2 * blk), 0)
    ki = lax.broadcasted_iota(jnp.int32, (blk, 2 * blk), 1)
    dist = blk + qi - ki
    band = jnp.logical_and(dist >= 0, dist <= ATT_BLOCK)
    band_first = jnp.logical_and(band, ki >= jnp.where(i > 0, 0, blk))
    scale = ATT_HD ** -0.5
    for r in range(dil):
        for sb in range(qb // blk):
            rs = slice(sb * blk, (sb + 1) * blk)
            q = q_ref[r, rs, :].astype(BF16)
            if sb == 0:
                kprev, vprev, mask = kp_ref[r], vp_ref[r], band_first
            else:
                ps = slice((sb - 1) * blk, sb * blk)
                kprev, vprev, mask = k_ref[r, ps, :], v_ref[r, ps, :], band
            kk = jnp.concatenate([kprev, k_ref[r, rs, :]], axis=0).astype(BF16)
            vv = jnp.concatenate([vprev, v_ref[r, rs, :]], axis=0).astype(BF16)
            s = jnp.where(mask, _dot_nt(q, kk) * scale, NEG_BIG)
            m = jnp.max(s, axis=-1, keepdims=True)
            p = jnp.exp(s - m)
            lsum = jnp.sum(p, axis=-1, keepdims=True)
            o = _dot(p.astype(BF16), vv) / lsum
            lse = jnp.broadcast_to(m + jnp.log(lsum), (blk, ATT_HD))
            rows = rs if dil == 1 else pl.ds(sb * blk * dil + r, blk, stride=dil)
            o_ref[rows, :] = o
            lse_ref[rows, :] = lse


def _band_attention(qa, ka, va, cq, ck, cv, n, l, dil):
    lq = l // dil
    hd = ATT_HD
    tokens = min(ATT_TOKENS, l)
    qb = tokens // dil
    nqb = lq // qb
    sub = qb // ATT_BLOCK

    def main(c0):
        return pl.BlockSpec((None, dil, qb, hd), lambda b, i, h: (b, 0, i, c0 + h))

    def prev(c0):
        return pl.BlockSpec((None, dil, ATT_BLOCK, hd), lambda b, i, h: (b, 0, jnp.maximum(i * sub - 1, 0), c0 + h))

    out = pl.BlockSpec((tokens, hd), lambda b, i, h: (b * nqb + i, h))
    shp = jax.ShapeDtypeStruct((n * l, ATT_HEADS * hd), F32)
    return pl.pallas_call(
        functools.partial(_band_kernel, dil=dil, qb=qb),
        grid=(n, nqb, ATT_HEADS),
        in_specs=[main(cq), main(ck), prev(ck), main(cv), prev(cv)],
        out_specs=[out, out],
        out_shape=[shp, shp],
        compiler_params=_cparams("arbitrary", "arbitrary", "arbitrary"),
        name=f"band_attention_d{dil}",
    )(qa, ka, ka, va, va)


def _att_step_kernel(q_ref, kn_ref, vn_ref, c0_ref, c1_ref, c2_ref, o_ref):
    scale = ATT_HD ** -0.5
    outs, lses = [], []
    for g, c_ref in enumerate((c0_ref, c1_ref, c2_ref)):
        q, kn, vn = q_ref[g], kn_ref[g], vn_ref[g]
        s = jnp.sum(c_ref[:, 0] * q[None], axis=-1, keepdims=True) * scale
        sn = jnp.sum(kn * q, axis=-1, keepdims=True) * scale
        m = jnp.maximum(jnp.max(s, axis=0), sn)
        p = jnp.exp(s - m[None])
        pn = jnp.exp(sn - m)
        lsum = jnp.sum(p, axis=0) + pn
        outs.append((jnp.sum(p * c_ref[:, 1], axis=0) + pn * vn) / lsum)
        lses.append(m + jnp.log(lsum))
    mx = functools.reduce(jnp.maximum, lses)
    es = [jnp.exp(x - mx) for x in lses]
    inv = 1.0 / functools.reduce(lambda a, b: a + b, es)
    o_ref[...] = functools.reduce(lambda a, b: a + b, [e * inv * o for e, o in zip(es, outs)]).astype(o_ref.dtype)


def _att_step(qs, ks, vs, caches):
    nb = qs.shape[0]
    views, cspecs = [], []
    for (window, dil), cch in zip(ATT_GROUPS, caches):
        assert cch.shape[1] == window, "sample step expects full caches"
        views.append(cch.reshape((nb, window // dil, dil) + cch.shape[2:]))
        cspecs.append(pl.BlockSpec((None, ATT_BLOCK, None) + cch.shape[2:], lambda b: (b, 0, 0, 0, 0, 0)))
    row = pl.BlockSpec((None,) + qs.shape[1:], lambda b: (b, 0, 0, 0))
    o = pl.pallas_call(
        _att_step_kernel,
        grid=(nb,),
        in_specs=[row] * 3 + cspecs,
        out_specs=pl.BlockSpec((None, ATT_HEADS, ATT_HD), lambda b: (b, 0, 0)),
        out_shape=jax.ShapeDtypeStruct((nb, ATT_HEADS, ATT_HD), BF16),
        compiler_params=_cparams("arbitrary"),
        name="att_step",
    )(qs, ks, vs, *views)
    return o.reshape(nb, ATT_HEADS * ATT_HD)


MERGE_PART = 256


def _merge_kernel(*refs, n_att, n_alias, n_tiles):
    oa_ref = refs[0]
    att = refs[1:1 + n_att]
    (gza_ref, gzb_ref, x_ref, g1_ref, sc2_ref, sh2_ref, n2_ref, wa_ref, wb_ref, wo_ref,
     wq_ref) = refs[1 + n_att:12 + n_att]
    x1_ref, h2_ref, pq_ref = refs[12 + n_att + n_alias:]
    tm = x_ref.shape[0]
    pr = min(MERGE_PART, tm)
    parts = [slice(k * pr, (k + 1) * pr) for k in range(tm // pr)]
    mod = lambda ref, part: ref[...] if ref.shape[0] == 1 else ref[part, :]
    add = lambda a, b: a + b

    def attention_out(part):
        if n_att == 1:
            return att[0][part, :]
        ng = n_att // 2
        ls = [r[part, :] for r in att[ng:]]
        mx = functools.reduce(jnp.maximum, ls)
        es = [jnp.exp(x - mx) for x in ls]
        inv = 1.0 / functools.reduce(add, es)
        return functools.reduce(add, [e * inv * r[part, :] for e, r in zip(es, att[:ng])]).astype(BF16)

    @pl.when(pl.program_id(0) >= n_tiles)
    def _():
        h2_ref[...] = jnp.zeros_like(h2_ref)
        pq_ref[...] = jnp.zeros_like(pq_ref)

    @pl.when(pl.program_id(0) < n_tiles)
    def _():
        _merge_rows(parts, attention_out, mod, oa_ref, gza_ref, gzb_ref, x_ref, g1_ref, sc2_ref, sh2_ref, n2_ref,
                    wa_ref, wb_ref, wo_ref, wq_ref, x1_ref, h2_ref, pq_ref)
        if h2_ref.shape[0] > tm:
            h2_ref[tm:] = jnp.zeros((h2_ref.shape[0] - tm, h2_ref.shape[1]), h2_ref.dtype)
            pq_ref[tm:] = jnp.zeros((pq_ref.shape[0] - tm, pq_ref.shape[1]), pq_ref.dtype)


def _merge_rows(parts, attention_out, mod, oa_ref, gza_ref, gzb_ref, x_ref, g1_ref, sc2_ref, sh2_ref, n2_ref,
                wa_ref, wb_ref, wo_ref, wq_ref, x1_ref, h2_ref, pq_ref):
    ob = [attention_out(p) for p in parts]
    ma = [_dot(oa_ref[p, :], wa_ref[...]) for p in parts]
    mb = [_dot(o, wb_ref[...]) for o in ob]
    merged = [(_sigmoid(gza_ref[p, :]) * a + _sigmoid(gzb_ref[p, :]) * b).astype(BF16)
              for p, a, b in zip(parts, ma, mb)]
    y = [_dot(m, wo_ref[...]) for m in merged]
    h2 = []
    for p, yy in zip(parts, y):
        x1 = x_ref[p, :] + mod(g1_ref, p) * yy
        x1_ref[p, :] = x1
        h2.append((_rms(x1, n2_ref[...]) * (1.0 + mod(sc2_ref, p)) + mod(sh2_ref, p)).astype(BF16))
    for p, h in zip(parts, h2):
        h2_ref[p, :] = h
        pq_ref[p, :] = _dot(h, wq_ref[...])


def _merge(oa, att, z, x2, g1, sc2, sh2, n2g, wa, wb, wo, wq, tm, rows_per_mod, gza_blk, gzb_blk,
           peer_rows, peer_tile, peer_blk0, alias=(), zero_tiles=0):
    t, d = x2.shape
    r = g1.shape[1]
    wqn = wq.shape[1]
    tiles_per_mod = rows_per_mod // tm
    n_tiles = t // tm
    row = lambda i: jnp.minimum(i, n_tiles - 1)
    tok = lambda w: pl.BlockSpec((tm, w), lambda i: (row(i), 0))
    mod = pl.BlockSpec((None, r, d), lambda i: (row(i) // tiles_per_mod, 0, 0))
    const = lambda a: pl.BlockSpec(a.shape, lambda i: (0, 0), pipeline_mode=pl.Buffered(1))
    peer = lambda w: pl.BlockSpec((peer_tile, w), lambda i: (peer_blk0 + i, 0))
    n_in = 12 + len(att)
    return pl.pallas_call(
        functools.partial(_merge_kernel, n_att=len(att), n_alias=len(alias), n_tiles=n_tiles),
        grid=(n_tiles + zero_tiles,),
        in_specs=[tok(oa.shape[1])] + [tok(a.shape[1]) for a in att]
                 + [pl.BlockSpec((tm, d), lambda i: (row(i), gza_blk)),
                    pl.BlockSpec((tm, d), lambda i: (row(i), gzb_blk)),
                    tok(d), mod, mod, mod, const(n2g), const(wa), const(wb), const(wo), const(wq)]
                 + [pl.BlockSpec(memory_space=pl.ANY)] * len(alias),
        out_specs=[tok(d), peer(d), peer(wqn)],
        out_shape=[jax.ShapeDtypeStruct((t, d), F32), jax.ShapeDtypeStruct((peer_rows, d), BF16),
                   jax.ShapeDtypeStruct((peer_rows, wqn), F32)],
        input_output_aliases={n_in + k: 1 + k for k in range(len(alias))},
        compiler_params=_cparams("arbitrary"),
        name="merge",
    )(oa, *att, z, z, x2, g1, sc2, sh2, n2g, wa, wb, wo, wq, *alias)


def _strict_max_below(rows, prev):
    m = None
    for x in rows:
        y = jnp.where(x < prev, x, NEG_BIG) if prev is not None else x
        m = y if m is None else jnp.maximum(m, y)
    return m


def _oddeven_merge(lo, hi, r):
    step = r * 2
    if step < hi - lo:
        yield from _oddeven_merge(lo, hi, step)
        yield from _oddeven_merge(lo + r, hi, step)
        yield from [(i, i + r) for i in range(lo + r, hi - r, step)]
    else:
        yield (lo, lo + r)


def _oddeven_merge_sort(lo, hi):
    if hi - lo >= 1:
        mid = lo + (hi - lo) // 2
        yield from _oddeven_merge_sort(lo, mid)
        yield from _oddeven_merge_sort(mid + 1, hi)
        yield from _oddeven_merge(lo, hi, 1)


def _exchange(p, i, j):
    p[i], p[j] = jnp.maximum(p[i], p[j]), jnp.minimum(p[i], p[j])


def _topk_desc(s, k):
    assert s.shape[0] == k * SUBLANES and k & (k - 1) == 0
    p = [s[i * SUBLANES:(i + 1) * SUBLANES] for i in range(k)]
    for i, j in _oddeven_merge_sort(0, k - 1):
        _exchange(p, i, j)
    shift = SUBLANES // 2
    while shift:
        q = [pltpu.roll(v, shift, 0) for v in p]
        p = [jnp.maximum(p[i], q[k - 1 - i]) for i in range(k)]
        stride = k // 2
        while stride:
            for i in range(k):
                if not i & stride:
                    _exchange(p, i, i + stride)
            stride //= 2
        shift //= 2
    return [v[0:1] for v in p]


def _route_kernel(pq_ref, k1_ref, k2_ref, n1_ref, e1_ref, r2_ref, e2_ref, s1_scr, s2_scr):
    nk = PEER_NKEYS
    pq = pq_ref[...].astype(BF16)
    s1_scr[...] = _dot_nt(k1_ref[...], pq)
    s2_scr[...] = _dot_nt(k2_ref[...], pq)
    v1h, v2h = [], []
    for h in range(PEER_HEADS):
        v1h.append(_topk_desc(s1_scr[h * nk:(h + 1) * nk], PEER_TOPK))
        v2h.append(_topk_desc(s2_scr[h * nk:(h + 1) * nk], PEER_TOPK))
    v1 = [jnp.concatenate([v1h[h][k] for h in range(PEER_HEADS)], axis=0) for k in range(PEER_TOPK)]
    v2 = [jnp.concatenate([v2h[h][k] for h in range(PEER_HEADS)], axis=0) for k in range(PEER_TOPK)]
    pairs = [(i, j) for i in range(PEER_TOPK) for j in range(PEER_TOPK) if (i + 1) * (j + 1) <= PEER_TOPK]
    cands = [v1[i] + v2[j] for i, j in pairs]
    tau = None
    for _ in range(PEER_TOPK):
        tau = _strict_max_below(cands, tau)
    cmax = v1[0] + v2[0]
    zsum = None
    counts = [None] * PEER_TOPK
    for (i, _), cnd in zip(pairs, cands):
        sel = cnd >= tau
        e = jnp.where(sel, jnp.exp(cnd - cmax), 0.0)
        zsum = e if zsum is None else zsum + e
        one = jnp.where(sel, 1.0, 0.0)
        counts[i] = one if counts[i] is None else counts[i] + one
    zinv = 1.0 / zsum
    for h in range(PEER_HEADS):
        a = s1_scr[h * nk:(h + 1) * nk]
        b = s2_scr[h * nk:(h + 1) * nk]
        n1 = jnp.zeros_like(a)
        r2 = jnp.full_like(b, float(PEER_TOPK))
        for k in range(PEER_TOPK):
            n1 = jnp.where(a == v1h[h][k], counts[k][h:h + 1, :], n1)
            r2 = jnp.where(b == v2h[h][k], float(k), r2)
        n1_ref[h] = n1
        e1_ref[h] = jnp.exp(a - v1h[h][0]) * zinv[h:h + 1, :]
        r2_ref[h] = r2.astype(r2_ref.dtype)
        e2_ref[h] = jnp.exp(b - v2h[h][0]).astype(e2_ref.dtype)


def _peer_route(pq, k1big, k2big, tt):
    t, w = pq.shape
    nh, nk = PEER_HEADS, PEER_NKEYS
    sspec = pl.BlockSpec((nh, nk, tt), lambda i: (0, 0, i))
    shp = lambda dt: jax.ShapeDtypeStruct((nh, nk, t), dt)
    return pl.pallas_call(
        _route_kernel,
        grid=(t // tt,),
        in_specs=[pl.BlockSpec((tt, w), lambda i: (i, 0)),
                  pl.BlockSpec(k1big.shape, lambda i: (0, 0)), pl.BlockSpec(k2big.shape, lambda i: (0, 0))],
        out_specs=[sspec] * 4,
        out_shape=[shp(F32), shp(F32), shp(BF16), shp(BF16)],
        scratch_shapes=[pltpu.VMEM((nh * nk, tt), F32)] * 2,
        compiler_params=_cparams("arbitrary"),
        name="peer_route",
    )(pq, k1big, k2big)


def _gelu_tanh(x):
    return 0.5 * x * (1.0 + jnp.tanh(GELU_C * (x + 0.044715 * (x * x * x))))


BF16_ROWS = 16
GATE_LANES = 256
PEER_PART = 256
PEER_TILE = 2048


def _gelu_tanh_bf16(x):
    hx = 0.5 * x
    return hx + hx * jnp.tanh(x * (GELU_C + (GELU_C * 0.044715) * (x * x)))


def _bf16_rows(row):
    return jnp.broadcast_to(row, (BF16_ROWS, row.shape[1])).astype(BF16)


CACHE_CHUNK = 512


def _cache_shift_jobs(caches):
    nb = caches[0].shape[0]
    chunks, base = [], 0
    for c, cache in enumerate(caches):
        w = cache.shape[1]
        for r0 in range(0, w, CACHE_CHUNK):
            n = min(CACHE_CHUNK, w - r0)
            chunks.append((c, r0, n, r0 + n == w, base))
            base += nb
    return chunks, base


def _cache_shift_step(step, caches, new_rows, outs, buf, sems):
    nb = caches[0].shape[0]
    chunks, n_jobs = _cache_shift_jobs(caches)

    def for_job(job, act):
        for c, r0, n, last, base in chunks:
            @pl.when(jnp.logical_and(job >= base, job < base + nb))
            def _(c=c, r0=r0, n=n, last=last, base=base):
                b, slot = job - base, job % 2
                n_in = n - 1 if last else n
                rd = pltpu.make_async_copy(caches[c].at[b, pl.ds(r0 + 1, n_in)], buf.at[slot, pl.ds(0, n_in)],
                                           sems.at[slot, 0])
                wr = pltpu.make_async_copy(buf.at[slot, pl.ds(0, n)], outs[c].at[b, pl.ds(r0, n)], sems.at[slot, 1])
                act(rd, wr, lambda: new_rows[c][b] if last else None, slot, n_in)

    def finish_read_start_write(rd, wr, new_row, slot, n_in):
        rd.wait()
        row = new_row()
        if row is not None:
            buf[slot, pl.ds(n_in, 1)] = row
        wr.start()

    @pl.when(jnp.logical_and(step >= 1, step <= n_jobs))
    def _():
        for_job(step - 1, finish_read_start_write)

    @pl.when(jnp.logical_and(step >= 2, step <= n_jobs + 1))
    def _():
        for_job(step - 2, lambda rd, wr, *_: wr.wait())

    @pl.when(step < n_jobs)
    def _():
        for_job(step, lambda rd, wr, *_: rd.start())


def _peer_dense_kernel(h2_ref, u_ref, vt_ref, n1_ref, e1_ref, r2_ref, e2_ref, *rest, et, n_kv):
    caches, new_rows = rest[:n_kv], rest[n_kv:2 * n_kv]
    o_ref, kv_outs = rest[2 * n_kv], rest[2 * n_kv + 1:3 * n_kv + 1]
    acc_scr, wg_scr, st_scr = rest[3 * n_kv + 1:3 * n_kv + 4]
    e = pl.program_id(1)
    nk = PEER_NKEYS
    tt = h2_ref.shape[0]
    if n_kv:
        _cache_shift_step(pl.program_id(0) * pl.num_programs(1) + e, caches, new_rows, kv_outs, rest[-2], rest[-1])

    @pl.when(e == 0)
    def _():
        acc_scr[...] = jnp.zeros_like(acc_scr)

    zero = jnp.zeros((), BF16)
    groups = nk // BF16_ROWS

    def scores(part):
        st_scr[part, :] = _dot_nt(u_ref[part, :], h2_ref[...]).astype(BF16)

    def gates(ii):
        rs = slice(ii * nk, (ii + 1) * nk)
        for lc in range(tt // GATE_LANES):
            ls = slice(lc * GATE_LANES, (lc + 1) * GATE_LANES)
            w = None
            for h in range(PEER_HEADS):
                n1 = _bf16_rows(n1_ref[h, ii:ii + 1, ls])
                e1 = _bf16_rows(e1_ref[h, ii:ii + 1, ls])
                r2 = r2_ref[h, :, ls].reshape(groups, BF16_ROWS, GATE_LANES)
                e2 = e2_ref[h, :, ls].reshape(groups, BF16_ROWS, GATE_LANES)
                c = jnp.where(r2 < n1[None], e2, zero) * e1[None]
                w = c if w is None else w + c
            wg_scr[rs, ls] = w.reshape(nk, GATE_LANES) * _gelu_tanh_bf16(st_scr[rs, ls])

    def mix(part):
        return _dot(vt_ref[:, part], wg_scr[part, :])

    per = PEER_PART // nk
    parts = [slice(p * PEER_PART, (p + 1) * PEER_PART) for p in range(et // PEER_PART)]
    out = None
    scores(parts[0])
    for p, part in enumerate(parts):
        if p + 1 < len(parts):
            scores(parts[p + 1])
        for ii in range(p * per, (p + 1) * per):
            gates(ii)
        if p > 0:
            d = mix(parts[p - 1])
            out = d if out is None else out + d
    d = mix(parts[-1])
    acc_scr[...] += d if out is None else out + d

    @pl.when(e == pl.num_programs(1) - 1)
    def _():
        o_ref[...] = acc_scr[...].T


def _peer_dense(h2, u_b, vt_b, n1, e1, r2, e2, tt, et, caches=(), new_rows=()):
    t, d = h2.shape
    ne = u_b.shape[0]
    nh, nk = PEER_HEADS, PEER_NKEYS
    n_kv = len(caches)
    grid = (t // tt, ne // et)
    if n_kv:
        assert grid[0] * grid[1] >= _cache_shift_jobs(caches)[1] + 2, "one cache-shift job per grid step"
    rowside = pl.BlockSpec((nh, et // nk, tt), lambda i, e: (0, e, i))
    colside = pl.BlockSpec((nh, nk, tt), lambda i, e: (0, 0, i))
    anywhere = pl.BlockSpec(memory_space=pl.ANY)
    whole = lambda a: pl.BlockSpec(a.shape, lambda i, e: (0,) * a.ndim)
    outs = pl.pallas_call(
        functools.partial(_peer_dense_kernel, et=et, n_kv=n_kv),
        grid=grid,
        in_specs=[pl.BlockSpec((tt, d), lambda i, e: (i, 0)),
                  pl.BlockSpec((et, d), lambda i, e: (e, 0)),
                  pl.BlockSpec((d, et), lambda i, e: (0, e)),
                  rowside, rowside, colside, colside] + [anywhere] * n_kv + [whole(a) for a in new_rows],
        out_specs=[pl.BlockSpec((tt, d), lambda i, e: (i, 0))] + [anywhere] * n_kv,
        out_shape=[jax.ShapeDtypeStruct((t, d), F32)] + [jax.ShapeDtypeStruct(c.shape, c.dtype) for c in caches],
        scratch_shapes=[pltpu.VMEM((d, tt), F32), pltpu.VMEM((et, tt), BF16), pltpu.VMEM((et, tt), BF16)]
                       + ([pltpu.VMEM((2, CACHE_CHUNK) + caches[0].shape[2:], caches[0].dtype),
                           pltpu.SemaphoreType.DMA((2, 2))] if n_kv else []),
        compiler_params=_cparams("arbitrary", "arbitrary"),
        name="peer_dense",
    )(h2, u_b, vt_b, n1, e1, r2, e2, *caches, *new_rows)
    return outs[0], outs[1:]


def _final_kernel(x1_ref, p_ref, g2_ref, fg_ref, y_ref):
    y_ref[...] = _rms(x1_ref[...] + g2_ref[...] * p_ref[...], fg_ref[...])


def _final(x1, p, g2, fg, tm, rows_per_mod, p_blk0):
    t, d = x1.shape
    r = g2.shape[1]
    tiles_per_mod = rows_per_mod // tm
    tok = pl.BlockSpec((tm, d), lambda i: (i, 0))
    return pl.pallas_call(
        _final_kernel,
        grid=(t // tm,),
        in_specs=[tok, pl.BlockSpec((tm, d), lambda i: (p_blk0 + i, 0)),
                  pl.BlockSpec((None, r, d), lambda i: (i // tiles_per_mod, 0, 0)),
                  pl.BlockSpec((1, d), lambda i: (0, 0))],
        out_specs=tok,
        out_shape=jax.ShapeDtypeStruct((t, d), F32),
        compiler_params=_cparams("arbitrary"),
        name="final_norm",
    )(x1, p, g2, fg)


def _rope_tables(pos):
    half = ROPE_DIM // 2
    inv_freq = ROPE_THETA ** (-jnp.arange(half, dtype=F32) / half)
    ang = pos.astype(F32)[:, None] * inv_freq[None, :]
    cos, sin = jnp.cos(ang), jnp.sin(ang)
    n = pos.shape[0]
    ones = jnp.ones((n, LANES - ROPE_DIM), F32)
    zeros = jnp.zeros((n, LANES - ROPE_DIM), F32)
    return (jnp.concatenate([cos, cos, ones], axis=1), jnp.concatenate([-sin, sin, zeros], axis=1))


def _block_diag_keys(keys, half):
    nk, dh = keys.shape
    eye = jnp.eye(PEER_HEADS, dtype=keys.dtype)
    blk = jnp.zeros((PEER_HEADS, nk, PEER_HEADS, 2, dh), keys.dtype)
    blk = blk.at[:, :, :, half, :].set(eye[:, None, :, None] * keys[None, :, None, :])
    return blk.reshape(PEER_HEADS * nk, PEER_HEADS * 2 * dh).astype(BF16)


def _pick_tile(n, pref):
    t = min(pref, n)
    while n % t:
        t //= 2
    return t


def kernel(x_prompt, x_sample, c_prompt, c_sample, state_gla, cache_kv_w128, cache_kv_w512, cache_kv_w2048,
           ada_w, ada_b, norm1_g, w_in, gla_a_w2, gla_a_b, gla_gn_g, w_branch_a, w_branch_b, w_out,
           norm2_g, peer_wq, peer_k1, peer_k2, peer_u, peer_v, final_g):
    depth = ada_w.shape[0]
    assert depth == 1, "single-layer trunk"
    n_p, l_p, d = x_prompt.shape
    n_s, l_s, _ = x_sample.shape
    assert l_s == 1, "sample group decodes one token per sequence"
    caches = (cache_kv_w128[0], cache_kv_w512[0], cache_kv_w2048[0])
    n_groups = len(ATT_GROUPS)

    w_qa = gla_a_w2.shape[2]
    w_va = gla_gn_g.shape[1]
    w_ob = ATT_HEADS * ATT_HD
    w_qb = n_groups * w_ob
    assert w_qa == TN_IN and w_va == 2 * TN_IN and d == 2 * TN_IN and w_ob == TN_IN
    widths = (w_qa, w_qa, w_va, w_va, GLA_RANK, w_qb, w_qb, w_qb, d, d)
    offs = [0]
    for w in widths:
        offs.append(offs[-1] + w)
    seg = lambda i: w_in[0][:, offs[i]:offs[i + 1]]
    att_col = lambda g, k: seg(5 + k)[:, g * w_ob:(g + 1) * w_ob]
    att_cols = [att_col(g, k) for g in range(n_groups) for k in range(3)]
    w_main = jnp.concatenate([seg(0), seg(1), seg(3), seg(8), seg(9), seg(2)] + att_cols, axis=1).astype(BF16)
    w_kv = jnp.concatenate([att_col(g, k) for g in range(n_groups) for k in (1, 2)], axis=1).astype(BF16)
    n_tiles = w_main.shape[1] // TN_IN
    ra_off, gza_blk, gzb_blk = 2 * w_qa, 2, 3
    w_alr = jnp.pad(seg(4), ((0, 0), (0, LANES - GLA_RANK))).astype(BF16)
    w2p = jnp.pad(gla_a_w2[0], ((0, LANES - GLA_RANK), (0, 0))).astype(BF16)
    a_b = gla_a_b[0].reshape(1, w_qa)
    g1n = norm1_g[0].reshape(1, d)
    gn_g = gla_gn_g[0].reshape(1, w_va)
    n2g = norm2_g[0].reshape(1, d)
    wa, wb = w_branch_a[0].astype(BF16), w_branch_b[0].astype(BF16)
    wo, wq = w_out[0].astype(BF16), peer_wq[0].astype(BF16)
    q_tile = lambda g: ATT_TILE0 + 3 * g
    rope_tiles = [q_tile(g) + k for g in range(n_groups) for k in (0, 1)]

    n_mod = n_p + n_s
    n_pad = -(-n_mod // SUBLANES) * SUBLANES
    c_all = jnp.concatenate([c_prompt, c_sample, jnp.zeros((n_pad - n_mod, d), F32)], axis=0)
    mod = _modulation(c_all, ada_w[0], ada_b[0])
    mods_p = [mod[:n_p, i * d:(i + 1) * d].reshape(n_p, 1, d) for i in range(6)]
    mods_s = [mod[n_p:n_mod, i * d:(i + 1) * d].reshape(1, n_s, d) for i in range(6)]

    t_p = n_p * l_p
    tt = _pick_tile(t_p, 512)
    t_all = t_p + (-(-n_s // tt)) * tt
    assert t_all - t_p == tt

    xp2 = x_prompt.reshape(t_p, d)
    tm_p = _pick_tile(l_p, 1024)
    pos_p = jnp.arange(l_p, dtype=jnp.int32)
    split = []
    for g, (window, dil) in enumerate(ATT_GROUPS):
        assert window // dil == ATT_BLOCK
        if dil > 1:
            split += [(q_tile(g) + k, len(split) + k, dil) for k in range(3)]
    first_split = min([t for t, _, _ in split], default=n_tiles)
    outs = _in_proj(xp2, g1n, mods_p[1], mods_p[0], w_main, w_alr, w2p, a_b, pos_p, tm_p, l_p,
                    ((0, VA_TILE0, F32), (VA_TILE0, first_split, BF16)), rope_tiles, split)
    z_p, zb_p, la_p, split_p = outs[0], outs[1], outs[2], outs[3:]
    s0_p = jnp.zeros((n_p,) + state_gla.shape[2:], F32)
    oa_p, sfin_p = _gla_prompt(z_p, zb_p, la_p, s0_p, gn_g, n_p, l_p, _pick_tile(l_p, 256))
    zb4 = zb_p.reshape(n_p, 1, l_p, zb_p.shape[1])
    att_o, att_l = [], []
    for g, (window, dil) in enumerate(ATT_GROUPS):
        if dil > 1:
            qa, ka, va = [split_p[slot] for t, slot, _ in split if q_tile(g) <= t < q_tile(g) + 3]
            o_g, lse_g = _band_attention(qa, ka, va, 0, 0, 0, n_p, l_p, dil)
        else:
            c0 = lambda k: (q_tile(g) + k - VA_TILE0) * (TN_IN // ATT_HD)
            o_g, lse_g = _band_attention(zb4, zb4, zb4, c0(0), c0(1), c0(2), n_p, l_p, dil)
        att_o.append(o_g)
        att_l.append(lse_g)
    keep_max = min(max(w for w, _ in ATT_GROUPS), l_p)
    x_tail = x_prompt[:, l_p - keep_max:].reshape(n_p * keep_max, d)
    kv_tiles = 2 * n_groups
    z_kv = _in_proj(x_tail, g1n, mods_p[1], mods_p[0], w_kv, w_alr, w2p, a_b, pos_p[l_p - keep_max:],
                    _pick_tile(keep_max, 1024), keep_max, ((0, kv_tiles, F32),), range(0, kv_tiles, 2), [])[0]
    z_kv = z_kv.reshape(n_p, keep_max, kv_tiles, ATT_HEADS, ATT_HD)
    kv_p = []
    for g, (window, dil) in enumerate(ATT_GROUPS):
        keep = min(window, l_p)
        kv_p.append(z_kv[:, keep_max - keep:, 2 * g:2 * g + 2][None])
    tm_m = _pick_tile(l_p, 512)
    x1_p, h2_all, pq_all = _merge(oa_p, att_o + att_l, z_p, xp2, mods_p[2], mods_p[4], mods_p[3], n2g,
                                  wa, wb, wo, wq, tm_m, l_p, gza_blk, gzb_blk, t_all, tm_m, 0,
                                  zero_tiles=(t_all - t_p) // tm_m)

    xs2 = x_sample.reshape(n_s, d)
    pos_s = jnp.full((n_s,), PAST_LEN, dtype=jnp.int32)
    z_s, la_s = _in_proj(xs2, g1n, mods_s[1], mods_s[0], w_main, w_alr, w2p, a_b, pos_s, n_s, n_s,
                         ((0, n_tiles, F32),), rope_tiles, [])
    col = lambda tile: z_s[:, tile * TN_IN:(tile + 1) * TN_IN]
    oa_s, state_s = _gla_step(col(0), col(1), la_s, z_s[:, VA_TILE0 * TN_IN:VA_TILE0 * TN_IN + w_va],
                              z_s[:, ra_off:ra_off + w_va], state_gla[0], gn_g)
    stack = lambda k: jnp.stack([col(q_tile(g) + k) for g in range(n_groups)], axis=1).reshape(
        n_s, n_groups, ATT_HEADS, ATT_HD)
    ob_s = _att_step(stack(0), stack(1), stack(2), caches)
    x1_s, h2_all, pq_all = _merge(oa_s, [ob_s], z_s, xs2, mods_s[2], mods_s[4], mods_s[3], n2g,
                                  wa, wb, wo, wq, n_s, n_s, gza_blk, gzb_blk, t_all, tt, t_p // tt,
                                  alias=(h2_all, pq_all))
    row_tiles = (2 * w_ob // LANES, LANES)
    new_rows = [jnp.concatenate([col(q_tile(g) + 1), col(q_tile(g) + 2)], axis=1).reshape((n_s, 1) + row_tiles)
                for g in range(n_groups)]

    k1big = _block_diag_keys(peer_k1[0], 0)
    k2big = _block_diag_keys(peer_k2[0], 1)
    n1, e1, r2, e2 = _peer_route(pq_all, k1big, k2big, _pick_tile(tt, 256))
    u_b = peer_u[0].astype(BF16)
    vt_b = peer_v[0].astype(BF16).T
    p_all, kv_s = _peer_dense(h2_all, u_b, vt_b, n1, e1, r2, e2, tt, PEER_TILE,
                              [c.reshape(c.shape[:2] + row_tiles) for c in caches], new_rows)
    kv_s = [a.reshape(c.shape)[None] for a, c in zip(kv_s, caches)]

    fg = final_g.reshape(1, d)
    y_p = _final(x1_p, p_all, mods_p[5], fg, tm_m, l_p, 0)
    y_s = _final(x1_s, p_all, mods_s[5], fg, n_s, n_s, t_p // n_s)

    return (y_p.reshape(n_p, l_p, d), y_s.reshape(n_s, l_s, d), sfin_p[None], state_s[None],
            kv_p[0], kv_s[0], kv_p[1], kv_s[1], kv_p[2], kv_s[2])
```

```python
import functools
import math

import jax
import jax.numpy as jnp
from jax import lax
from jax.experimental import pallas as pl
from jax.experimental.pallas import tpu as pltpu

F32 = jnp.float32
BF16 = jnp.bfloat16

PAST_LEN = 16384
GLA_HEADS = 4
GLA_RANK = 16
GLA_TAU = 16.0
GLA_CHUNK = 128
ATT_GROUPS = ((128, 1), (512, 4), (2048, 16))
ATT_HEADS = 4
ATT_HD = 128
ATT_BLOCK = 128
ATT_TOKENS = 2048
ROPE_DIM = ATT_HD // 4
ROPE_THETA = 500000.0
PEER_HEADS = 8
PEER_NKEYS = 128
PEER_DQ = 128
PEER_TOPK = 16
NORM_EPS = 1e-6
GELU_C = math.sqrt(2.0 / math.pi)

LANES = 128
SUBLANES = 8
VMEM_LIMIT = 56 * 1024 * 1024
NEG_BIG = -1e30


def _cparams(*sem):
    return pltpu.CompilerParams(dimension_semantics=sem, vmem_limit_bytes=VMEM_LIMIT)


def _sigmoid(x):
    return 1.0 / (1.0 + jnp.exp(-x))


def _silu(x):
    return x * _sigmoid(x)


def _log_sigmoid(x):
    return jnp.minimum(x, 0.0) - jnp.log(1.0 + jnp.exp(-jnp.abs(x)))


def _rms(x, g):
    return x * lax.rsqrt(jnp.mean(x * x, axis=-1, keepdims=True) + NORM_EPS) * g


def _dot(a, b):
    return jnp.dot(a, b, preferred_element_type=F32)


def _dot_nt(a, b):
    return lax.dot_general(a, b, (((1,), (1,)), ((), ())), preferred_element_type=F32)


def _dot_tn(a, b):
    return lax.dot_general(a, b, (((0,), (0,)), ((), ())), preferred_element_type=F32)


def _mod_kernel(c_ref, w_ref, b_ref, o_ref):
    s = _silu(c_ref[...]).astype(BF16)
    o_ref[...] = _dot(s, w_ref[...].astype(BF16)) + b_ref[...]


def _modulation(c, ada_w, ada_b):
    n, d = c.shape
    n6 = ada_w.shape[1]
    tn = n6 // 4
    return pl.pallas_call(
        _mod_kernel,
        grid=(n6 // tn,),
        in_specs=[pl.BlockSpec((n, d), lambda j: (0, 0)),
                  pl.BlockSpec((d, tn), lambda j: (0, j)),
                  pl.BlockSpec((1, tn), lambda j: (0, j))],
        out_specs=pl.BlockSpec((n, tn), lambda j: (0, j)),
        out_shape=jax.ShapeDtypeStruct((n, n6), F32),
        compiler_params=_cparams("arbitrary"),
        name="modulation",
    )(c, ada_w, ada_b.reshape(1, n6))


TN_IN = 512
VA_TILE0 = 8
ATT_TILE0 = 10


def _rope_tile(z, c, s):
    reps = z.shape[1] // LANES
    n = z.shape[1]
    half = ROPE_DIM // 2
    lane = lax.broadcasted_iota(jnp.int32, (1, LANES), 1)
    first = jnp.concatenate([lane < half] * reps, axis=1)
    partner = jnp.where(first, pltpu.roll(z, n - half, 1), pltpu.roll(z, half, 1))
    return z * jnp.concatenate([c] * reps, axis=1) + partner * jnp.concatenate([s] * reps, axis=1)


def _inproj_kernel(x_ref, g_ref, sc_ref, sh_ref, w_ref, walr_ref, w2_ref, ab_ref, *rest,
                   segs, rope_tiles, split_tiles):
    dils = sorted({dil for _, _, dil in split_tiles})
    n_seg, n_tab = len(segs), 2 * (1 + len(dils))
    tabs = [(rest[2 * v], rest[2 * v + 1]) for v in range(1 + len(dils))]
    seg_refs, la_ref = rest[n_tab:n_tab + n_seg], rest[n_tab + n_seg]
    split_refs, (h_scr, hf_scr) = rest[n_tab + n_seg + 1:-2], rest[-2:]
    j = pl.program_id(1)
    tm, d = x_ref.shape

    @pl.when(j == 0)
    def _():
        h = _rms(x_ref[...], g_ref[...]) * (1.0 + sc_ref[...]) + sh_ref[...]
        hb = h.astype(BF16)
        h_scr[0] = hb
        alr = _dot(hb, walr_ref[...])
        p = _dot(alr.astype(BF16), w2_ref[...]) + ab_ref[...]
        la_ref[...] = _log_sigmoid(p) * (1.0 / GLA_TAU)
        for c in range((d // LANES) if dils else 0):
            cs = slice(c * LANES, (c + 1) * LANES)
            hf_scr[c] = h[:, cs]
            for v, dil in enumerate(dils, 1):
                per = tm // dil
                for r in range(dil):
                    h_scr[v, r * per:(r + 1) * per, cs] = hf_scr[c, pl.ds(r, per, stride=dil), :].astype(BF16)

    def tile(v, roped):
        z = _dot(h_scr[v], w_ref[...])
        return _rope_tile(z, tabs[v][0][...], tabs[v][1][...]) if roped else z

    for (lo, hi), ref in zip(segs, seg_refs):
        ropes = [t for t in rope_tiles if lo <= t < hi]
        in_seg = jnp.logical_and(j >= lo, j < hi)
        is_rope = functools.reduce(jnp.logical_or, [j == t for t in ropes], False)
        if ropes:
            @pl.when(jnp.logical_and(in_seg, is_rope))
            def _(ref=ref):
                ref[...] = tile(0, True).astype(ref.dtype)

        @pl.when(jnp.logical_and(in_seg, jnp.logical_not(is_rope)))
        def _(ref=ref):
            ref[...] = tile(0, False).astype(ref.dtype)

    for t, slot, dil in split_tiles:
        @pl.when(j == t)
        def _(t=t, slot=slot, dil=dil):
            z = tile(1 + dils.index(dil), t in rope_tiles)
            split_refs[slot][...] = z.reshape(dil, tm // dil, z.shape[1]).astype(split_refs[slot].dtype)


def _in_proj(x2, g, sc, sh, w_main, w_alr, w2p, a_b, pos, tm, rows_per_mod, segs, rope_tiles, split_tiles):
    t, d = x2.shape
    r = sc.shape[1]
    wq = a_b.shape[1]
    n_tiles = w_main.shape[1] // TN_IN
    tiles_per_mod = rows_per_mod // tm
    tab_tiles = pos.shape[0] // tm
    nmod = t // rows_per_mod
    dils = sorted({dil for _, _, dil in split_tiles})
    tabs = list(_rope_tables(pos))
    for dil in dils:
        tabs += [tb.reshape(tab_tiles, tm // dil, dil, LANES).transpose(0, 2, 1, 3).reshape(-1, LANES)
                 for tb in tabs[:2]]
    mod_spec = pl.BlockSpec((None, r, d), lambda i, j: (i // tiles_per_mod, 0, 0))
    tab_spec = pl.BlockSpec((tm, LANES), lambda i, j: (i % tab_tiles, 0))
    const = lambda shape: pl.BlockSpec(shape, lambda i, j: (0, 0))
    out_specs, out_shape = [], []
    for lo, hi, dt in segs:
        out_specs.append(pl.BlockSpec((tm, TN_IN), lambda i, j, lo=lo, hi=hi: (i, jnp.clip(j - lo, 0, hi - lo - 1))))
        out_shape.append(jax.ShapeDtypeStruct((t, (hi - lo) * TN_IN), dt))
    out_specs.append(pl.BlockSpec((tm, wq), lambda i, j: (i, 0)))
    out_shape.append(jax.ShapeDtypeStruct((t, wq), F32))
    for _, _, dil in split_tiles:
        out_specs.append(pl.BlockSpec((None, dil, tm // dil, TN_IN),
                                      lambda i, j: (i // tiles_per_mod, 0, i % tiles_per_mod, 0)))
        out_shape.append(jax.ShapeDtypeStruct((nmod, dil, rows_per_mod // dil, TN_IN), BF16))
    return pl.pallas_call(
        functools.partial(_inproj_kernel, segs=tuple((lo, hi) for lo, hi, _ in segs),
                          rope_tiles=tuple(rope_tiles), split_tiles=tuple(split_tiles)),
        grid=(t // tm, n_tiles),
        in_specs=[pl.BlockSpec((tm, d), lambda i, j: (i, 0)), const((1, d)), mod_spec, mod_spec,
                  pl.BlockSpec((d, TN_IN), lambda i, j: (0, j)), const((d, LANES)), const((LANES, wq)),
                  const((1, wq))] + [tab_spec] * len(tabs),
        out_specs=out_specs,
        out_shape=out_shape,
        scratch_shapes=[pltpu.VMEM((1 + len(dils), tm, d), BF16),
                        pltpu.VMEM((d // LANES, tm if dils else SUBLANES, LANES), F32)],
        compiler_params=_cparams("arbitrary", "arbitrary"),
        name="in_proj",
    )(x2, g, sc, sh, w_main, w_alr, w2p, a_b, *tabs)


GLA_EXP_CLAMP = 80.0


def _gla_kernel(q_ref, k_ref, v_ref, ra_ref, la_ref, s0_ref, gn_ref, o_ref, sfin_ref, st_scr, *, rows, dk, dv):
    c = pl.program_id(1)
    nc = pl.num_programs(1)
    ch = GLA_CHUNK

    @pl.when(c == 0)
    def _():
        for h in range(GLA_HEADS):
            st_scr[h] = s0_ref[h].T

    ti = lax.broadcasted_iota(jnp.int32, (ch, ch), 0)
    si = lax.broadcasted_iota(jnp.int32, (ch, ch), 1)
    causal = si <= ti
    tri = causal.astype(BF16)
    scale = dk ** -0.5
    wk = GLA_HEADS * dk
    heads = range(GLA_HEADS)
    ks = [slice(h * dk, (h + 1) * dk) for h in heads]
    vs = [slice(h * dv, (h + 1) * dv) for h in heads]

    def chunk(ci, carry):
        rows = pl.ds(pl.multiple_of(ci * ch, ch), ch)
        la = la_ref[rows, :]
        hi = la.astype(BF16)
        r1 = la - hi.astype(F32)
        mid = r1.astype(BF16)
        lo = (r1 - mid.astype(F32)).astype(BF16)
        b3 = _dot(tri, jnp.concatenate([hi, mid, lo], axis=1))
        b = b3[:, :wk] + b3[:, wk:2 * wk] + b3[:, 2 * wk:]
        bl = b[ch - 1:ch, :]
        q = q_ref[rows, :] * scale
        k = k_ref[rows, :]
        qe = (q * jnp.exp(b)).astype(BF16)
        ke = (k * jnp.exp(jnp.minimum(-b, GLA_EXP_CLAMP))).astype(BF16)
        kh = (k * jnp.exp(bl - b)).astype(BF16)
        dec = jnp.exp(bl)
        v = v_ref[rows, :].astype(BF16)
        att = [jnp.where(causal, _dot_nt(qe[:, ks[h]], ke[:, ks[h]]), 0.0).astype(BF16) for h in heads]
        st = [st_scr[h] for h in heads]
        o = [_dot_nt(qe[:, ks[h]], st[h].astype(BF16)) + _dot(att[h], v[:, vs[h]]) for h in heads]
        for h in heads:
            st_scr[h] = st[h] * dec[:, ks[h]] + _dot_tn(v[:, vs[h]], kh[:, ks[h]])
        for h in heads:
            on = _rms(o[h], gn_ref[:, vs[h]])
            o_ref[rows, vs[h]] = (on * _silu(ra_ref[rows, vs[h]])).astype(o_ref.dtype)
        return carry

    lax.fori_loop(0, rows // ch, chunk, 0)

    @pl.when(c == nc - 1)
    def _():
        for h in range(GLA_HEADS):
            sfin_ref[h] = st_scr[h].T


def _gla_prompt(z, zv, la, s0, gn_g, n, l, rows):
    t = n * l
    dk = la.shape[1] // GLA_HEADS
    dv = gn_g.shape[1] // GLA_HEADS
    wk, wv = GLA_HEADS * dk, GLA_HEADS * dv
    cpb = l // rows
    row = lambda b, c: b * cpb + c
    return pl.pallas_call(
        functools.partial(_gla_kernel, rows=rows, dk=dk, dv=dv),
        grid=(n, cpb),
        in_specs=[pl.BlockSpec((rows, wk), lambda b, c: (row(b, c), 0)),
                  pl.BlockSpec((rows, wk), lambda b, c: (row(b, c), 1)),
                  pl.BlockSpec((rows, wv), lambda b, c: (row(b, c), 0)),
                  pl.BlockSpec((rows, wv), lambda b, c: (row(b, c), 1)),
                  pl.BlockSpec((rows, wk), lambda b, c: (row(b, c), 0)),
                  pl.BlockSpec((None, GLA_HEADS, dk, dv), lambda b, c: (b, 0, 0, 0)),
                  pl.BlockSpec((1, wv), lambda b, c: (0, 0))],
        out_specs=[pl.BlockSpec((rows, wv), lambda b, c: (row(b, c), 0)),
                   pl.BlockSpec((None, GLA_HEADS, dk, dv), lambda b, c: (b, 0, 0, 0))],
        out_shape=[jax.ShapeDtypeStruct((t, wv), BF16), jax.ShapeDtypeStruct((n, GLA_HEADS, dk, dv), F32)],
        scratch_shapes=[pltpu.VMEM((GLA_HEADS, dv, dk), F32)],
        compiler_params=_cparams("arbitrary", "arbitrary"),
        name="gla_prompt",
    )(z, z, zv, z, la, s0, gn_g)


def _gla_step_kernel(q_ref, k_ref, la_ref, v_ref, ra_ref, s_ref, gn_ref, o_ref, so_ref, *, dk):
    sn = jnp.exp(la_ref[...]) * s_ref[...] + k_ref[...] * v_ref[...]
    so_ref[...] = sn
    o = jnp.sum((q_ref[...] * dk ** -0.5) * sn, axis=1, keepdims=True)
    on = _rms(o, gn_ref[...])
    o_ref[...] = (on * _silu(ra_ref[...])).astype(o_ref.dtype)


def _gla_step(q, k, la, v, ra, s, gn_g):
    nb, nh, dk, dv = s.shape
    col = pl.BlockSpec((None, nh, dk, 1), lambda b: (b, 0, 0, 0))
    rowv = pl.BlockSpec((None, nh, 1, dv), lambda b: (b, 0, 0, 0))
    st = pl.BlockSpec((None, nh, dk, dv), lambda b: (b, 0, 0, 0))
    o, so = pl.pallas_call(
        functools.partial(_gla_step_kernel, dk=dk),
        grid=(nb,),
        in_specs=[col, col, col, rowv, rowv, st, pl.BlockSpec((nh, 1, dv), lambda b: (0, 0, 0))],
        out_specs=[rowv, st],
        out_shape=[jax.ShapeDtypeStruct((nb, nh, 1, dv), BF16), jax.ShapeDtypeStruct(s.shape, F32)],
        compiler_params=_cparams("arbitrary"),
        name="gla_step",
    )(q.reshape(nb, nh, dk, 1), k.reshape(nb, nh, dk, 1), la.reshape(nb, nh, dk, 1),
      v.reshape(nb, nh, 1, dv), ra.reshape(nb, nh, 1, dv), s, gn_g.reshape(nh, 1, dv))
    return o.reshape(nb, nh * dv), so


def _band_kernel(q_ref, k_ref, kp_ref, v_ref, vp_ref, o_ref, lse_ref, *, dil, qb):
    i = pl.program_id(1)
    blk = ATT_BLOCK
    qi = lax.broadcasted_iota(jnp.int32, (blk, 2 * blk), 0)
    ki = lax.broadcasted_iota(jnp.int32, (blk, 2 * blk), 1)
    dist = blk + qi - ki
    band = jnp.logical_and(dist >= 0, dist <= ATT_BLOCK)
    band_first = jnp.logical_and(band, ki >= jnp.where(i > 0, 0, blk))
    scale = ATT_HD ** -0.5
    for r in range(dil):
        for sb in range(qb // blk):
            rs = slice(sb * blk, (sb + 1) * blk)
            q = q_ref[r, rs, :].astype(BF16)
            if sb == 0:
                kprev, vprev, mask = kp_ref[r], vp_ref[r], band_first
            else:
                ps = slice((sb - 1) * blk, sb * blk)
                kprev, vprev, mask = k_ref[r, ps, :], v_ref[r, ps, :], band
            kk = jnp.concatenate([kprev, k_ref[r, rs, :]], axis=0).astype(BF16)
            vv = jnp.concatenate([vprev, v_ref[r, rs, :]], axis=0).astype(BF16)
            s = jnp.where(mask, _dot_nt(q, kk) * scale, NEG_BIG)
            m = jnp.max(s, axis=-1, keepdims=True)
            p = jnp.exp(s - m)
            lsum = jnp.sum(p, axis=-1, keepdims=True)
            o = _dot(p.astype(BF16), vv) / lsum
            lse = jnp.broadcast_to(m + jnp.log(lsum), (blk, ATT_HD))
            rows = rs if dil == 1 else pl.ds(sb * blk * dil + r, blk, stride=dil)
            o_ref[rows, :] = o
            lse_ref[rows, :] = lse


def _band_attention(qa, ka, va, cq, ck, cv, n, l, dil):
    lq = l // dil
    hd = ATT_HD
    tokens = min(ATT_TOKENS, l)
    qb = tokens // dil
    nqb = lq // qb
    sub = qb // ATT_BLOCK

    def main(c0):
        return pl.BlockSpec((None, dil, qb, hd), lambda b, i, h: (b, 0, i, c0 + h))

    def prev(c0):
        return pl.BlockSpec((None, dil, ATT_BLOCK, hd), lambda b, i, h: (b, 0, jnp.maximum(i * sub - 1, 0), c0 + h))

    out = pl.BlockSpec((tokens, hd), lambda b, i, h: (b * nqb + i, h))
    shp = jax.ShapeDtypeStruct((n * l, ATT_HEADS * hd), F32)
    return pl.pallas_call(
        functools.partial(_band_kernel, dil=dil, qb=qb),
        grid=(n, nqb, ATT_HEADS),
        in_specs=[main(cq), main(ck), prev(ck), main(cv), prev(cv)],
        out_specs=[out, out],
        out_shape=[shp, shp],
        compiler_params=_cparams("arbitrary", "arbitrary", "arbitrary"),
        name=f"band_attention_d{dil}",
    )(qa, ka, ka, va, va)


def _att_step_kernel(q_ref, kn_ref, vn_ref, c0_ref, c1_ref, c2_ref, o_ref):
    scale = ATT_HD ** -0.5
    outs, lses = [], []
    for g, c_ref in enumerate((c0_ref, c1_ref, c2_ref)):
        q, kn, vn = q_ref[g], kn_ref[g], vn_ref[g]
        s = jnp.sum(c_ref[:, 0] * q[None], axis=-1, keepdims=True) * scale
        sn = jnp.sum(kn * q, axis=-1, keepdims=True) * scale
        m = jnp.maximum(jnp.max(s, axis=0), sn)
        p = jnp.exp(s - m[None])
        pn = jnp.exp(sn - m)
        lsum = jnp.sum(p, axis=0) + pn
        outs.append((jnp.sum(p * c_ref[:, 1], axis=0) + pn * vn) / lsum)
        lses.append(m + jnp.log(lsum))
    mx = functools.reduce(jnp.maximum, lses)
    es = [jnp.exp(x - mx) for x in lses]
    inv = 1.0 / functools.reduce(lambda a, b: a + b, es)
    o_ref[...] = functools.reduce(lambda a, b: a + b, [e * inv * o for e, o in zip(es, outs)]).astype(o_ref.dtype)


def _att_step(qs, ks, vs, caches):
    nb = qs.shape[0]
    views, cspecs = [], []
    for (window, dil), cch in zip(ATT_GROUPS, caches):
        assert cch.shape[1] == window, "sample step expects full caches"
        views.append(cch.reshape((nb, window // dil, dil) + cch.shape[2:]))
        cspecs.append(pl.BlockSpec((None, ATT_BLOCK, None) + cch.shape[2:], lambda b: (b, 0, 0, 0, 0, 0)))
    row = pl.BlockSpec((None,) + qs.shape[1:], lambda b: (b, 0, 0, 0))
    o = pl.pallas_call(
        _att_step_kernel,
        grid=(nb,),
        in_specs=[row] * 3 + cspecs,
        out_specs=pl.BlockSpec((None, ATT_HEADS, ATT_HD), lambda b: (b, 0, 0)),
        out_shape=jax.ShapeDtypeStruct((nb, ATT_HEADS, ATT_HD), BF16),
        compiler_params=_cparams("arbitrary"),
        name="att_step",
    )(qs, ks, vs, *views)
    return o.reshape(nb, ATT_HEADS * ATT_HD)


MERGE_PART = 256


def _merge_kernel(*refs, n_att, n_alias, n_tiles):
    oa_ref = refs[0]
    att = refs[1:1 + n_att]
    (gza_ref, gzb_ref, x_ref, g1_ref, sc2_ref, sh2_ref, n2_ref, wa_ref, wb_ref, wo_ref,
     wq_ref) = refs[1 + n_att:12 + n_att]
    x1_ref, h2_ref, pq_ref = refs[12 + n_att + n_alias:]
    tm = x_ref.shape[0]
    pr = min(MERGE_PART, tm)
    parts = [slice(k * pr, (k + 1) * pr) for k in range(tm // pr)]
    mod = lambda ref, part: ref[...] if ref.shape[0] == 1 else ref[part, :]
    add = lambda a, b: a + b

    def attention_out(part):
        if n_att == 1:
            return att[0][part, :]
        ng = n_att // 2
        ls = [r[part, :] for r in att[ng:]]
        mx = functools.reduce(jnp.maximum, ls)
        es = [jnp.exp(x - mx) for x in ls]
        inv = 1.0 / functools.reduce(add, es)
        return functools.reduce(add, [e * inv * r[part, :] for e, r in zip(es, att[:ng])]).astype(BF16)

    @pl.when(pl.program_id(0) >= n_tiles)
    def _():
        h2_ref[...] = jnp.zeros_like(h2_ref)
        pq_ref[...] = jnp.zeros_like(pq_ref)

    @pl.when(pl.program_id(0) < n_tiles)
    def _():
        _merge_rows(parts, attention_out, mod, oa_ref, gza_ref, gzb_ref, x_ref, g1_ref, sc2_ref, sh2_ref, n2_ref,
                    wa_ref, wb_ref, wo_ref, wq_ref, x1_ref, h2_ref, pq_ref)
        if h2_ref.shape[0] > tm:
            h2_ref[tm:] = jnp.zeros((h2_ref.shape[0] - tm, h2_ref.shape[1]), h2_ref.dtype)
            pq_ref[tm:] = jnp.zeros((pq_ref.shape[0] - tm, pq_ref.shape[1]), pq_ref.dtype)


def _merge_rows(parts, attention_out, mod, oa_ref, gza_ref, gzb_ref, x_ref, g1_ref, sc2_ref, sh2_ref, n2_ref,
                wa_ref, wb_ref, wo_ref, wq_ref, x1_ref, h2_ref, pq_ref):
    ob = [attention_out(p) for p in parts]
    ma = [_dot(oa_ref[p, :], wa_ref[...]) for p in parts]
    mb = [_dot(o, wb_ref[...]) for o in ob]
    merged = [(_sigmoid(gza_ref[p, :]) * a + _sigmoid(gzb_ref[p, :]) * b).astype(BF16)
              for p, a, b in zip(parts, ma, mb)]
    y = [_dot(m, wo_ref[...]) for m in merged]
    h2 = []
    for p, yy in zip(parts, y):
        x1 = x_ref[p, :] + mod(g1_ref, p) * yy
        x1_ref[p, :] = x1
        h2.append((_rms(x1, n2_ref[...]) * (1.0 + mod(sc2_ref, p)) + mod(sh2_ref, p)).astype(BF16))
    for p, h in zip(parts, h2):
        h2_ref[p, :] = h
        pq_ref[p, :] = _dot(h, wq_ref[...])


def _merge(oa, att, z, x2, g1, sc2, sh2, n2g, wa, wb, wo, wq, tm, rows_per_mod, gza_blk, gzb_blk,
           peer_rows, peer_tile, peer_blk0, alias=(), zero_tiles=0):
    t, d = x2.shape
    r = g1.shape[1]
    wqn = wq.shape[1]
    tiles_per_mod = rows_per_mod // tm
    n_tiles = t // tm
    row = lambda i: jnp.minimum(i, n_tiles - 1)
    tok = lambda w: pl.BlockSpec((tm, w), lambda i: (row(i), 0))
    mod = pl.BlockSpec((None, r, d), lambda i: (row(i) // tiles_per_mod, 0, 0))
    const = lambda a: pl.BlockSpec(a.shape, lambda i: (0, 0), pipeline_mode=pl.Buffered(1))
    peer = lambda w: pl.BlockSpec((peer_tile, w), lambda i: (peer_blk0 + i, 0))
    n_in = 12 + len(att)
    return pl.pallas_call(
        functools.partial(_merge_kernel, n_att=len(att), n_alias=len(alias), n_tiles=n_tiles),
        grid=(n_tiles + zero_tiles,),
        in_specs=[tok(oa.shape[1])] + [tok(a.shape[1]) for a in att]
                 + [pl.BlockSpec((tm, d), lambda i: (row(i), gza_blk)),
                    pl.BlockSpec((tm, d), lambda i: (row(i), gzb_blk)),
                    tok(d), mod, mod, mod, const(n2g), const(wa), const(wb), const(wo), const(wq)]
                 + [pl.BlockSpec(memory_space=pl.ANY)] * len(alias),
        out_specs=[tok(d), peer(d), peer(wqn)],
        out_shape=[jax.ShapeDtypeStruct((t, d), F32), jax.ShapeDtypeStruct((peer_rows, d), BF16),
                   jax.ShapeDtypeStruct((peer_rows, wqn), F32)],
        input_output_aliases={n_in + k: 1 + k for k in range(len(alias))},
        compiler_params=_cparams("arbitrary"),
        name="merge",
    )(oa, *att, z, z, x2, g1, sc2, sh2, n2g, wa, wb, wo, wq, *alias)


def _strict_max_below(rows, prev):
    m = None
    for x in rows:
        y = jnp.where(x < prev, x, NEG_BIG) if prev is not None else x
        m = y if m is None else jnp.maximum(m, y)
    return m


def _oddeven_merge(lo, hi, r):
    step = r * 2
    if step < hi - lo:
        yield from _oddeven_merge(lo, hi, step)
        yield from _oddeven_merge(lo + r, hi, step)
        yield from [(i, i + r) for i in range(lo + r, hi - r, step)]
    else:
        yield (lo, lo + r)


def _oddeven_merge_sort(lo, hi):
    if hi - lo >= 1:
        mid = lo + (hi - lo) // 2
        yield from _oddeven_merge_sort(lo, mid)
        yield from _oddeven_merge_sort(mid + 1, hi)
        yield from _oddeven_merge(lo, hi, 1)


def _exchange(p, i, j):
    p[i], p[j] = jnp.maximum(p[i], p[j]), jnp.minimum(p[i], p[j])


def _topk_desc(s, k):
    assert s.shape[0] == k * SUBLANES and k & (k - 1) == 0
    p = [s[i * SUBLANES:(i + 1) * SUBLANES] for i in range(k)]
    for i, j in _oddeven_merge_sort(0, k - 1):
        _exchange(p, i, j)
    shift = SUBLANES // 2
    while shift:
        q = [pltpu.roll(v, shift, 0) for v in p]
        p = [jnp.maximum(p[i], q[k - 1 - i]) for i in range(k)]
        stride = k // 2
        while stride:
            for i in range(k):
                if not i & stride:
                    _exchange(p, i, i + stride)
            stride //= 2
        shift //= 2
    return [v[0:1] for v in p]


def _route_kernel(pq_ref, k1_ref, k2_ref, n1_ref, e1_ref, r2_ref, e2_ref, s1_scr, s2_scr):
    nk = PEER_NKEYS
    pq = pq_ref[...].astype(BF16)
    s1_scr[...] = _dot_nt(k1_ref[...], pq)
    s2_scr[...] = _dot_nt(k2_ref[...], pq)
    v1h, v2h = [], []
    for h in range(PEER_HEADS):
        v1h.append(_topk_desc(s1_scr[h * nk:(h + 1) * nk], PEER_TOPK))
        v2h.append(_topk_desc(s2_scr[h * nk:(h + 1) * nk], PEER_TOPK))
    v1 = [jnp.concatenate([v1h[h][k] for h in range(PEER_HEADS)], axis=0) for k in range(PEER_TOPK)]
    v2 = [jnp.concatenate([v2h[h][k] for h in range(PEER_HEADS)], axis=0) for k in range(PEER_TOPK)]
    pairs = [(i, j) for i in range(PEER_TOPK) for j in range(PEER_TOPK) if (i + 1) * (j + 1) <= PEER_TOPK]
    cands = [v1[i] + v2[j] for i, j in pairs]
    tau = None
    for _ in range(PEER_TOPK):
        tau = _strict_max_below(cands, tau)
    cmax = v1[0] + v2[0]
    zsum = None
    counts = [None] * PEER_TOPK
    for (i, _), cnd in zip(pairs, cands):
        sel = cnd >= tau
        e = jnp.where(sel, jnp.exp(cnd - cmax), 0.0)
        zsum = e if zsum is None else zsum + e
        one = jnp.where(sel, 1.0, 0.0)
        counts[i] = one if counts[i] is None else counts[i] + one
    zinv = 1.0 / zsum
    for h in range(PEER_HEADS):
        a = s1_scr[h * nk:(h + 1) * nk]
        b = s2_scr[h * nk:(h + 1) * nk]
        n1 = jnp.zeros_like(a)
        r2 = jnp.full_like(b, float(PEER_TOPK))
        for k in range(PEER_TOPK):
            n1 = jnp.where(a == v1h[h][k], counts[k][h:h + 1, :], n1)
            r2 = jnp.where(b == v2h[h][k], float(k), r2)
        n1_ref[h] = n1
        e1_ref[h] = jnp.exp(a - v1h[h][0]) * zinv[h:h + 1, :]
        r2_ref[h] = r2.astype(r2_ref.dtype)
        e2_ref[h] = jnp.exp(b - v2h[h][0]).astype(e2_ref.dtype)


def _peer_route(pq, k1big, k2big, tt):
    t, w = pq.shape
    nh, nk = PEER_HEADS, PEER_NKEYS
    sspec = pl.BlockSpec((nh, nk, tt), lambda i: (0, 0, i))
    shp = lambda dt: jax.ShapeDtypeStruct((nh, nk, t), dt)
    return pl.pallas_call(
        _route_kernel,
        grid=(t // tt,),
        in_specs=[pl.BlockSpec((tt, w), lambda i: (i, 0)),
                  pl.BlockSpec(k1big.shape, lambda i: (0, 0)), pl.BlockSpec(k2big.shape, lambda i: (0, 0))],
        out_specs=[sspec] * 4,
        out_shape=[shp(F32), shp(F32), shp(BF16), shp(BF16)],
        scratch_shapes=[pltpu.VMEM((nh * nk, tt), F32)] * 2,
        compiler_params=_cparams("arbitrary"),
        name="peer_route",
    )(pq, k1big, k2big)


def _gelu_tanh(x):
    return 0.5 * x * (1.0 + jnp.tanh(GELU_C * (x + 0.044715 * (x * x * x))))


BF16_ROWS = 16
GATE_LANES = 256
PEER_PART = 256
PEER_TILE = 2048


def _gelu_tanh_bf16(x):
    hx = 0.5 * x
    return hx + hx * jnp.tanh(x * (GELU_C + (GELU_C * 0.044715) * (x * x)))


def _bf16_rows(row):
    return jnp.broadcast_to(row, (BF16_ROWS, row.shape[1])).astype(BF16)


CACHE_CHUNK = 512


def _cache_shift_jobs(caches):
    nb = caches[0].shape[0]
    chunks, base = [], 0
    for c, cache in enumerate(caches):
        w = cache.shape[1]
        for r0 in range(0, w, CACHE_CHUNK):
            n = min(CACHE_CHUNK, w - r0)
            chunks.append((c, r0, n, r0 + n == w, base))
            base += nb
    return chunks, base


def _cache_shift_step(step, caches, new_rows, outs, buf, sems):
    nb = caches[0].shape[0]
    chunks, n_jobs = _cache_shift_jobs(caches)

    def for_job(job, act):
        for c, r0, n, last, base in chunks:
            @pl.when(jnp.logical_and(job >= base, job < base + nb))
            def _(c=c, r0=r0, n=n, last=last, base=base):
                b, slot = job - base, job % 2
                n_in = n - 1 if last else n
                rd = pltpu.make_async_copy(caches[c].at[b, pl.ds(r0 + 1, n_in)], buf.at[slot, pl.ds(0, n_in)],
                                           sems.at[slot, 0])
                wr = pltpu.make_async_copy(buf.at[slot, pl.ds(0, n)], outs[c].at[b, pl.ds(r0, n)], sems.at[slot, 1])
                act(rd, wr, lambda: new_rows[c][b] if last else None, slot, n_in)

    def finish_read_start_write(rd, wr, new_row, slot, n_in):
        rd.wait()
        row = new_row()
        if row is not None:
            buf[slot, pl.ds(n_in, 1)] = row
        wr.start()

    @pl.when(jnp.logical_and(step >= 1, step <= n_jobs))
    def _():
        for_job(step - 1, finish_read_start_write)

    @pl.when(jnp.logical_and(step >= 2, step <= n_jobs + 1))
    def _():
        for_job(step - 2, lambda rd, wr, *_: wr.wait())

    @pl.when(step < n_jobs)
    def _():
        for_job(step, lambda rd, wr, *_: rd.start())


def _peer_dense_kernel(h2_ref, u_ref, vt_ref, n1_ref, e1_ref, r2_ref, e2_ref, h2_next_ref, u_next_ref, *rest,
                       et, n_kv):
    caches, new_rows = rest[:n_kv], rest[n_kv:2 * n_kv]
    o_ref, kv_outs = rest[2 * n_kv], rest[2 * n_kv + 1:3 * n_kv + 1]
    acc_scr, wg_scr, st_scr, st0_scr = rest[3 * n_kv + 1:3 * n_kv + 5]
    e = pl.program_id(1)
    nk = PEER_NKEYS
    tt = h2_ref.shape[0]
    step = pl.program_id(0) * pl.num_programs(1) + e
    if n_kv:
        _cache_shift_step(step, caches, new_rows, kv_outs, rest[-2], rest[-1])

    @pl.when(step == 0)
    def _():
        st0_scr[...] = _dot_nt(u_ref[0:PEER_PART, :], h2_ref[...]).astype(BF16)

    @pl.when(e == 0)
    def _():
        acc_scr[...] = jnp.zeros_like(acc_scr)

    zero = jnp.zeros((), BF16)
    groups = nk // BF16_ROWS

    def scores(part):
        st_scr[part, :] = _dot_nt(u_ref[part, :], h2_ref[...]).astype(BF16)

    def gates(ii):
        rs = slice(ii * nk, (ii + 1) * nk)
        for lc in range(tt // GATE_LANES):
            ls = slice(lc * GATE_LANES, (lc + 1) * GATE_LANES)
            w = None
            for h in range(PEER_HEADS):
                n1 = _bf16_rows(n1_ref[h, ii:ii + 1, ls])
                e1 = _bf16_rows(e1_ref[h, ii:ii + 1, ls])
                r2 = r2_ref[h, :, ls].reshape(groups, BF16_ROWS, GATE_LANES)
                e2 = e2_ref[h, :, ls].reshape(groups, BF16_ROWS, GATE_LANES)
                c = jnp.where(r2 < n1[None], e2, zero) * e1[None]
                w = c if w is None else w + c
            st = st0_scr if rs.stop <= PEER_PART else st_scr
            wg_scr[rs, ls] = w.reshape(nk, GATE_LANES) * _gelu_tanh_bf16(st[rs, ls])

    def mix(part):
        return _dot(vt_ref[:, part], wg_scr[part, :])

    parts = [slice(p * PEER_PART, (p + 1) * PEER_PART) for p in range(et // PEER_PART)]
    out = None
    next_scores = None
    for p, part in enumerate(parts):
        if p + 1 < len(parts):
            scores(parts[p + 1])
        else:
            next_scores = _dot_nt(u_next_ref[...], h2_next_ref[...]).astype(BF16)
        for ii in range(part.start // nk, part.stop // nk):
            gates(ii)
        if p > 0:
            d = mix(parts[p - 1])
            out = d if out is None else out + d
    st0_scr[...] = next_scores
    d = mix(parts[-1])
    acc_scr[...] += d if out is None else out + d

    @pl.when(e == pl.num_programs(1) - 1)
    def _():
        o_ref[...] = acc_scr[...].T


def _peer_dense(h2, u_b, vt_b, n1, e1, r2, e2, tt, et, caches=(), new_rows=()):
    t, d = h2.shape
    ne = u_b.shape[0]
    nh, nk = PEER_HEADS, PEER_NKEYS
    n_kv = len(caches)
    grid = (t // tt, ne // et)
    if n_kv:
        assert grid[0] * grid[1] >= _cache_shift_jobs(caches)[1] + 2, "one cache-shift job per grid step"
    rowside = pl.BlockSpec((nh, et // nk, tt), lambda i, e: (0, e, i))
    colside = pl.BlockSpec((nh, nk, tt), lambda i, e: (0, 0, i))
    anywhere = pl.BlockSpec(memory_space=pl.ANY)
    whole = lambda a: pl.BlockSpec(a.shape, lambda i, e: (0,) * a.ndim)
    outs = pl.pallas_call(
        functools.partial(_peer_dense_kernel, et=et, n_kv=n_kv),
        grid=grid,
        in_specs=[pl.BlockSpec((tt, d), lambda i, e: (i, 0)),
                  pl.BlockSpec((et, d), lambda i, e: (e, 0)),
                  pl.BlockSpec((d, et), lambda i, e: (0, e)),
                  rowside, rowside, colside, colside,
                  pl.BlockSpec((tt, d), lambda i, e: (jnp.minimum(i + (e + 1) // grid[1], grid[0] - 1), 0)),
                  pl.BlockSpec((PEER_PART, d), lambda i, e: (((e + 1) % grid[1]) * (et // PEER_PART), 0))]
                 + [anywhere] * n_kv + [whole(a) for a in new_rows],
        out_specs=[pl.BlockSpec((tt, d), lambda i, e: (i, 0))] + [anywhere] * n_kv,
        out_shape=[jax.ShapeDtypeStruct((t, d), F32)] + [jax.ShapeDtypeStruct(c.shape, c.dtype) for c in caches],
        scratch_shapes=[pltpu.VMEM((d, tt), F32), pltpu.VMEM((et, tt), BF16), pltpu.VMEM((et, tt), BF16),
                        pltpu.VMEM((PEER_PART, tt), BF16)]
                       + ([pltpu.VMEM((2, CACHE_CHUNK) + caches[0].shape[2:], caches[0].dtype),
                           pltpu.SemaphoreType.DMA((2, 2))] if n_kv else []),
        compiler_params=_cparams("arbitrary", "arbitrary"),
        name="peer_dense",
    )(h2, u_b, vt_b, n1, e1, r2, e2, h2, u_b, *caches, *new_rows)
    return outs[0], outs[1:]


def _final_kernel(x1_ref, p_ref, g2_ref, fg_ref, y_ref):
    y_ref[...] = _rms(x1_ref[...] + g2_ref[...] * p_ref[...], fg_ref[...])


def _final(x1, p, g2, fg, tm, rows_per_mod, p_blk0):
    t, d = x1.shape
    r = g2.shape[1]
    tiles_per_mod = rows_per_mod // tm
    tok = pl.BlockSpec((tm, d), lambda i: (i, 0))
    return pl.pallas_call(
        _final_kernel,
        grid=(t // tm,),
        in_specs=[tok, pl.BlockSpec((tm, d), lambda i: (p_blk0 + i, 0)),
                  pl.BlockSpec((None, r, d), lambda i: (i // tiles_per_mod, 0, 0)),
                  pl.BlockSpec((1, d), lambda i: (0, 0))],
        out_specs=tok,
        out_shape=jax.ShapeDtypeStruct((t, d), F32),
        compiler_params=_cparams("arbitrary"),
        name="final_norm",
    )(x1, p, g2, fg)


def _rope_tables(pos):
    half = ROPE_DIM // 2
    inv_freq = ROPE_THETA ** (-jnp.arange(half, dtype=F32) / half)
    ang = pos.astype(F32)[:, None] * inv_freq[None, :]
    cos, sin = jnp.cos(ang), jnp.sin(ang)
    n = pos.shape[0]
    ones = jnp.ones((n, LANES - ROPE_DIM), F32)
    zeros = jnp.zeros((n, LANES - ROPE_DIM), F32)
    return (jnp.concatenate([cos, cos, ones], axis=1), jnp.concatenate([-sin, sin, zeros], axis=1))


def _block_diag_keys(keys, half):
    nk, dh = keys.shape
    eye = jnp.eye(PEER_HEADS, dtype=keys.dtype)
    blk = jnp.zeros((PEER_HEADS, nk, PEER_HEADS, 2, dh), keys.dtype)
    blk = blk.at[:, :, :, half, :].set(eye[:, None, :, None] * keys[None, :, None, :])
    return blk.reshape(PEER_HEADS * nk, PEER_HEADS * 2 * dh).astype(BF16)


def _pick_tile(n, pref):
    t = min(pref, n)
    while n % t:
        t //= 2
    return t


def kernel(x_prompt, x_sample, c_prompt, c_sample, state_gla, cache_kv_w128, cache_kv_w512, cache_kv_w2048,
           ada_w, ada_b, norm1_g, w_in, gla_a_w2, gla_a_b, gla_gn_g, w_branch_a, w_branch_b, w_out,
           norm2_g, peer_wq, peer_k1, peer_k2, peer_u, peer_v, final_g):
    depth = ada_w.shape[0]
    assert depth == 1, "single-layer trunk"
    n_p, l_p, d = x_prompt.shape
    n_s, l_s, _ = x_sample.shape
    assert l_s == 1, "sample group decodes one token per sequence"
    caches = (cache_kv_w128[0], cache_kv_w512[0], cache_kv_w2048[0])
    n_groups = len(ATT_GROUPS)

    w_qa = gla_a_w2.shape[2]
    w_va = gla_gn_g.shape[1]
    w_ob = ATT_HEADS * ATT_HD
    w_qb = n_groups * w_ob
    assert w_qa == TN_IN and w_va == 2 * TN_IN and d == 2 * TN_IN and w_ob == TN_IN
    widths = (w_qa, w_qa, w_va, w_va, GLA_RANK, w_qb, w_qb, w_qb, d, d)
    offs = [0]
    for w in widths:
        offs.append(offs[-1] + w)
    seg = lambda i: w_in[0][:, offs[i]:offs[i + 1]]
    att_col = lambda g, k: seg(5 + k)[:, g * w_ob:(g + 1) * w_ob]
    att_cols = [att_col(g, k) for g in range(n_groups) for k in range(3)]
    w_main = jnp.concatenate([seg(0), seg(1), seg(3), seg(8), seg(9), seg(2)] + att_cols, axis=1).astype(BF16)
    w_kv = jnp.concatenate([att_col(g, k) for g in range(n_groups) for k in (1, 2)], axis=1).astype(BF16)
    n_tiles = w_main.shape[1] // TN_IN
    ra_off, gza_blk, gzb_blk = 2 * w_qa, 2, 3
    w_alr = jnp.pad(seg(4), ((0, 0), (0, LANES - GLA_RANK))).astype(BF16)
    w2p = jnp.pad(gla_a_w2[0], ((0, LANES - GLA_RANK), (0, 0))).astype(BF16)
    a_b = gla_a_b[0].reshape(1, w_qa)
    g1n = norm1_g[0].reshape(1, d)
    gn_g = gla_gn_g[0].reshape(1, w_va)
    n2g = norm2_g[0].reshape(1, d)
    wa, wb = w_branch_a[0].astype(BF16), w_branch_b[0].astype(BF16)
    wo, wq = w_out[0].astype(BF16), peer_wq[0].astype(BF16)
    q_tile = lambda g: ATT_TILE0 + 3 * g
    rope_tiles = [q_tile(g) + k for g in range(n_groups) for k in (0, 1)]

    n_mod = n_p + n_s
    n_pad = -(-n_mod // SUBLANES) * SUBLANES
    c_all = jnp.concatenate([c_prompt, c_sample, jnp.zeros((n_pad - n_mod, d), F32)], axis=0)
    mod = _modulation(c_all, ada_w[0], ada_b[0])
    mods_p = [mod[:n_p, i * d:(i + 1) * d].reshape(n_p, 1, d) for i in range(6)]
    mods_s = [mod[n_p:n_mod, i * d:(i + 1) * d].reshape(1, n_s, d) for i in range(6)]

    t_p = n_p * l_p
    tt = _pick_tile(t_p, 512)
    t_all = t_p + (-(-n_s // tt)) * tt
    assert t_all - t_p == tt

    xp2 = x_prompt.reshape(t_p, d)
    tm_p = _pick_tile(l_p, 1024)
    pos_p = jnp.arange(l_p, dtype=jnp.int32)
    split = []
    for g, (window, dil) in enumerate(ATT_GROUPS):
        assert window // dil == ATT_BLOCK
        if dil > 1:
            split += [(q_tile(g) + k, len(split) + k, dil) for k in range(3)]
    first_split = min([t for t, _, _ in split], default=n_tiles)
    outs = _in_proj(xp2, g1n, mods_p[1], mods_p[0], w_main, w_alr, w2p, a_b, pos_p, tm_p, l_p,
                    ((0, VA_TILE0, F32), (VA_TILE0, first_split, BF16)), rope_tiles, split)
    z_p, zb_p, la_p, split_p = outs[0], outs[1], outs[2], outs[3:]
    s0_p = jnp.zeros((n_p,) + state_gla.shape[2:], F32)
    oa_p, sfin_p = _gla_prompt(z_p, zb_p, la_p, s0_p, gn_g, n_p, l_p, _pick_tile(l_p, 256))
    zb4 = zb_p.reshape(n_p, 1, l_p, zb_p.shape[1])
    att_o, att_l = [], []
    for g, (window, dil) in enumerate(ATT_GROUPS):
        if dil > 1:
            qa, ka, va = [split_p[slot] for t, slot, _ in split if q_tile(g) <= t < q_tile(g) + 3]
            o_g, lse_g = _band_attention(qa, ka, va, 0, 0, 0, n_p, l_p, dil)
        else:
            c0 = lambda k: (q_tile(g) + k - VA_TILE0) * (TN_IN // ATT_HD)
            o_g, lse_g = _band_attention(zb4, zb4, zb4, c0(0), c0(1), c0(2), n_p, l_p, dil)
        att_o.append(o_g)
        att_l.append(lse_g)
    keep_max = min(max(w for w, _ in ATT_GROUPS), l_p)
    x_tail = x_prompt[:, l_p - keep_max:].reshape(n_p * keep_max, d)
    kv_tiles = 2 * n_groups
    z_kv = _in_proj(x_tail, g1n, mods_p[1], mods_p[0], w_kv, w_alr, w2p, a_b, pos_p[l_p - keep_max:],
                    _pick_tile(keep_max, 1024), keep_max, ((0, kv_tiles, F32),), range(0, kv_tiles, 2), [])[0]
    z_kv = z_kv.reshape(n_p, keep_max, kv_tiles, ATT_HEADS, ATT_HD)
    kv_p = []
    for g, (window, dil) in enumerate(ATT_GROUPS):
        keep = min(window, l_p)
        kv_p.append(z_kv[:, keep_max - keep:, 2 * g:2 * g + 2][None])
    tm_m = _pick_tile(l_p, 512)
    x1_p, h2_all, pq_all = _merge(oa_p, att_o + att_l, z_p, xp2, mods_p[2], mods_p[4], mods_p[3], n2g,
                                  wa, wb, wo, wq, tm_m, l_p, gza_blk, gzb_blk, t_all, tm_m, 0,
                                  zero_tiles=(t_all - t_p) // tm_m)

    xs2 = x_sample.reshape(n_s, d)
    pos_s = jnp.full((n_s,), PAST_LEN, dtype=jnp.int32)
    z_s, la_s = _in_proj(xs2, g1n, mods_s[1], mods_s[0], w_main, w_alr, w2p, a_b, pos_s, n_s, n_s,
                         ((0, n_tiles, F32),), rope_tiles, [])
    col = lambda tile: z_s[:, tile * TN_IN:(tile + 1) * TN_IN]
    oa_s, state_s = _gla_step(col(0), col(1), la_s, z_s[:, VA_TILE0 * TN_IN:VA_TILE0 * TN_IN + w_va],
                              z_s[:, ra_off:ra_off + w_va], state_gla[0], gn_g)
    stack = lambda k: jnp.stack([col(q_tile(g) + k) for g in range(n_groups)], axis=1).reshape(
        n_s, n_groups, ATT_HEADS, ATT_HD)
    ob_s = _att_step(stack(0), stack(1), stack(2), caches)
    x1_s, h2_all, pq_all = _merge(oa_s, [ob_s], z_s, xs2, mods_s[2], mods_s[4], mods_s[3], n2g,
                                  wa, wb, wo, wq, n_s, n_s, gza_blk, gzb_blk, t_all, tt, t_p // tt,
                                  alias=(h2_all, pq_all))
    row_tiles = (2 * w_ob // LANES, LANES)
    new_rows = [jnp.concatenate([col(q_tile(g) + 1), col(q_tile(g) + 2)], axis=1).reshape((n_s, 1) + row_tiles)
                for g in range(n_groups)]

    k1big = _block_diag_keys(peer_k1[0], 0)
    k2big = _block_diag_keys(peer_k2[0], 1)
    n1, e1, r2, e2 = _peer_route(pq_all, k1big, k2big, _pick_tile(tt, 256))
    u_b = peer_u[0].astype(BF16)
    vt_b = peer_v[0].astype(BF16).T
    p_all, kv_s = _peer_dense(h2_all, u_b, vt_b, n1, e1, r2, e2, tt, PEER_TILE,
                              [c.reshape(c.shape[:2] + row_tiles) for c in caches], new_rows)
    kv_s = [a.reshape(c.shape)[None] for a, c in zip(kv_s, caches)]

    fg = final_g.reshape(1, d)
    y_p = _final(x1_p, p_all, mods_p[5], fg, tm_m, l_p, 0)
    y_s = _final(x1_s, p_all, mods_s[5], fg, n_s, n_s, t_p // n_s)

    return (y_p.reshape(n_p, l_p, d), y_s.reshape(n_s, l_s, d), sfin_p[None], state_s[None],
            kv_p[0], kv_s[0], kv_p[1], kv_s[1], kv_p[2], kv_s[2])
```

```python
import functools
import math

import jax
import jax.numpy as jnp
from jax import lax
from jax.experimental import pallas as pl
from jax.experimental.pallas import tpu as pltpu

F32 = jnp.float32
BF16 = jnp.bfloat16

PAST_LEN = 16384
GLA_HEADS = 4
GLA_RANK = 16
GLA_TAU = 16.0
GLA_CHUNK = 128
ATT_GROUPS = ((128, 1), (512, 4), (2048, 16))
ATT_HEADS = 4
ATT_HD = 128
ATT_BLOCK = 128
ATT_TOKENS = 2048
ROPE_DIM = ATT_HD // 4
ROPE_THETA = 500000.0
PEER_HEADS = 8
PEER_NKEYS = 128
PEER_DQ = 128
PEER_TOPK = 16
NORM_EPS = 1e-6
GELU_C = math.sqrt(2.0 / math.pi)

LANES = 128
SUBLANES = 8
VMEM_LIMIT = 56 * 1024 * 1024
NEG_BIG = -1e30


def _cparams(*sem):
    return pltpu.CompilerParams(dimension_semantics=sem, vmem_limit_bytes=VMEM_LIMIT)


def _sigmoid(x):
    return 1.0 / (1.0 + jnp.exp(-x))


def _silu(x):
    return x * _sigmoid(x)


def _log_sigmoid(x):
    return jnp.minimum(x, 0.0) - jnp.log(1.0 + jnp.exp(-jnp.abs(x)))


def _rms(x, g):
    return x * lax.rsqrt(jnp.mean(x * x, axis=-1, keepdims=True) + NORM_EPS) * g


def _dot(a, b):
    return jnp.dot(a, b, preferred_element_type=F32)


def _dot_nt(a, b):
    return lax.dot_general(a, b, (((1,), (1,)), ((), ())), preferred_element_type=F32)


def _dot_tn(a, b):
    return lax.dot_general(a, b, (((0,), (0,)), ((), ())), preferred_element_type=F32)


def _mod_kernel(c_ref, w_ref, b_ref, o_ref):
    s = _silu(c_ref[...]).astype(BF16)
    o_ref[...] = _dot(s, w_ref[...].astype(BF16)) + b_ref[...]


def _modulation(c, ada_w, ada_b):
    n, d = c.shape
    n6 = ada_w.shape[1]
    tn = n6 // 4
    return pl.pallas_call(
        _mod_kernel,
        grid=(n6 // tn,),
        in_specs=[pl.BlockSpec((n, d), lambda j: (0, 0)),
                  pl.BlockSpec((d, tn), lambda j: (0, j)),
                  pl.BlockSpec((1, tn), lambda j: (0, j))],
        out_specs=pl.BlockSpec((n, tn), lambda j: (0, j)),
        out_shape=jax.ShapeDtypeStruct((n, n6), F32),
        compiler_params=_cparams("arbitrary"),
        name="modulation",
    )(c, ada_w, ada_b.reshape(1, n6))


TN_IN = 512
VA_TILE0 = 8
ATT_TILE0 = 10


def _rope_tile(z, c, s):
    reps = z.shape[1] // LANES
    n = z.shape[1]
    half = ROPE_DIM // 2
    lane = lax.broadcasted_iota(jnp.int32, (1, LANES), 1)
    first = jnp.concatenate([lane < half] * reps, axis=1)
    partner = jnp.where(first, pltpu.roll(z, n - half, 1), pltpu.roll(z, half, 1))
    return z * jnp.concatenate([c] * reps, axis=1) + partner * jnp.concatenate([s] * reps, axis=1)


def _inproj_kernel(x_ref, g_ref, sc_ref, sh_ref, w_ref, walr_ref, w2_ref, ab_ref, *rest,
                   segs, rope_tiles, split_tiles):
    dils = sorted({dil for _, _, dil in split_tiles})
    n_seg, n_tab = len(segs), 2 * (1 + len(dils))
    tabs = [(rest[2 * v], rest[2 * v + 1]) for v in range(1 + len(dils))]
    seg_refs, la_ref = rest[n_tab:n_tab + n_seg], rest[n_tab + n_seg]
    split_refs, (h_scr, hf_scr) = rest[n_tab + n_seg + 1:-2], rest[-2:]
    j = pl.program_id(1)
    tm, d = x_ref.shape

    @pl.when(j == 0)
    def _():
        h = _rms(x_ref[...], g_ref[...]) * (1.0 + sc_ref[...]) + sh_ref[...]
        hb = h.astype(BF16)
        h_scr[0] = hb
        alr = _dot(hb, walr_ref[...])
        p = _dot(alr.astype(BF16), w2_ref[...]) + ab_ref[...]
        la_ref[...] = _log_sigmoid(p) * (1.0 / GLA_TAU)
        for c in range((d // LANES) if dils else 0):
            cs = slice(c * LANES, (c + 1) * LANES)
            hf_scr[c] = h[:, cs]
            for v, dil in enumerate(dils, 1):
                per = tm // dil
                for r in range(dil):
                    h_scr[v, r * per:(r + 1) * per, cs] = hf_scr[c, pl.ds(r, per, stride=dil), :].astype(BF16)

    def tile(v, roped):
        z = _dot(h_scr[v], w_ref[...])
        return _rope_tile(z, tabs[v][0][...], tabs[v][1][...]) if roped else z

    for (lo, hi), ref in zip(segs, seg_refs):
        ropes = [t for t in rope_tiles if lo <= t < hi]
        in_seg = jnp.logical_and(j >= lo, j < hi)
        is_rope = functools.reduce(jnp.logical_or, [j == t for t in ropes], False)
        if ropes:
            @pl.when(jnp.logical_and(in_seg, is_rope))
            def _(ref=ref):
                ref[...] = tile(0, True).astype(ref.dtype)

        @pl.when(jnp.logical_and(in_seg, jnp.logical_not(is_rope)))
        def _(ref=ref):
            ref[...] = tile(0, False).astype(ref.dtype)

    for t, slot, dil in split_tiles:
        @pl.when(j == t)
        def _(t=t, slot=slot, dil=dil):
            z = tile(1 + dils.index(dil), t in rope_tiles)
            split_refs[slot][...] = z.reshape(dil, tm // dil, z.shape[1]).astype(split_refs[slot].dtype)


def _in_proj(x2, g, sc, sh, w_main, w_alr, w2p, a_b, pos, tm, rows_per_mod, segs, rope_tiles, split_tiles):
    t, d = x2.shape
    r = sc.shape[1]
    wq = a_b.shape[1]
    n_tiles = w_main.shape[1] // TN_IN
    tiles_per_mod = rows_per_mod // tm
    tab_tiles = pos.shape[0] // tm
    nmod = t // rows_per_mod
    dils = sorted({dil for _, _, dil in split_tiles})
    tabs = list(_rope_tables(pos))
    for dil in dils:
        tabs += [tb.reshape(tab_tiles, tm // dil, dil, LANES).transpose(0, 2, 1, 3).reshape(-1, LANES)
                 for tb in tabs[:2]]
    mod_spec = pl.BlockSpec((None, r, d), lambda i, j: (i // tiles_per_mod, 0, 0))
    tab_spec = pl.BlockSpec((tm, LANES), lambda i, j: (i % tab_tiles, 0))
    const = lambda shape: pl.BlockSpec(shape, lambda i, j: (0, 0))
    out_specs, out_shape = [], []
    for lo, hi, dt in segs:
        out_specs.append(pl.BlockSpec((tm, TN_IN), lambda i, j, lo=lo, hi=hi: (i, jnp.clip(j - lo, 0, hi - lo - 1))))
        out_shape.append(jax.ShapeDtypeStruct((t, (hi - lo) * TN_IN), dt))
    out_specs.append(pl.BlockSpec((tm, wq), lambda i, j: (i, 0)))
    out_shape.append(jax.ShapeDtypeStruct((t, wq), F32))
    for _, _, dil in split_tiles:
        out_specs.append(pl.BlockSpec((None, dil, tm // dil, TN_IN),
                                      lambda i, j: (i // tiles_per_mod, 0, i % tiles_per_mod, 0)))
        out_shape.append(jax.ShapeDtypeStruct((nmod, dil, rows_per_mod // dil, TN_IN), BF16))
    return pl.pallas_call(
        functools.partial(_inproj_kernel, segs=tuple((lo, hi) for lo, hi, _ in segs),
                          rope_tiles=tuple(rope_tiles), split_tiles=tuple(split_tiles)),
        grid=(t // tm, n_tiles),
        in_specs=[pl.BlockSpec((tm, d), lambda i, j: (i, 0)), const((1, d)), mod_spec, mod_spec,
                  pl.BlockSpec((d, TN_IN), lambda i, j: (0, j)), const((d, LANES)), const((LANES, wq)),
                  const((1, wq))] + [tab_spec] * len(tabs),
        out_specs=out_specs,
        out_shape=out_shape,
        scratch_shapes=[pltpu.VMEM((1 + len(dils), tm, d), BF16),
                        pltpu.VMEM((d // LANES, tm if dils else SUBLANES, LANES), F32)],
        compiler_params=_cparams("arbitrary", "arbitrary"),
        name="in_proj",
    )(x2, g, sc, sh, w_main, w_alr, w2p, a_b, *tabs)


GLA_EXP_CLAMP = 80.0


def _gla_kernel(q_ref, k_ref, v_ref, ra_ref, la_ref, s0_ref, gn_ref, o_ref, sfin_ref, st_scr, *, rows, dk, dv):
    c = pl.program_id(1)
    nc = pl.num_programs(1)
    ch = GLA_CHUNK

    @pl.when(c == 0)
    def _():
        for h in range(GLA_HEADS):
            st_scr[h] = s0_ref[h].T

    ti = lax.broadcasted_iota(jnp.int32, (ch, ch), 0)
    si = lax.broadcasted_iota(jnp.int32, (ch, ch), 1)
    causal = si <= ti
    tri = causal.astype(BF16)
    scale = dk ** -0.5
    wk = GLA_HEADS * dk
    heads = range(GLA_HEADS)
    ks = [slice(h * dk, (h + 1) * dk) for h in heads]
    vs = [slice(h * dv, (h + 1) * dv) for h in heads]

    def chunk(ci, carry):
        rows = pl.ds(pl.multiple_of(ci * ch, ch), ch)
        la = la_ref[rows, :]
        hi = la.astype(BF16)
        r1 = la - hi.astype(F32)
        mid = r1.astype(BF16)
        lo = (r1 - mid.astype(F32)).astype(BF16)
        b3 = _dot(tri, jnp.concatenate([hi, mid, lo], axis=1))
        b = b3[:, :wk] + b3[:, wk:2 * wk] + b3[:, 2 * wk:]
        bl = b[ch - 1:ch, :]
        q = q_ref[rows, :] * scale
        k = k_ref[rows, :]
        qe = (q * jnp.exp(b)).astype(BF16)
        ke = (k * jnp.exp(jnp.minimum(-b, GLA_EXP_CLAMP))).astype(BF16)
        kh = (k * jnp.exp(bl - b)).astype(BF16)
        dec = jnp.exp(bl)
        v = v_ref[rows, :].astype(BF16)
        att = [jnp.where(causal, _dot_nt(qe[:, ks[h]], ke[:, ks[h]]), 0.0).astype(BF16) for h in heads]
        st = [st_scr[h] for h in heads]
        o = [_dot_nt(qe[:, ks[h]], st[h].astype(BF16)) + _dot(att[h], v[:, vs[h]]) for h in heads]
        for h in heads:
            st_scr[h] = st[h] * dec[:, ks[h]] + _dot_tn(v[:, vs[h]], kh[:, ks[h]])
        for h in heads:
            on = _rms(o[h], gn_ref[:, vs[h]])
            o_ref[rows, vs[h]] = (on * _silu(ra_ref[rows, vs[h]])).astype(o_ref.dtype)
        return carry

    lax.fori_loop(0, rows // ch, chunk, 0)

    @pl.when(c == nc - 1)
    def _():
        for h in range(GLA_HEADS):
            sfin_ref[h] = st_scr[h].T


def _gla_prompt(z, zv, la, s0, gn_g, n, l, rows):
    t = n * l
    dk = la.shape[1] // GLA_HEADS
    dv = gn_g.shape[1] // GLA_HEADS
    wk, wv = GLA_HEADS * dk, GLA_HEADS * dv
    cpb = l // rows
    row = lambda b, c: b * cpb + c
    return pl.pallas_call(
        functools.partial(_gla_kernel, rows=rows, dk=dk, dv=dv),
        grid=(n, cpb),
        in_specs=[pl.BlockSpec((rows, wk), lambda b, c: (row(b, c), 0)),
                  pl.BlockSpec((rows, wk), lambda b, c: (row(b, c), 1)),
                  pl.BlockSpec((rows, wv), lambda b, c: (row(b, c), 0)),
                  pl.BlockSpec((rows, wv), lambda b, c: (row(b, c), 1)),
                  pl.BlockSpec((rows, wk), lambda b, c: (row(b, c), 0)),
                  pl.BlockSpec((None, GLA_HEADS, dk, dv), lambda b, c: (b, 0, 0, 0)),
                  pl.BlockSpec((1, wv), lambda b, c: (0, 0))],
        out_specs=[pl.BlockSpec((rows, wv), lambda b, c: (row(b, c), 0)),
                   pl.BlockSpec((None, GLA_HEADS, dk, dv), lambda b, c: (b, 0, 0, 0))],
        out_shape=[jax.ShapeDtypeStruct((t, wv), BF16), jax.ShapeDtypeStruct((n, GLA_HEADS, dk, dv), F32)],
        scratch_shapes=[pltpu.VMEM((GLA_HEADS, dv, dk), F32)],
        compiler_params=_cparams("arbitrary", "arbitrary"),
        name="gla_prompt",
    )(z, z, zv, z, la, s0, gn_g)


GLA_STEP_SEQS = 4


def _gla_step_kernel(q_ref, k_ref, la_ref, v_ref, ra_ref, s_ref, gn_ref, o_ref, so_ref, *, dk):
    for b in range(s_ref.shape[0]):
        sn = jnp.exp(la_ref[b]) * s_ref[b] + k_ref[b] * v_ref[b]
        so_ref[b] = sn
        o = jnp.sum((q_ref[b] * dk ** -0.5) * sn, axis=1, keepdims=True)
        on = _rms(o, gn_ref[...])
        o_ref[b] = (on * _silu(ra_ref[b])).astype(o_ref.dtype)


def _gla_step(q, k, la, v, ra, s, gn_g):
    nb, nh, dk, dv = s.shape
    bb = _pick_tile(nb, GLA_STEP_SEQS)
    col = pl.BlockSpec((bb, nh, dk, 1), lambda b: (b, 0, 0, 0))
    rowv = pl.BlockSpec((bb, nh, 1, dv), lambda b: (b, 0, 0, 0))
    st = pl.BlockSpec((bb, nh, dk, dv), lambda b: (b, 0, 0, 0))
    o, so = pl.pallas_call(
        functools.partial(_gla_step_kernel, dk=dk),
        grid=(nb // bb,),
        in_specs=[col, col, col, rowv, rowv, st, pl.BlockSpec((nh, 1, dv), lambda b: (0, 0, 0))],
        out_specs=[rowv, st],
        out_shape=[jax.ShapeDtypeStruct((nb, nh, 1, dv), BF16), jax.ShapeDtypeStruct(s.shape, F32)],
        compiler_params=_cparams("arbitrary"),
        name="gla_step",
    )(q.reshape(nb, nh, dk, 1), k.reshape(nb, nh, dk, 1), la.reshape(nb, nh, dk, 1),
      v.reshape(nb, nh, 1, dv), ra.reshape(nb, nh, 1, dv), s, gn_g.reshape(nh, 1, dv))
    return o.reshape(nb, nh * dv), so


def _band_kernel(q_ref, k_ref, kp_ref, v_ref, vp_ref, o_ref, lse_ref, *, dil, qb):
    i = pl.program_id(1)
    blk = ATT_BLOCK
    qi = lax.broadcasted_iota(jnp.int32, (blk, 2 * blk), 0)
    ki = lax.broadcasted_iota(jnp.int32, (blk, 2 * blk), 1)
    dist = blk + qi - ki
    band = jnp.logical_and(dist >= 0, dist <= ATT_BLOCK)
    band_first = jnp.logical_and(band, ki >= jnp.where(i > 0, 0, blk))
    scale = ATT_HD ** -0.5
    for r in range(dil):
        for sb in range(qb // blk):
            rs = slice(sb * blk, (sb + 1) * blk)
            q = q_ref[r, rs, :].astype(BF16)
            if sb == 0:
                kprev, vprev, mask = kp_ref[r], vp_ref[r], band_first
            else:
                ps = slice((sb - 1) * blk, sb * blk)
                kprev, vprev, mask = k_ref[r, ps, :], v_ref[r, ps, :], band
            kk = jnp.concatenate([kprev, k_ref[r, rs, :]], axis=0).astype(BF16)
            vv = jnp.concatenate([vprev, v_ref[r, rs, :]], axis=0).astype(BF16)
            s = jnp.where(mask, _dot_nt(q, kk) * scale, NEG_BIG)
            m = jnp.max(s, axis=-1, keepdims=True)
            p = jnp.exp(s - m)
            lsum = jnp.sum(p, axis=-1, keepdims=True)
            o = _dot(p.astype(BF16), vv) / lsum
            lse = jnp.broadcast_to(m + jnp.log(lsum), (blk, ATT_HD))
            rows = rs if dil == 1 else pl.ds(sb * blk * dil + r, blk, stride=dil)
            o_ref[rows, :] = o
            lse_ref[rows, :] = lse


def _band_attention(qa, ka, va, cq, ck, cv, n, l, dil):
    lq = l // dil
    hd = ATT_HD
    tokens = min(ATT_TOKENS, l)
    qb = tokens // dil
    nqb = lq // qb
    sub = qb // ATT_BLOCK

    def main(c0):
        return pl.BlockSpec((None, dil, qb, hd), lambda b, i, h: (b, 0, i, c0 + h))

    def prev(c0):
        return pl.BlockSpec((None, dil, ATT_BLOCK, hd), lambda b, i, h: (b, 0, jnp.maximum(i * sub - 1, 0), c0 + h))

    out = pl.BlockSpec((tokens, hd), lambda b, i, h: (b * nqb + i, h))
    shp = jax.ShapeDtypeStruct((n * l, ATT_HEADS * hd), F32)
    return pl.pallas_call(
        functools.partial(_band_kernel, dil=dil, qb=qb),
        grid=(n, nqb, ATT_HEADS),
        in_specs=[main(cq), main(ck), prev(ck), main(cv), prev(cv)],
        out_specs=[out, out],
        out_shape=[shp, shp],
        compiler_params=_cparams("arbitrary", "arbitrary", "arbitrary"),
        name=f"band_attention_d{dil}",
    )(qa, ka, ka, va, va)


def _att_step_kernel(q_ref, kn_ref, vn_ref, c0_ref, c1_ref, c2_ref, o_ref):
    scale = ATT_HD ** -0.5
    outs, lses = [], []
    for g, c_ref in enumerate((c0_ref, c1_ref, c2_ref)):
        q, kn, vn = q_ref[g], kn_ref[g], vn_ref[g]
        s = jnp.sum(c_ref[:, 0] * q[None], axis=-1, keepdims=True) * scale
        sn = jnp.sum(kn * q, axis=-1, keepdims=True) * scale
        m = jnp.maximum(jnp.max(s, axis=0), sn)
        p = jnp.exp(s - m[None])
        pn = jnp.exp(sn - m)
        lsum = jnp.sum(p, axis=0) + pn
        outs.append((jnp.sum(p * c_ref[:, 1], axis=0) + pn * vn) / lsum)
        lses.append(m + jnp.log(lsum))
    mx = functools.reduce(jnp.maximum, lses)
    es = [jnp.exp(x - mx) for x in lses]
    inv = 1.0 / functools.reduce(lambda a, b: a + b, es)
    o_ref[...] = functools.reduce(lambda a, b: a + b, [e * inv * o for e, o in zip(es, outs)]).astype(o_ref.dtype)


def _att_step(qs, ks, vs, caches):
    nb = qs.shape[0]
    views, cspecs = [], []
    for (window, dil), cch in zip(ATT_GROUPS, caches):
        assert cch.shape[1] == window, "sample step expects full caches"
        views.append(cch.reshape((nb, window // dil, dil) + cch.shape[2:]))
        cspecs.append(pl.BlockSpec((None, ATT_BLOCK, None) + cch.shape[2:], lambda b: (b, 0, 0, 0, 0, 0)))
    row = pl.BlockSpec((None,) + qs.shape[1:], lambda b: (b, 0, 0, 0))
    o = pl.pallas_call(
        _att_step_kernel,
        grid=(nb,),
        in_specs=[row] * 3 + cspecs,
        out_specs=pl.BlockSpec((None, ATT_HEADS, ATT_HD), lambda b: (b, 0, 0)),
        out_shape=jax.ShapeDtypeStruct((nb, ATT_HEADS, ATT_HD), BF16),
        compiler_params=_cparams("arbitrary"),
        name="att_step",
    )(qs, ks, vs, *views)
    return o.reshape(nb, ATT_HEADS * ATT_HD)


MERGE_PART = 256


def _merge_kernel(*refs, n_att, n_alias, n_tiles):
    oa_ref = refs[0]
    att = refs[1:1 + n_att]
    (gza_ref, gzb_ref, x_ref, g1_ref, sc2_ref, sh2_ref, n2_ref, wa_ref, wb_ref, wo_ref,
     wq_ref) = refs[1 + n_att:12 + n_att]
    x1_ref, h2_ref, pq_ref = refs[12 + n_att + n_alias:]
    tm = x_ref.shape[0]
    pr = min(MERGE_PART, tm)
    parts = [slice(k * pr, (k + 1) * pr) for k in range(tm // pr)]
    mod = lambda ref, part: ref[...] if ref.shape[0] == 1 else ref[part, :]
    add = lambda a, b: a + b

    def attention_out(part):
        if n_att == 1:
            return att[0][part, :]
        ng = n_att // 2
        ls = [r[part, :] for r in att[ng:]]
        mx = functools.reduce(jnp.maximum, ls)
        es = [jnp.exp(x - mx) for x in ls]
        inv = 1.0 / functools.reduce(add, es)
        return functools.reduce(add, [e * inv * r[part, :] for e, r in zip(es, att[:ng])]).astype(BF16)

    @pl.when(pl.program_id(0) >= n_tiles)
    def _():
        h2_ref[...] = jnp.zeros_like(h2_ref)
        pq_ref[...] = jnp.zeros_like(pq_ref)

    @pl.when(pl.program_id(0) < n_tiles)
    def _():
        _merge_rows(parts, attention_out, mod, oa_ref, gza_ref, gzb_ref, x_ref, g1_ref, sc2_ref, sh2_ref, n2_ref,
                    wa_ref, wb_ref, wo_ref, wq_ref, x1_ref, h2_ref, pq_ref)
        if h2_ref.shape[0] > tm:
            h2_ref[tm:] = jnp.zeros((h2_ref.shape[0] - tm, h2_ref.shape[1]), h2_ref.dtype)
            pq_ref[tm:] = jnp.zeros((pq_ref.shape[0] - tm, pq_ref.shape[1]), pq_ref.dtype)


def _merge_rows(parts, attention_out, mod, oa_ref, gza_ref, gzb_ref, x_ref, g1_ref, sc2_ref, sh2_ref, n2_ref,
                wa_ref, wb_ref, wo_ref, wq_ref, x1_ref, h2_ref, pq_ref):
    ob = [attention_out(p) for p in parts]
    ma = [_dot(oa_ref[p, :], wa_ref[...]) for p in parts]
    mb = [_dot(o, wb_ref[...]) for o in ob]
    merged = [(_sigmoid(gza_ref[p, :]) * a + _sigmoid(gzb_ref[p, :]) * b).astype(BF16)
              for p, a, b in zip(parts, ma, mb)]
    y = [_dot(m, wo_ref[...]) for m in merged]
    h2 = []
    for p, yy in zip(parts, y):
        x1 = x_ref[p, :] + mod(g1_ref, p) * yy
        x1_ref[p, :] = x1
        h2.append((_rms(x1, n2_ref[...]) * (1.0 + mod(sc2_ref, p)) + mod(sh2_ref, p)).astype(BF16))
    for p, h in zip(parts, h2):
        h2_ref[p, :] = h
        pq_ref[p, :] = _dot(h, wq_ref[...])


def _merge(oa, att, z, x2, g1, sc2, sh2, n2g, wa, wb, wo, wq, tm, rows_per_mod, gza_blk, gzb_blk,
           peer_rows, peer_tile, peer_blk0, alias=(), zero_tiles=0):
    t, d = x2.shape
    r = g1.shape[1]
    wqn = wq.shape[1]
    tiles_per_mod = rows_per_mod // tm
    n_tiles = t // tm
    row = lambda i: jnp.minimum(i, n_tiles - 1)
    tok = lambda w: pl.BlockSpec((tm, w), lambda i: (row(i), 0))
    mod = pl.BlockSpec((None, r, d), lambda i: (row(i) // tiles_per_mod, 0, 0))
    const = lambda a: pl.BlockSpec(a.shape, lambda i: (0, 0), pipeline_mode=pl.Buffered(1))
    peer = lambda w: pl.BlockSpec((peer_tile, w), lambda i: (peer_blk0 + i, 0))
    n_in = 12 + len(att)
    return pl.pallas_call(
        functools.partial(_merge_kernel, n_att=len(att), n_alias=len(alias), n_tiles=n_tiles),
        grid=(n_tiles + zero_tiles,),
        in_specs=[tok(oa.shape[1])] + [tok(a.shape[1]) for a in att]
                 + [pl.BlockSpec((tm, d), lambda i: (row(i), gza_blk)),
                    pl.BlockSpec((tm, d), lambda i: (row(i), gzb_blk)),
                    tok(d), mod, mod, mod, const(n2g), const(wa), const(wb), const(wo), const(wq)]
                 + [pl.BlockSpec(memory_space=pl.ANY)] * len(alias),
        out_specs=[tok(d), peer(d), peer(wqn)],
        out_shape=[jax.ShapeDtypeStruct((t, d), F32), jax.ShapeDtypeStruct((peer_rows, d), BF16),
                   jax.ShapeDtypeStruct((peer_rows, wqn), F32)],
        input_output_aliases={n_in + k: 1 + k for k in range(len(alias))},
        compiler_params=_cparams("arbitrary"),
        name="merge",
    )(oa, *att, z, z, x2, g1, sc2, sh2, n2g, wa, wb, wo, wq, *alias)


def _strict_max_below(rows, prev):
    m = None
    for x in rows:
        y = jnp.where(x < prev, x, NEG_BIG) if prev is not None else x
        m = y if m is None else jnp.maximum(m, y)
    return m


def _oddeven_merge(lo, hi, r):
    step = r * 2
    if step < hi - lo:
        yield from _oddeven_merge(lo, hi, step)
        yield from _oddeven_merge(lo + r, hi, step)
        yield from [(i, i + r) for i in range(lo + r, hi - r, step)]
    else:
        yield (lo, lo + r)


def _oddeven_merge_sort(lo, hi):
    if hi - lo >= 1:
        mid = lo + (hi - lo) // 2
        yield from _oddeven_merge_sort(lo, mid)
        yield from _oddeven_merge_sort(mid + 1, hi)
        yield from _oddeven_merge(lo, hi, 1)


def _exchange(p, i, j):
    p[i], p[j] = jnp.maximum(p[i], p[j]), jnp.minimum(p[i], p[j])


def _topk_desc(s, k):
    assert s.shape[0] == k * SUBLANES and k & (k - 1) == 0
    p = [s[i * SUBLANES:(i + 1) * SUBLANES] for i in range(k)]
    for i, j in _oddeven_merge_sort(0, k - 1):
        _exchange(p, i, j)
    shift = SUBLANES // 2
    while shift:
        q = [pltpu.roll(v, shift, 0) for v in p]
        p = [jnp.maximum(p[i], q[k - 1 - i]) for i in range(k)]
        stride = k // 2
        while stride:
            for i in range(k):
                if not i & stride:
                    _exchange(p, i, i + stride)
            stride //= 2
        shift //= 2
    return [v[0:1] for v in p]


def _route_kernel(pq_ref, k1_ref, k2_ref, n1_ref, e1_ref, r2_ref, e2_ref, s1_scr, s2_scr):
    nk = PEER_NKEYS
    pq = pq_ref[...].astype(BF16)
    s1_scr[...] = _dot_nt(k1_ref[...], pq)
    s2_scr[...] = _dot_nt(k2_ref[...], pq)
    v1h, v2h = [], []
    for h in range(PEER_HEADS):
        v1h.append(_topk_desc(s1_scr[h * nk:(h + 1) * nk], PEER_TOPK))
        v2h.append(_topk_desc(s2_scr[h * nk:(h + 1) * nk], PEER_TOPK))
    v1 = [jnp.concatenate([v1h[h][k] for h in range(PEER_HEADS)], axis=0) for k in range(PEER_TOPK)]
    v2 = [jnp.concatenate([v2h[h][k] for h in range(PEER_HEADS)], axis=0) for k in range(PEER_TOPK)]
    pairs = [(i, j) for i in range(PEER_TOPK) for j in range(PEER_TOPK) if (i + 1) * (j + 1) <= PEER_TOPK]
    cands = [v1[i] + v2[j] for i, j in pairs]
    tau = None
    for _ in range(PEER_TOPK):
        tau = _strict_max_below(cands, tau)
    cmax = v1[0] + v2[0]
    zsum = None
    counts = [None] * PEER_TOPK
    for (i, _), cnd in zip(pairs, cands):
        sel = cnd >= tau
        e = jnp.where(sel, jnp.exp(cnd - cmax), 0.0)
        zsum = e if zsum is None else zsum + e
        one = jnp.where(sel, 1.0, 0.0)
        counts[i] = one if counts[i] is None else counts[i] + one
    zinv = 1.0 / zsum
    for h in range(PEER_HEADS):
        a = s1_scr[h * nk:(h + 1) * nk]
        b = s2_scr[h * nk:(h + 1) * nk]
        n1 = jnp.zeros_like(a)
        r2 = jnp.full_like(b, float(PEER_TOPK))
        for k in range(PEER_TOPK):
            n1 = jnp.where(a == v1h[h][k], counts[k][h:h + 1, :], n1)
            r2 = jnp.where(b == v2h[h][k], float(k), r2)
        n1_ref[h] = n1
        e1_ref[h] = jnp.exp(a - v1h[h][0]) * zinv[h:h + 1, :]
        r2_ref[h] = r2.astype(r2_ref.dtype)
        e2_ref[h] = jnp.exp(b - v2h[h][0]).astype(e2_ref.dtype)


def _peer_route(pq, k1big, k2big, tt):
    t, w = pq.shape
    nh, nk = PEER_HEADS, PEER_NKEYS
    sspec = pl.BlockSpec((nh, nk, tt), lambda i: (0, 0, i))
    shp = lambda dt: jax.ShapeDtypeStruct((nh, nk, t), dt)
    return pl.pallas_call(
        _route_kernel,
        grid=(t // tt,),
        in_specs=[pl.BlockSpec((tt, w), lambda i: (i, 0)),
                  pl.BlockSpec(k1big.shape, lambda i: (0, 0)), pl.BlockSpec(k2big.shape, lambda i: (0, 0))],
        out_specs=[sspec] * 4,
        out_shape=[shp(F32), shp(F32), shp(BF16), shp(BF16)],
        scratch_shapes=[pltpu.VMEM((nh * nk, tt), F32)] * 2,
        compiler_params=_cparams("arbitrary"),
        name="peer_route",
    )(pq, k1big, k2big)


def _gelu_tanh(x):
    return 0.5 * x * (1.0 + jnp.tanh(GELU_C * (x + 0.044715 * (x * x * x))))


BF16_ROWS = 16
GATE_LANES = 256
PEER_PART = 256
PEER_TILE = 2048


def _gelu_tanh_bf16(x):
    hx = 0.5 * x
    return hx + hx * jnp.tanh(x * (GELU_C + (GELU_C * 0.044715) * (x * x)))


def _bf16_rows(row):
    return jnp.broadcast_to(row, (BF16_ROWS, row.shape[1])).astype(BF16)


CACHE_CHUNK = 512


def _cache_shift_jobs(caches):
    nb = caches[0].shape[0]
    chunks, base = [], 0
    for c, cache in enumerate(caches):
        w = cache.shape[1]
        for r0 in range(0, w, CACHE_CHUNK):
            n = min(CACHE_CHUNK, w - r0)
            chunks.append((c, r0, n, r0 + n == w, base))
            base += nb
    return chunks, base


def _cache_shift_step(step, caches, new_rows, outs, buf, sems):
    nb = caches[0].shape[0]
    chunks, n_jobs = _cache_shift_jobs(caches)

    def for_job(job, act):
        for c, r0, n, last, base in chunks:
            @pl.when(jnp.logical_and(job >= base, job < base + nb))
            def _(c=c, r0=r0, n=n, last=last, base=base):
                b, slot = job - base, job % 2
                n_in = n - 1 if last else n
                rd = lambda: pltpu.make_async_copy(caches[c].at[b, pl.ds(r0 + 1, n_in)],
                                                   buf.at[slot, pl.ds(0, n_in)], sems.at[slot, 0])
                wr = lambda: pltpu.make_async_copy(buf.at[slot, pl.ds(0, n)], outs[c].at[b, pl.ds(r0, n)],
                                                   sems.at[slot, 1])
                act(rd, wr, lambda: new_rows[c][b] if last else None, slot, n_in)

    def finish_read_start_write(rd, wr, new_row, slot, n_in):
        rd().wait()
        row = new_row()
        if row is not None:
            buf[slot, pl.ds(n_in, 1)] = row
        wr().start()

    @pl.when(jnp.logical_and(step >= 1, step <= n_jobs))
    def _():
        for_job(step - 1, finish_read_start_write)

    @pl.when(jnp.logical_and(step >= 2, step <= n_jobs + 1))
    def _():
        for_job(step - 2, lambda rd, wr, *_: wr().wait())

    @pl.when(step < n_jobs)
    def _():
        for_job(step, lambda rd, wr, *_: rd().start())


def _peer_dense_kernel(h2_ref, u_ref, vt_ref, n1_ref, e1_ref, r2_ref, e2_ref, h2_next_ref, u_next_ref, *rest,
                       et, n_kv):
    caches, new_rows = rest[:n_kv], rest[n_kv:2 * n_kv]
    o_ref, kv_outs = rest[2 * n_kv], rest[2 * n_kv + 1:3 * n_kv + 1]
    acc_scr, wg_scr, st_scr, st0_scr = rest[3 * n_kv + 1:3 * n_kv + 5]
    e = pl.program_id(1)
    nk = PEER_NKEYS
    tt = h2_ref.shape[0]
    step = pl.program_id(0) * pl.num_programs(1) + e
    if n_kv:
        _cache_shift_step(step, caches, new_rows, kv_outs, rest[-2], rest[-1])

    @pl.when(step == 0)
    def _():
        st0_scr[...] = _dot_nt(u_ref[0:PEER_PART, :], h2_ref[...]).astype(BF16)

    @pl.when(e == 0)
    def _():
        acc_scr[...] = jnp.zeros_like(acc_scr)

    zero = jnp.zeros((), BF16)
    groups = nk // BF16_ROWS

    def scores(part):
        st_scr[part, :] = _dot_nt(u_ref[part, :], h2_ref[...]).astype(BF16)

    def gates(ii):
        rs = slice(ii * nk, (ii + 1) * nk)
        for lc in range(tt // GATE_LANES):
            ls = slice(lc * GATE_LANES, (lc + 1) * GATE_LANES)
            w = None
            for h in range(PEER_HEADS):
                n1 = _bf16_rows(n1_ref[h, ii:ii + 1, ls])
                e1 = _bf16_rows(e1_ref[h, ii:ii + 1, ls])
                r2 = r2_ref[h, :, ls].reshape(groups, BF16_ROWS, GATE_LANES)
                e2 = e2_ref[h, :, ls].reshape(groups, BF16_ROWS, GATE_LANES)
                c = jnp.where(r2 < n1[None], e2, zero) * e1[None]
                w = c if w is None else w + c
            st = st0_scr if rs.stop <= PEER_PART else st_scr
            wg_scr[rs, ls] = w.reshape(nk, GATE_LANES) * _gelu_tanh_bf16(st[rs, ls])

    def mix(part):
        return _dot(vt_ref[:, part], wg_scr[part, :])

    parts = [slice(p * PEER_PART, (p + 1) * PEER_PART) for p in range(et // PEER_PART)]
    out = None
    next_scores = None
    for p, part in enumerate(parts):
        if p + 1 < len(parts):
            scores(parts[p + 1])
        else:
            next_scores = _dot_nt(u_next_ref[...], h2_next_ref[...]).astype(BF16)
        for ii in range(part.start // nk, part.stop // nk):
            gates(ii)
        if p > 0:
            d = mix(parts[p - 1])
            out = d if out is None else out + d
    st0_scr[...] = next_scores
    d = mix(parts[-1])
    acc_scr[...] += d if out is None else out + d

    @pl.when(e == pl.num_programs(1) - 1)
    def _():
        o_ref[...] = acc_scr[...].T


def _peer_dense(h2, u_b, vt_b, n1, e1, r2, e2, tt, et, caches=(), new_rows=()):
    t, d = h2.shape
    ne = u_b.shape[0]
    nh, nk = PEER_HEADS, PEER_NKEYS
    n_kv = len(caches)
    grid = (t // tt, ne // et)
    if n_kv:
        assert grid[0] * grid[1] >= _cache_shift_jobs(caches)[1] + 2, "one cache-shift job per grid step"
    rowside = pl.BlockSpec((nh, et // nk, tt), lambda i, e: (0, e, i))
    colside = pl.BlockSpec((nh, nk, tt), lambda i, e: (0, 0, i))
    anywhere = pl.BlockSpec(memory_space=pl.ANY)
    whole = lambda a: pl.BlockSpec(a.shape, lambda i, e: (0,) * a.ndim)
    outs = pl.pallas_call(
        functools.partial(_peer_dense_kernel, et=et, n_kv=n_kv),
        grid=grid,
        in_specs=[pl.BlockSpec((tt, d), lambda i, e: (i, 0)),
                  pl.BlockSpec((et, d), lambda i, e: (e, 0)),
                  pl.BlockSpec((d, et), lambda i, e: (0, e)),
                  rowside, rowside, colside, colside,
                  pl.BlockSpec((tt, d), lambda i, e: (jnp.minimum(i + (e + 1) // grid[1], grid[0] - 1), 0)),
                  pl.BlockSpec((PEER_PART, d), lambda i, e: (((e + 1) % grid[1]) * (et // PEER_PART), 0))]
                 + [anywhere] * n_kv + [whole(a) for a in new_rows],
        out_specs=[pl.BlockSpec((tt, d), lambda i, e: (i, 0))] + [anywhere] * n_kv,
        out_shape=[jax.ShapeDtypeStruct((t, d), F32)] + [jax.ShapeDtypeStruct(c.shape, c.dtype) for c in caches],
        scratch_shapes=[pltpu.VMEM((d, tt), F32), pltpu.VMEM((et, tt), BF16), pltpu.VMEM((et, tt), BF16),
                        pltpu.VMEM((PEER_PART, tt), BF16)]
                       + ([pltpu.VMEM((2, CACHE_CHUNK) + caches[0].shape[2:], caches[0].dtype),
                           pltpu.SemaphoreType.DMA((2, 2))] if n_kv else []),
        compiler_params=_cparams("arbitrary", "arbitrary"),
        name="peer_dense",
    )(h2, u_b, vt_b, n1, e1, r2, e2, h2, u_b, *caches, *new_rows)
    return outs[0], outs[1:]


def _final_kernel(x1_ref, p_ref, g2_ref, fg_ref, y_ref):
    y_ref[...] = _rms(x1_ref[...] + g2_ref[...] * p_ref[...], fg_ref[...])


def _final(x1, p, g2, fg, tm, rows_per_mod, p_blk0):
    t, d = x1.shape
    r = g2.shape[1]
    tiles_per_mod = rows_per_mod // tm
    tok = pl.BlockSpec((tm, d), lambda i: (i, 0))
    return pl.pallas_call(
        _final_kernel,
        grid=(t // tm,),
        in_specs=[tok, pl.BlockSpec((tm, d), lambda i: (p_blk0 + i, 0)),
                  pl.BlockSpec((None, r, d), lambda i: (i // tiles_per_mod, 0, 0)),
                  pl.BlockSpec((1, d), lambda i: (0, 0))],
        out_specs=tok,
        out_shape=jax.ShapeDtypeStruct((t, d), F32),
        compiler_params=_cparams("arbitrary"),
        name="final_norm",
    )(x1, p, g2, fg)


def _rope_tables(pos):
    half = ROPE_DIM // 2
    inv_freq = ROPE_THETA ** (-jnp.arange(half, dtype=F32) / half)
    ang = pos.astype(F32)[:, None] * inv_freq[None, :]
    cos, sin = jnp.cos(ang), jnp.sin(ang)
    n = pos.shape[0]
    ones = jnp.ones((n, LANES - ROPE_DIM), F32)
    zeros = jnp.zeros((n, LANES - ROPE_DIM), F32)
    return (jnp.concatenate([cos, cos, ones], axis=1), jnp.concatenate([-sin, sin, zeros], axis=1))


def _block_diag_keys(keys, half):
    nk, dh = keys.shape
    eye = jnp.eye(PEER_HEADS, dtype=keys.dtype)
    blk = jnp.zeros((PEER_HEADS, nk, PEER_HEADS, 2, dh), keys.dtype)
    blk = blk.at[:, :, :, half, :].set(eye[:, None, :, None] * keys[None, :, None, :])
    return blk.reshape(PEER_HEADS * nk, PEER_HEADS * 2 * dh).astype(BF16)


def _pick_tile(n, pref):
    t = min(pref, n)
    while n % t:
        t //= 2
    return t


def kernel(x_prompt, x_sample, c_prompt, c_sample, state_gla, cache_kv_w128, cache_kv_w512, cache_kv_w2048,
           ada_w, ada_b, norm1_g, w_in, gla_a_w2, gla_a_b, gla_gn_g, w_branch_a, w_branch_b, w_out,
           norm2_g, peer_wq, peer_k1, peer_k2, peer_u, peer_v, final_g):
    depth = ada_w.shape[0]
    assert depth == 1, "single-layer trunk"
    n_p, l_p, d = x_prompt.shape
    n_s, l_s, _ = x_sample.shape
    assert l_s == 1, "sample group decodes one token per sequence"
    caches = (cache_kv_w128[0], cache_kv_w512[0], cache_kv_w2048[0])
    n_groups = len(ATT_GROUPS)

    w_qa = gla_a_w2.shape[2]
    w_va = gla_gn_g.shape[1]
    w_ob = ATT_HEADS * ATT_HD
    w_qb = n_groups * w_ob
    assert w_qa == TN_IN and w_va == 2 * TN_IN and d == 2 * TN_IN and w_ob == TN_IN
    widths = (w_qa, w_qa, w_va, w_va, GLA_RANK, w_qb, w_qb, w_qb, d, d)
    offs = [0]
    for w in widths:
        offs.append(offs[-1] + w)
    seg = lambda i: w_in[0][:, offs[i]:offs[i + 1]]
    att_col = lambda g, k: seg(5 + k)[:, g * w_ob:(g + 1) * w_ob]
    att_cols = [att_col(g, k) for g in range(n_groups) for k in range(3)]
    w_main = jnp.concatenate([seg(0), seg(1), seg(3), seg(8), seg(9), seg(2)] + att_cols, axis=1).astype(BF16)
    w_kv = jnp.concatenate([att_col(g, k) for g in range(n_groups) for k in (1, 2)], axis=1).astype(BF16)
    n_tiles = w_main.shape[1] // TN_IN
    ra_off, gza_blk, gzb_blk = 2 * w_qa, 2, 3
    w_alr = jnp.pad(seg(4), ((0, 0), (0, LANES - GLA_RANK))).astype(BF16)
    w2p = jnp.pad(gla_a_w2[0], ((0, LANES - GLA_RANK), (0, 0))).astype(BF16)
    a_b = gla_a_b[0].reshape(1, w_qa)
    g1n = norm1_g[0].reshape(1, d)
    gn_g = gla_gn_g[0].reshape(1, w_va)
    n2g = norm2_g[0].reshape(1, d)
    wa, wb = w_branch_a[0].astype(BF16), w_branch_b[0].astype(BF16)
    wo, wq = w_out[0].astype(BF16), peer_wq[0].astype(BF16)
    q_tile = lambda g: ATT_TILE0 + 3 * g
    rope_tiles = [q_tile(g) + k for g in range(n_groups) for k in (0, 1)]

    n_mod = n_p + n_s
    n_pad = -(-n_mod // SUBLANES) * SUBLANES
    c_all = jnp.concatenate([c_prompt, c_sample, jnp.zeros((n_pad - n_mod, d), F32)], axis=0)
    mod = _modulation(c_all, ada_w[0], ada_b[0])
    mods_p = [mod[:n_p, i * d:(i + 1) * d].reshape(n_p, 1, d) for i in range(6)]
    mods_s = [mod[n_p:n_mod, i * d:(i + 1) * d].reshape(1, n_s, d) for i in range(6)]

    t_p = n_p * l_p
    tt = _pick_tile(t_p, 512)
    t_all = t_p + (-(-n_s // tt)) * tt
    assert t_all - t_p == tt

    xp2 = x_prompt.reshape(t_p, d)
    tm_p = _pick_tile(l_p, 1024)
    pos_p = jnp.arange(l_p, dtype=jnp.int32)
    split = []
    for g, (window, dil) in enumerate(ATT_GROUPS):
        assert window // dil == ATT_BLOCK
        if dil > 1:
            split += [(q_tile(g) + k, len(split) + k, dil) for k in range(3)]
    first_split = min([t for t, _, _ in split], default=n_tiles)
    outs = _in_proj(xp2, g1n, mods_p[1], mods_p[0], w_main, w_alr, w2p, a_b, pos_p, tm_p, l_p,
                    ((0, VA_TILE0, F32), (VA_TILE0, first_split, BF16)), rope_tiles, split)
    z_p, zb_p, la_p, split_p = outs[0], outs[1], outs[2], outs[3:]
    s0_p = jnp.zeros((n_p,) + state_gla.shape[2:], F32)
    oa_p, sfin_p = _gla_prompt(z_p, zb_p, la_p, s0_p, gn_g, n_p, l_p, _pick_tile(l_p, 512))
    zb4 = zb_p.reshape(n_p, 1, l_p, zb_p.shape[1])
    att_o, att_l = [], []
    for g, (window, dil) in enumerate(ATT_GROUPS):
        if dil > 1:
            qa, ka, va = [split_p[slot] for t, slot, _ in split if q_tile(g) <= t < q_tile(g) + 3]
            o_g, lse_g = _band_attention(qa, ka, va, 0, 0, 0, n_p, l_p, dil)
        else:
            c0 = lambda k: (q_tile(g) + k - VA_TILE0) * (TN_IN // ATT_HD)
            o_g, lse_g = _band_attention(zb4, zb4, zb4, c0(0), c0(1), c0(2), n_p, l_p, dil)
        att_o.append(o_g)
        att_l.append(lse_g)
    keep_max = min(max(w for w, _ in ATT_GROUPS), l_p)
    x_tail = x_prompt[:, l_p - keep_max:].reshape(n_p * keep_max, d)
    kv_tiles = 2 * n_groups
    z_kv = _in_proj(x_tail, g1n, mods_p[1], mods_p[0], w_kv, w_alr, w2p, a_b, pos_p[l_p - keep_max:],
                    _pick_tile(keep_max, 1024), keep_max, ((0, kv_tiles, F32),), range(0, kv_tiles, 2), [])[0]
    z_kv = z_kv.reshape(n_p, keep_max, kv_tiles * TN_IN)
    kv_p = []
    for g, (window, dil) in enumerate(ATT_GROUPS):
        keep = min(window, l_p)
        rows = z_kv[:, keep_max - keep:, 2 * g * TN_IN:(2 * g + 2) * TN_IN]
        kv_p.append(rows.reshape(1, n_p, keep, 2, ATT_HEADS, ATT_HD))
    tm_m = _pick_tile(l_p, 512)
    x1_p, h2_all, pq_all = _merge(oa_p, att_o + att_l, z_p, xp2, mods_p[2], mods_p[4], mods_p[3], n2g,
                                  wa, wb, wo, wq, tm_m, l_p, gza_blk, gzb_blk, t_all, tm_m, 0,
                                  zero_tiles=(t_all - t_p) // tm_m)

    xs2 = x_sample.reshape(n_s, d)
    pos_s = jnp.full((n_s,), PAST_LEN, dtype=jnp.int32)
    z_s, la_s = _in_proj(xs2, g1n, mods_s[1], mods_s[0], w_main, w_alr, w2p, a_b, pos_s, n_s, n_s,
                         ((0, n_tiles, F32),), rope_tiles, [])
    col = lambda tile: z_s[:, tile * TN_IN:(tile + 1) * TN_IN]
    oa_s, state_s = _gla_step(col(0), col(1), la_s, z_s[:, VA_TILE0 * TN_IN:VA_TILE0 * TN_IN + w_va],
                              z_s[:, ra_off:ra_off + w_va], state_gla[0], gn_g)
    stack = lambda k: jnp.stack([col(q_tile(g) + k) for g in range(n_groups)], axis=1).reshape(
        n_s, n_groups, ATT_HEADS, ATT_HD)
    ob_s = _att_step(stack(0), stack(1), stack(2), caches)
    x1_s, h2_all, pq_all = _merge(oa_s, [ob_s], z_s, xs2, mods_s[2], mods_s[4], mods_s[3], n2g,
                                  wa, wb, wo, wq, n_s, n_s, gza_blk, gzb_blk, t_all, tt, t_p // tt,
                                  alias=(h2_all, pq_all))
    row_tiles = (2 * w_ob // LANES, LANES)
    new_rows = [jnp.concatenate([col(q_tile(g) + 1), col(q_tile(g) + 2)], axis=1).reshape((n_s, 1) + row_tiles)
                for g in range(n_groups)]

    k1big = _block_diag_keys(peer_k1[0], 0)
    k2big = _block_diag_keys(peer_k2[0], 1)
    n1, e1, r2, e2 = _peer_route(pq_all, k1big, k2big, _pick_tile(tt, 256))
    u_b = peer_u[0].astype(BF16)
    vt_b = peer_v[0].astype(BF16).T
    p_all, kv_s = _peer_dense(h2_all, u_b, vt_b, n1, e1, r2, e2, tt, PEER_TILE,
                              [c.reshape(c.shape[:2] + row_tiles) for c in caches], new_rows)
    kv_s = [a.reshape(c.shape)[None] for a, c in zip(kv_s, caches)]

    fg = final_g.reshape(1, d)
    y_p = _final(x1_p, p_all, mods_p[5], fg, tm_m, l_p, 0)
    y_s = _final(x1_s, p_all, mods_s[5], fg, n_s, n_s, t_p // n_s)

    return (y_p.reshape(n_p, l_p, d), y_s.reshape(n_s, l_s, d), sfin_p[None], state_s[None],
            kv_p[0], kv_s[0], kv_p[1], kv_s[1], kv_p[2], kv_s[2])
```

```python
import functools
import math

import jax
import jax.numpy as jnp
from jax import lax
from jax.experimental import pallas as pl
from jax.experimental.pallas import tpu as pltpu

F32 = jnp.float32
BF16 = jnp.bfloat16

PAST_LEN = 16384
GLA_HEADS = 4
GLA_RANK = 16
GLA_TAU = 16.0
GLA_CHUNK = 128
ATT_GROUPS = ((128, 1), (512, 4), (2048, 16))
ATT_HEADS = 4
ATT_HD = 128
ATT_BLOCK = 128
ATT_TOKENS = 2048
ROPE_DIM = ATT_HD // 4
ROPE_THETA = 500000.0
PEER_HEADS = 8
PEER_NKEYS = 128
PEER_DQ = 128
PEER_TOPK = 16
NORM_EPS = 1e-6
GELU_C = math.sqrt(2.0 / math.pi)

LANES = 128
SUBLANES = 8
VMEM_LIMIT = 56 * 1024 * 1024
NEG_BIG = -1e30


def _cparams(*sem):
    return pltpu.CompilerParams(dimension_semantics=sem, vmem_limit_bytes=VMEM_LIMIT)


def _sigmoid(x):
    return 1.0 / (1.0 + jnp.exp(-x))


def _silu(x):
    return x * _sigmoid(x)


def _log_sigmoid(x):
    return jnp.minimum(x, 0.0) - jnp.log(1.0 + jnp.exp(-jnp.abs(x)))


def _rms(x, g):
    return x * lax.rsqrt(jnp.mean(x * x, axis=-1, keepdims=True) + NORM_EPS) * g


def _dot(a, b):
    return jnp.dot(a, b, preferred_element_type=F32)


def _dot_nt(a, b):
    return lax.dot_general(a, b, (((1,), (1,)), ((), ())), preferred_element_type=F32)


def _dot_tn(a, b):
    return lax.dot_general(a, b, (((0,), (0,)), ((), ())), preferred_element_type=F32)


def _mod_kernel(c_ref, w_ref, b_ref, o_ref):
    s = _silu(c_ref[...]).astype(BF16)
    o_ref[...] = _dot(s, w_ref[...].astype(BF16)) + b_ref[...]


def _modulation(c, ada_w, ada_b):
    n, d = c.shape
    n6 = ada_w.shape[1]
    tn = n6 // 4
    return pl.pallas_call(
        _mod_kernel,
        grid=(n6 // tn,),
        in_specs=[pl.BlockSpec((n, d), lambda j: (0, 0)),
                  pl.BlockSpec((d, tn), lambda j: (0, j)),
                  pl.BlockSpec((1, tn), lambda j: (0, j))],
        out_specs=pl.BlockSpec((n, tn), lambda j: (0, j)),
        out_shape=jax.ShapeDtypeStruct((n, n6), F32),
        compiler_params=_cparams("arbitrary"),
        name="modulation",
    )(c, ada_w, ada_b.reshape(1, n6))


TN_IN = 512
VA_TILE0 = 8
ATT_TILE0 = 10


def _rope_tile(z, c, s):
    reps = z.shape[1] // LANES
    n = z.shape[1]
    half = ROPE_DIM // 2
    lane = lax.broadcasted_iota(jnp.int32, (1, LANES), 1)
    first = jnp.concatenate([lane < half] * reps, axis=1)
    partner = jnp.where(first, pltpu.roll(z, n - half, 1), pltpu.roll(z, half, 1))
    return z * jnp.concatenate([c] * reps, axis=1) + partner * jnp.concatenate([s] * reps, axis=1)


def _inproj_kernel(x_ref, g_ref, sc_ref, sh_ref, w_ref, walr_ref, w2_ref, ab_ref, *rest,
                   segs, rope_tiles, split_tiles, decay):
    dils = sorted({dil for _, _, dil in split_tiles})
    n_seg, n_tab = len(segs), 2 * (1 + len(dils))
    tabs = [(rest[2 * v], rest[2 * v + 1]) for v in range(1 + len(dils))]
    seg_refs = rest[n_tab:n_tab + n_seg]
    la_ref = rest[n_tab + n_seg] if decay else None
    split_refs, (h_scr, hf_scr) = rest[n_tab + n_seg + int(decay):-2], rest[-2:]
    j = pl.program_id(1)
    tm, d = x_ref.shape

    @pl.when(j == 0)
    def _():
        h = _rms(x_ref[...], g_ref[...]) * (1.0 + sc_ref[...]) + sh_ref[...]
        hb = h.astype(BF16)
        h_scr[0] = hb
        if decay:
            alr = _dot(hb, walr_ref[...])
            p = _dot(alr.astype(BF16), w2_ref[...]) + ab_ref[...]
            la_ref[...] = _log_sigmoid(p) * (1.0 / GLA_TAU)
        for c in range((d // LANES) if dils else 0):
            cs = slice(c * LANES, (c + 1) * LANES)
            hf_scr[c] = h[:, cs]
            for v, dil in enumerate(dils, 1):
                per = tm // dil
                for r in range(dil):
                    h_scr[v, r * per:(r + 1) * per, cs] = hf_scr[c, pl.ds(r, per, stride=dil), :].astype(BF16)

    def tile(v, roped):
        z = _dot(h_scr[v], w_ref[...])
        return _rope_tile(z, tabs[v][0][...], tabs[v][1][...]) if roped else z

    for (lo, hi), ref in zip(segs, seg_refs):
        ropes = [t for t in rope_tiles if lo <= t < hi]
        in_seg = jnp.logical_and(j >= lo, j < hi)
        is_rope = functools.reduce(jnp.logical_or, [j == t for t in ropes], False)
        if ropes:
            @pl.when(jnp.logical_and(in_seg, is_rope))
            def _(ref=ref):
                ref[...] = tile(0, True).astype(ref.dtype)

        @pl.when(jnp.logical_and(in_seg, jnp.logical_not(is_rope)))
        def _(ref=ref):
            ref[...] = tile(0, False).astype(ref.dtype)

    for t, slot, dil in split_tiles:
        @pl.when(j == t)
        def _(t=t, slot=slot, dil=dil):
            z = tile(1 + dils.index(dil), t in rope_tiles)
            split_refs[slot][...] = z.reshape(dil, tm // dil, z.shape[1]).astype(split_refs[slot].dtype)


def _in_proj(x2, g, sc, sh, w_main, w_alr, w2p, a_b, pos, tm, rows_per_mod, segs, rope_tiles, split_tiles,
             decay=True):
    t, d = x2.shape
    r = sc.shape[1]
    wq = a_b.shape[1]
    n_tiles = w_main.shape[1] // TN_IN
    tiles_per_mod = rows_per_mod // tm
    tab_tiles = pos.shape[0] // tm
    nmod = t // rows_per_mod
    dils = sorted({dil for _, _, dil in split_tiles})
    tabs = list(_rope_tables(pos))
    for dil in dils:
        tabs += [tb.reshape(tab_tiles, tm // dil, dil, LANES).transpose(0, 2, 1, 3).reshape(-1, LANES)
                 for tb in tabs[:2]]
    mod_spec = pl.BlockSpec((None, r, d), lambda i, j: (i // tiles_per_mod, 0, 0))
    tab_spec = pl.BlockSpec((tm, LANES), lambda i, j: (i % tab_tiles, 0))
    const = lambda shape: pl.BlockSpec(shape, lambda i, j: (0, 0))
    out_specs, out_shape = [], []
    for lo, hi, dt in segs:
        out_specs.append(pl.BlockSpec((tm, TN_IN), lambda i, j, lo=lo, hi=hi: (i, jnp.clip(j - lo, 0, hi - lo - 1))))
        out_shape.append(jax.ShapeDtypeStruct((t, (hi - lo) * TN_IN), dt))
    if decay:
        out_specs.append(pl.BlockSpec((tm, wq), lambda i, j: (i, 0)))
        out_shape.append(jax.ShapeDtypeStruct((t, wq), F32))
    for _, _, dil in split_tiles:
        out_specs.append(pl.BlockSpec((None, dil, tm // dil, TN_IN),
                                      lambda i, j: (i // tiles_per_mod, 0, i % tiles_per_mod, 0)))
        out_shape.append(jax.ShapeDtypeStruct((nmod, dil, rows_per_mod // dil, TN_IN), BF16))
    return pl.pallas_call(
        functools.partial(_inproj_kernel, segs=tuple((lo, hi) for lo, hi, _ in segs),
                          rope_tiles=tuple(rope_tiles), split_tiles=tuple(split_tiles), decay=decay),
        grid=(t // tm, n_tiles),
        in_specs=[pl.BlockSpec((tm, d), lambda i, j: (i, 0)), const((1, d)), mod_spec, mod_spec,
                  pl.BlockSpec((d, TN_IN), lambda i, j: (0, j)), const((d, LANES)), const((LANES, wq)),
                  const((1, wq))] + [tab_spec] * len(tabs),
        out_specs=out_specs,
        out_shape=out_shape,
        scratch_shapes=[pltpu.VMEM((1 + len(dils), tm, d), BF16),
                        pltpu.VMEM((d // LANES, tm if dils else SUBLANES, LANES), F32)],
        compiler_params=_cparams("arbitrary", "arbitrary"),
        name="in_proj",
    )(x2, g, sc, sh, w_main, w_alr, w2p, a_b, *tabs)


GLA_EXP_CLAMP = 80.0


def _gla_kernel(q_ref, k_ref, v_ref, ra_ref, la_ref, s0_ref, gn_ref, o_ref, sfin_ref, st_scr, *, rows, dk, dv):
    c = pl.program_id(1)
    nc = pl.num_programs(1)
    ch = GLA_CHUNK

    @pl.when(c == 0)
    def _():
        for h in range(GLA_HEADS):
            st_scr[h] = s0_ref[h].T

    ti = lax.broadcasted_iota(jnp.int32, (ch, ch), 0)
    si = lax.broadcasted_iota(jnp.int32, (ch, ch), 1)
    causal = si <= ti
    tri = causal.astype(BF16)
    scale = dk ** -0.5
    wk = GLA_HEADS * dk
    heads = range(GLA_HEADS)
    ks = [slice(h * dk, (h + 1) * dk) for h in heads]
    vs = [slice(h * dv, (h + 1) * dv) for h in heads]

    def chunk(ci, carry):
        rows = pl.ds(pl.multiple_of(ci * ch, ch), ch)
        la = la_ref[rows, :]
        hi = la.astype(BF16)
        r1 = la - hi.astype(F32)
        mid = r1.astype(BF16)
        lo = (r1 - mid.astype(F32)).astype(BF16)
        b3 = _dot(tri, jnp.concatenate([hi, mid, lo], axis=1))
        b = b3[:, :wk] + b3[:, wk:2 * wk] + b3[:, 2 * wk:]
        bl = b[ch - 1:ch, :]
        q = q_ref[rows, :] * scale
        k = k_ref[rows, :]
        qe = (q * jnp.exp(b)).astype(BF16)
        ke = (k * jnp.exp(jnp.minimum(-b, GLA_EXP_CLAMP))).astype(BF16)
        kh = (k * jnp.exp(bl - b)).astype(BF16)
        dec = jnp.exp(bl)
        v = v_ref[rows, :].astype(BF16)
        att = [jnp.where(causal, _dot_nt(qe[:, ks[h]], ke[:, ks[h]]), 0.0).astype(BF16) for h in heads]
        st = [st_scr[h] for h in heads]
        o = [_dot_nt(qe[:, ks[h]], st[h].astype(BF16)) + _dot(att[h], v[:, vs[h]]) for h in heads]
        for h in heads:
            st_scr[h] = st[h] * dec[:, ks[h]] + _dot_tn(v[:, vs[h]], kh[:, ks[h]])
        for h in heads:
            on = _rms(o[h], gn_ref[:, vs[h]])
            o_ref[rows, vs[h]] = (on * _silu(ra_ref[rows, vs[h]])).astype(o_ref.dtype)
        return carry

    lax.fori_loop(0, rows // ch, chunk, 0)

    @pl.when(c == nc - 1)
    def _():
        for h in range(GLA_HEADS):
            sfin_ref[h] = st_scr[h].T


def _gla_prompt(z, zv, la, s0, gn_g, n, l, rows):
    t = n * l
    dk = la.shape[1] // GLA_HEADS
    dv = gn_g.shape[1] // GLA_HEADS
    wk, wv = GLA_HEADS * dk, GLA_HEADS * dv
    cpb = l // rows
    row = lambda b, c: b * cpb + c
    return pl.pallas_call(
        functools.partial(_gla_kernel, rows=rows, dk=dk, dv=dv),
        grid=(n, cpb),
        in_specs=[pl.BlockSpec((rows, wk), lambda b, c: (row(b, c), 0)),
                  pl.BlockSpec((rows, wk), lambda b, c: (row(b, c), 1)),
                  pl.BlockSpec((rows, wv), lambda b, c: (row(b, c), 0)),
                  pl.BlockSpec((rows, wv), lambda b, c: (row(b, c), 1)),
                  pl.BlockSpec((rows, wk), lambda b, c: (row(b, c), 0)),
                  pl.BlockSpec((None, GLA_HEADS, dk, dv), lambda b, c: (b, 0, 0, 0)),
                  pl.BlockSpec((1, wv), lambda b, c: (0, 0))],
        out_specs=[pl.BlockSpec((rows, wv), lambda b, c: (row(b, c), 0)),
                   pl.BlockSpec((None, GLA_HEADS, dk, dv), lambda b, c: (b, 0, 0, 0))],
        out_shape=[jax.ShapeDtypeStruct((t, wv), BF16), jax.ShapeDtypeStruct((n, GLA_HEADS, dk, dv), F32)],
        scratch_shapes=[pltpu.VMEM((GLA_HEADS, dv, dk), F32)],
        compiler_params=_cparams("arbitrary", "arbitrary"),
        name="gla_prompt",
    )(z, z, zv, z, la, s0, gn_g)


GLA_STEP_SEQS = 4


def _gla_step_kernel(q_ref, k_ref, la_ref, v_ref, ra_ref, s_ref, gn_ref, o_ref, so_ref, *, dk):
    for b in range(s_ref.shape[0]):
        sn = jnp.exp(la_ref[b]) * s_ref[b] + k_ref[b] * v_ref[b]
        so_ref[b] = sn
        o = jnp.sum((q_ref[b] * dk ** -0.5) * sn, axis=1, keepdims=True)
        on = _rms(o, gn_ref[...])
        o_ref[b] = (on * _silu(ra_ref[b])).astype(o_ref.dtype)


def _gla_step(q, k, la, v, ra, s, gn_g):
    nb, nh, dk, dv = s.shape
    bb = _pick_tile(nb, GLA_STEP_SEQS)
    col = pl.BlockSpec((bb, nh, dk, 1), lambda b: (b, 0, 0, 0))
    rowv = pl.BlockSpec((bb, nh, 1, dv), lambda b: (b, 0, 0, 0))
    st = pl.BlockSpec((bb, nh, dk, dv), lambda b: (b, 0, 0, 0))
    o, so = pl.pallas_call(
        functools.partial(_gla_step_kernel, dk=dk),
        grid=(nb // bb,),
        in_specs=[col, col, col, rowv, rowv, st, pl.BlockSpec((nh, 1, dv), lambda b: (0, 0, 0))],
        out_specs=[rowv, st],
        out_shape=[jax.ShapeDtypeStruct((nb, nh, 1, dv), BF16), jax.ShapeDtypeStruct(s.shape, F32)],
        compiler_params=_cparams("arbitrary"),
        name="gla_step",
    )(q.reshape(nb, nh, dk, 1), k.reshape(nb, nh, dk, 1), la.reshape(nb, nh, dk, 1),
      v.reshape(nb, nh, 1, dv), ra.reshape(nb, nh, 1, dv), s, gn_g.reshape(nh, 1, dv))
    return o.reshape(nb, nh * dv), so


def _band_kernel(q_ref, k_ref, kp_ref, v_ref, vp_ref, o_ref, lse_ref, *, dil, qb):
    i = pl.program_id(1)
    blk = ATT_BLOCK
    qi = lax.broadcasted_iota(jnp.int32, (blk, 2 * blk), 0)
    ki = lax.broadcasted_iota(jnp.int32, (blk, 2 * blk), 1)
    dist = blk + qi - ki
    band = jnp.logical_and(dist >= 0, dist <= ATT_BLOCK)
    band_first = jnp.logical_and(band, ki >= jnp.where(i > 0, 0, blk))
    scale = ATT_HD ** -0.5
    for r in range(dil):
        for sb in range(qb // blk):
            rs = slice(sb * blk, (sb + 1) * blk)
            q = q_ref[r, rs, :].astype(BF16)
            if sb == 0:
                kprev, vprev, mask = kp_ref[r], vp_ref[r], band_first
            else:
                ps = slice((sb - 1) * blk, sb * blk)
                kprev, vprev, mask = k_ref[r, ps, :], v_ref[r, ps, :], band
            kk = jnp.concatenate([kprev, k_ref[r, rs, :]], axis=0).astype(BF16)
            vv = jnp.concatenate([vprev, v_ref[r, rs, :]], axis=0).astype(BF16)
            s = jnp.where(mask, _dot_nt(q, kk) * scale, NEG_BIG)
            m = jnp.max(s, axis=-1, keepdims=True)
            p = jnp.exp(s - m)
            lsum = jnp.sum(p, axis=-1, keepdims=True)
            o = _dot(p.astype(BF16), vv) / lsum
            lse = jnp.broadcast_to(m + jnp.log(lsum), (blk, ATT_HD))
            rows = rs if dil == 1 else pl.ds(sb * blk * dil + r, blk, stride=dil)
            o_ref[rows, :] = o
            lse_ref[rows, :] = lse


def _band_attention(qa, ka, va, cq, ck, cv, n, l, dil):
    lq = l // dil
    hd = ATT_HD
    tokens = min(ATT_TOKENS, l)
    qb = tokens // dil
    nqb = lq // qb
    sub = qb // ATT_BLOCK

    def main(c0):
        return pl.BlockSpec((None, dil, qb, hd), lambda b, i, h: (b, 0, i, c0 + h))

    def prev(c0):
        return pl.BlockSpec((None, dil, ATT_BLOCK, hd), lambda b, i, h: (b, 0, jnp.maximum(i * sub - 1, 0), c0 + h))

    out = pl.BlockSpec((tokens, hd), lambda b, i, h: (b * nqb + i, h))
    shp = jax.ShapeDtypeStruct((n * l, ATT_HEADS * hd), F32)
    return pl.pallas_call(
        functools.partial(_band_kernel, dil=dil, qb=qb),
        grid=(n, nqb, ATT_HEADS),
        in_specs=[main(cq), main(ck), prev(ck), main(cv), prev(cv)],
        out_specs=[out, out],
        out_shape=[shp, shp],
        compiler_params=_cparams("arbitrary", "arbitrary", "arbitrary"),
        name=f"band_attention_d{dil}",
    )(qa, ka, ka, va, va)


ATT_STEP_SEQS = 4


def _att_step_kernel(q_ref, kn_ref, vn_ref, c0_ref, c1_ref, c2_ref, o_ref):
    scale = ATT_HD ** -0.5
    add = lambda a, b: a + b
    for b in range(q_ref.shape[0]):
        outs, lses = [], []
        for g, c_ref in enumerate((c0_ref, c1_ref, c2_ref)):
            q, kn, vn = q_ref[b, g], kn_ref[b, g], vn_ref[b, g]
            s = jnp.sum(c_ref[b, :, 0] * q[None], axis=-1, keepdims=True) * scale
            sn = jnp.sum(kn * q, axis=-1, keepdims=True) * scale
            m = jnp.maximum(jnp.max(s, axis=0), sn)
            p = jnp.exp(s - m[None])
            pn = jnp.exp(sn - m)
            lsum = jnp.sum(p, axis=0) + pn
            outs.append((jnp.sum(p * c_ref[b, :, 1], axis=0) + pn * vn) / lsum)
            lses.append(m + jnp.log(lsum))
        mx = functools.reduce(jnp.maximum, lses)
        es = [jnp.exp(x - mx) for x in lses]
        inv = 1.0 / functools.reduce(add, es)
        o_ref[b] = functools.reduce(add, [e * inv * o for e, o in zip(es, outs)]).astype(o_ref.dtype)


def _att_step(qs, ks, vs, caches):
    nb = qs.shape[0]
    bb = _pick_tile(nb, ATT_STEP_SEQS)
    views, cspecs = [], []
    for (window, dil), cch in zip(ATT_GROUPS, caches):
        assert cch.shape[1] == window, "sample step expects full caches"
        views.append(cch.reshape((nb, window // dil, dil) + cch.shape[2:]))
        cspecs.append(pl.BlockSpec((bb, ATT_BLOCK, None) + cch.shape[2:], lambda b: (b, 0, 0, 0, 0, 0)))
    row = pl.BlockSpec((bb,) + qs.shape[1:], lambda b: (b, 0, 0, 0))
    o = pl.pallas_call(
        _att_step_kernel,
        grid=(nb // bb,),
        in_specs=[row] * 3 + cspecs,
        out_specs=pl.BlockSpec((bb, ATT_HEADS, ATT_HD), lambda b: (b, 0, 0)),
        out_shape=jax.ShapeDtypeStruct((nb, ATT_HEADS, ATT_HD), BF16),
        compiler_params=_cparams("arbitrary"),
        name="att_step",
    )(qs, ks, vs, *views)
    return o.reshape(nb, ATT_HEADS * ATT_HD)


MERGE_PART = 256


def _merge_kernel(*refs, n_att, n_alias, n_tiles):
    oa_ref = refs[0]
    att = refs[1:1 + n_att]
    (gza_ref, gzb_ref, x_ref, g1_ref, sc2_ref, sh2_ref, n2_ref, wa_ref, wb_ref, wo_ref,
     wq_ref) = refs[1 + n_att:12 + n_att]
    x1_ref, h2_ref, pq_ref = refs[12 + n_att + n_alias:]
    tm = x_ref.shape[0]
    pr = min(MERGE_PART, tm)
    parts = [slice(k * pr, (k + 1) * pr) for k in range(tm // pr)]
    mod = lambda ref, part: ref[...] if ref.shape[0] == 1 else ref[part, :]
    add = lambda a, b: a + b

    def attention_out(part):
        if n_att == 1:
            return att[0][part, :]
        ng = n_att // 2
        ls = [r[part, :] for r in att[ng:]]
        mx = functools.reduce(jnp.maximum, ls)
        es = [jnp.exp(x - mx) for x in ls]
        inv = 1.0 / functools.reduce(add, es)
        return functools.reduce(add, [e * inv * r[part, :] for e, r in zip(es, att[:ng])]).astype(BF16)

    @pl.when(pl.program_id(0) >= n_tiles)
    def _():
        h2_ref[...] = jnp.zeros_like(h2_ref)
        pq_ref[...] = jnp.zeros_like(pq_ref)

    @pl.when(pl.program_id(0) < n_tiles)
    def _():
        _merge_rows(parts, attention_out, mod, oa_ref, gza_ref, gzb_ref, x_ref, g1_ref, sc2_ref, sh2_ref, n2_ref,
                    wa_ref, wb_ref, wo_ref, wq_ref, x1_ref, h2_ref, pq_ref)
        if h2_ref.shape[0] > tm:
            h2_ref[tm:] = jnp.zeros((h2_ref.shape[0] - tm, h2_ref.shape[1]), h2_ref.dtype)
            pq_ref[tm:] = jnp.zeros((pq_ref.shape[0] - tm, pq_ref.shape[1]), pq_ref.dtype)


def _merge_rows(parts, attention_out, mod, oa_ref, gza_ref, gzb_ref, x_ref, g1_ref, sc2_ref, sh2_ref, n2_ref,
                wa_ref, wb_ref, wo_ref, wq_ref, x1_ref, h2_ref, pq_ref):
    ob = [attention_out(p) for p in parts]
    ma = [_dot(oa_ref[p, :], wa_ref[...]) for p in parts]
    mb = [_dot(o, wb_ref[...]) for o in ob]
    merged = [(_sigmoid(gza_ref[p, :]) * a + _sigmoid(gzb_ref[p, :]) * b).astype(BF16)
              for p, a, b in zip(parts, ma, mb)]
    y = [_dot(m, wo_ref[...]) for m in merged]
    h2 = []
    for p, yy in zip(parts, y):
        x1 = x_ref[p, :] + mod(g1_ref, p) * yy
        x1_ref[p, :] = x1
        h2.append((_rms(x1, n2_ref[...]) * (1.0 + mod(sc2_ref, p)) + mod(sh2_ref, p)).astype(BF16))
    for p, h in zip(parts, h2):
        h2_ref[p, :] = h
        pq_ref[p, :] = _dot(h, wq_ref[...])


def _merge(oa, att, z, x2, g1, sc2, sh2, n2g, wa, wb, wo, wq, tm, rows_per_mod, gza_blk, gzb_blk,
           peer_rows, peer_tile, peer_blk0, alias=(), zero_tiles=0):
    t, d = x2.shape
    r = g1.shape[1]
    wqn = wq.shape[1]
    tiles_per_mod = rows_per_mod // tm
    n_tiles = t // tm
    row = lambda i: jnp.minimum(i, n_tiles - 1)
    tok = lambda w: pl.BlockSpec((tm, w), lambda i: (row(i), 0))
    mod = pl.BlockSpec((None, r, d), lambda i: (row(i) // tiles_per_mod, 0, 0))
    const = lambda a: pl.BlockSpec(a.shape, lambda i: (0, 0), pipeline_mode=pl.Buffered(1))
    peer = lambda w: pl.BlockSpec((peer_tile, w), lambda i: (peer_blk0 + i, 0))
    n_in = 12 + len(att)
    return pl.pallas_call(
        functools.partial(_merge_kernel, n_att=len(att), n_alias=len(alias), n_tiles=n_tiles),
        grid=(n_tiles + zero_tiles,),
        in_specs=[tok(oa.shape[1])] + [tok(a.shape[1]) for a in att]
                 + [pl.BlockSpec((tm, d), lambda i: (row(i), gza_blk)),
                    pl.BlockSpec((tm, d), lambda i: (row(i), gzb_blk)),
                    tok(d), mod, mod, mod, const(n2g), const(wa), const(wb), const(wo), const(wq)]
                 + [pl.BlockSpec(memory_space=pl.ANY)] * len(alias),
        out_specs=[tok(d), peer(d), peer(wqn)],
        out_shape=[jax.ShapeDtypeStruct((t, d), F32), jax.ShapeDtypeStruct((peer_rows, d), BF16),
                   jax.ShapeDtypeStruct((peer_rows, wqn), F32)],
        input_output_aliases={n_in + k: 1 + k for k in range(len(alias))},
        compiler_params=_cparams("arbitrary"),
        name="merge",
    )(oa, *att, z, z, x2, g1, sc2, sh2, n2g, wa, wb, wo, wq, *alias)


def _strict_max_below(rows, prev):
    m = None
    for x in rows:
        y = jnp.where(x < prev, x, NEG_BIG) if prev is not None else x
        m = y if m is None else jnp.maximum(m, y)
    return m


def _oddeven_merge(lo, hi, r):
    step = r * 2
    if step < hi - lo:
        yield from _oddeven_merge(lo, hi, step)
        yield from _oddeven_merge(lo + r, hi, step)
        yield from [(i, i + r) for i in range(lo + r, hi - r, step)]
    else:
        yield (lo, lo + r)


def _oddeven_merge_sort(lo, hi):
    if hi - lo >= 1:
        mid = lo + (hi - lo) // 2
        yield from _oddeven_merge_sort(lo, mid)
        yield from _oddeven_merge_sort(mid + 1, hi)
        yield from _oddeven_merge(lo, hi, 1)


def _exchange(p, i, j):
    p[i], p[j] = jnp.maximum(p[i], p[j]), jnp.minimum(p[i], p[j])


def _topk_desc(s, k):
    assert s.shape[0] == k * SUBLANES and k & (k - 1) == 0
    p = [s[i * SUBLANES:(i + 1) * SUBLANES] for i in range(k)]
    for i, j in _oddeven_merge_sort(0, k - 1):
        _exchange(p, i, j)
    shift = SUBLANES // 2
    while shift:
        q = [pltpu.roll(v, shift, 0) for v in p]
        p = [jnp.maximum(p[i], q[k - 1 - i]) for i in range(k)]
        stride = k // 2
        while stride:
            for i in range(k):
                if not i & stride:
                    _exchange(p, i, i + stride)
            stride //= 2
        shift //= 2
    return [v[0:1] for v in p]


def _route_kernel(pq_ref, k1_ref, k2_ref, n1_ref, e1_ref, r2_ref, e2_ref, s1_scr, s2_scr):
    nk = PEER_NKEYS
    pq = pq_ref[...].astype(BF16)
    s1_scr[...] = _dot_nt(k1_ref[...], pq)
    s2_scr[...] = _dot_nt(k2_ref[...], pq)
    v1h, v2h = [], []
    for h in range(PEER_HEADS):
        v1h.append(_topk_desc(s1_scr[h * nk:(h + 1) * nk], PEER_TOPK))
        v2h.append(_topk_desc(s2_scr[h * nk:(h + 1) * nk], PEER_TOPK))
    v1 = [jnp.concatenate([v1h[h][k] for h in range(PEER_HEADS)], axis=0) for k in range(PEER_TOPK)]
    v2 = [jnp.concatenate([v2h[h][k] for h in range(PEER_HEADS)], axis=0) for k in range(PEER_TOPK)]
    pairs = [(i, j) for i in range(PEER_TOPK) for j in range(PEER_TOPK) if (i + 1) * (j + 1) <= PEER_TOPK]
    cands = [v1[i] + v2[j] for i, j in pairs]
    tau = None
    for _ in range(PEER_TOPK):
        tau = _strict_max_below(cands, tau)
    cmax = v1[0] + v2[0]
    zsum = None
    counts = [None] * PEER_TOPK
    for (i, _), cnd in zip(pairs, cands):
        sel = cnd >= tau
        e = jnp.where(sel, jnp.exp(cnd - cmax), 0.0)
        zsum = e if zsum is None else zsum + e
        one = jnp.where(sel, 1.0, 0.0)
        counts[i] = one if counts[i] is None else counts[i] + one
    zinv = 1.0 / zsum
    for h in range(PEER_HEADS):
        a = s1_scr[h * nk:(h + 1) * nk]
        b = s2_scr[h * nk:(h + 1) * nk]
        n1 = jnp.zeros_like(a)
        r2 = jnp.full_like(b, float(PEER_TOPK))
        for k in range(PEER_TOPK):
            n1 = jnp.where(a == v1h[h][k], counts[k][h:h + 1, :], n1)
            r2 = jnp.where(b == v2h[h][k], float(k), r2)
        n1_ref[h] = n1
        e1_ref[h] = jnp.exp(a - v1h[h][0]) * zinv[h:h + 1, :]
        r2_ref[h] = r2.astype(r2_ref.dtype)
        e2_ref[h] = jnp.exp(b - v2h[h][0]).astype(e2_ref.dtype)


def _peer_route(pq, k1big, k2big, tt):
    t, w = pq.shape
    nh, nk = PEER_HEADS, PEER_NKEYS
    sspec = pl.BlockSpec((nh, nk, tt), lambda i: (0, 0, i))
    shp = lambda dt: jax.ShapeDtypeStruct((nh, nk, t), dt)
    return pl.pallas_call(
        _route_kernel,
        grid=(t // tt,),
        in_specs=[pl.BlockSpec((tt, w), lambda i: (i, 0)),
                  pl.BlockSpec(k1big.shape, lambda i: (0, 0)), pl.BlockSpec(k2big.shape, lambda i: (0, 0))],
        out_specs=[sspec] * 4,
        out_shape=[shp(F32), shp(F32), shp(BF16), shp(BF16)],
        scratch_shapes=[pltpu.VMEM((nh * nk, tt), F32)] * 2,
        compiler_params=_cparams("arbitrary"),
        name="peer_route",
    )(pq, k1big, k2big)


def _gelu_tanh(x):
    return 0.5 * x * (1.0 + jnp.tanh(GELU_C * (x + 0.044715 * (x * x * x))))


BF16_ROWS = 16
GATE_LANES = 256
PEER_PART = 256
PEER_TILE = 2048


def _gelu_tanh_bf16(x):
    hx = 0.5 * x
    return hx + hx * jnp.tanh(x * (GELU_C + (GELU_C * 0.044715) * (x * x)))


def _bf16_rows(row):
    return jnp.broadcast_to(row, (BF16_ROWS, row.shape[1])).astype(BF16)


CACHE_CHUNK = 512


def _cache_shift_jobs(caches):
    nb = caches[0].shape[0]
    chunks, base = [], 0
    for c, cache in enumerate(caches):
        w = cache.shape[1]
        for r0 in range(0, w, CACHE_CHUNK):
            n = min(CACHE_CHUNK, w - r0)
            chunks.append((c, r0, n, r0 + n == w, base))
            base += nb
    return chunks, base


def _cache_shift_step(step, caches, new_rows, outs, buf, sems):
    nb = caches[0].shape[0]
    chunks, n_jobs = _cache_shift_jobs(caches)

    def for_job(job, act):
        for c, r0, n, last, base in chunks:
            @pl.when(jnp.logical_and(job >= base, job < base + nb))
            def _(c=c, r0=r0, n=n, last=last, base=base):
                b, slot = job - base, job % 2
                n_in = n - 1 if last else n
                rd = lambda: pltpu.make_async_copy(caches[c].at[b, pl.ds(r0 + 1, n_in)],
                                                   buf.at[slot, pl.ds(0, n_in)], sems.at[slot, 0])
                wr = lambda: pltpu.make_async_copy(buf.at[slot, pl.ds(0, n)], outs[c].at[b, pl.ds(r0, n)],
                                                   sems.at[slot, 1])
                act(rd, wr, lambda: new_rows[c][b] if last else None, slot, n_in)

    def finish_read_start_write(rd, wr, new_row, slot, n_in):
        rd().wait()
        row = new_row()
        if row is not None:
            buf[slot, pl.ds(n_in, 1)] = row
        wr().start()

    @pl.when(jnp.logical_and(step >= 1, step <= n_jobs))
    def _():
        for_job(step - 1, finish_read_start_write)

    @pl.when(jnp.logical_and(step >= 2, step <= n_jobs + 1))
    def _():
        for_job(step - 2, lambda rd, wr, *_: wr().wait())

    @pl.when(step < n_jobs)
    def _():
        for_job(step, lambda rd, wr, *_: rd().start())


def _peer_dense_kernel(h2_ref, u_ref, vt_ref, n1_ref, e1_ref, r2_ref, e2_ref, h2_next_ref, u_next_ref, *rest,
                       et, n_kv):
    caches, new_rows = rest[:n_kv], rest[n_kv:2 * n_kv]
    o_ref, kv_outs = rest[2 * n_kv], rest[2 * n_kv + 1:3 * n_kv + 1]
    acc_scr, wg_scr, st_scr, st0_scr = rest[3 * n_kv + 1:3 * n_kv + 5]
    e = pl.program_id(1)
    nk = PEER_NKEYS
    tt = h2_ref.shape[0]
    step = pl.program_id(0) * pl.num_programs(1) + e
    if n_kv:
        _cache_shift_step(step, caches, new_rows, kv_outs, rest[-2], rest[-1])

    @pl.when(step == 0)
    def _():
        st0_scr[...] = _dot_nt(u_ref[0:PEER_PART, :], h2_ref[...]).astype(BF16)

    @pl.when(e == 0)
    def _():
        acc_scr[...] = jnp.zeros_like(acc_scr)

    zero = jnp.zeros((), BF16)
    groups = nk // BF16_ROWS

    def scores(part):
        st_scr[part, :] = _dot_nt(u_ref[part, :], h2_ref[...]).astype(BF16)

    def gates(ii):
        rs = slice(ii * nk, (ii + 1) * nk)
        for lc in range(tt // GATE_LANES):
            ls = slice(lc * GATE_LANES, (lc + 1) * GATE_LANES)
            w = None
            for h in range(PEER_HEADS):
                n1 = _bf16_rows(n1_ref[h, ii:ii + 1, ls])
                e1 = _bf16_rows(e1_ref[h, ii:ii + 1, ls])
                r2 = r2_ref[h, :, ls].reshape(groups, BF16_ROWS, GATE_LANES)
                e2 = e2_ref[h, :, ls].reshape(groups, BF16_ROWS, GATE_LANES)
                c = jnp.where(r2 < n1[None], e2, zero) * e1[None]
                w = c if w is None else w + c
            st = st0_scr if rs.stop <= PEER_PART else st_scr
            wg_scr[rs, ls] = w.reshape(nk, GATE_LANES) * _gelu_tanh_bf16(st[rs, ls])

    def mix(part):
        return _dot(vt_ref[:, part], wg_scr[part, :])

    parts = [slice(p * PEER_PART, (p + 1) * PEER_PART) for p in range(et // PEER_PART)]
    out = None
    next_scores = None
    for p, part in enumerate(parts):
        if p + 1 < len(parts):
            scores(parts[p + 1])
        else:
            next_scores = _dot_nt(u_next_ref[...], h2_next_ref[...]).astype(BF16)
        for ii in range(part.start // nk, part.stop // nk):
            gates(ii)
        if p > 0:
            d = mix(parts[p - 1])
            out = d if out is None else out + d
    st0_scr[...] = next_scores
    d = mix(parts[-1])
    acc_scr[...] += d if out is None else out + d

    @pl.when(e == pl.num_programs(1) - 1)
    def _():
        o_ref[...] = acc_scr[...].T


def _peer_dense(h2, u_b, vt_b, n1, e1, r2, e2, tt, et, caches=(), new_rows=()):
    t, d = h2.shape
    ne = u_b.shape[0]
    nh, nk = PEER_HEADS, PEER_NKEYS
    n_kv = len(caches)
    grid = (t // tt, ne // et)
    if n_kv:
        assert grid[0] * grid[1] >= _cache_shift_jobs(caches)[1] + 2, "one cache-shift job per grid step"
    rowside = pl.BlockSpec((nh, et // nk, tt), lambda i, e: (0, e, i))
    colside = pl.BlockSpec((nh, nk, tt), lambda i, e: (0, 0, i))
    anywhere = pl.BlockSpec(memory_space=pl.ANY)
    whole = lambda a: pl.BlockSpec(a.shape, lambda i, e: (0,) * a.ndim)
    outs = pl.pallas_call(
        functools.partial(_peer_dense_kernel, et=et, n_kv=n_kv),
        grid=grid,
        in_specs=[pl.BlockSpec((tt, d), lambda i, e: (i, 0)),
                  pl.BlockSpec((et, d), lambda i, e: (e, 0)),
                  pl.BlockSpec((d, et), lambda i, e: (0, e)),
                  rowside, rowside, colside, colside,
                  pl.BlockSpec((tt, d), lambda i, e: (jnp.minimum(i + (e + 1) // grid[1], grid[0] - 1), 0)),
                  pl.BlockSpec((PEER_PART, d), lambda i, e: (((e + 1) % grid[1]) * (et // PEER_PART), 0))]
                 + [anywhere] * n_kv + [whole(a) for a in new_rows],
        out_specs=[pl.BlockSpec((tt, d), lambda i, e: (i, 0))] + [anywhere] * n_kv,
        out_shape=[jax.ShapeDtypeStruct((t, d), F32)] + [jax.ShapeDtypeStruct(c.shape, c.dtype) for c in caches],
        scratch_shapes=[pltpu.VMEM((d, tt), F32), pltpu.VMEM((et, tt), BF16), pltpu.VMEM((et, tt), BF16),
                        pltpu.VMEM((PEER_PART, tt), BF16)]
                       + ([pltpu.VMEM((2, CACHE_CHUNK) + caches[0].shape[2:], caches[0].dtype),
                           pltpu.SemaphoreType.DMA((2, 2))] if n_kv else []),
        compiler_params=_cparams("arbitrary", "arbitrary"),
        name="peer_dense",
    )(h2, u_b, vt_b, n1, e1, r2, e2, h2, u_b, *caches, *new_rows)
    return outs[0], outs[1:]


def _final_kernel(x1_ref, p_ref, g2_ref, fg_ref, y_ref):
    y_ref[...] = _rms(x1_ref[...] + g2_ref[...] * p_ref[...], fg_ref[...])


def _final(x1, p, g2, fg, tm, rows_per_mod, p_blk0):
    t, d = x1.shape
    r = g2.shape[1]
    tiles_per_mod = rows_per_mod // tm
    tok = pl.BlockSpec((tm, d), lambda i: (i, 0))
    return pl.pallas_call(
        _final_kernel,
        grid=(t // tm,),
        in_specs=[tok, pl.BlockSpec((tm, d), lambda i: (p_blk0 + i, 0)),
                  pl.BlockSpec((None, r, d), lambda i: (i // tiles_per_mod, 0, 0)),
                  pl.BlockSpec((1, d), lambda i: (0, 0))],
        out_specs=tok,
        out_shape=jax.ShapeDtypeStruct((t, d), F32),
        compiler_params=_cparams("arbitrary"),
        name="final_norm",
    )(x1, p, g2, fg)


def _rope_tables(pos):
    half = ROPE_DIM // 2
    inv_freq = ROPE_THETA ** (-jnp.arange(half, dtype=F32) / half)
    ang = pos.astype(F32)[:, None] * inv_freq[None, :]
    cos, sin = jnp.cos(ang), jnp.sin(ang)
    n = pos.shape[0]
    ones = jnp.ones((n, LANES - ROPE_DIM), F32)
    zeros = jnp.zeros((n, LANES - ROPE_DIM), F32)
    return (jnp.concatenate([cos, cos, ones], axis=1), jnp.concatenate([-sin, sin, zeros], axis=1))


def _block_diag_keys(keys, half):
    nk, dh = keys.shape
    eye = jnp.eye(PEER_HEADS, dtype=keys.dtype)
    blk = jnp.zeros((PEER_HEADS, nk, PEER_HEADS, 2, dh), keys.dtype)
    blk = blk.at[:, :, :, half, :].set(eye[:, None, :, None] * keys[None, :, None, :])
    return blk.reshape(PEER_HEADS * nk, PEER_HEADS * 2 * dh).astype(BF16)


def _pick_tile(n, pref):
    t = min(pref, n)
    while n % t:
        t //= 2
    return t


def kernel(x_prompt, x_sample, c_prompt, c_sample, state_gla, cache_kv_w128, cache_kv_w512, cache_kv_w2048,
           ada_w, ada_b, norm1_g, w_in, gla_a_w2, gla_a_b, gla_gn_g, w_branch_a, w_branch_b, w_out,
           norm2_g, peer_wq, peer_k1, peer_k2, peer_u, peer_v, final_g):
    depth = ada_w.shape[0]
    assert depth == 1, "single-layer trunk"
    n_p, l_p, d = x_prompt.shape
    n_s, l_s, _ = x_sample.shape
    assert l_s == 1, "sample group decodes one token per sequence"
    caches = (cache_kv_w128[0], cache_kv_w512[0], cache_kv_w2048[0])
    n_groups = len(ATT_GROUPS)

    w_qa = gla_a_w2.shape[2]
    w_va = gla_gn_g.shape[1]
    w_ob = ATT_HEADS * ATT_HD
    w_qb = n_groups * w_ob
    assert w_qa == TN_IN and w_va == 2 * TN_IN and d == 2 * TN_IN and w_ob == TN_IN
    widths = (w_qa, w_qa, w_va, w_va, GLA_RANK, w_qb, w_qb, w_qb, d, d)
    offs = [0]
    for w in widths:
        offs.append(offs[-1] + w)
    seg = lambda i: w_in[0][:, offs[i]:offs[i + 1]]
    att_col = lambda g, k: seg(5 + k)[:, g * w_ob:(g + 1) * w_ob]
    att_cols = [att_col(g, k) for g in range(n_groups) for k in range(3)]
    w_main = jnp.concatenate([seg(0), seg(1), seg(3), seg(8), seg(9), seg(2)] + att_cols, axis=1).astype(BF16)
    w_kv = jnp.concatenate([att_col(g, k) for g in range(n_groups) for k in (1, 2)], axis=1).astype(BF16)
    n_tiles = w_main.shape[1] // TN_IN
    ra_off, gza_blk, gzb_blk = 2 * w_qa, 2, 3
    w_alr = jnp.pad(seg(4), ((0, 0), (0, LANES - GLA_RANK))).astype(BF16)
    w2p = jnp.pad(gla_a_w2[0], ((0, LANES - GLA_RANK), (0, 0))).astype(BF16)
    a_b = gla_a_b[0].reshape(1, w_qa)
    g1n = norm1_g[0].reshape(1, d)
    gn_g = gla_gn_g[0].reshape(1, w_va)
    n2g = norm2_g[0].reshape(1, d)
    wa, wb = w_branch_a[0].astype(BF16), w_branch_b[0].astype(BF16)
    wo, wq = w_out[0].astype(BF16), peer_wq[0].astype(BF16)
    q_tile = lambda g: ATT_TILE0 + 3 * g
    rope_tiles = [q_tile(g) + k for g in range(n_groups) for k in (0, 1)]

    n_mod = n_p + n_s
    n_pad = -(-n_mod // SUBLANES) * SUBLANES
    c_all = jnp.concatenate([c_prompt, c_sample, jnp.zeros((n_pad - n_mod, d), F32)], axis=0)
    mod = _modulation(c_all, ada_w[0], ada_b[0])
    mods_p = [mod[:n_p, i * d:(i + 1) * d].reshape(n_p, 1, d) for i in range(6)]
    mods_s = [mod[n_p:n_mod, i * d:(i + 1) * d].reshape(1, n_s, d) for i in range(6)]

    t_p = n_p * l_p
    tt = _pick_tile(t_p, 512)
    t_all = t_p + (-(-n_s // tt)) * tt
    assert t_all - t_p == tt

    xp2 = x_prompt.reshape(t_p, d)
    tm_p = _pick_tile(l_p, 1024)
    pos_p = jnp.arange(l_p, dtype=jnp.int32)
    split = []
    for g, (window, dil) in enumerate(ATT_GROUPS):
        assert window // dil == ATT_BLOCK
        if dil > 1:
            split += [(q_tile(g) + k, len(split) + k, dil) for k in range(3)]
    first_split = min([t for t, _, _ in split], default=n_tiles)
    outs = _in_proj(xp2, g1n, mods_p[1], mods_p[0], w_main, w_alr, w2p, a_b, pos_p, tm_p, l_p,
                    ((0, VA_TILE0, F32), (VA_TILE0, first_split, BF16)), rope_tiles, split)
    z_p, zb_p, la_p, split_p = outs[0], outs[1], outs[2], outs[3:]
    s0_p = jnp.zeros((n_p,) + state_gla.shape[2:], F32)
    oa_p, sfin_p = _gla_prompt(z_p, zb_p, la_p, s0_p, gn_g, n_p, l_p, _pick_tile(l_p, 512))
    zb4 = zb_p.reshape(n_p, 1, l_p, zb_p.shape[1])
    att_o, att_l = [], []
    for g, (window, dil) in enumerate(ATT_GROUPS):
        if dil > 1:
            qa, ka, va = [split_p[slot] for t, slot, _ in split if q_tile(g) <= t < q_tile(g) + 3]
            o_g, lse_g = _band_attention(qa, ka, va, 0, 0, 0, n_p, l_p, dil)
        else:
            c0 = lambda k: (q_tile(g) + k - VA_TILE0) * (TN_IN // ATT_HD)
            o_g, lse_g = _band_attention(zb4, zb4, zb4, c0(0), c0(1), c0(2), n_p, l_p, dil)
        att_o.append(o_g)
        att_l.append(lse_g)
    keep_max = min(max(w for w, _ in ATT_GROUPS), l_p)
    x_tail = x_prompt[:, l_p - keep_max:].reshape(n_p * keep_max, d)
    kv_tiles = 2 * n_groups
    z_kv = _in_proj(x_tail, g1n, mods_p[1], mods_p[0], w_kv, w_alr, w2p, a_b, pos_p[l_p - keep_max:],
                    _pick_tile(keep_max, 1024), keep_max, ((0, kv_tiles, F32),), range(0, kv_tiles, 2), [],
                    decay=False)[0]
    z_kv = z_kv.reshape(n_p, keep_max, kv_tiles * TN_IN)
    kv_p = []
    for g, (window, dil) in enumerate(ATT_GROUPS):
        keep = min(window, l_p)
        rows = z_kv[:, keep_max - keep:, 2 * g * TN_IN:(2 * g + 2) * TN_IN]
        kv_p.append(rows.reshape(1, n_p, keep, 2, ATT_HEADS, ATT_HD))
    tm_m = _pick_tile(l_p, 512)
    x1_p, h2_all, pq_all = _merge(oa_p, att_o + att_l, z_p, xp2, mods_p[2], mods_p[4], mods_p[3], n2g,
                                  wa, wb, wo, wq, tm_m, l_p, gza_blk, gzb_blk, t_all, tm_m, 0,
                                  zero_tiles=(t_all - t_p) // tm_m)

    xs2 = x_sample.reshape(n_s, d)
    pos_s = jnp.full((n_s,), PAST_LEN, dtype=jnp.int32)
    z_s, la_s = _in_proj(xs2, g1n, mods_s[1], mods_s[0], w_main, w_alr, w2p, a_b, pos_s, n_s, n_s,
                         ((0, n_tiles, F32),), rope_tiles, [])
    col = lambda tile: z_s[:, tile * TN_IN:(tile + 1) * TN_IN]
    oa_s, state_s = _gla_step(col(0), col(1), la_s, z_s[:, VA_TILE0 * TN_IN:VA_TILE0 * TN_IN + w_va],
                              z_s[:, ra_off:ra_off + w_va], state_gla[0], gn_g)
    stack = lambda k: jnp.stack([col(q_tile(g) + k) for g in range(n_groups)], axis=1).reshape(
        n_s, n_groups, ATT_HEADS, ATT_HD)
    ob_s = _att_step(stack(0), stack(1), stack(2), caches)
    x1_s, h2_all, pq_all = _merge(oa_s, [ob_s], z_s, xs2, mods_s[2], mods_s[4], mods_s[3], n2g,
                                  wa, wb, wo, wq, n_s, n_s, gza_blk, gzb_blk, t_all, tt, t_p // tt,
                                  alias=(h2_all, pq_all))
    row_tiles = (2 * w_ob // LANES, LANES)
    new_rows = [jnp.concatenate([col(q_tile(g) + 1), col(q_tile(g) + 2)], axis=1).reshape((n_s, 1) + row_tiles)
                for g in range(n_groups)]

    k1big = _block_diag_keys(peer_k1[0], 0)
    k2big = _block_diag_keys(peer_k2[0], 1)
    n1, e1, r2, e2 = _peer_route(pq_all, k1big, k2big, _pick_tile(tt, 256))
    u_b = peer_u[0].astype(BF16)
    vt_b = peer_v[0].astype(BF16).T
    p_all, kv_s = _peer_dense(h2_all, u_b, vt_b, n1, e1, r2, e2, tt, PEER_TILE,
                              [c.reshape(c.shape[:2] + row_tiles) for c in caches], new_rows)
    kv_s = [a.reshape(c.shape)[None] for a, c in zip(kv_s, caches)]

    fg = final_g.reshape(1, d)
    y_p = _final(x1_p, p_all, mods_p[5], fg, tm_m, l_p, 0)
    y_s = _final(x1_s, p_all, mods_s[5], fg, n_s, n_s, t_p // n_s)

    return (y_p.reshape(n_p, l_p, d), y_s.reshape(n_s, l_s, d), sfin_p[None], state_s[None],
            kv_p[0], kv_s[0], kv_p[1], kv_s[1], kv_p[2], kv_s[2])
```

```python
import functools
import math

import jax
import jax.numpy as jnp
from jax import lax
from jax.experimental import pallas as pl
from jax.experimental.pallas import tpu as pltpu

F32 = jnp.float32
BF16 = jnp.bfloat16

PAST_LEN = 16384
GLA_HEADS = 4
GLA_RANK = 16
GLA_TAU = 16.0
GLA_CHUNK = 128
ATT_GROUPS = ((128, 1), (512, 4), (2048, 16))
ATT_HEADS = 4
ATT_HD = 128
ATT_BLOCK = 128
ATT_TOKENS = 2048
ROPE_DIM = ATT_HD // 4
ROPE_THETA = 500000.0
PEER_HEADS = 8
PEER_NKEYS = 128
PEER_TOPK = 16
NORM_EPS = 1e-6
GELU_C = math.sqrt(2.0 / math.pi)

LANES = 128
SUBLANES = 8
VMEM_LIMIT = 56 * 1024 * 1024
NEG_BIG = -1e30


def _cparams(*sem):
    return pltpu.CompilerParams(dimension_semantics=sem, vmem_limit_bytes=VMEM_LIMIT)


def _sigmoid(x):
    return 1.0 / (1.0 + jnp.exp(-x))


def _silu(x):
    return x * _sigmoid(x)


def _log_sigmoid(x):
    return jnp.minimum(x, 0.0) - jnp.log(1.0 + jnp.exp(-jnp.abs(x)))


def _rms(x, g):
    return x * lax.rsqrt(jnp.mean(x * x, axis=-1, keepdims=True) + NORM_EPS) * g


def _dot(a, b):
    return jnp.dot(a, b, preferred_element_type=F32)


def _dot_nt(a, b):
    return lax.dot_general(a, b, (((1,), (1,)), ((), ())), preferred_element_type=F32)


def _dot_tn(a, b):
    return lax.dot_general(a, b, (((0,), (0,)), ((), ())), preferred_element_type=F32)


def _mod_kernel(c_ref, w_ref, b_ref, o_ref):
    s = _silu(c_ref[...]).astype(BF16)
    o_ref[...] = _dot(s, w_ref[...].astype(BF16)) + b_ref[...]


def _modulation(c, ada_w, ada_b):
    n, d = c.shape
    n6 = ada_w.shape[1]
    tn = n6 // 4
    return pl.pallas_call(
        _mod_kernel,
        grid=(n6 // tn,),
        in_specs=[pl.BlockSpec((n, d), lambda j: (0, 0)),
                  pl.BlockSpec((d, tn), lambda j: (0, j)),
                  pl.BlockSpec((1, tn), lambda j: (0, j))],
        out_specs=pl.BlockSpec((n, tn), lambda j: (0, j)),
        out_shape=jax.ShapeDtypeStruct((n, n6), F32),
        compiler_params=_cparams("arbitrary"),
        name="modulation",
    )(c, ada_w, ada_b.reshape(1, n6))


TN_IN = 512
VA_TILE0 = 8
ATT_TILE0 = 10


def _rope_tile(z, c, s):
    reps = z.shape[1] // LANES
    n = z.shape[1]
    half = ROPE_DIM // 2
    lane = lax.broadcasted_iota(jnp.int32, (1, LANES), 1)
    first = jnp.concatenate([lane < half] * reps, axis=1)
    partner = jnp.where(first, pltpu.roll(z, n - half, 1), pltpu.roll(z, half, 1))
    return z * jnp.concatenate([c] * reps, axis=1) + partner * jnp.concatenate([s] * reps, axis=1)


def _inproj_kernel(x_ref, g_ref, sc_ref, sh_ref, w_ref, walr_ref, w2_ref, ab_ref, *rest,
                   segs, rope_tiles, split_tiles, decay):
    dils = sorted({dil for _, _, dil in split_tiles})
    n_seg, n_tab = len(segs), 2 * (1 + len(dils))
    tabs = [(rest[2 * v], rest[2 * v + 1]) for v in range(1 + len(dils))]
    seg_refs = rest[n_tab:n_tab + n_seg]
    la_ref = rest[n_tab + n_seg] if decay else None
    split_refs, (h_scr, hf_scr) = rest[n_tab + n_seg + int(decay):-2], rest[-2:]
    j = pl.program_id(1)
    tm, d = x_ref.shape

    @pl.when(j == 0)
    def _():
        h = _rms(x_ref[...], g_ref[...]) * (1.0 + sc_ref[...]) + sh_ref[...]
        hb = h.astype(BF16)
        h_scr[0] = hb
        if decay:
            alr = _dot(hb, walr_ref[...])
            p = _dot(alr.astype(BF16), w2_ref[...]) + ab_ref[...]
            la_ref[...] = _log_sigmoid(p) * (1.0 / GLA_TAU)
        for c in range((d // LANES) if dils else 0):
            cs = slice(c * LANES, (c + 1) * LANES)
            hf_scr[c] = h[:, cs]
            for v, dil in enumerate(dils, 1):
                per = tm // dil
                for r in range(dil):
                    h_scr[v, r * per:(r + 1) * per, cs] = hf_scr[c, pl.ds(r, per, stride=dil), :].astype(BF16)

    def tile(v, roped):
        z = _dot(h_scr[v], w_ref[...])
        return _rope_tile(z, tabs[v][0][...], tabs[v][1][...]) if roped else z

    for (lo, hi), ref in zip(segs, seg_refs):
        ropes = [t for t in rope_tiles if lo <= t < hi]
        in_seg = jnp.logical_and(j >= lo, j < hi)
        is_rope = functools.reduce(jnp.logical_or, [j == t for t in ropes], False)
        if ropes:
            @pl.when(jnp.logical_and(in_seg, is_rope))
            def _(ref=ref):
                ref[...] = tile(0, True).astype(ref.dtype)

        @pl.when(jnp.logical_and(in_seg, jnp.logical_not(is_rope)))
        def _(ref=ref):
            ref[...] = tile(0, False).astype(ref.dtype)

    for t, slot, dil in split_tiles:
        @pl.when(j == t)
        def _(t=t, slot=slot, dil=dil):
            z = tile(1 + dils.index(dil), t in rope_tiles)
            split_refs[slot][...] = z.reshape(dil, tm // dil, z.shape[1]).astype(split_refs[slot].dtype)


def _in_proj(x2, g, sc, sh, w_main, w_alr, w2p, a_b, pos, tm, rows_per_mod, segs, rope_tiles, split_tiles,
             decay=True):
    t, d = x2.shape
    r = sc.shape[1]
    wq = a_b.shape[1]
    n_tiles = w_main.shape[1] // TN_IN
    tiles_per_mod = rows_per_mod // tm
    tab_tiles = pos.shape[0] // tm
    nmod = t // rows_per_mod
    dils = sorted({dil for _, _, dil in split_tiles})
    tabs = list(_rope_tables(pos))
    for dil in dils:
        tabs += [tb.reshape(tab_tiles, tm // dil, dil, LANES).transpose(0, 2, 1, 3).reshape(-1, LANES)
                 for tb in tabs[:2]]
    mod_spec = pl.BlockSpec((None, r, d), lambda i, j: (i // tiles_per_mod, 0, 0))
    tab_spec = pl.BlockSpec((tm, LANES), lambda i, j: (i % tab_tiles, 0))
    const = lambda shape: pl.BlockSpec(shape, lambda i, j: (0, 0))
    out_specs, out_shape = [], []
    for lo, hi, dt in segs:
        out_specs.append(pl.BlockSpec((tm, TN_IN), lambda i, j, lo=lo, hi=hi: (i, jnp.clip(j - lo, 0, hi - lo - 1))))
        out_shape.append(jax.ShapeDtypeStruct((t, (hi - lo) * TN_IN), dt))
    if decay:
        out_specs.append(pl.BlockSpec((tm, wq), lambda i, j: (i, 0)))
        out_shape.append(jax.ShapeDtypeStruct((t, wq), F32))
    for _, _, dil in split_tiles:
        out_specs.append(pl.BlockSpec((None, dil, tm // dil, TN_IN),
                                      lambda i, j: (i // tiles_per_mod, 0, i % tiles_per_mod, 0)))
        out_shape.append(jax.ShapeDtypeStruct((nmod, dil, rows_per_mod // dil, TN_IN), BF16))
    return pl.pallas_call(
        functools.partial(_inproj_kernel, segs=tuple((lo, hi) for lo, hi, _ in segs),
                          rope_tiles=tuple(rope_tiles), split_tiles=tuple(split_tiles), decay=decay),
        grid=(t // tm, n_tiles),
        in_specs=[pl.BlockSpec((tm, d), lambda i, j: (i, 0)), const((1, d)), mod_spec, mod_spec,
                  pl.BlockSpec((d, TN_IN), lambda i, j: (0, j)), const((d, LANES)), const((LANES, wq)),
                  const((1, wq))] + [tab_spec] * len(tabs),
        out_specs=out_specs,
        out_shape=out_shape,
        scratch_shapes=[pltpu.VMEM((1 + len(dils), tm, d), BF16),
                        pltpu.VMEM((d // LANES, tm if dils else SUBLANES, LANES), F32)],
        compiler_params=_cparams("arbitrary", "arbitrary"),
        name="in_proj",
    )(x2, g, sc, sh, w_main, w_alr, w2p, a_b, *tabs)


GLA_EXP_CLAMP = 80.0


def _gla_kernel(q_ref, k_ref, v_ref, ra_ref, la_ref, s0_ref, gn_ref, o_ref, sfin_ref, st_scr, *, rows, dk, dv):
    c = pl.program_id(1)
    nc = pl.num_programs(1)
    ch = GLA_CHUNK

    @pl.when(c == 0)
    def _():
        for h in range(GLA_HEADS):
            st_scr[h] = s0_ref[h].T

    ti = lax.broadcasted_iota(jnp.int32, (ch, ch), 0)
    si = lax.broadcasted_iota(jnp.int32, (ch, ch), 1)
    causal = si <= ti
    tri = causal.astype(BF16)
    scale = dk ** -0.5
    wk = GLA_HEADS * dk
    heads = range(GLA_HEADS)
    ks = [slice(h * dk, (h + 1) * dk) for h in heads]
    vs = [slice(h * dv, (h + 1) * dv) for h in heads]

    def chunk(ci, carry):
        rows = pl.ds(pl.multiple_of(ci * ch, ch), ch)
        la = la_ref[rows, :]
        hi = la.astype(BF16)
        r1 = la - hi.astype(F32)
        mid = r1.astype(BF16)
        lo = (r1 - mid.astype(F32)).astype(BF16)
        b3 = _dot(tri, jnp.concatenate([hi, mid, lo], axis=1))
        b = b3[:, :wk] + b3[:, wk:2 * wk] + b3[:, 2 * wk:]
        bl = b[ch - 1:ch, :]
        q = q_ref[rows, :] * scale
        k = k_ref[rows, :]
        qe = (q * jnp.exp(b)).astype(BF16)
        ke = (k * jnp.exp(jnp.minimum(-b, GLA_EXP_CLAMP))).astype(BF16)
        kh = (k * jnp.exp(bl - b)).astype(BF16)
        dec = jnp.exp(bl)
        v = v_ref[rows, :].astype(BF16)
        att = [jnp.where(causal, _dot_nt(qe[:, ks[h]], ke[:, ks[h]]), 0.0).astype(BF16) for h in heads]
        st = [st_scr[h] for h in heads]
        o = [_dot_nt(qe[:, ks[h]], st[h].astype(BF16)) + _dot(att[h], v[:, vs[h]]) for h in heads]
        for h in heads:
            st_scr[h] = st[h] * dec[:, ks[h]] + _dot_tn(v[:, vs[h]], kh[:, ks[h]])
        for h in heads:
            on = _rms(o[h], gn_ref[:, vs[h]])
            o_ref[rows, vs[h]] = (on * _silu(ra_ref[rows, vs[h]])).astype(o_ref.dtype)
        return carry

    lax.fori_loop(0, rows // ch, chunk, 0)

    @pl.when(c == nc - 1)
    def _():
        for h in range(GLA_HEADS):
            sfin_ref[h] = st_scr[h].T


def _gla_prompt(z, zv, la, s0, gn_g, n, l, rows):
    t = n * l
    dk = la.shape[1] // GLA_HEADS
    dv = gn_g.shape[1] // GLA_HEADS
    wk, wv = GLA_HEADS * dk, GLA_HEADS * dv
    cpb = l // rows
    row = lambda b, c: b * cpb + c
    return pl.pallas_call(
        functools.partial(_gla_kernel, rows=rows, dk=dk, dv=dv),
        grid=(n, cpb),
        in_specs=[pl.BlockSpec((rows, wk), lambda b, c: (row(b, c), 0)),
                  pl.BlockSpec((rows, wk), lambda b, c: (row(b, c), 1)),
                  pl.BlockSpec((rows, wv), lambda b, c: (row(b, c), 0)),
                  pl.BlockSpec((rows, wv), lambda b, c: (row(b, c), 1)),
                  pl.BlockSpec((rows, wk), lambda b, c: (row(b, c), 0)),
                  pl.BlockSpec((None, GLA_HEADS, dk, dv), lambda b, c: (b, 0, 0, 0)),
                  pl.BlockSpec((1, wv), lambda b, c: (0, 0))],
        out_specs=[pl.BlockSpec((rows, wv), lambda b, c: (row(b, c), 0)),
                   pl.BlockSpec((None, GLA_HEADS, dk, dv), lambda b, c: (b, 0, 0, 0))],
        out_shape=[jax.ShapeDtypeStruct((t, wv), BF16), jax.ShapeDtypeStruct((n, GLA_HEADS, dk, dv), F32)],
        scratch_shapes=[pltpu.VMEM((GLA_HEADS, dv, dk), F32)],
        compiler_params=_cparams("arbitrary", "arbitrary"),
        name="gla_prompt",
    )(z, z, zv, z, la, s0, gn_g)


GLA_STEP_SEQS = 4


def _gla_step_kernel(q_ref, k_ref, la_ref, v_ref, ra_ref, s_ref, gn_ref, o_ref, so_ref, *, dk):
    for b in range(s_ref.shape[0]):
        sn = jnp.exp(la_ref[b]) * s_ref[b] + k_ref[b] * v_ref[b]
        so_ref[b] = sn
        o = jnp.sum((q_ref[b] * dk ** -0.5) * sn, axis=1, keepdims=True)
        on = _rms(o, gn_ref[...])
        o_ref[b] = (on * _silu(ra_ref[b])).astype(o_ref.dtype)


def _gla_step(q, k, la, v, ra, s, gn_g):
    nb, nh, dk, dv = s.shape
    bb = _pick_tile(nb, GLA_STEP_SEQS)
    col = pl.BlockSpec((bb, nh, dk, 1), lambda b: (b, 0, 0, 0))
    rowv = pl.BlockSpec((bb, nh, 1, dv), lambda b: (b, 0, 0, 0))
    st = pl.BlockSpec((bb, nh, dk, dv), lambda b: (b, 0, 0, 0))
    o, so = pl.pallas_call(
        functools.partial(_gla_step_kernel, dk=dk),
        grid=(nb // bb,),
        in_specs=[col, col, col, rowv, rowv, st, pl.BlockSpec((nh, 1, dv), lambda b: (0, 0, 0))],
        out_specs=[rowv, st],
        out_shape=[jax.ShapeDtypeStruct((nb, nh, 1, dv), BF16), jax.ShapeDtypeStruct(s.shape, F32)],
        compiler_params=_cparams("arbitrary"),
        name="gla_step",
    )(q.reshape(nb, nh, dk, 1), k.reshape(nb, nh, dk, 1), la.reshape(nb, nh, dk, 1),
      v.reshape(nb, nh, 1, dv), ra.reshape(nb, nh, 1, dv), s, gn_g.reshape(nh, 1, dv))
    return o.reshape(nb, nh * dv), so


def _band_kernel(q_ref, k_ref, kp_ref, v_ref, vp_ref, o_ref, lse_ref, *, dil, qb):
    i = pl.program_id(1)
    blk = ATT_BLOCK
    qi = lax.broadcasted_iota(jnp.int32, (blk, 2 * blk), 0)
    ki = lax.broadcasted_iota(jnp.int32, (blk, 2 * blk), 1)
    dist = blk + qi - ki
    band = jnp.logical_and(dist >= 0, dist <= ATT_BLOCK)
    band_first = jnp.logical_and(band, ki >= jnp.where(i > 0, 0, blk))
    scale = ATT_HD ** -0.5
    for r in range(dil):
        for sb in range(qb // blk):
            rs = slice(sb * blk, (sb + 1) * blk)
            q = q_ref[r, rs, :].astype(BF16)
            if sb == 0:
                kprev, vprev, mask = kp_ref[r], vp_ref[r], band_first
            else:
                ps = slice((sb - 1) * blk, sb * blk)
                kprev, vprev, mask = k_ref[r, ps, :], v_ref[r, ps, :], band
            kk = jnp.concatenate([kprev, k_ref[r, rs, :]], axis=0).astype(BF16)
            vv = jnp.concatenate([vprev, v_ref[r, rs, :]], axis=0).astype(BF16)
            s = jnp.where(mask, _dot_nt(q, kk) * scale, NEG_BIG)
            m = jnp.max(s, axis=-1, keepdims=True)
            p = jnp.exp(s - m)
            lsum = jnp.sum(p, axis=-1, keepdims=True)
            o = _dot(p.astype(BF16), vv) / lsum
            lse = jnp.broadcast_to(m + jnp.log(lsum), (blk, ATT_HD))
            rows = rs if dil == 1 else pl.ds(sb * blk * dil + r, blk, stride=dil)
            o_ref[rows, :] = o
            lse_ref[rows, :] = lse


def _band_attention(qa, ka, va, cq, ck, cv, n, l, dil):
    lq = l // dil
    hd = ATT_HD
    tokens = min(ATT_TOKENS, l)
    qb = tokens // dil
    nqb = lq // qb
    sub = qb // ATT_BLOCK

    def main(c0):
        return pl.BlockSpec((None, dil, qb, hd), lambda b, i, h: (b, 0, i, c0 + h))

    def prev(c0):
        return pl.BlockSpec((None, dil, ATT_BLOCK, hd), lambda b, i, h: (b, 0, jnp.maximum(i * sub - 1, 0), c0 + h))

    out = pl.BlockSpec((tokens, hd), lambda b, i, h: (b * nqb + i, h))
    shp = jax.ShapeDtypeStruct((n * l, ATT_HEADS * hd), F32)
    return pl.pallas_call(
        functools.partial(_band_kernel, dil=dil, qb=qb),
        grid=(n, nqb, ATT_HEADS),
        in_specs=[main(cq), main(ck), prev(ck), main(cv), prev(cv)],
        out_specs=[out, out],
        out_shape=[shp, shp],
        compiler_params=_cparams("arbitrary", "arbitrary", "arbitrary"),
        name=f"band_attention_d{dil}",
    )(qa, ka, ka, va, va)


ATT_STEP_SEQS = 4


def _att_step_kernel(q_ref, kn_ref, vn_ref, c0_ref, c1_ref, c2_ref, o_ref):
    scale = ATT_HD ** -0.5
    add = lambda a, b: a + b
    for b in range(q_ref.shape[0]):
        outs, lses = [], []
        for g, c_ref in enumerate((c0_ref, c1_ref, c2_ref)):
            q, kn, vn = q_ref[b, g], kn_ref[b, g], vn_ref[b, g]
            s = jnp.sum(c_ref[b, :, 0] * q[None], axis=-1, keepdims=True) * scale
            sn = jnp.sum(kn * q, axis=-1, keepdims=True) * scale
            m = jnp.maximum(jnp.max(s, axis=0), sn)
            p = jnp.exp(s - m[None])
            pn = jnp.exp(sn - m)
            lsum = jnp.sum(p, axis=0) + pn
            outs.append((jnp.sum(p * c_ref[b, :, 1], axis=0) + pn * vn) / lsum)
            lses.append(m + jnp.log(lsum))
        mx = functools.reduce(jnp.maximum, lses)
        es = [jnp.exp(x - mx) for x in lses]
        inv = 1.0 / functools.reduce(add, es)
        o_ref[b] = functools.reduce(add, [e * inv * o for e, o in zip(es, outs)]).astype(o_ref.dtype)


def _att_step(qs, ks, vs, caches):
    nb = qs.shape[0]
    bb = _pick_tile(nb, ATT_STEP_SEQS)
    views, cspecs = [], []
    for (window, dil), cch in zip(ATT_GROUPS, caches):
        assert cch.shape[1] == window, "sample step expects full caches"
        views.append(cch.reshape((nb, window // dil, dil) + cch.shape[2:]))
        cspecs.append(pl.BlockSpec((bb, ATT_BLOCK, None) + cch.shape[2:], lambda b: (b, 0, 0, 0, 0, 0)))
    row = pl.BlockSpec((bb,) + qs.shape[1:], lambda b: (b, 0, 0, 0))
    o = pl.pallas_call(
        _att_step_kernel,
        grid=(nb // bb,),
        in_specs=[row] * 3 + cspecs,
        out_specs=pl.BlockSpec((bb, ATT_HEADS, ATT_HD), lambda b: (b, 0, 0)),
        out_shape=jax.ShapeDtypeStruct((nb, ATT_HEADS, ATT_HD), BF16),
        compiler_params=_cparams("arbitrary"),
        name="att_step",
    )(qs, ks, vs, *views)
    return o.reshape(nb, ATT_HEADS * ATT_HD)


MERGE_PART = 256


def _merge_kernel(*refs, n_att, n_alias, n_tiles):
    oa_ref = refs[0]
    att = refs[1:1 + n_att]
    (gza_ref, gzb_ref, x_ref, g1_ref, sc2_ref, sh2_ref, n2_ref, wa_ref, wb_ref, wo_ref,
     wq_ref) = refs[1 + n_att:12 + n_att]
    x1_ref, h2_ref, pq_ref = refs[12 + n_att + n_alias:]
    tm = x_ref.shape[0]
    pr = min(MERGE_PART, tm)
    parts = [slice(k * pr, (k + 1) * pr) for k in range(tm // pr)]
    mod = lambda ref, part: ref[...] if ref.shape[0] == 1 else ref[part, :]
    add = lambda a, b: a + b

    def attention_out(part):
        if n_att == 1:
            return att[0][part, :]
        ng = n_att // 2
        ls = [r[part, :] for r in att[ng:]]
        mx = functools.reduce(jnp.maximum, ls)
        es = [jnp.exp(x - mx) for x in ls]
        inv = 1.0 / functools.reduce(add, es)
        return functools.reduce(add, [e * inv * r[part, :] for e, r in zip(es, att[:ng])]).astype(BF16)

    @pl.when(pl.program_id(0) >= n_tiles)
    def _():
        h2_ref[...] = jnp.zeros_like(h2_ref)
        pq_ref[...] = jnp.zeros_like(pq_ref)

    @pl.when(pl.program_id(0) < n_tiles)
    def _():
        _merge_rows(parts, attention_out, mod, oa_ref, gza_ref, gzb_ref, x_ref, g1_ref, sc2_ref, sh2_ref, n2_ref,
                    wa_ref, wb_ref, wo_ref, wq_ref, x1_ref, h2_ref, pq_ref)
        if h2_ref.shape[0] > tm:
            h2_ref[tm:] = jnp.zeros((h2_ref.shape[0] - tm, h2_ref.shape[1]), h2_ref.dtype)
            pq_ref[tm:] = jnp.zeros((pq_ref.shape[0] - tm, pq_ref.shape[1]), pq_ref.dtype)


def _merge_rows(parts, attention_out, mod, oa_ref, gza_ref, gzb_ref, x_ref, g1_ref, sc2_ref, sh2_ref, n2_ref,
                wa_ref, wb_ref, wo_ref, wq_ref, x1_ref, h2_ref, pq_ref):
    ob = [attention_out(p) for p in parts]
    ma = [_dot(oa_ref[p, :], wa_ref[...]) for p in parts]
    mb = [_dot(o, wb_ref[...]) for o in ob]
    merged = [(_sigmoid(gza_ref[p, :]) * a + _sigmoid(gzb_ref[p, :]) * b).astype(BF16)
              for p, a, b in zip(parts, ma, mb)]
    y = [_dot(m, wo_ref[...]) for m in merged]
    h2 = []
    for p, yy in zip(parts, y):
        x1 = x_ref[p, :] + mod(g1_ref, p) * yy
        x1_ref[p, :] = x1
        h2.append((_rms(x1, n2_ref[...]) * (1.0 + mod(sc2_ref, p)) + mod(sh2_ref, p)).astype(BF16))
    for p, h in zip(parts, h2):
        h2_ref[p, :] = h
        pq_ref[p, :] = _dot(h, wq_ref[...])


def _merge(oa, att, z, x2, g1, sc2, sh2, n2g, wa, wb, wo, wq, tm, rows_per_mod, gza_blk, gzb_blk,
           peer_rows, peer_tile, peer_blk0, alias=(), zero_tiles=0):
    t, d = x2.shape
    r = g1.shape[1]
    wqn = wq.shape[1]
    tiles_per_mod = rows_per_mod // tm
    n_tiles = t // tm
    row = lambda i: jnp.minimum(i, n_tiles - 1)
    tok = lambda w: pl.BlockSpec((tm, w), lambda i: (row(i), 0))
    mod = pl.BlockSpec((None, r, d), lambda i: (row(i) // tiles_per_mod, 0, 0))
    const = lambda a: pl.BlockSpec(a.shape, lambda i: (0, 0), pipeline_mode=pl.Buffered(1))
    peer = lambda w: pl.BlockSpec((peer_tile, w), lambda i: (peer_blk0 + i, 0))
    n_in = 12 + len(att)
    return pl.pallas_call(
        functools.partial(_merge_kernel, n_att=len(att), n_alias=len(alias), n_tiles=n_tiles),
        grid=(n_tiles + zero_tiles,),
        in_specs=[tok(oa.shape[1])] + [tok(a.shape[1]) for a in att]
                 + [pl.BlockSpec((tm, d), lambda i: (row(i), gza_blk)),
                    pl.BlockSpec((tm, d), lambda i: (row(i), gzb_blk)),
                    tok(d), mod, mod, mod, const(n2g), const(wa), const(wb), const(wo), const(wq)]
                 + [pl.BlockSpec(memory_space=pl.ANY)] * len(alias),
        out_specs=[tok(d), peer(d), peer(wqn)],
        out_shape=[jax.ShapeDtypeStruct((t, d), F32), jax.ShapeDtypeStruct((peer_rows, d), BF16),
                   jax.ShapeDtypeStruct((peer_rows, wqn), F32)],
        input_output_aliases={n_in + k: 1 + k for k in range(len(alias))},
        compiler_params=_cparams("arbitrary"),
        name="merge",
    )(oa, *att, z, z, x2, g1, sc2, sh2, n2g, wa, wb, wo, wq, *alias)


def _strict_max_below(rows, prev):
    m = None
    for x in rows:
        y = jnp.where(x < prev, x, NEG_BIG) if prev is not None else x
        m = y if m is None else jnp.maximum(m, y)
    return m


def _oddeven_merge(lo, hi, r):
    step = r * 2
    if step < hi - lo:
        yield from _oddeven_merge(lo, hi, step)
        yield from _oddeven_merge(lo + r, hi, step)
        yield from [(i, i + r) for i in range(lo + r, hi - r, step)]
    else:
        yield (lo, lo + r)


def _oddeven_merge_sort(lo, hi):
    if hi - lo >= 1:
        mid = lo + (hi - lo) // 2
        yield from _oddeven_merge_sort(lo, mid)
        yield from _oddeven_merge_sort(mid + 1, hi)
        yield from _oddeven_merge(lo, hi, 1)


def _exchange(p, i, j):
    p[i], p[j] = jnp.maximum(p[i], p[j]), jnp.minimum(p[i], p[j])


def _topk_desc(s, k):
    assert s.shape[0] == k * SUBLANES and k & (k - 1) == 0
    p = [s[i * SUBLANES:(i + 1) * SUBLANES] for i in range(k)]
    for i, j in _oddeven_merge_sort(0, k - 1):
        _exchange(p, i, j)
    shift = SUBLANES // 2
    while shift:
        q = [pltpu.roll(v, shift, 0) for v in p]
        p = [jnp.maximum(p[i], q[k - 1 - i]) for i in range(k)]
        stride = k // 2
        while stride:
            for i in range(k):
                if not i & stride:
                    _exchange(p, i, i + stride)
            stride //= 2
        shift //= 2
    return [v[0:1] for v in p]


def _route_kernel(pq_ref, k1_ref, k2_ref, n1_ref, e1_ref, r2_ref, e2_ref, s1_scr, s2_scr):
    nk = PEER_NKEYS
    pq = pq_ref[...].astype(BF16)
    s1_scr[...] = _dot_nt(k1_ref[...], pq)
    s2_scr[...] = _dot_nt(k2_ref[...], pq)
    v1h, v2h = [], []
    for h in range(PEER_HEADS):
        v1h.append(_topk_desc(s1_scr[h * nk:(h + 1) * nk], PEER_TOPK))
        v2h.append(_topk_desc(s2_scr[h * nk:(h + 1) * nk], PEER_TOPK))
    v1 = [jnp.concatenate([v1h[h][k] for h in range(PEER_HEADS)], axis=0) for k in range(PEER_TOPK)]
    v2 = [jnp.concatenate([v2h[h][k] for h in range(PEER_HEADS)], axis=0) for k in range(PEER_TOPK)]
    pairs = [(i, j) for i in range(PEER_TOPK) for j in range(PEER_TOPK) if (i + 1) * (j + 1) <= PEER_TOPK]
    cands = [v1[i] + v2[j] for i, j in pairs]
    tau = None
    for _ in range(PEER_TOPK):
        tau = _strict_max_below(cands, tau)
    cmax = v1[0] + v2[0]
    zsum = None
    counts = [None] * PEER_TOPK
    for (i, _), cnd in zip(pairs, cands):
        sel = cnd >= tau
        e = jnp.where(sel, jnp.exp(cnd - cmax), 0.0)
        zsum = e if zsum is None else zsum + e
        one = jnp.where(sel, 1.0, 0.0)
        counts[i] = one if counts[i] is None else counts[i] + one
    zinv = 1.0 / zsum
    for h in range(PEER_HEADS):
        a = s1_scr[h * nk:(h + 1) * nk]
        b = s2_scr[h * nk:(h + 1) * nk]
        n1 = jnp.zeros_like(a)
        r2 = jnp.full_like(b, float(PEER_TOPK))
        for k in range(PEER_TOPK):
            n1 = jnp.where(a == v1h[h][k], counts[k][h:h + 1, :], n1)
            r2 = jnp.where(b == v2h[h][k], float(k), r2)
        n1_ref[h] = n1
        e1_ref[h] = jnp.exp(a - v1h[h][0]) * zinv[h:h + 1, :]
        r2_ref[h] = r2.astype(r2_ref.dtype)
        e2_ref[h] = jnp.exp(b - v2h[h][0]).astype(e2_ref.dtype)


def _peer_route(pq, k1big, k2big, tt):
    t, w = pq.shape
    nh, nk = PEER_HEADS, PEER_NKEYS
    sspec = pl.BlockSpec((nh, nk, tt), lambda i: (0, 0, i))
    shp = lambda dt: jax.ShapeDtypeStruct((nh, nk, t), dt)
    return pl.pallas_call(
        _route_kernel,
        grid=(t // tt,),
        in_specs=[pl.BlockSpec((tt, w), lambda i: (i, 0)),
                  pl.BlockSpec(k1big.shape, lambda i: (0, 0)), pl.BlockSpec(k2big.shape, lambda i: (0, 0))],
        out_specs=[sspec] * 4,
        out_shape=[shp(F32), shp(F32), shp(BF16), shp(BF16)],
        scratch_shapes=[pltpu.VMEM((nh * nk, tt), F32)] * 2,
        compiler_params=_cparams("arbitrary"),
        name="peer_route",
    )(pq, k1big, k2big)


BF16_ROWS = 16
GATE_LANES = 256
PEER_PART = 256
PEER_TILE = 2048


def _gelu_tanh_bf16(x):
    hx = 0.5 * x
    return hx + hx * jnp.tanh(x * (GELU_C + (GELU_C * 0.044715) * (x * x)))


def _bf16_rows(row):
    return jnp.broadcast_to(row, (BF16_ROWS, row.shape[1])).astype(BF16)


CACHE_CHUNK = 512


def _cache_shift_jobs(caches):
    nb = caches[0].shape[0]
    chunks, base = [], 0
    for c, cache in enumerate(caches):
        w = cache.shape[1]
        for r0 in range(0, w, CACHE_CHUNK):
            n = min(CACHE_CHUNK, w - r0)
            chunks.append((c, r0, n, r0 + n == w, base))
            base += nb
    return chunks, base


def _cache_shift_step(step, caches, new_rows, outs, buf, sems):
    nb = caches[0].shape[0]
    chunks, n_jobs = _cache_shift_jobs(caches)

    def for_job(job, act):
        for c, r0, n, last, base in chunks:
            @pl.when(jnp.logical_and(job >= base, job < base + nb))
            def _(c=c, r0=r0, n=n, last=last, base=base):
                b, slot = job - base, job % 2
                n_in = n - 1 if last else n
                rd = lambda: pltpu.make_async_copy(caches[c].at[b, pl.ds(r0 + 1, n_in)],
                                                   buf.at[slot, pl.ds(0, n_in)], sems.at[slot, 0])
                wr = lambda: pltpu.make_async_copy(buf.at[slot, pl.ds(0, n)], outs[c].at[b, pl.ds(r0, n)],
                                                   sems.at[slot, 1])
                act(rd, wr, lambda: new_rows[c][b] if last else None, slot, n_in)

    def finish_read_start_write(rd, wr, new_row, slot, n_in):
        rd().wait()
        row = new_row()
        if row is not None:
            buf[slot, pl.ds(n_in, 1)] = row
        wr().start()

    @pl.when(jnp.logical_and(step >= 1, step <= n_jobs))
    def _():
        for_job(step - 1, finish_read_start_write)

    @pl.when(jnp.logical_and(step >= 2, step <= n_jobs + 1))
    def _():
        for_job(step - 2, lambda rd, wr, *_: wr().wait())

    @pl.when(step < n_jobs)
    def _():
        for_job(step, lambda rd, wr, *_: rd().start())


def _peer_dense_kernel(h2_ref, u_ref, vt_ref, n1_ref, e1_ref, r2_ref, e2_ref, h2_next_ref, u_next_ref, *rest,
                       et, n_kv):
    caches, new_rows = rest[:n_kv], rest[n_kv:2 * n_kv]
    o_ref, kv_outs = rest[2 * n_kv], rest[2 * n_kv + 1:3 * n_kv + 1]
    acc_scr, wg_scr, st_scr, st0_scr = rest[3 * n_kv + 1:3 * n_kv + 5]
    e = pl.program_id(1)
    nk = PEER_NKEYS
    tt = h2_ref.shape[0]
    step = pl.program_id(0) * pl.num_programs(1) + e
    if n_kv:
        _cache_shift_step(step, caches, new_rows, kv_outs, rest[-2], rest[-1])

    @pl.when(step == 0)
    def _():
        st0_scr[...] = _dot_nt(u_ref[0:PEER_PART, :], h2_ref[...]).astype(BF16)

    @pl.when(e == 0)
    def _():
        acc_scr[...] = jnp.zeros_like(acc_scr)

    zero = jnp.zeros((), BF16)
    groups = nk // BF16_ROWS

    def scores(part):
        st_scr[part, :] = _dot_nt(u_ref[part, :], h2_ref[...]).astype(BF16)

    def gates(ii):
        rs = slice(ii * nk, (ii + 1) * nk)
        for lc in range(tt // GATE_LANES):
            ls = slice(lc * GATE_LANES, (lc + 1) * GATE_LANES)
            w = None
            for h in range(PEER_HEADS):
                n1 = _bf16_rows(n1_ref[h, ii:ii + 1, ls])
                e1 = _bf16_rows(e1_ref[h, ii:ii + 1, ls])
                r2 = r2_ref[h, :, ls].reshape(groups, BF16_ROWS, GATE_LANES)
                e2 = e2_ref[h, :, ls].reshape(groups, BF16_ROWS, GATE_LANES)
                c = jnp.where(r2 < n1[None], e2, zero) * e1[None]
                w = c if w is None else w + c
            st = st0_scr if rs.stop <= PEER_PART else st_scr
            wg_scr[rs, ls] = w.reshape(nk, GATE_LANES) * _gelu_tanh_bf16(st[rs, ls])

    def mix(part):
        return _dot(vt_ref[:, part], wg_scr[part, :])

    parts = [slice(p * PEER_PART, (p + 1) * PEER_PART) for p in range(et // PEER_PART)]
    out = None
    next_scores = None
    for p, part in enumerate(parts):
        if p + 1 < len(parts):
            scores(parts[p + 1])
        else:
            next_scores = _dot_nt(u_next_ref[...], h2_next_ref[...]).astype(BF16)
        for ii in range(part.start // nk, part.stop // nk):
            gates(ii)
        if p > 0:
            d = mix(parts[p - 1])
            out = d if out is None else out + d
    st0_scr[...] = next_scores
    d = mix(parts[-1])
    acc_scr[...] += d if out is None else out + d

    @pl.when(e == pl.num_programs(1) - 1)
    def _():
        o_ref[...] = acc_scr[...].T


def _peer_dense(h2, u_b, vt_b, n1, e1, r2, e2, tt, et, caches=(), new_rows=()):
    t, d = h2.shape
    ne = u_b.shape[0]
    nh, nk = PEER_HEADS, PEER_NKEYS
    n_kv = len(caches)
    grid = (t // tt, ne // et)
    if n_kv:
        assert grid[0] * grid[1] >= _cache_shift_jobs(caches)[1] + 2, "one cache-shift job per grid step"
    rowside = pl.BlockSpec((nh, et // nk, tt), lambda i, e: (0, e, i))
    colside = pl.BlockSpec((nh, nk, tt), lambda i, e: (0, 0, i))
    anywhere = pl.BlockSpec(memory_space=pl.ANY)
    whole = lambda a: pl.BlockSpec(a.shape, lambda i, e: (0,) * a.ndim)
    outs = pl.pallas_call(
        functools.partial(_peer_dense_kernel, et=et, n_kv=n_kv),
        grid=grid,
        in_specs=[pl.BlockSpec((tt, d), lambda i, e: (i, 0)),
                  pl.BlockSpec((et, d), lambda i, e: (e, 0)),
                  pl.BlockSpec((d, et), lambda i, e: (0, e)),
                  rowside, rowside, colside, colside,
                  pl.BlockSpec((tt, d), lambda i, e: (jnp.minimum(i + (e + 1) // grid[1], grid[0] - 1), 0)),
                  pl.BlockSpec((PEER_PART, d), lambda i, e: (((e + 1) % grid[1]) * (et // PEER_PART), 0))]
                 + [anywhere] * n_kv + [whole(a) for a in new_rows],
        out_specs=[pl.BlockSpec((tt, d), lambda i, e: (i, 0))] + [anywhere] * n_kv,
        out_shape=[jax.ShapeDtypeStruct((t, d), F32)] + [jax.ShapeDtypeStruct(c.shape, c.dtype) for c in caches],
        scratch_shapes=[pltpu.VMEM((d, tt), F32), pltpu.VMEM((et, tt), BF16), pltpu.VMEM((et, tt), BF16),
                        pltpu.VMEM((PEER_PART, tt), BF16)]
                       + ([pltpu.VMEM((2, CACHE_CHUNK) + caches[0].shape[2:], caches[0].dtype),
                           pltpu.SemaphoreType.DMA((2, 2))] if n_kv else []),
        compiler_params=_cparams("arbitrary", "arbitrary"),
        name="peer_dense",
    )(h2, u_b, vt_b, n1, e1, r2, e2, h2, u_b, *caches, *new_rows)
    return outs[0], outs[1:]


def _final_kernel(x1_ref, p_ref, g2_ref, fg_ref, y_ref):
    y_ref[...] = _rms(x1_ref[...] + g2_ref[...] * p_ref[...], fg_ref[...])


def _final(x1, p, g2, fg, tm, rows_per_mod, p_blk0):
    t, d = x1.shape
    r = g2.shape[1]
    tiles_per_mod = rows_per_mod // tm
    tok = pl.BlockSpec((tm, d), lambda i: (i, 0))
    return pl.pallas_call(
        _final_kernel,
        grid=(t // tm,),
        in_specs=[tok, pl.BlockSpec((tm, d), lambda i: (p_blk0 + i, 0)),
                  pl.BlockSpec((None, r, d), lambda i: (i // tiles_per_mod, 0, 0)),
                  pl.BlockSpec((1, d), lambda i: (0, 0))],
        out_specs=tok,
        out_shape=jax.ShapeDtypeStruct((t, d), F32),
        compiler_params=_cparams("arbitrary"),
        name="final_norm",
    )(x1, p, g2, fg)


def _rope_tables(pos):
    half = ROPE_DIM // 2
    inv_freq = ROPE_THETA ** (-jnp.arange(half, dtype=F32) / half)
    ang = pos.astype(F32)[:, None] * inv_freq[None, :]
    cos, sin = jnp.cos(ang), jnp.sin(ang)
    n = pos.shape[0]
    ones = jnp.ones((n, LANES - ROPE_DIM), F32)
    zeros = jnp.zeros((n, LANES - ROPE_DIM), F32)
    return (jnp.concatenate([cos, cos, ones], axis=1), jnp.concatenate([-sin, sin, zeros], axis=1))


def _block_diag_keys(keys, half):
    nk, dh = keys.shape
    eye = jnp.eye(PEER_HEADS, dtype=keys.dtype)
    blk = jnp.zeros((PEER_HEADS, nk, PEER_HEADS, 2, dh), keys.dtype)
    blk = blk.at[:, :, :, half, :].set(eye[:, None, :, None] * keys[None, :, None, :])
    return blk.reshape(PEER_HEADS * nk, PEER_HEADS * 2 * dh).astype(BF16)


def _pick_tile(n, pref):
    t = min(pref, n)
    while n % t:
        t //= 2
    return t


def kernel(x_prompt, x_sample, c_prompt, c_sample, state_gla, cache_kv_w128, cache_kv_w512, cache_kv_w2048,
           ada_w, ada_b, norm1_g, w_in, gla_a_w2, gla_a_b, gla_gn_g, w_branch_a, w_branch_b, w_out,
           norm2_g, peer_wq, peer_k1, peer_k2, peer_u, peer_v, final_g):
    depth = ada_w.shape[0]
    assert depth == 1, "single-layer trunk"
    n_p, l_p, d = x_prompt.shape
    n_s, l_s, _ = x_sample.shape
    assert l_s == 1, "sample group decodes one token per sequence"
    caches = (cache_kv_w128[0], cache_kv_w512[0], cache_kv_w2048[0])
    n_groups = len(ATT_GROUPS)

    w_qa = gla_a_w2.shape[2]
    w_va = gla_gn_g.shape[1]
    w_ob = ATT_HEADS * ATT_HD
    w_qb = n_groups * w_ob
    assert w_qa == TN_IN and w_va == 2 * TN_IN and d == 2 * TN_IN and w_ob == TN_IN
    widths = (w_qa, w_qa, w_va, w_va, GLA_RANK, w_qb, w_qb, w_qb, d, d)
    offs = [0]
    for w in widths:
        offs.append(offs[-1] + w)
    seg = lambda i: w_in[0][:, offs[i]:offs[i + 1]]
    att_col = lambda g, k: seg(5 + k)[:, g * w_ob:(g + 1) * w_ob]
    att_cols = [att_col(g, k) for g in range(n_groups) for k in range(3)]
    w_main = jnp.concatenate([seg(0), seg(1), seg(3), seg(8), seg(9), seg(2)] + att_cols, axis=1).astype(BF16)
    w_kv = jnp.concatenate([att_col(g, k) for g in range(n_groups) for k in (1, 2)], axis=1).astype(BF16)
    n_tiles = w_main.shape[1] // TN_IN
    ra_off, gza_blk, gzb_blk = 2 * w_qa, 2, 3
    w_alr = jnp.pad(seg(4), ((0, 0), (0, LANES - GLA_RANK))).astype(BF16)
    w2p = jnp.pad(gla_a_w2[0], ((0, LANES - GLA_RANK), (0, 0))).astype(BF16)
    a_b = gla_a_b[0].reshape(1, w_qa)
    g1n = norm1_g[0].reshape(1, d)
    gn_g = gla_gn_g[0].reshape(1, w_va)
    n2g = norm2_g[0].reshape(1, d)
    wa, wb = w_branch_a[0].astype(BF16), w_branch_b[0].astype(BF16)
    wo, wq = w_out[0].astype(BF16), peer_wq[0].astype(BF16)
    q_tile = lambda g: ATT_TILE0 + 3 * g
    rope_tiles = [q_tile(g) + k for g in range(n_groups) for k in (0, 1)]

    n_mod = n_p + n_s
    n_pad = -(-n_mod // SUBLANES) * SUBLANES
    c_all = jnp.concatenate([c_prompt, c_sample, jnp.zeros((n_pad - n_mod, d), F32)], axis=0)
    mod = _modulation(c_all, ada_w[0], ada_b[0])
    mods_p = [mod[:n_p, i * d:(i + 1) * d].reshape(n_p, 1, d) for i in range(6)]
    mods_s = [mod[n_p:n_mod, i * d:(i + 1) * d].reshape(1, n_s, d) for i in range(6)]

    t_p = n_p * l_p
    tt = _pick_tile(t_p, 512)
    t_all = t_p + (-(-n_s // tt)) * tt
    assert t_all - t_p == tt

    xp2 = x_prompt.reshape(t_p, d)
    tm_p = _pick_tile(l_p, 1024)
    pos_p = jnp.arange(l_p, dtype=jnp.int32)
    split = []
    for g, (window, dil) in enumerate(ATT_GROUPS):
        assert window // dil == ATT_BLOCK
        if dil > 1:
            split += [(q_tile(g) + k, len(split) + k, dil) for k in range(3)]
    first_split = min([t for t, _, _ in split], default=n_tiles)
    outs = _in_proj(xp2, g1n, mods_p[1], mods_p[0], w_main, w_alr, w2p, a_b, pos_p, tm_p, l_p,
                    ((0, VA_TILE0, F32), (VA_TILE0, first_split, BF16)), rope_tiles, split)
    z_p, zb_p, la_p, split_p = outs[0], outs[1], outs[2], outs[3:]
    s0_p = jnp.zeros((n_p,) + state_gla.shape[2:], F32)
    oa_p, sfin_p = _gla_prompt(z_p, zb_p, la_p, s0_p, gn_g, n_p, l_p, _pick_tile(l_p, 512))
    zb4 = zb_p.reshape(n_p, 1, l_p, zb_p.shape[1])
    att_o, att_l = [], []
    for g, (window, dil) in enumerate(ATT_GROUPS):
        if dil > 1:
            qa, ka, va = [split_p[slot] for t, slot, _ in split if q_tile(g) <= t < q_tile(g) + 3]
            o_g, lse_g = _band_attention(qa, ka, va, 0, 0, 0, n_p, l_p, dil)
        else:
            c0 = lambda k: (q_tile(g) + k - VA_TILE0) * (TN_IN // ATT_HD)
            o_g, lse_g = _band_attention(zb4, zb4, zb4, c0(0), c0(1), c0(2), n_p, l_p, dil)
        att_o.append(o_g)
        att_l.append(lse_g)
    keep_max = min(max(w for w, _ in ATT_GROUPS), l_p)
    x_tail = x_prompt[:, l_p - keep_max:].reshape(n_p * keep_max, d)
    kv_tiles = 2 * n_groups
    z_kv = _in_proj(x_tail, g1n, mods_p[1], mods_p[0], w_kv, w_alr, w2p, a_b, pos_p[l_p - keep_max:],
                    _pick_tile(keep_max, 1024), keep_max, ((0, kv_tiles, F32),), range(0, kv_tiles, 2), [],
                    decay=False)[0]
    z_kv = z_kv.reshape(n_p, keep_max, kv_tiles * TN_IN)
    kv_p = []
    for g, (window, dil) in enumerate(ATT_GROUPS):
        keep = min(window, l_p)
        rows = z_kv[:, keep_max - keep:, 2 * g * TN_IN:(2 * g + 2) * TN_IN]
        kv_p.append(rows.reshape(1, n_p, keep, 2, ATT_HEADS, ATT_HD))
    tm_m = _pick_tile(l_p, 512)
    x1_p, h2_all, pq_all = _merge(oa_p, att_o + att_l, z_p, xp2, mods_p[2], mods_p[4], mods_p[3], n2g,
                                  wa, wb, wo, wq, tm_m, l_p, gza_blk, gzb_blk, t_all, tm_m, 0,
                                  zero_tiles=(t_all - t_p) // tm_m)

    xs2 = x_sample.reshape(n_s, d)
    pos_s = jnp.full((n_s,), PAST_LEN, dtype=jnp.int32)
    z_s, la_s = _in_proj(xs2, g1n, mods_s[1], mods_s[0], w_main, w_alr, w2p, a_b, pos_s, n_s, n_s,
                         ((0, n_tiles, F32),), rope_tiles, [])
    col = lambda tile: z_s[:, tile * TN_IN:(tile + 1) * TN_IN]
    oa_s, state_s = _gla_step(col(0), col(1), la_s, z_s[:, VA_TILE0 * TN_IN:VA_TILE0 * TN_IN + w_va],
                              z_s[:, ra_off:ra_off + w_va], state_gla[0], gn_g)
    stack = lambda k: jnp.stack([col(q_tile(g) + k) for g in range(n_groups)], axis=1).reshape(
        n_s, n_groups, ATT_HEADS, ATT_HD)
    ob_s = _att_step(stack(0), stack(1), stack(2), caches)
    x1_s, h2_all, pq_all = _merge(oa_s, [ob_s], z_s, xs2, mods_s[2], mods_s[4], mods_s[3], n2g,
                                  wa, wb, wo, wq, n_s, n_s, gza_blk, gzb_blk, t_all, tt, t_p // tt,
                                  alias=(h2_all, pq_all))
    row_tiles = (2 * w_ob // LANES, LANES)
    new_rows = [jnp.concatenate([col(q_tile(g) + 1), col(q_tile(g) + 2)], axis=1).reshape((n_s, 1) + row_tiles)
                for g in range(n_groups)]

    k1big = _block_diag_keys(peer_k1[0], 0)
    k2big = _block_diag_keys(peer_k2[0], 1)
    n1, e1, r2, e2 = _peer_route(pq_all, k1big, k2big, _pick_tile(tt, 256))
    u_b = peer_u[0].astype(BF16)
    vt_b = peer_v[0].astype(BF16).T
    p_all, kv_s = _peer_dense(h2_all, u_b, vt_b, n1, e1, r2, e2, tt, PEER_TILE,
                              [c.reshape(c.shape[:2] + row_tiles) for c in caches], new_rows)
    kv_s = [a.reshape(c.shape)[None] for a, c in zip(kv_s, caches)]

    fg = final_g.reshape(1, d)
    y_p = _final(x1_p, p_all, mods_p[5], fg, tm_m, l_p, 0)
    y_s = _final(x1_s, p_all, mods_s[5], fg, n_s, n_s, t_p // n_s)

    return (y_p.reshape(n_p, l_p, d), y_s.reshape(n_s, l_s, d), sfin_p[None], state_s[None],
            kv_p[0], kv_s[0], kv_p[1], kv_s[1], kv_p[2], kv_s[2])
```

```python
import functools
import math

import jax
import jax.numpy as jnp
from jax import lax
from jax.experimental import pallas as pl
from jax.experimental.pallas import tpu as pltpu

F32 = jnp.float32
BF16 = jnp.bfloat16

PAST_LEN = 16384
GLA_HEADS = 4
GLA_RANK = 16
GLA_TAU = 16.0
GLA_CHUNK = 128
ATT_GROUPS = ((128, 1), (512, 4), (2048, 16))
ATT_HEADS = 4
ATT_HD = 128
ATT_BLOCK = 128
ATT_TOKENS = 2048
ROPE_DIM = ATT_HD // 4
ROPE_THETA = 500000.0
PEER_HEADS = 8
PEER_NKEYS = 128
PEER_TOPK = 16
NORM_EPS = 1e-6
GELU_C = math.sqrt(2.0 / math.pi)

LANES = 128
SUBLANES = 8
VMEM_LIMIT = 56 * 1024 * 1024
NEG_BIG = -1e30


def _cparams(*sem):
    return pltpu.CompilerParams(dimension_semantics=sem, vmem_limit_bytes=VMEM_LIMIT)


def _sigmoid(x):
    return 1.0 / (1.0 + jnp.exp(-x))


def _silu(x):
    return x * _sigmoid(x)


def _log_sigmoid(x):
    return jnp.minimum(x, 0.0) - jnp.log(1.0 + jnp.exp(-jnp.abs(x)))


def _rms(x, g):
    return x * lax.rsqrt(jnp.mean(x * x, axis=-1, keepdims=True) + NORM_EPS) * g


def _dot(a, b):
    return jnp.dot(a, b, preferred_element_type=F32)


def _dot_nt(a, b):
    return lax.dot_general(a, b, (((1,), (1,)), ((), ())), preferred_element_type=F32)


def _dot_tn(a, b):
    return lax.dot_general(a, b, (((0,), (0,)), ((), ())), preferred_element_type=F32)


def _mod_kernel(c_ref, w_ref, b_ref, o_ref):
    s = _silu(c_ref[...]).astype(BF16)
    o_ref[...] = _dot(s, w_ref[...].astype(BF16)) + b_ref[...]


def _modulation(c, ada_w, ada_b):
    n, d = c.shape
    n6 = ada_w.shape[1]
    tn = n6 // 4
    return pl.pallas_call(
        _mod_kernel,
        grid=(n6 // tn,),
        in_specs=[pl.BlockSpec((n, d), lambda j: (0, 0)),
                  pl.BlockSpec((d, tn), lambda j: (0, j)),
                  pl.BlockSpec((1, tn), lambda j: (0, j))],
        out_specs=pl.BlockSpec((n, tn), lambda j: (0, j)),
        out_shape=jax.ShapeDtypeStruct((n, n6), F32),
        compiler_params=_cparams("arbitrary"),
        name="modulation",
    )(c, ada_w, ada_b.reshape(1, n6))


TN_IN = 512
VA_TILE0 = 8
ATT_TILE0 = 10


def _rope_tile(z, c, s):
    reps = z.shape[1] // LANES
    n = z.shape[1]
    half = ROPE_DIM // 2
    lane = lax.broadcasted_iota(jnp.int32, (1, LANES), 1)
    first = jnp.concatenate([lane < half] * reps, axis=1)
    partner = jnp.where(first, pltpu.roll(z, n - half, 1), pltpu.roll(z, half, 1))
    return z * jnp.concatenate([c] * reps, axis=1) + partner * jnp.concatenate([s] * reps, axis=1)


def _inproj_kernel(x_ref, g_ref, sc_ref, sh_ref, w_ref, walr_ref, w2_ref, ab_ref, *rest,
                   segs, rope_tiles, split_tiles, decay):
    dils = sorted({dil for _, _, dil in split_tiles})
    n_seg, n_tab = len(segs), 2 * (1 + len(dils))
    tabs = [(rest[2 * v], rest[2 * v + 1]) for v in range(1 + len(dils))]
    seg_refs = rest[n_tab:n_tab + n_seg]
    la_ref = rest[n_tab + n_seg] if decay else None
    split_refs, (h_scr, hf_scr) = rest[n_tab + n_seg + int(decay):-2], rest[-2:]
    j = pl.program_id(1)
    tm, d = x_ref.shape

    @pl.when(j == 0)
    def _():
        h = _rms(x_ref[...], g_ref[...]) * (1.0 + sc_ref[...]) + sh_ref[...]
        hb = h.astype(BF16)
        h_scr[0] = hb
        if decay:
            alr = _dot(hb, walr_ref[...])
            p = _dot(alr.astype(BF16), w2_ref[...]) + ab_ref[...]
            la_ref[...] = _log_sigmoid(p) * (1.0 / GLA_TAU)
        for c in range((d // LANES) if dils else 0):
            cs = slice(c * LANES, (c + 1) * LANES)
            hf_scr[c] = h[:, cs]
            for v, dil in enumerate(dils, 1):
                per = tm // dil
                for r in range(dil):
                    h_scr[v, r * per:(r + 1) * per, cs] = hf_scr[c, pl.ds(r, per, stride=dil), :].astype(BF16)

    def tile(v, roped):
        z = _dot(h_scr[v], w_ref[...])
        return _rope_tile(z, tabs[v][0][...], tabs[v][1][...]) if roped else z

    for (lo, hi), ref in zip(segs, seg_refs):
        ropes = [t for t in rope_tiles if lo <= t < hi]
        in_seg = jnp.logical_and(j >= lo, j < hi)
        is_rope = functools.reduce(jnp.logical_or, [j == t for t in ropes], False)
        if ropes:
            @pl.when(jnp.logical_and(in_seg, is_rope))
            def _(ref=ref):
                ref[...] = tile(0, True).astype(ref.dtype)

        @pl.when(jnp.logical_and(in_seg, jnp.logical_not(is_rope)))
        def _(ref=ref):
            ref[...] = tile(0, False).astype(ref.dtype)

    for t, slot, dil in split_tiles:
        @pl.when(j == t)
        def _(t=t, slot=slot, dil=dil):
            z = tile(1 + dils.index(dil), t in rope_tiles)
            split_refs[slot][...] = z.reshape(dil, tm // dil, z.shape[1]).astype(split_refs[slot].dtype)


def _in_proj(x2, g, sc, sh, w_main, w_alr, w2p, a_b, pos, tm, rows_per_mod, segs, rope_tiles, split_tiles,
             decay=True):
    t, d = x2.shape
    r = sc.shape[1]
    wq = a_b.shape[1]
    n_tiles = w_main.shape[1] // TN_IN
    tiles_per_mod = rows_per_mod // tm
    tab_tiles = pos.shape[0] // tm
    nmod = t // rows_per_mod
    dils = sorted({dil for _, _, dil in split_tiles})
    tabs = list(_rope_tables(pos))
    for dil in dils:
        tabs += [tb.reshape(tab_tiles, tm // dil, dil, LANES).transpose(0, 2, 1, 3).reshape(-1, LANES)
                 for tb in tabs[:2]]
    mod_spec = pl.BlockSpec((None, r, d), lambda i, j: (i // tiles_per_mod, 0, 0))
    tab_spec = pl.BlockSpec((tm, LANES), lambda i, j: (i % tab_tiles, 0))
    const = lambda shape: pl.BlockSpec(shape, lambda i, j: (0, 0))
    out_specs, out_shape = [], []
    for lo, hi, dt in segs:
        out_specs.append(pl.BlockSpec((tm, TN_IN), lambda i, j, lo=lo, hi=hi: (i, jnp.clip(j - lo, 0, hi - lo - 1))))
        out_shape.append(jax.ShapeDtypeStruct((t, (hi - lo) * TN_IN), dt))
    if decay:
        out_specs.append(pl.BlockSpec((tm, wq), lambda i, j: (i, 0)))
        out_shape.append(jax.ShapeDtypeStruct((t, wq), F32))
    for _, _, dil in split_tiles:
        out_specs.append(pl.BlockSpec((None, dil, tm // dil, TN_IN),
                                      lambda i, j: (i // tiles_per_mod, 0, i % tiles_per_mod, 0)))
        out_shape.append(jax.ShapeDtypeStruct((nmod, dil, rows_per_mod // dil, TN_IN), BF16))
    return pl.pallas_call(
        functools.partial(_inproj_kernel, segs=tuple((lo, hi) for lo, hi, _ in segs),
                          rope_tiles=tuple(rope_tiles), split_tiles=tuple(split_tiles), decay=decay),
        grid=(t // tm, n_tiles),
        in_specs=[pl.BlockSpec((tm, d), lambda i, j: (i, 0)), const((1, d)), mod_spec, mod_spec,
                  pl.BlockSpec((d, TN_IN), lambda i, j: (0, j)), const((d, LANES)), const((LANES, wq)),
                  const((1, wq))] + [tab_spec] * len(tabs),
        out_specs=out_specs,
        out_shape=out_shape,
        scratch_shapes=[pltpu.VMEM((1 + len(dils), tm, d), BF16),
                        pltpu.VMEM((d // LANES, tm if dils else SUBLANES, LANES), F32)],
        compiler_params=_cparams("arbitrary", "arbitrary"),
        name="in_proj",
    )(x2, g, sc, sh, w_main, w_alr, w2p, a_b, *tabs)


GLA_EXP_CLAMP = 80.0


def _gla_kernel(q_ref, k_ref, v_ref, ra_ref, la_ref, s0_ref, gn_ref, o_ref, sfin_ref, st_scr, *, rows, dk, dv):
    c = pl.program_id(1)
    nc = pl.num_programs(1)
    ch = GLA_CHUNK

    @pl.when(c == 0)
    def _():
        for h in range(GLA_HEADS):
            st_scr[h] = s0_ref[h].T

    ti = lax.broadcasted_iota(jnp.int32, (ch, ch), 0)
    si = lax.broadcasted_iota(jnp.int32, (ch, ch), 1)
    causal = si <= ti
    tri = causal.astype(BF16)
    scale = dk ** -0.5
    wk = GLA_HEADS * dk
    heads = range(GLA_HEADS)
    ks = [slice(h * dk, (h + 1) * dk) for h in heads]
    vs = [slice(h * dv, (h + 1) * dv) for h in heads]

    def chunk(ci, carry):
        rows = pl.ds(pl.multiple_of(ci * ch, ch), ch)
        la = la_ref[rows, :]
        hi = la.astype(BF16)
        r1 = la - hi.astype(F32)
        mid = r1.astype(BF16)
        lo = (r1 - mid.astype(F32)).astype(BF16)
        b3 = _dot(tri, jnp.concatenate([hi, mid, lo], axis=1))
        b = b3[:, :wk] + b3[:, wk:2 * wk] + b3[:, 2 * wk:]
        bl = b[ch - 1:ch, :]
        q = q_ref[rows, :] * scale
        k = k_ref[rows, :]
        qe = (q * jnp.exp(b)).astype(BF16)
        ke = (k * jnp.exp(jnp.minimum(-b, GLA_EXP_CLAMP))).astype(BF16)
        kh = (k * jnp.exp(bl - b)).astype(BF16)
        dec = jnp.exp(bl)
        v = v_ref[rows, :].astype(BF16)
        att = [jnp.where(causal, _dot_nt(qe[:, ks[h]], ke[:, ks[h]]), 0.0).astype(BF16) for h in heads]
        st = [st_scr[h] for h in heads]
        o = [_dot_nt(qe[:, ks[h]], st[h].astype(BF16)) + _dot(att[h], v[:, vs[h]]) for h in heads]
        for h in heads:
            st_scr[h] = st[h] * dec[:, ks[h]] + _dot_tn(v[:, vs[h]], kh[:, ks[h]])
        for h in heads:
            on = _rms(o[h], gn_ref[:, vs[h]])
            o_ref[rows, vs[h]] = (on * _silu(ra_ref[rows, vs[h]])).astype(o_ref.dtype)
        return carry

    lax.fori_loop(0, rows // ch, chunk, 0)

    @pl.when(c == nc - 1)
    def _():
        for h in range(GLA_HEADS):
            sfin_ref[h] = st_scr[h].T


def _gla_prompt(z, zv, la, s0, gn_g, n, l, rows):
    t = n * l
    dk = la.shape[1] // GLA_HEADS
    dv = gn_g.shape[1] // GLA_HEADS
    wk, wv = GLA_HEADS * dk, GLA_HEADS * dv
    cpb = l // rows
    row = lambda b, c: b * cpb + c
    return pl.pallas_call(
        functools.partial(_gla_kernel, rows=rows, dk=dk, dv=dv),
        grid=(n, cpb),
        in_specs=[pl.BlockSpec((rows, wk), lambda b, c: (row(b, c), 0)),
                  pl.BlockSpec((rows, wk), lambda b, c: (row(b, c), 1)),
                  pl.BlockSpec((rows, wv), lambda b, c: (row(b, c), 0)),
                  pl.BlockSpec((rows, wv), lambda b, c: (row(b, c), 1)),
                  pl.BlockSpec((rows, wk), lambda b, c: (row(b, c), 0)),
                  pl.BlockSpec((None, GLA_HEADS, dk, dv), lambda b, c: (b, 0, 0, 0)),
                  pl.BlockSpec((1, wv), lambda b, c: (0, 0))],
        out_specs=[pl.BlockSpec((rows, wv), lambda b, c: (row(b, c), 0)),
                   pl.BlockSpec((None, GLA_HEADS, dk, dv), lambda b, c: (b, 0, 0, 0))],
        out_shape=[jax.ShapeDtypeStruct((t, wv), BF16), jax.ShapeDtypeStruct((n, GLA_HEADS, dk, dv), F32)],
        scratch_shapes=[pltpu.VMEM((GLA_HEADS, dv, dk), F32)],
        compiler_params=_cparams("arbitrary", "arbitrary"),
        name="gla_prompt",
    )(z, z, zv, z, la, s0, gn_g)


GLA_STEP_SEQS = 4


def _gla_step_kernel(q_ref, k_ref, la_ref, v_ref, ra_ref, s_ref, gn_ref, o_ref, so_ref, *, dk):
    for b in range(s_ref.shape[0]):
        sn = jnp.exp(la_ref[b]) * s_ref[b] + k_ref[b] * v_ref[b]
        so_ref[b] = sn
        o = jnp.sum((q_ref[b] * dk ** -0.5) * sn, axis=1, keepdims=True)
        on = _rms(o, gn_ref[...])
        o_ref[b] = (on * _silu(ra_ref[b])).astype(o_ref.dtype)


def _gla_step(q, k, la, v, ra, s, gn_g):
    nb, nh, dk, dv = s.shape
    bb = _pick_tile(nb, GLA_STEP_SEQS)
    col = pl.BlockSpec((bb, nh, dk, 1), lambda b: (b, 0, 0, 0))
    rowv = pl.BlockSpec((bb, nh, 1, dv), lambda b: (b, 0, 0, 0))
    st = pl.BlockSpec((bb, nh, dk, dv), lambda b: (b, 0, 0, 0))
    o, so = pl.pallas_call(
        functools.partial(_gla_step_kernel, dk=dk),
        grid=(nb // bb,),
        in_specs=[col, col, col, rowv, rowv, st, pl.BlockSpec((nh, 1, dv), lambda b: (0, 0, 0))],
        out_specs=[rowv, st],
        out_shape=[jax.ShapeDtypeStruct((nb, nh, 1, dv), BF16), jax.ShapeDtypeStruct(s.shape, F32)],
        compiler_params=_cparams("arbitrary"),
        name="gla_step",
    )(q.reshape(nb, nh, dk, 1), k.reshape(nb, nh, dk, 1), la.reshape(nb, nh, dk, 1),
      v.reshape(nb, nh, 1, dv), ra.reshape(nb, nh, 1, dv), s, gn_g.reshape(nh, 1, dv))
    return o.reshape(nb, nh * dv), so


def _band_kernel(q_ref, k_ref, kp_ref, v_ref, vp_ref, o_ref, lse_ref, *, dil, qb):
    i = pl.program_id(1)
    blk = ATT_BLOCK
    qi = lax.broadcasted_iota(jnp.int32, (blk, 2 * blk), 0)
    ki = lax.broadcasted_iota(jnp.int32, (blk, 2 * blk), 1)
    dist = blk + qi - ki
    band = jnp.logical_and(dist >= 0, dist <= ATT_BLOCK)
    band_first = jnp.logical_and(band, ki >= jnp.where(i > 0, 0, blk))
    scale = ATT_HD ** -0.5
    for r in range(dil):
        for sb in range(qb // blk):
            rs = slice(sb * blk, (sb + 1) * blk)
            q = q_ref[r, rs, :].astype(BF16)
            if sb == 0:
                kprev, vprev, mask = kp_ref[r], vp_ref[r], band_first
            else:
                ps = slice((sb - 1) * blk, sb * blk)
                kprev, vprev, mask = k_ref[r, ps, :], v_ref[r, ps, :], band
            kk = jnp.concatenate([kprev, k_ref[r, rs, :]], axis=0).astype(BF16)
            vv = jnp.concatenate([vprev, v_ref[r, rs, :]], axis=0).astype(BF16)
            s = jnp.where(mask, _dot_nt(q, kk) * scale, NEG_BIG)
            m = jnp.max(s, axis=-1, keepdims=True)
            p = jnp.exp(s - m)
            lsum = jnp.sum(p, axis=-1, keepdims=True)
            o = _dot(p.astype(BF16), vv) / lsum
            lse = jnp.broadcast_to(m + jnp.log(lsum), (blk, ATT_HD))
            rows = rs if dil == 1 else pl.ds(sb * blk * dil + r, blk, stride=dil)
            o_ref[rows, :] = o
            lse_ref[rows, :] = lse


def _band_attention(qa, ka, va, cq, ck, cv, n, l, dil):
    lq = l // dil
    hd = ATT_HD
    tokens = min(ATT_TOKENS, l)
    qb = tokens // dil
    nqb = lq // qb
    sub = qb // ATT_BLOCK

    def main(c0):
        return pl.BlockSpec((None, dil, qb, hd), lambda b, i, h: (b, 0, i, c0 + h))

    def prev(c0):
        return pl.BlockSpec((None, dil, ATT_BLOCK, hd), lambda b, i, h: (b, 0, jnp.maximum(i * sub - 1, 0), c0 + h))

    out = pl.BlockSpec((tokens, hd), lambda b, i, h: (b * nqb + i, h))
    shp = jax.ShapeDtypeStruct((n * l, ATT_HEADS * hd), F32)
    return pl.pallas_call(
        functools.partial(_band_kernel, dil=dil, qb=qb),
        grid=(n, nqb, ATT_HEADS),
        in_specs=[main(cq), main(ck), prev(ck), main(cv), prev(cv)],
        out_specs=[out, out],
        out_shape=[shp, shp],
        compiler_params=_cparams("arbitrary", "arbitrary", "arbitrary"),
        name=f"band_attention_d{dil}",
    )(qa, ka, ka, va, va)


ATT_STEP_SEQS = 4


def _att_step_kernel(q_ref, kn_ref, vn_ref, c0_ref, c1_ref, c2_ref, o_ref):
    scale = ATT_HD ** -0.5
    add = lambda a, b: a + b
    for b in range(q_ref.shape[0]):
        outs, lses = [], []
        for g, c_ref in enumerate((c0_ref, c1_ref, c2_ref)):
            q, kn, vn = q_ref[b, g], kn_ref[b, g], vn_ref[b, g]
            s = jnp.sum(c_ref[b, :, 0] * q[None], axis=-1, keepdims=True) * scale
            sn = jnp.sum(kn * q, axis=-1, keepdims=True) * scale
            m = jnp.maximum(jnp.max(s, axis=0), sn)
            p = jnp.exp(s - m[None])
            pn = jnp.exp(sn - m)
            lsum = jnp.sum(p, axis=0) + pn
            outs.append((jnp.sum(p * c_ref[b, :, 1], axis=0) + pn * vn) / lsum)
            lses.append(m + jnp.log(lsum))
        mx = functools.reduce(jnp.maximum, lses)
        es = [jnp.exp(x - mx) for x in lses]
        inv = 1.0 / functools.reduce(add, es)
        o_ref[b] = functools.reduce(add, [e * inv * o for e, o in zip(es, outs)]).astype(o_ref.dtype)


def _att_step(qs, ks, vs, caches):
    nb = qs.shape[0]
    bb = _pick_tile(nb, ATT_STEP_SEQS)
    views, cspecs = [], []
    for (window, dil), cch in zip(ATT_GROUPS, caches):
        assert cch.shape[1] == window, "sample step expects full caches"
        views.append(cch.reshape((nb, window // dil, dil) + cch.shape[2:]))
        cspecs.append(pl.BlockSpec((bb, ATT_BLOCK, None) + cch.shape[2:], lambda b: (b, 0, 0, 0, 0, 0)))
    row = pl.BlockSpec((bb,) + qs.shape[1:], lambda b: (b, 0, 0, 0))
    o = pl.pallas_call(
        _att_step_kernel,
        grid=(nb // bb,),
        in_specs=[row] * 3 + cspecs,
        out_specs=pl.BlockSpec((bb, ATT_HEADS, ATT_HD), lambda b: (b, 0, 0)),
        out_shape=jax.ShapeDtypeStruct((nb, ATT_HEADS, ATT_HD), BF16),
        compiler_params=_cparams("arbitrary"),
        name="att_step",
    )(qs, ks, vs, *views)
    return o.reshape(nb, ATT_HEADS * ATT_HD)


MERGE_PART = 256


def _merge_kernel(*refs, n_att, n_alias, n_tiles):
    oa_ref = refs[0]
    att = refs[1:1 + n_att]
    (gza_ref, gzb_ref, x_ref, g1_ref, sc2_ref, sh2_ref, n2_ref, wa_ref, wb_ref, wo_ref,
     wq_ref) = refs[1 + n_att:12 + n_att]
    x1_ref, h2_ref, pq_ref = refs[12 + n_att + n_alias:]
    tm = x_ref.shape[0]
    pr = min(MERGE_PART, tm)
    parts = [slice(k * pr, (k + 1) * pr) for k in range(tm // pr)]
    mod = lambda ref, part: ref[...] if ref.shape[0] == 1 else ref[part, :]
    add = lambda a, b: a + b

    def attention_out(part):
        if n_att == 1:
            return att[0][part, :]
        ng = n_att // 2
        ls = [r[part, :] for r in att[ng:]]
        mx = functools.reduce(jnp.maximum, ls)
        es = [jnp.exp(x - mx) for x in ls]
        inv = 1.0 / functools.reduce(add, es)
        return functools.reduce(add, [e * inv * r[part, :] for e, r in zip(es, att[:ng])]).astype(BF16)

    @pl.when(pl.program_id(0) >= n_tiles)
    def _():
        h2_ref[...] = jnp.zeros_like(h2_ref)
        pq_ref[...] = jnp.zeros_like(pq_ref)

    @pl.when(pl.program_id(0) < n_tiles)
    def _():
        _merge_rows(parts, attention_out, mod, oa_ref, gza_ref, gzb_ref, x_ref, g1_ref, sc2_ref, sh2_ref, n2_ref,
                    wa_ref, wb_ref, wo_ref, wq_ref, x1_ref, h2_ref, pq_ref)
        if h2_ref.shape[0] > tm:
            h2_ref[tm:] = jnp.zeros((h2_ref.shape[0] - tm, h2_ref.shape[1]), h2_ref.dtype)
            pq_ref[tm:] = jnp.zeros((pq_ref.shape[0] - tm, pq_ref.shape[1]), pq_ref.dtype)


def _merge_rows(parts, attention_out, mod, oa_ref, gza_ref, gzb_ref, x_ref, g1_ref, sc2_ref, sh2_ref, n2_ref,
                wa_ref, wb_ref, wo_ref, wq_ref, x1_ref, h2_ref, pq_ref):
    ob = [attention_out(p) for p in parts]
    ma = [_dot(oa_ref[p, :], wa_ref[...]) for p in parts]
    mb = [_dot(o, wb_ref[...]) for o in ob]
    merged = [(_sigmoid(gza_ref[p, :]) * a + _sigmoid(gzb_ref[p, :]) * b).astype(BF16)
              for p, a, b in zip(parts, ma, mb)]
    y = [_dot(m, wo_ref[...]) for m in merged]
    h2 = []
    for p, yy in zip(parts, y):
        x1 = x_ref[p, :] + mod(g1_ref, p) * yy
        x1_ref[p, :] = x1
        h2.append((_rms(x1, n2_ref[...]) * (1.0 + mod(sc2_ref, p)) + mod(sh2_ref, p)).astype(BF16))
    for p, h in zip(parts, h2):
        h2_ref[p, :] = h
        pq_ref[p, :] = _dot(h, wq_ref[...])


def _merge(oa, att, z, x2, g1, sc2, sh2, n2g, wa, wb, wo, wq, tm, rows_per_mod, gza_blk, gzb_blk,
           peer_rows, peer_tile, peer_blk0, alias=(), zero_tiles=0):
    t, d = x2.shape
    r = g1.shape[1]
    wqn = wq.shape[1]
    tiles_per_mod = rows_per_mod // tm
    n_tiles = t // tm
    row = lambda i: jnp.minimum(i, n_tiles - 1)
    tok = lambda w: pl.BlockSpec((tm, w), lambda i: (row(i), 0))
    mod = pl.BlockSpec((None, r, d), lambda i: (row(i) // tiles_per_mod, 0, 0))
    const = lambda a: pl.BlockSpec(a.shape, lambda i: (0, 0), pipeline_mode=pl.Buffered(1))
    peer = lambda w: pl.BlockSpec((peer_tile, w), lambda i: (peer_blk0 + i, 0))
    n_in = 12 + len(att)
    return pl.pallas_call(
        functools.partial(_merge_kernel, n_att=len(att), n_alias=len(alias), n_tiles=n_tiles),
        grid=(n_tiles + zero_tiles,),
        in_specs=[tok(oa.shape[1])] + [tok(a.shape[1]) for a in att]
                 + [pl.BlockSpec((tm, d), lambda i: (row(i), gza_blk)),
                    pl.BlockSpec((tm, d), lambda i: (row(i), gzb_blk)),
                    tok(d), mod, mod, mod, const(n2g), const(wa), const(wb), const(wo), const(wq)]
                 + [pl.BlockSpec(memory_space=pl.ANY)] * len(alias),
        out_specs=[tok(d), peer(d), peer(wqn)],
        out_shape=[jax.ShapeDtypeStruct((t, d), F32), jax.ShapeDtypeStruct((peer_rows, d), BF16),
                   jax.ShapeDtypeStruct((peer_rows, wqn), F32)],
        input_output_aliases={n_in + k: 1 + k for k in range(len(alias))},
        compiler_params=_cparams("arbitrary"),
        name="merge",
    )(oa, *att, z, z, x2, g1, sc2, sh2, n2g, wa, wb, wo, wq, *alias)


def _strict_max_below(rows, prev):
    m = None
    for x in rows:
        y = jnp.where(x < prev, x, NEG_BIG) if prev is not None else x
        m = y if m is None else jnp.maximum(m, y)
    return m


def _oddeven_merge(lo, hi, r):
    step = r * 2
    if step < hi - lo:
        yield from _oddeven_merge(lo, hi, step)
        yield from _oddeven_merge(lo + r, hi, step)
        yield from [(i, i + r) for i in range(lo + r, hi - r, step)]
    else:
        yield (lo, lo + r)


def _oddeven_merge_sort(lo, hi):
    if hi - lo >= 1:
        mid = lo + (hi - lo) // 2
        yield from _oddeven_merge_sort(lo, mid)
        yield from _oddeven_merge_sort(mid + 1, hi)
        yield from _oddeven_merge(lo, hi, 1)


def _exchange(p, i, j):
    p[i], p[j] = jnp.maximum(p[i], p[j]), jnp.minimum(p[i], p[j])


def _topk_desc(s, k):
    assert s.shape[0] == k * SUBLANES and k & (k - 1) == 0
    p = [s[i * SUBLANES:(i + 1) * SUBLANES] for i in range(k)]
    for i, j in _oddeven_merge_sort(0, k - 1):
        _exchange(p, i, j)
    shift = SUBLANES // 2
    while shift:
        q = [pltpu.roll(v, shift, 0) for v in p]
        p = [jnp.maximum(p[i], q[k - 1 - i]) for i in range(k)]
        stride = k // 2
        while stride:
            for i in range(k):
                if not i & stride:
                    _exchange(p, i, i + stride)
            stride //= 2
        shift //= 2
    return [v[0:1] for v in p]


def _route_kernel(pq_ref, k1_ref, k2_ref, n1_ref, e1_ref, r2_ref, e2_ref, s1_scr, s2_scr):
    nk = PEER_NKEYS
    pq = pq_ref[...].astype(BF16)
    s1_scr[...] = _dot_nt(k1_ref[...], pq)
    s2_scr[...] = _dot_nt(k2_ref[...], pq)
    v1h, v2h = [], []
    for h in range(PEER_HEADS):
        v1h.append(_topk_desc(s1_scr[h * nk:(h + 1) * nk], PEER_TOPK))
        v2h.append(_topk_desc(s2_scr[h * nk:(h + 1) * nk], PEER_TOPK))
    v1 = [jnp.concatenate([v1h[h][k] for h in range(PEER_HEADS)], axis=0) for k in range(PEER_TOPK)]
    v2 = [jnp.concatenate([v2h[h][k] for h in range(PEER_HEADS)], axis=0) for k in range(PEER_TOPK)]
    pairs = [(i, j) for i in range(PEER_TOPK) for j in range(PEER_TOPK) if (i + 1) * (j + 1) <= PEER_TOPK]
    cands = [v1[i] + v2[j] for i, j in pairs]
    tau = None
    for _ in range(PEER_TOPK):
        tau = _strict_max_below(cands, tau)
    cmax = v1[0] + v2[0]
    zsum = None
    counts = [None] * PEER_TOPK
    for (i, _), cnd in zip(pairs, cands):
        sel = cnd >= tau
        e = jnp.where(sel, jnp.exp(cnd - cmax), 0.0)
        zsum = e if zsum is None else zsum + e
        one = jnp.where(sel, 1.0, 0.0)
        counts[i] = one if counts[i] is None else counts[i] + one
    zinv = 1.0 / zsum
    for h in range(PEER_HEADS):
        a = s1_scr[h * nk:(h + 1) * nk]
        b = s2_scr[h * nk:(h + 1) * nk]
        n1 = jnp.zeros_like(a)
        r2 = jnp.full_like(b, float(PEER_TOPK))
        for k in range(PEER_TOPK):
            n1 = jnp.where(a == v1h[h][k], counts[k][h:h + 1, :], n1)
            r2 = jnp.where(b == v2h[h][k], float(k), r2)
        n1_ref[h] = n1
        e1_ref[h] = jnp.exp(a - v1h[h][0]) * zinv[h:h + 1, :]
        r2_ref[h] = r2.astype(r2_ref.dtype)
        e2_ref[h] = jnp.exp(b - v2h[h][0]).astype(e2_ref.dtype)


def _peer_route(pq, k1big, k2big, tt):
    t, w = pq.shape
    nh, nk = PEER_HEADS, PEER_NKEYS
    sspec = pl.BlockSpec((nh, nk, tt), lambda i: (0, 0, i))
    shp = lambda dt: jax.ShapeDtypeStruct((nh, nk, t), dt)
    return pl.pallas_call(
        _route_kernel,
        grid=(t // tt,),
        in_specs=[pl.BlockSpec((tt, w), lambda i: (i, 0)),
                  pl.BlockSpec(k1big.shape, lambda i: (0, 0)), pl.BlockSpec(k2big.shape, lambda i: (0, 0))],
        out_specs=[sspec] * 4,
        out_shape=[shp(F32), shp(F32), shp(BF16), shp(BF16)],
        scratch_shapes=[pltpu.VMEM((nh * nk, tt), F32)] * 2,
        compiler_params=_cparams("arbitrary"),
        name="peer_route",
    )(pq, k1big, k2big)


BF16_ROWS = 16
GATE_LANES = 256
PEER_PART = 256
PEER_TILE = 2048


def _gelu_tanh_bf16(x):
    hx = 0.5 * x
    return hx + hx * jnp.tanh(x * (GELU_C + (GELU_C * 0.044715) * (x * x)))


def _bf16_rows(row):
    return jnp.broadcast_to(row, (BF16_ROWS, row.shape[1])).astype(BF16)


CACHE_CHUNK = 512


def _cache_shift_jobs(caches):
    nb = caches[0].shape[0]
    chunks, base = [], 0
    for c, cache in enumerate(caches):
        w = cache.shape[1]
        for r0 in range(0, w, CACHE_CHUNK):
            n = min(CACHE_CHUNK, w - r0)
            chunks.append((c, r0, n, r0 + n == w, base))
            base += nb
    return chunks, base


def _cache_shift_step(step, caches, new_rows, outs, buf, sems):
    nb = caches[0].shape[0]
    chunks, n_jobs = _cache_shift_jobs(caches)

    def for_job(job, act):
        for c, r0, n, last, base in chunks:
            @pl.when(jnp.logical_and(job >= base, job < base + nb))
            def _(c=c, r0=r0, n=n, last=last, base=base):
                b, slot = job - base, job % 2
                n_in = n - 1 if last else n
                rd = lambda: pltpu.make_async_copy(caches[c].at[b, pl.ds(r0 + 1, n_in)],
                                                   buf.at[slot, pl.ds(0, n_in)], sems.at[slot, 0])
                wr = lambda: pltpu.make_async_copy(buf.at[slot, pl.ds(0, n)], outs[c].at[b, pl.ds(r0, n)],
                                                   sems.at[slot, 1])
                act(rd, wr, lambda: new_rows[c][b] if last else None, slot, n_in)

    def finish_read_start_write(rd, wr, new_row, slot, n_in):
        rd().wait()
        row = new_row()
        if row is not None:
            buf[slot, pl.ds(n_in, 1)] = row
        wr().start()

    @pl.when(jnp.logical_and(step >= 1, step <= n_jobs))
    def _():
        for_job(step - 1, finish_read_start_write)

    @pl.when(jnp.logical_and(step >= 2, step <= n_jobs + 1))
    def _():
        for_job(step - 2, lambda rd, wr, *_: wr().wait())

    @pl.when(step < n_jobs)
    def _():
        for_job(step, lambda rd, wr, *_: rd().start())


def _peer_dense_kernel(h2_ref, u_ref, vt_ref, n1_ref, e1_ref, r2_ref, e2_ref, h2_next_ref, u_next_ref, *rest,
                       et, n_kv):
    caches, new_rows = rest[:n_kv], rest[n_kv:2 * n_kv]
    o_ref, kv_outs = rest[2 * n_kv], rest[2 * n_kv + 1:3 * n_kv + 1]
    acc_scr, wg_scr, st_scr, st0_scr = rest[3 * n_kv + 1:3 * n_kv + 5]
    e = pl.program_id(1)
    nk = PEER_NKEYS
    tt = h2_ref.shape[0]
    step = pl.program_id(0) * pl.num_programs(1) + e
    if n_kv:
        _cache_shift_step(step, caches, new_rows, kv_outs, rest[-2], rest[-1])

    @pl.when(step == 0)
    def _():
        st0_scr[...] = _dot_nt(u_ref[0:PEER_PART, :], h2_ref[...]).astype(BF16)

    @pl.when(e == 0)
    def _():
        acc_scr[...] = jnp.zeros_like(acc_scr)

    zero = jnp.zeros((), BF16)
    groups = nk // BF16_ROWS

    def scores(part):
        st_scr[part, :] = _dot_nt(u_ref[part, :], h2_ref[...]).astype(BF16)

    def gates(ii):
        rs = slice(ii * nk, (ii + 1) * nk)
        gl = min(GATE_LANES, tt)
        for lc in range(tt // gl):
            ls = slice(lc * gl, (lc + 1) * gl)
            w = None
            for h in range(PEER_HEADS):
                n1 = _bf16_rows(n1_ref[h, ii:ii + 1, ls])
                e1 = _bf16_rows(e1_ref[h, ii:ii + 1, ls])
                r2 = r2_ref[h, :, ls].reshape(groups, BF16_ROWS, gl)
                e2 = e2_ref[h, :, ls].reshape(groups, BF16_ROWS, gl)
                c = jnp.where(r2 < n1[None], e2, zero) * e1[None]
                w = c if w is None else w + c
            st = st0_scr if rs.stop <= PEER_PART else st_scr
            wg_scr[rs, ls] = w.reshape(nk, gl) * _gelu_tanh_bf16(st[rs, ls])

    def mix(part):
        return _dot(vt_ref[:, part], wg_scr[part, :])

    parts = [slice(p * PEER_PART, (p + 1) * PEER_PART) for p in range(et // PEER_PART)]
    out = None
    next_scores = None
    for p, part in enumerate(parts):
        if p + 1 < len(parts):
            scores(parts[p + 1])
        else:
            next_scores = _dot_nt(u_next_ref[...], h2_next_ref[...]).astype(BF16)
        for ii in range(part.start // nk, part.stop // nk):
            gates(ii)
        if p > 0:
            d = mix(parts[p - 1])
            out = d if out is None else out + d
    st0_scr[...] = next_scores
    d = mix(parts[-1])
    acc_scr[...] += d if out is None else out + d

    @pl.when(e == pl.num_programs(1) - 1)
    def _():
        o_ref[...] = acc_scr[...].T


def _peer_dense(h2, u_b, vt_b, n1, e1, r2, e2, tt, et, caches=(), new_rows=()):
    t, d = h2.shape
    ne = u_b.shape[0]
    nh, nk = PEER_HEADS, PEER_NKEYS
    n_kv = len(caches)
    grid = (t // tt, ne // et)
    if n_kv:
        assert grid[0] * grid[1] >= _cache_shift_jobs(caches)[1] + 2, "one cache-shift job per grid step"
    rowside = pl.BlockSpec((nh, et // nk, tt), lambda i, e: (0, e, i))
    colside = pl.BlockSpec((nh, nk, tt), lambda i, e: (0, 0, i))
    anywhere = pl.BlockSpec(memory_space=pl.ANY)
    whole = lambda a: pl.BlockSpec(a.shape, lambda i, e: (0,) * a.ndim)
    outs = pl.pallas_call(
        functools.partial(_peer_dense_kernel, et=et, n_kv=n_kv),
        grid=grid,
        in_specs=[pl.BlockSpec((tt, d), lambda i, e: (i, 0)),
                  pl.BlockSpec((et, d), lambda i, e: (e, 0)),
                  pl.BlockSpec((d, et), lambda i, e: (0, e)),
                  rowside, rowside, colside, colside,
                  pl.BlockSpec((tt, d), lambda i, e: (jnp.minimum(i + (e + 1) // grid[1], grid[0] - 1), 0)),
                  pl.BlockSpec((PEER_PART, d), lambda i, e: (((e + 1) % grid[1]) * (et // PEER_PART), 0))]
                 + [anywhere] * n_kv + [whole(a) for a in new_rows],
        out_specs=[pl.BlockSpec((tt, d), lambda i, e: (i, 0))] + [anywhere] * n_kv,
        out_shape=[jax.ShapeDtypeStruct((t, d), F32)] + [jax.ShapeDtypeStruct(c.shape, c.dtype) for c in caches],
        scratch_shapes=[pltpu.VMEM((d, tt), F32), pltpu.VMEM((et, tt), BF16), pltpu.VMEM((et, tt), BF16),
                        pltpu.VMEM((PEER_PART, tt), BF16)]
                       + ([pltpu.VMEM((2, CACHE_CHUNK) + caches[0].shape[2:], caches[0].dtype),
                           pltpu.SemaphoreType.DMA((2, 2))] if n_kv else []),
        compiler_params=_cparams("arbitrary", "arbitrary"),
        name="peer_dense",
    )(h2, u_b, vt_b, n1, e1, r2, e2, h2, u_b, *caches, *new_rows)
    return outs[0], outs[1:]


def _final_kernel(x1_ref, p_ref, g2_ref, fg_ref, y_ref):
    y_ref[...] = _rms(x1_ref[...] + g2_ref[...] * p_ref[...], fg_ref[...])


def _final(x1, p, g2, fg, tm, rows_per_mod, p_blk0):
    t, d = x1.shape
    r = g2.shape[1]
    tiles_per_mod = rows_per_mod // tm
    tok = pl.BlockSpec((tm, d), lambda i: (i, 0))
    return pl.pallas_call(
        _final_kernel,
        grid=(t // tm,),
        in_specs=[tok, pl.BlockSpec((tm, d), lambda i: (p_blk0 + i, 0)),
                  pl.BlockSpec((None, r, d), lambda i: (i // tiles_per_mod, 0, 0)),
                  pl.BlockSpec((1, d), lambda i: (0, 0))],
        out_specs=tok,
        out_shape=jax.ShapeDtypeStruct((t, d), F32),
        compiler_params=_cparams("arbitrary"),
        name="final_norm",
    )(x1, p, g2, fg)


def _rope_tables(pos):
    half = ROPE_DIM // 2
    inv_freq = ROPE_THETA ** (-jnp.arange(half, dtype=F32) / half)
    ang = pos.astype(F32)[:, None] * inv_freq[None, :]
    cos, sin = jnp.cos(ang), jnp.sin(ang)
    n = pos.shape[0]
    ones = jnp.ones((n, LANES - ROPE_DIM), F32)
    zeros = jnp.zeros((n, LANES - ROPE_DIM), F32)
    return (jnp.concatenate([cos, cos, ones], axis=1), jnp.concatenate([-sin, sin, zeros], axis=1))


def _block_diag_keys(keys, half):
    nk, dh = keys.shape
    eye = jnp.eye(PEER_HEADS, dtype=keys.dtype)
    blk = jnp.zeros((PEER_HEADS, nk, PEER_HEADS, 2, dh), keys.dtype)
    blk = blk.at[:, :, :, half, :].set(eye[:, None, :, None] * keys[None, :, None, :])
    return blk.reshape(PEER_HEADS * nk, PEER_HEADS * 2 * dh).astype(BF16)


def _pick_tile(n, pref):
    t = min(pref, n)
    while n % t:
        t //= 2
    return t


def kernel(x_prompt, x_sample, c_prompt, c_sample, state_gla, cache_kv_w128, cache_kv_w512, cache_kv_w2048,
           ada_w, ada_b, norm1_g, w_in, gla_a_w2, gla_a_b, gla_gn_g, w_branch_a, w_branch_b, w_out,
           norm2_g, peer_wq, peer_k1, peer_k2, peer_u, peer_v, final_g):
    depth = ada_w.shape[0]
    assert depth == 1, "single-layer trunk"
    n_p, l_p, d = x_prompt.shape
    n_s, l_s, _ = x_sample.shape
    assert l_s == 1, "sample group decodes one token per sequence"
    caches = (cache_kv_w128[0], cache_kv_w512[0], cache_kv_w2048[0])
    n_groups = len(ATT_GROUPS)

    w_qa = gla_a_w2.shape[2]
    w_va = gla_gn_g.shape[1]
    w_ob = ATT_HEADS * ATT_HD
    w_qb = n_groups * w_ob
    assert w_qa == TN_IN and w_va == 2 * TN_IN and d == 2 * TN_IN and w_ob == TN_IN
    widths = (w_qa, w_qa, w_va, w_va, GLA_RANK, w_qb, w_qb, w_qb, d, d)
    offs = [0]
    for w in widths:
        offs.append(offs[-1] + w)
    seg = lambda i: w_in[0][:, offs[i]:offs[i + 1]]
    att_col = lambda g, k: seg(5 + k)[:, g * w_ob:(g + 1) * w_ob]
    att_cols = [att_col(g, k) for g in range(n_groups) for k in range(3)]
    w_main = jnp.concatenate([seg(0), seg(1), seg(3), seg(8), seg(9), seg(2)] + att_cols, axis=1).astype(BF16)
    w_kv = jnp.concatenate([att_col(g, k) for g in range(n_groups) for k in (1, 2)], axis=1).astype(BF16)
    n_tiles = w_main.shape[1] // TN_IN
    ra_off, gza_blk, gzb_blk = 2 * w_qa, 2, 3
    w_alr = jnp.pad(seg(4), ((0, 0), (0, LANES - GLA_RANK))).astype(BF16)
    w2p = jnp.pad(gla_a_w2[0], ((0, LANES - GLA_RANK), (0, 0))).astype(BF16)
    a_b = gla_a_b[0].reshape(1, w_qa)
    g1n = norm1_g[0].reshape(1, d)
    gn_g = gla_gn_g[0].reshape(1, w_va)
    n2g = norm2_g[0].reshape(1, d)
    wa, wb = w_branch_a[0].astype(BF16), w_branch_b[0].astype(BF16)
    wo, wq = w_out[0].astype(BF16), peer_wq[0].astype(BF16)
    q_tile = lambda g: ATT_TILE0 + 3 * g
    rope_tiles = [q_tile(g) + k for g in range(n_groups) for k in (0, 1)]

    n_mod = n_p + n_s
    n_pad = -(-n_mod // SUBLANES) * SUBLANES
    c_all = jnp.concatenate([c_prompt, c_sample, jnp.zeros((n_pad - n_mod, d), F32)], axis=0)
    mod = _modulation(c_all, ada_w[0], ada_b[0])
    mods_p = [mod[:n_p, i * d:(i + 1) * d].reshape(n_p, 1, d) for i in range(6)]
    mods_s = [mod[n_p:n_mod, i * d:(i + 1) * d].reshape(1, n_s, d) for i in range(6)]

    t_p = n_p * l_p
    tt = _pick_tile(t_p, 512)
    tt_s = -(-n_s // LANES) * LANES

    xp2 = x_prompt.reshape(t_p, d)
    tm_p = _pick_tile(l_p, 1024)
    pos_p = jnp.arange(l_p, dtype=jnp.int32)
    split = []
    for g, (window, dil) in enumerate(ATT_GROUPS):
        assert window // dil == ATT_BLOCK
        if dil > 1:
            split += [(q_tile(g) + k, len(split) + k, dil) for k in range(3)]
    first_split = min([t for t, _, _ in split], default=n_tiles)
    outs = _in_proj(xp2, g1n, mods_p[1], mods_p[0], w_main, w_alr, w2p, a_b, pos_p, tm_p, l_p,
                    ((0, VA_TILE0, F32), (VA_TILE0, first_split, BF16)), rope_tiles, split)
    z_p, zb_p, la_p, split_p = outs[0], outs[1], outs[2], outs[3:]
    s0_p = jnp.zeros((n_p,) + state_gla.shape[2:], F32)
    oa_p, sfin_p = _gla_prompt(z_p, zb_p, la_p, s0_p, gn_g, n_p, l_p, _pick_tile(l_p, 512))
    zb4 = zb_p.reshape(n_p, 1, l_p, zb_p.shape[1])
    att_o, att_l = [], []
    for g, (window, dil) in enumerate(ATT_GROUPS):
        if dil > 1:
            qa, ka, va = [split_p[slot] for t, slot, _ in split if q_tile(g) <= t < q_tile(g) + 3]
            o_g, lse_g = _band_attention(qa, ka, va, 0, 0, 0, n_p, l_p, dil)
        else:
            c0 = lambda k: (q_tile(g) + k - VA_TILE0) * (TN_IN // ATT_HD)
            o_g, lse_g = _band_attention(zb4, zb4, zb4, c0(0), c0(1), c0(2), n_p, l_p, dil)
        att_o.append(o_g)
        att_l.append(lse_g)
    keep_max = min(max(w for w, _ in ATT_GROUPS), l_p)
    x_tail = x_prompt[:, l_p - keep_max:].reshape(n_p * keep_max, d)
    kv_tiles = 2 * n_groups
    z_kv = _in_proj(x_tail, g1n, mods_p[1], mods_p[0], w_kv, w_alr, w2p, a_b, pos_p[l_p - keep_max:],
                    _pick_tile(keep_max, 1024), keep_max, ((0, kv_tiles, F32),), range(0, kv_tiles, 2), [],
                    decay=False)[0]
    z_kv = z_kv.reshape(n_p, keep_max, kv_tiles * TN_IN)
    kv_p = []
    for g, (window, dil) in enumerate(ATT_GROUPS):
        keep = min(window, l_p)
        rows = z_kv[:, keep_max - keep:, 2 * g * TN_IN:(2 * g + 2) * TN_IN]
        kv_p.append(rows.reshape(1, n_p, keep, 2, ATT_HEADS, ATT_HD))
    tm_m = _pick_tile(l_p, 512)
    x1_p, h2_p, pq_p = _merge(oa_p, att_o + att_l, z_p, xp2, mods_p[2], mods_p[4], mods_p[3], n2g,
                              wa, wb, wo, wq, tm_m, l_p, gza_blk, gzb_blk, t_p, tm_m, 0)

    xs2 = x_sample.reshape(n_s, d)
    pos_s = jnp.full((n_s,), PAST_LEN, dtype=jnp.int32)
    z_s, la_s = _in_proj(xs2, g1n, mods_s[1], mods_s[0], w_main, w_alr, w2p, a_b, pos_s, n_s, n_s,
                         ((0, n_tiles, F32),), rope_tiles, [])
    col = lambda tile: z_s[:, tile * TN_IN:(tile + 1) * TN_IN]
    oa_s, state_s = _gla_step(col(0), col(1), la_s, z_s[:, VA_TILE0 * TN_IN:VA_TILE0 * TN_IN + w_va],
                              z_s[:, ra_off:ra_off + w_va], state_gla[0], gn_g)
    stack = lambda k: jnp.stack([col(q_tile(g) + k) for g in range(n_groups)], axis=1).reshape(
        n_s, n_groups, ATT_HEADS, ATT_HD)
    ob_s = _att_step(stack(0), stack(1), stack(2), caches)
    x1_s, h2_s, pq_s = _merge(oa_s, [ob_s], z_s, xs2, mods_s[2], mods_s[4], mods_s[3], n2g,
                              wa, wb, wo, wq, n_s, n_s, gza_blk, gzb_blk, tt_s, tt_s, 0)
    row_tiles = (2 * w_ob // LANES, LANES)
    new_rows = [jnp.concatenate([col(q_tile(g) + 1), col(q_tile(g) + 2)], axis=1).reshape((n_s, 1) + row_tiles)
                for g in range(n_groups)]

    k1big = _block_diag_keys(peer_k1[0], 0)
    k2big = _block_diag_keys(peer_k2[0], 1)
    u_b = peer_u[0].astype(BF16)
    vt_b = peer_v[0].astype(BF16).T
    route_p = _peer_route(pq_p, k1big, k2big, _pick_tile(tt, 256))
    p_p, kv_s = _peer_dense(h2_p, u_b, vt_b, *route_p, tt, PEER_TILE,
                            [c.reshape(c.shape[:2] + row_tiles) for c in caches], new_rows)
    kv_s = [a.reshape(c.shape)[None] for a, c in zip(kv_s, caches)]
    route_s = _peer_route(pq_s, k1big, k2big, tt_s)
    p_s, _ = _peer_dense(h2_s, u_b, vt_b, *route_s, tt_s, PEER_TILE)

    fg = final_g.reshape(1, d)
    y_p = _final(x1_p, p_p, mods_p[5], fg, tm_m, l_p, 0)
    y_s = _final(x1_s, p_s, mods_s[5], fg, n_s, n_s, 0)

    return (y_p.reshape(n_p, l_p, d), y_s.reshape(n_s, l_s, d), sfin_p[None], state_s[None],
            kv_p[0], kv_s[0], kv_p[1], kv_s[1], kv_p[2], kv_s[2])
```
